```python
import math
import jax
import jax.numpy as jnp
from jax import lax
import numpy as np

D_MODEL = 1024
BATCH = 16
SEQ = 256
DEPTH = 4
DEC_BATCH = 2
DEC_SEQ = 1024
PAST_LEN = 512

GRID_W = 64
EPS = 1e-6
H_A = 4
DK_A = 128
DV_A = 128
CHUNK = 64
SHORT_CONV = 3
H_B = 8
DH_B = 64
WIN_R = 8
WIN_C = 16
QB_W = 16
KB_W = QB_W + WIN_C
A_QKV = 2 * H_A * DK_A + H_A * DV_A
A_GATE = H_A * DV_A
A_BETA = 2 * H_A
A_DECAY = 2 * H_A
B_QKV = 3 * H_B * DH_B
EVEN_IN = A_QKV + A_GATE + A_BETA + A_DECAY + B_QKV
EVEN_MIX = H_A * DV_A + H_B * DH_B
D_C = D_MODEL
HY_CONV = 3
HY_EMB = 33
HY_HID = 64
HY_FAST = 0.3
HY_SLOW = 1.5
HY_TARGET = 1e-2
D_FF = 2816
N_EXP = 8
TOP_K = 2
D_FF_E = 3584

kernel_name = 'hybrid_gdn_nat_hyena_prefix_dit_step'


def _rmsnorm(x, g):
    xf = x.astype(jnp.float32)
    y = xf * lax.rsqrt(jnp.mean(xf * xf, axis=-1, keepdims=True) + EPS)
    return (y * g.astype(jnp.float32)).astype(x.dtype)


def _l2norm(x):
    xf = x.astype(jnp.float32)
    return xf * lax.rsqrt(jnp.sum(xf * xf, axis=-1, keepdims=True) + EPS)


def _ada(cvec, w, b):
    m = jax.nn.silu(cvec) @ w + b
    return [t[..., None, :] for t in jnp.split(m, 6, axis=-1)]


def _modulate(x, g, shift, scale):
    return _rmsnorm(x, g) * (1 + scale) + shift


def _dwconv(x, w):
    k = w.shape[0]
    return lax.conv_general_dilated(x, w[:, None, :].astype(x.dtype), (1,), [(k // 2, k // 2)],
                                    dimension_numbers=('NWC', 'WIO', 'NWC'),
                                    feature_group_count=x.shape[-1])


def _gdn_chunked(q, k, v, beta, g, s0):
    b, L, h, dk = q.shape
    dv = v.shape[-1]
    n = L // CHUNK

    def chunk(t):
        return jnp.swapaxes(t.reshape((b, n, CHUNK) + t.shape[2:]), 2, 3)

    qc, kc, vc, bc, gc = chunk(q), chunk(k), chunk(v), chunk(beta), chunk(g)
    gcum = jnp.cumsum(gc, axis=-1)
    tri_incl = jnp.tril(jnp.ones((CHUNK, CHUNK), dtype=bool))
    tri_strict = jnp.tril(jnp.ones((CHUNK, CHUNK), dtype=bool), -1)
    decay_mat = jnp.exp(jnp.where(tri_incl, gcum[..., :, None] - gcum[..., None, :], -jnp.inf))
    kb = kc * bc[..., None]
    a_mat = jnp.where(tri_strict, jnp.einsum('bnhid,bnhjd->bnhij', kb, kc) * decay_mat, 0.0)
    rhs = jnp.concatenate([vc * bc[..., None], kb * jnp.exp(gcum)[..., None]], axis=-1)
    sol = lax.linalg.triangular_solve(a_mat, rhs, left_side=True, lower=True, unit_diagonal=True)
    u_base, w_mat = sol[..., :dv], sol[..., dv:]
    attn = jnp.einsum('bnhid,bnhjd->bnhij', qc, kc) * decay_mat
    q_dec = qc * jnp.exp(gcum)[..., None]
    k_dec = kc * jnp.exp(gcum[..., -1:] - gcum)[..., None]
    g_last = jnp.exp(gcum[..., -1])

    def step(s, xs):
        ub, wm, at, qd, kd, gl = xs
        u = ub - jnp.einsum('bhcd,bhde->bhce', wm, s)
        o = jnp.einsum('bhcd,bhde->bhce', qd, s) + jnp.einsum('bhij,bhje->bhie', at, u)
        s = s * gl[..., None, None] + jnp.einsum('bhcd,bhce->bhde', kd, u)
        return s, o

    xs = tuple(jnp.moveaxis(t, 1, 0) for t in (u_base, w_mat, attn, q_dec, k_dec, g_last))
    s_fin, o = lax.scan(step, s0, xs)
    o = jnp.swapaxes(jnp.moveaxis(o, 0, 1), 2, 3).reshape(b, L, h, dv)
    return o, s_fin


def _attend_dense(q, k, v):
    s = jnp.einsum('bhqd,bhkd->bhqk', q, k).astype(jnp.float32) * (DH_B ** -0.5)
    p = jax.nn.softmax(s, axis=-1).astype(v.dtype)
    return jnp.einsum('bhqk,bhkd->bhqd', p, v)


def _nat_latent(q, k, v, k_ctx, v_ctx, rpb):
    b, h, t, dh = q.shape
    rows = t // GRID_W
    wr = min(WIN_R, rows)
    ncb = GRID_W // QB_W
    q_row = jnp.arange(rows)
    key_rows = jnp.clip(q_row - WIN_R // 2, 0, rows - wr)[:, None] + jnp.arange(wr)[None, :]
    q_col = jnp.arange(GRID_W).reshape(ncb, QB_W)
    win_c0 = jnp.clip(q_col - WIN_C // 2, 0, GRID_W - WIN_C)
    key_cols = (jnp.clip(jnp.arange(ncb) * QB_W - WIN_C // 2, 0, GRID_W - KB_W)[:, None]
                + jnp.arange(KB_W)[None, :])
    ri = key_rows[:, None, :, None]
    ci = key_cols[None, :, None, :]
    n_loc = wr * KB_W
    kg = k.reshape(b, h, rows, GRID_W, dh)[:, :, ri, ci].reshape(b, h, rows, ncb, n_loc, dh)
    vg = v.reshape(b, h, rows, GRID_W, dh)[:, :, ri, ci].reshape(b, h, rows, ncb, n_loc, dh)
    qg = q.reshape(b, h, rows, ncb, QB_W, dh)
    scale = DH_B ** -0.5
    s_loc = jnp.einsum('bhrnqd,bhrnkd->bhrnqk', qg, kg).astype(jnp.float32) * scale
    dr = key_rows - q_row[:, None] + (WIN_R - 1)
    dc = key_cols[:, None, :] - q_col[:, :, None]
    kc_ = key_cols[:, None, :]
    c0 = win_c0[:, :, None]
    col_ok = (kc_ >= c0) & (kc_ < c0 + WIN_C)
    dc_idx = jnp.clip(dc + WIN_C - 1, 0, 2 * WIN_C - 2)
    bias = rpb.astype(jnp.float32)[:, dr[:, None, None, :, None], dc_idx[None, :, :, None, :]]
    bias = jnp.where(col_ok[None, None, :, :, None, :], bias, -jnp.inf).reshape(h, rows, ncb, QB_W, n_loc)
    s_loc = s_loc + bias[None]
    s_ctx = (jnp.einsum('bhtd,bhkd->bhtk', q, k_ctx).astype(jnp.float32) * scale).reshape(b, h, rows, ncb, QB_W, -1)
    p = jax.nn.softmax(jnp.concatenate([s_loc, s_ctx], axis=-1), axis=-1).astype(v.dtype)
    o = (jnp.einsum('bhrnqk,bhrnkd->bhrnqd', p[..., :n_loc], vg)
         + jnp.einsum('bhrnqk,bhkd->bhrnqd', p[..., n_loc:], v_ctx))
    return o.reshape(b, h, t, dh)


def _even_mixer(h, w_in, conv_w, a_log, dt_bias, gdn_g, qn_g, kn_g, rpb, w_out, ctx):
    b, L, _ = h.shape
    p = h @ w_in
    c1 = A_QKV
    c2 = c1 + A_GATE
    c3 = c2 + A_BETA
    c4 = c3 + A_DECAY
    qkv_a, gate_a, beta_raw, decay_raw, qkv_b = jnp.split(p, [c1, c2, c3, c4], axis=-1)
    qkv_a = jax.nn.silu(_dwconv(qkv_a, conv_w))
    q_a, k_a, v_a = jnp.split(qkv_a, [H_A * DK_A, 2 * H_A * DK_A], axis=-1)
    q_a = _l2norm(q_a.reshape(b, L, H_A, DK_A)) * (DK_A ** -0.5)
    k_a = _l2norm(k_a.reshape(b, L, H_A, DK_A))
    v_a = v_a.reshape(b, L, H_A, DV_A).astype(jnp.float32)
    beta = jax.nn.sigmoid(beta_raw.astype(jnp.float32)).reshape(b, L, 2, H_A)
    decay = -jnp.exp(a_log.astype(jnp.float32)) * jax.nn.softplus(
        decay_raw.astype(jnp.float32).reshape(b, L, 2, H_A) + dt_bias.astype(jnp.float32))
    if ctx is None:
        s0f = jnp.zeros((b, H_A, DK_A, DV_A), jnp.float32)
        s0b = s0f
    else:
        s0f = ctx[0].astype(jnp.float32)
        s0b = ctx[1].astype(jnp.float32)
    o_f, s_f = _gdn_chunked(q_a, k_a, v_a, beta[:, :, 0], decay[:, :, 0], s0f)
    rev = lambda t: jnp.flip(t, axis=1)
    o_b, s_b = _gdn_chunked(rev(q_a), rev(k_a), rev(v_a), rev(beta[:, :, 1]), rev(decay[:, :, 1]), s0b)
    o_a = _rmsnorm(o_f + rev(o_b), gdn_g) * jax.nn.silu(gate_a.reshape(b, L, H_A, DV_A).astype(jnp.float32))
    o_a = o_a.reshape(b, L, H_A * DV_A).astype(h.dtype)
    q_b, k_b, v_b = jnp.split(qkv_b, 3, axis=-1)
    q_b = _rmsnorm(q_b.reshape(b, L, H_B, DH_B), qn_g).transpose(0, 2, 1, 3)
    k_b = _rmsnorm(k_b.reshape(b, L, H_B, DH_B), kn_g).transpose(0, 2, 1, 3)
    v_b = v_b.reshape(b, L, H_B, DH_B).transpose(0, 2, 1, 3)
    if ctx is None:
        o_bb = _attend_dense(q_b, k_b, v_b)
    else:
        o_bb = _nat_latent(q_b, k_b, v_b, ctx[2].astype(h.dtype), ctx[3].astype(h.dtype), rpb)
    o_bb = o_bb.transpose(0, 2, 1, 3).reshape(b, L, H_B * DH_B)
    out = jnp.concatenate([o_a, o_bb], axis=-1) @ w_out
    return out, s_f, s_b, k_b, v_b


def _hyena_filters(L, w1, b1, w2, b2, w3, freq):
    t = jnp.linspace(0.0, 1.0, L, dtype=jnp.float32)[:, None]
    bands = (HY_EMB - 1) // 2
    w = (2.0 * math.pi / L) * jnp.arange(L, dtype=jnp.float32)[:, None]
    f = jnp.linspace(1e-4, bands - 1, bands, dtype=jnp.float32)[None, :]
    z = jnp.concatenate([t, jnp.cos(f * w), -jnp.sin(f * w)], axis=-1)
    hh = jnp.sin(freq * (z @ w1 + b1))
    hh = jnp.sin(freq * (hh @ w2 + b2))
    hh = (hh @ w3).astype(jnp.float32).reshape(L, 2, D_C)
    deltas = jnp.abs(jnp.linspace(math.log(HY_TARGET) / HY_FAST, math.log(HY_TARGET) / HY_SLOW, D_C, dtype=jnp.float32))
    return hh * jnp.exp(-t * deltas[None, :])[:, None, :]


def _bidir_longconv(u, h_f, h_b, bias):
    L = u.shape[1]
    kern = jnp.concatenate([h_f, jnp.zeros_like(h_f[:1]), h_b[:0:-1]], axis=0)
    y = jnp.fft.irfft(jnp.fft.rfft(u.astype(jnp.float32), n=2 * L, axis=1)
                      * jnp.fft.rfft(kern, n=2 * L, axis=0)[None], n=2 * L, axis=1)[:, :L]
    return (y + u.astype(jnp.float32) * bias.astype(jnp.float32)).astype(u.dtype)


def _hyena(h, w_in, conv_w, w1, b1, w2, b2, w3, freq, bias, w_out):
    L = h.shape[1]
    x0, x1, v = jnp.split(_dwconv(h @ w_in, conv_w), 3, axis=-1)
    filt = _hyena_filters(L, w1, b1, w2, b2, w3, freq)
    z = _bidir_longconv(v * x1, filt[:, 0], filt[:, 1], bias)
    return (z * x0) @ w_out


def _swiglu(h, wg, wu, wd):
    return (jax.nn.silu(h @ wg) * (h @ wu)) @ wd


def _moe(h, router_w, router_b, wg, wu, wd):
    logits = (h @ router_w).astype(jnp.float32) + router_b.astype(jnp.float32)
    top_v, top_i = lax.top_k(logits, TOP_K)
    gates = jax.nn.softmax(top_v, axis=-1)
    comb = jnp.sum(jax.nn.one_hot(top_i, N_EXP, dtype=jnp.float32) * gates[..., None], axis=-2).astype(h.dtype)
    out = jnp.zeros_like(h)
    for e in range(N_EXP):
        out = out + comb[..., e:e + 1] * _swiglu(h, wg[e], wu[e], wd[e])
    return out


def setup_inputs(seed: int = 0) -> dict:
    key = jax.random.key(seed)
    ks = jax.random.split(key, 48)
    it = iter([ks[i] for i in range(48)])

    def nrm(shape, s=1.0):
        return s * jax.random.normal(next(it), shape, jnp.float32)

    def gain(shape):
        return 1.0 + nrm(shape, 0.02)

    ne, no = (DEPTH + 1) // 2, DEPTH // 2
    d = D_MODEL
    dt = jnp.exp(jax.random.uniform(next(it), (ne, 2, H_A), jnp.float32, math.log(1e-3), math.log(1e-1)))
    a_log = jnp.log(jax.random.uniform(next(it), (ne, 2, H_A), jnp.float32, 1.0, 16.0))
    return {
        'x_prompt': nrm((BATCH, SEQ, d)),
        'x_sample': nrm((DEC_BATCH, DEC_SEQ, d)),
        'state_gdn': nrm((DEC_BATCH, ne, 2, H_A, DK_A, DV_A), 0.1),
        'cache_nat_k': nrm((DEC_BATCH, ne, H_B, PAST_LEN, DH_B)),
        'cache_nat_v': nrm((DEC_BATCH, ne, H_B, PAST_LEN, DH_B)),
        'c': nrm((DEC_BATCH, d)),
        'c_ctx': nrm((d,)),
        'ada_w': nrm((DEPTH, d, 6 * d), 0.5 * d ** -0.5),
        'ada_b': nrm((DEPTH, 6 * d), 0.02),
        'norm_mix_g': gain((DEPTH, d)),
        'norm_ffn_g': gain((DEPTH, d)),
        'even_w_in': nrm((ne, d, EVEN_IN), d ** -0.5),
        'gdn_conv_w': nrm((ne, SHORT_CONV, A_QKV), SHORT_CONV ** -0.5),
        'gdn_a_log': a_log,
        'gdn_dt_bias': dt + jnp.log(-jnp.expm1(-dt)),
        'gdn_norm_g': gain((ne, DV_A)),
        'nat_q_norm_g': gain((ne, DH_B)),
        'nat_k_norm_g': gain((ne, DH_B)),
        'nat_rpb': nrm((ne, H_B, 2 * WIN_R - 1, 2 * WIN_C - 1), 0.1),
        'even_w_out': nrm((ne, EVEN_MIX, d), EVEN_MIX ** -0.5),
        'ffn_w_gate': nrm((ne, d, D_FF), d ** -0.5),
        'ffn_w_up': nrm((ne, d, D_FF), d ** -0.5),
        'ffn_w_down': nrm((ne, D_FF, d), D_FF ** -0.5),
        'odd_w_in': nrm((no, d, 3 * D_C), d ** -0.5),
        'hy_conv_w': nrm((no, HY_CONV, 3 * D_C), HY_CONV ** -0.5),
        'hy_w1': nrm((no, HY_EMB, HY_HID), 0.5),
        'hy_b1': nrm((no, HY_HID), 0.1),
        'hy_w2': nrm((no, HY_HID, HY_HID), HY_HID ** -0.5),
        'hy_b2': nrm((no, HY_HID), 0.1),
        'hy_w3': nrm((no, HY_HID, 2 * D_C), 0.05 * HY_HID ** -0.5),
        'hy_freq': 1.0 + nrm((no, HY_HID), 0.1),
        'hy_bias': nrm((no, D_C)),
        'odd_w_out': nrm((no, D_C, d), D_C ** -0.5),
        'moe_router_w': nrm((no, d, N_EXP), d ** -0.5),
        'moe_router_b': nrm((no, N_EXP), 0.01),
        'moe_w_gate': nrm((no, N_EXP, d, D_FF_E), d ** -0.5),
        'moe_w_up': nrm((no, N_EXP, d, D_FF_E), d ** -0.5),
        'moe_w_down': nrm((no, N_EXP, D_FF_E, d), D_FF_E ** -0.5),
    }


def reference(x_prompt, x_sample, state_gdn, cache_nat_k, cache_nat_v, c, c_ctx,
              ada_w, ada_b, norm_mix_g, norm_ffn_g,
              even_w_in, gdn_conv_w, gdn_a_log, gdn_dt_bias, gdn_norm_g,
              nat_q_norm_g, nat_k_norm_g, nat_rpb, even_w_out,
              ffn_w_gate, ffn_w_up, ffn_w_down,
              odd_w_in, hy_conv_w, hy_w1, hy_b1, hy_w2, hy_b2, hy_w3, hy_freq, hy_bias, odd_w_out,
              moe_router_w, moe_router_b, moe_w_gate, moe_w_up, moe_w_down):
    xp = x_prompt
    xs = x_sample
    gdn_states, ctx_k, ctx_v = [], [], []
    for l in range(DEPTH):
        i = l // 2
        sp1, cp1, gp1, sp2, cp2, gp2 = _ada(c_ctx, ada_w[l], ada_b[l])
        ss1, cs1, gs1, ss2, cs2, gs2 = _ada(c, ada_w[l], ada_b[l])
        hp = _modulate(xp, norm_mix_g[l], sp1, cp1)
        hs = _modulate(xs, norm_mix_g[l], ss1, cs1)
        if l % 2 == 0:
            ev = (even_w_in[i], gdn_conv_w[i], gdn_a_log[i], gdn_dt_bias[i], gdn_norm_g[i],
                  nat_q_norm_g[i], nat_k_norm_g[i], nat_rpb[i], even_w_out[i])
            op, s_f, s_b, kp, vp = _even_mixer(hp, *ev, None)
            st = state_gdn[:, i]
            os_, _, _, _, _ = _even_mixer(hs, *ev, (st[:, 0], st[:, 1], cache_nat_k[:, i], cache_nat_v[:, i]))
            gdn_states.append(jnp.stack([s_f, s_b], axis=1))
            ctx_k.append(kp)
            ctx_v.append(vp)
        else:
            od = (odd_w_in[i], hy_conv_w[i], hy_w1[i], hy_b1[i], hy_w2[i], hy_b2[i], hy_w3[i],
                  hy_freq[i], hy_bias[i], odd_w_out[i])
            op = _hyena(hp, *od)
            os_ = _hyena(hs, *od)
        xp = xp + gp1 * op
        xs = xs + gs1 * os_
        hp = _modulate(xp, norm_ffn_g[l], sp2, cp2)
        hs = _modulate(xs, norm_ffn_g[l], ss2, cs2)
        if l % 2 == 0:
            fp = _swiglu(hp, ffn_w_gate[i], ffn_w_up[i], ffn_w_down[i])
            fs = _swiglu(hs, ffn_w_gate[i], ffn_w_up[i], ffn_w_down[i])
        else:
            fp = _moe(hp, moe_router_w[i], moe_router_b[i], moe_w_gate[i], moe_w_up[i], moe_w_down[i])
            fs = _moe(hs, moe_router_w[i], moe_router_b[i], moe_w_gate[i], moe_w_up[i], moe_w_down[i])
        xp = xp + gp2 * fp
        xs = xs + gs2 * fs
    new_state_gdn = jnp.stack(gdn_states, axis=1).astype(x_prompt.dtype)
    new_cache_nat_k = jnp.stack(ctx_k, axis=1)
    new_cache_nat_v = jnp.stack(ctx_v, axis=1)
    return (xp, xs, new_state_gdn, new_cache_nat_k, new_cache_nat_v)
```

```python
import functools
import math

import jax
import jax.numpy as jnp
import numpy as np
from jax import lax
from jax.experimental import pallas as pl
from jax.experimental.pallas import tpu as pltpu

F32 = jnp.float32
BF16 = jnp.bfloat16

D = 1024
BATCH = 16
SEQ = 256
DEPTH = 4
DEC_BATCH = 2
DEC_SEQ = 1024
PAST_LEN = 512
GRID_W = 64
EPS = 1e-6
H_A = 4
DK_A = 128
DV_A = 128
CHUNK = 64
H_B = 8
DH_B = 64
WIN_R = 8
WIN_C = 16
A_QKV = 2 * H_A * DK_A + H_A * DV_A
A_GATE = H_A * DV_A
B_QKV = 3 * H_B * DH_B
HY_EMB = 33
HY_HID = 64
HY_FAST = 0.3
HY_SLOW = 1.5
HY_TARGET = 1e-2
D_FF = 2816
N_EXP = 8
D_FF_E = 3584

T_CTX = BATCH * SEQ
T_LAT = DEC_BATCH * DEC_SEQ
T_ALL = T_CTX + T_LAT
N_SEG = 8
ROW_TILE = 256

VMEM_LIMIT = 56 * 1024 * 1024


def _cparams(sem, vmem=None):
    return pltpu.CompilerParams(dimension_semantics=sem, vmem_limit_bytes=vmem)


def _sigmoid(x):
    return 1.0 / (1.0 + jnp.exp(-x))


def _seg_of_row(row):
    return jnp.where(row < T_CTX, 0, 1 + (row - T_CTX) // DEC_SEQ)


def _split_bf16(x):
    hi = x.astype(BF16)
    lo = (x - hi.astype(F32)).astype(BF16)
    return hi, lo


def _dot(a, b, dims=(((1,), (0,)), ((), ())), prec="bf16"):
    if prec == "bf16":
        return lax.dot_general(a.astype(BF16), b.astype(BF16), dims, preferred_element_type=F32)
    ah, al = _split_bf16(a.astype(F32))
    bh, bl = _split_bf16(b.astype(F32))
    r = lax.dot_general(ah, bh, dims, preferred_element_type=F32)
    r = r + lax.dot_general(ah, bl, dims, preferred_element_type=F32)
    r = r + lax.dot_general(al, bh, dims, preferred_element_type=F32)
    return r


NT_DIMS = (((1,), (1,)), ((), ()))
TN_DIMS = (((0,), (0,)), ((), ()))


def _ada_kernel(cv_ref, w_ref, b_ref, o_ref):
    cv = cv_ref[...]
    s = cv * _sigmoid(cv)
    o_ref[0] = _dot(s, w_ref[0]) + b_ref[0]


def _ada_all(c, c_ctx, ada_w, ada_b):
    cv = jnp.zeros((N_SEG, D), F32).at[0].set(c_ctx).at[1:1 + DEC_BATCH].set(c)
    tn = 1536
    out = pl.pallas_call(
        _ada_kernel,
        grid=(DEPTH, 6 * D // tn),
        in_specs=[
            pl.BlockSpec((N_SEG, D), lambda l, j: (0, 0)),
            pl.BlockSpec((1, D, tn), lambda l, j: (l, 0, j)),
            pl.BlockSpec((1, 1, tn), lambda l, j: (l, 0, j)),
        ],
        out_specs=pl.BlockSpec((1, N_SEG, tn), lambda l, j: (l, 0, j)),
        out_shape=jax.ShapeDtypeStruct((DEPTH, N_SEG, 6 * D), F32),
        compiler_params=_cparams(("arbitrary", "arbitrary"), VMEM_LIMIT),
        name="ada",
    )(cv, ada_w, ada_b.reshape(DEPTH, 1, 6 * D))
    return out.reshape(DEPTH, N_SEG, 6, D).transpose(0, 2, 1, 3).reshape(DEPTH * 6 * N_SEG, 1, D)


def _mod_spec(layer, k, tm, row_of_step):
    base = (layer * 6 + k) * N_SEG

    def imap(*ids):
        return (base + _seg_of_row(row_of_step(*ids) * tm), 0, 0)

    return pl.BlockSpec((1, 1, D), imap)


def _modulate_math(x, g, scale, shift):
    ms = jnp.mean(x * x, axis=-1, keepdims=True)
    y = x * lax.rsqrt(ms + EPS) * g
    return y * (1.0 + scale) + shift


def _modulate_kernel(x_ref, g_ref, sc_ref, sh_ref, o_ref):
    o_ref[...] = _modulate_math(x_ref[...], g_ref[...], sc_ref[0], sh_ref[0]).astype(o_ref.dtype)


def _modulate(x, g, mod, layer, k_shift, out_dtype):
    tm = 512
    return pl.pallas_call(
        _modulate_kernel,
        grid=(T_ALL // tm,),
        in_specs=[
            pl.BlockSpec((tm, D), lambda i: (i, 0)),
            pl.BlockSpec((1, D), lambda i: (0, 0)),
            _mod_spec(layer, k_shift + 1, tm, lambda i: i),
            _mod_spec(layer, k_shift, tm, lambda i: i),
        ],
        out_specs=pl.BlockSpec((tm, D), lambda i: (i, 0)),
        out_shape=jax.ShapeDtypeStruct((T_ALL, D), out_dtype),
        compiler_params=_cparams(("arbitrary",)),
        name="modulate",
    )(x, g.reshape(1, D), mod, mod)


def _mm_kernel(*refs, n_w, epilogue):
    h_ref = refs[0]
    w_refs = refs[1:1 + n_w]
    rest = refs[1 + n_w:]
    if epilogue == "residual":
        x_ref, gate_ref, o_ref, wbf = rest
    else:
        o_ref, wbf = rest

    @pl.when(pl.program_id(1) == 0)
    def _():
        for k in range(n_w):
            wbf[k] = w_refs[k][...].astype(BF16)

    h = h_ref[...].astype(BF16)
    a = jnp.dot(h, wbf[0], preferred_element_type=F32)
    if epilogue == "swiglu":
        b = jnp.dot(h, wbf[1], preferred_element_type=F32)
        r = a * _sigmoid(a) * b
    elif epilogue == "residual":
        r = x_ref[...] + gate_ref[0] * a
    else:
        r = a
    o_ref[...] = r.astype(o_ref.dtype)


def _mm(h, ws, *, tm, tn, out_dtype, epilogue="none", resid=None, mod=None, layer=None, k_gate=None):
    m, kdim = h.shape
    n = ws[0].shape[1]
    n_w = len(ws)
    in_specs = [pl.BlockSpec((tm, kdim), lambda j, i: (i, 0))]
    in_specs += [pl.BlockSpec((kdim, tn), lambda j, i: (0, j)) for _ in ws]
    args = [h, *ws]
    if epilogue == "residual":
        base = (layer * 6 + k_gate) * N_SEG
        in_specs.append(pl.BlockSpec((tm, tn), lambda j, i: (i, j)))
        in_specs.append(pl.BlockSpec((1, 1, tn), lambda j, i: (base + _seg_of_row(i * tm), 0, j)))
        args += [resid, mod]
    return pl.pallas_call(
        functools.partial(_mm_kernel, n_w=n_w, epilogue=epilogue),
        grid=(n // tn, m // tm),
        in_specs=in_specs,
        out_specs=pl.BlockSpec((tm, tn), lambda j, i: (i, j)),
        out_shape=jax.ShapeDtypeStruct((m, n), out_dtype),
        scratch_shapes=[pltpu.VMEM((n_w, kdim, tn), BF16)],
        compiler_params=_cparams(("arbitrary", "arbitrary"), VMEM_LIMIT),
        name="mm_" + epilogue,
    )(*args)


def _conv3(x, prev_ref, next_ref, w, seq_len):
    i = pl.program_id(0)
    rows_n = x.shape[0]
    if seq_len is None:
        j = jnp.maximum(i - T_CTX // rows_n, 0)
        per = DEC_SEQ // rows_n
        is_ctx = i < T_CTX // rows_n
        first = is_ctx | (j % per == 0)
        last = is_ctx | (j % per == per - 1)
    else:
        first = (i * rows_n) % seq_len == 0
        last = ((i + 1) * rows_n) % seq_len == 0
    prev_row = jnp.where(first, 0.0, prev_ref[7:8, :])
    next_row = jnp.where(last, 0.0, next_ref[0:1, :])
    rows = lax.broadcasted_iota(jnp.int32, x.shape, 0)
    xm = jnp.where(rows == 0, prev_row, pltpu.roll(x, 1, 0))
    xp = jnp.where(rows == rows_n - 1, next_row, pltpu.roll(x, rows_n - 1, 0))
    return xm * w[0:1, :] + x * w[1:2, :] + xp * w[2:3, :]


def _halo_specs(width, col_block, row_off_tiles):
    per = ROW_TILE // 8
    last_blk = T_ALL // 8 - 1
    main = pl.BlockSpec((ROW_TILE, width), lambda i: (i + row_off_tiles, col_block))
    prev = pl.BlockSpec((8, width), lambda i: (jnp.maximum((i + row_off_tiles) * per - 1, 0), col_block))
    nxt = pl.BlockSpec((8, width), lambda i: (jnp.minimum((i + row_off_tiles + 1) * per, last_blk), col_block))
    return main, prev, nxt


def _cumsum_rows(x, reverse):
    n = x.shape[0]
    rows = lax.broadcasted_iota(jnp.int32, x.shape, 0)
    k = 1
    while k < n:
        if reverse:
            x = x + jnp.where(rows < n - k, pltpu.roll(x, n - k, 0), 0.0)
        else:
            x = x + jnp.where(rows >= k, pltpu.roll(x, k, 0), 0.0)
        k *= 2
    return x


GDN_PREC = "bf16x3"


def _gdn_prep_kernel(pm_ref, prev_ref, next_ref, ps_ref, cw_ref, par_ref, qkv_ref, bg_ref, *, seq_len):
    x = _conv3(pm_ref[...], prev_ref, next_ref, cw_ref[...], seq_len)
    x = x * _sigmoid(x)
    for h in range(H_A):
        sl = slice(h * DK_A, (h + 1) * DK_A)
        qh = x[:, sl]
        qkv_ref[:, sl] = qh * lax.rsqrt(jnp.sum(qh * qh, axis=-1, keepdims=True) + EPS) * (DK_A ** -0.5)
        sl = slice(H_A * DK_A + h * DK_A, H_A * DK_A + (h + 1) * DK_A)
        kh = x[:, sl]
        qkv_ref[:, sl] = kh * lax.rsqrt(jnp.sum(kh * kh, axis=-1, keepdims=True) + EPS)
    qkv_ref[:, 2 * H_A * DK_A:] = x[:, 2 * H_A * DK_A:]
    raw = ps_ref[...]
    lane = lax.broadcasted_iota(jnp.int32, raw.shape, 1)
    beta = _sigmoid(raw)
    z = raw + par_ref[1:2, :]
    softplus = jnp.maximum(z, 0.0) + jnp.log(1.0 + jnp.exp(-jnp.abs(z)))
    g = -jnp.exp(par_ref[0:1, :]) * softplus
    bg_ref[...] = jnp.where(lane < 2 * H_A, beta, jnp.where(lane < 4 * H_A, g, 0.0))


def _gdn_prep(p_main, p_small, conv_w, a_log, dt_bias, row_off_tiles, n_rows, seq_len):
    par = jnp.zeros((2, 128), F32)
    par = par.at[0, 2 * H_A:4 * H_A].set(a_log.reshape(-1)).at[1, 2 * H_A:4 * H_A].set(dt_bias.reshape(-1))
    main, prev, nxt = _halo_specs(A_QKV, 0, row_off_tiles)
    return pl.pallas_call(
        functools.partial(_gdn_prep_kernel, seq_len=seq_len),
        grid=(n_rows // ROW_TILE,),
        in_specs=[main, prev, nxt,
                  pl.BlockSpec((ROW_TILE, 128), lambda i: (i + row_off_tiles, 0)),
                  pl.BlockSpec((3, A_QKV), lambda i: (0, 0)),
                  pl.BlockSpec((2, 128), lambda i: (0, 0))],
        out_specs=[pl.BlockSpec((ROW_TILE, A_QKV), lambda i: (i, 0)),
                   pl.BlockSpec((ROW_TILE, 128), lambda i: (i, 0))],
        out_shape=[jax.ShapeDtypeStruct((n_rows, A_QKV), F32),
                   jax.ShapeDtypeStruct((n_rows, 128), F32)],
        compiler_params=_cparams(("arbitrary",)),
        name="gdn_prep",
    )(p_main, p_main, p_main, p_small, conv_w, par)


def _gdn_chunk(qkv, bg, state, d, h):
    c = CHUNK
    qh = qkv[:, h * DK_A:(h + 1) * DK_A]
    kh = qkv[:, H_A * DK_A + h * DK_A:H_A * DK_A + (h + 1) * DK_A]
    vh = qkv[:, 2 * H_A * DK_A + h * DV_A:2 * H_A * DK_A + (h + 1) * DV_A]
    col = d * H_A + h
    beta = bg[:, col:col + 1]
    gcum = _cumsum_rows(bg[:, 2 * H_A:4 * H_A], reverse=(d == 1))
    gc = gcum[:, col:col + 1]
    ri = lax.broadcasted_iota(jnp.int32, (c, c), 0)
    ci = lax.broadcasted_iota(jnp.int32, (c, c), 1)
    eye = ri == ci
    incl = (ri >= ci) if d == 0 else (ri <= ci)
    strict = (ri > ci) if d == 0 else (ri < ci)
    gc_row = jnp.sum(jnp.where(eye, gc, 0.0), axis=0, keepdims=True)
    decay = jnp.where(incl, jnp.exp(jnp.where(incl, gc - gc_row, 0.0)), 0.0)
    kb = kh * beta
    a_mat = jnp.where(strict, _dot(kb, kh, NT_DIMS, GDN_PREC) * decay, 0.0)
    t_inv = jnp.where(eye, 1.0, 0.0) - a_mat
    pw = a_mat
    for _ in range(5):
        pw = _dot(pw, pw, prec=GDN_PREC)
        t_inv = t_inv + _dot(t_inv, pw, prec=GDN_PREC)
    eg = jnp.exp(gc)
    rhs = jnp.concatenate([vh * beta, kb * eg], axis=1)
    sol = _dot(t_inv, rhs, prec=GDN_PREC)
    ub, wm = sol[:, :DV_A], sol[:, DV_A:]
    attn = jnp.where(incl, _dot(qh, kh, NT_DIMS, GDN_PREC) * decay, 0.0)
    g_end = gc[c - 1:c, :] if d == 0 else gc[0:1, :]
    qd = qh * eg
    kd = kh * jnp.exp(g_end - gc)
    u = ub - _dot(wm, state, prec=GDN_PREC)
    o = _dot(qd, state, prec=GDN_PREC) + _dot(attn, u, prec=GDN_PREC)
    new_state = state * jnp.exp(g_end) + _dot(kd, u, TN_DIMS, GDN_PREC)
    return o, new_state


def _gdn_scan_kernel(*refs, ns, has_s0):
    if has_s0:
        qf_ref, qb_ref, bf_ref, bb_ref, s0_ref, of_ref, ob_ref, sfin_ref, st = refs
    else:
        qf_ref, qb_ref, bf_ref, bb_ref, of_ref, ob_ref, sfin_ref, st = refs
    step = pl.program_id(1)

    @pl.when(step == 0)
    def _():
        if has_s0:
            st[...] = s0_ref[:, 0]
        else:
            st[...] = jnp.zeros(st.shape, F32)

    def per_seq(s, carry):
        for d in range(2):
            qkv = (qf_ref if d == 0 else qb_ref)[s, 0]
            bg = (bf_ref if d == 0 else bb_ref)[s, 0]
            o_ref = of_ref if d == 0 else ob_ref
            for h in range(H_A):
                o, new_state = _gdn_chunk(qkv, bg, st[s, d, h], d, h)
                o_ref[s, 0, :, h * DV_A:(h + 1) * DV_A] = o
                st[s, d, h] = new_state
        return carry

    lax.fori_loop(0, ns, per_seq, 0)

    @pl.when(step == pl.num_programs(1) - 1)
    def _():
        sfin_ref[...] = st[...]


def _gdn_scan(qkv, bg, s0, layer_i, n_seq, seq_len, ns):
    n = seq_len // CHUNK
    qkv4 = qkv.reshape(n_seq, n, CHUNK, A_QKV)
    bg4 = bg.reshape(n_seq, n, CHUNK, 128)
    fwd = lambda g, c: (g, c, 0, 0)
    bwd = lambda g, c: (g, n - 1 - c, 0, 0)
    in_specs = [pl.BlockSpec((ns, 1, CHUNK, A_QKV), fwd), pl.BlockSpec((ns, 1, CHUNK, A_QKV), bwd),
                pl.BlockSpec((ns, 1, CHUNK, 128), fwd), pl.BlockSpec((ns, 1, CHUNK, 128), bwd)]
    args = [qkv4, qkv4, bg4, bg4]
    if s0 is not None:
        in_specs.append(pl.BlockSpec((ns, 1, 2, H_A, DK_A, DV_A), lambda g, c: (g, layer_i, 0, 0, 0, 0)))
        args.append(s0)
    o_shape = jax.ShapeDtypeStruct((n_seq, n, CHUNK, H_A * DV_A), F32)
    of, ob, sfin = pl.pallas_call(
        functools.partial(_gdn_scan_kernel, ns=ns, has_s0=s0 is not None),
        grid=(n_seq // ns, n),
        in_specs=in_specs,
        out_specs=[pl.BlockSpec((ns, 1, CHUNK, H_A * DV_A), fwd),
                   pl.BlockSpec((ns, 1, CHUNK, H_A * DV_A), bwd),
                   pl.BlockSpec((ns, 2, H_A, DK_A, DV_A), lambda g, c: (g, 0, 0, 0, 0))],
        out_shape=[o_shape, o_shape, jax.ShapeDtypeStruct((n_seq, 2, H_A, DK_A, DV_A), F32)],
        scratch_shapes=[pltpu.VMEM((ns, 2, H_A, DK_A, DV_A), F32)],
        compiler_params=_cparams(("arbitrary", "arbitrary")),
        name="gdn_scan",
    )(*args)
    rows = n_seq * seq_len
    return of.reshape(rows, H_A * DV_A), ob.reshape(rows, H_A * DV_A), sfin


def _gdn_post_kernel(of_ref, ob_ref, gate_ref, g_ref, o_ref):
    o = of_ref[...] + ob_ref[...]
    gate = gate_ref[...]
    for h in range(H_A):
        sl = slice(h * DV_A, (h + 1) * DV_A)
        oh = o[:, sl]
        y = oh * lax.rsqrt(jnp.mean(oh * oh, axis=-1, keepdims=True) + EPS) * g_ref[...]
        gh = gate[:, sl]
        o_ref[:, sl] = (y * (gh * _sigmoid(gh))).astype(o_ref.dtype)


def _gdn_post(of, ob, p_main, gdn_g, row_off_tiles, n_rows):
    gate_blk = A_QKV // A_GATE
    return pl.pallas_call(
        _gdn_post_kernel,
        grid=(n_rows // ROW_TILE,),
        in_specs=[pl.BlockSpec((ROW_TILE, A_GATE), lambda i: (i, 0)),
                  pl.BlockSpec((ROW_TILE, A_GATE), lambda i: (i, 0)),
                  pl.BlockSpec((ROW_TILE, A_GATE), lambda i: (i + row_off_tiles, gate_blk)),
                  pl.BlockSpec((1, DV_A), lambda i: (0, 0))],
        out_specs=pl.BlockSpec((ROW_TILE, A_GATE), lambda i: (i, 0)),
        out_shape=jax.ShapeDtypeStruct((n_rows, A_GATE), BF16),
        compiler_params=_cparams(("arbitrary",)),
        name="gdn_post",
    )(of, ob, p_main, gdn_g.reshape(1, DV_A))


NAT_W = H_B * DH_B
NAT_QCOL = (A_QKV + A_GATE) // NAT_W
NAT_SCALE = DH_B ** -0.5


def _nat_prep_kernel(q_ref, k_ref, bd_ref, gq_ref, gk_ref, qn_ref, kn_ref):
    bd = bd_ref[...]
    for x_ref, g_ref, o_ref in ((q_ref, gq_ref, qn_ref), (k_ref, gk_ref, kn_ref)):
        x = x_ref[...]
        hi, lo = _split_bf16(x * x)
        ms = jnp.dot(hi, bd, preferred_element_type=F32) + jnp.dot(lo, bd, preferred_element_type=F32)
        o_ref[...] = x * lax.rsqrt(ms + EPS) * g_ref[...]


def _nat_prep(p_main, qn_g, kn_g):
    grp = np.arange(NAT_W) // DH_B
    bd = jnp.asarray((grp[:, None] == grp[None, :]).astype(np.float32) / DH_B, BF16)
    spec = lambda cb: pl.BlockSpec((ROW_TILE, NAT_W), lambda i: (i, cb))
    return pl.pallas_call(
        _nat_prep_kernel,
        grid=(T_ALL // ROW_TILE,),
        in_specs=[spec(NAT_QCOL), spec(NAT_QCOL + 1),
                  pl.BlockSpec((NAT_W, NAT_W), lambda i: (0, 0)),
                  pl.BlockSpec((1, NAT_W), lambda i: (0, 0)),
                  pl.BlockSpec((1, NAT_W), lambda i: (0, 0))],
        out_specs=[spec(0), spec(0)],
        out_shape=[jax.ShapeDtypeStruct((T_ALL, NAT_W), F32)] * 2,
        compiler_params=_cparams(("arbitrary",)),
        name="nat_prep",
    )(p_main, p_main, bd, jnp.tile(qn_g, H_B).reshape(1, NAT_W), jnp.tile(kn_g, H_B).reshape(1, NAT_W))


def _pair_masks():
    lane = lax.broadcasted_iota(jnp.int32, (1, 2 * DH_B), 1)
    return lane < DH_B


def _nat_ctx_kernel(q_ref, k_ref, v_ref, o_ref):
    lo = _pair_masks()
    for p in range(H_B // 2):
        sl = slice(p * 2 * DH_B, (p + 1) * 2 * DH_B)
        q2, k2, v2 = q_ref[:, sl], k_ref[:, sl], v_ref[:, sl]
        halves = []
        for half in range(2):
            qm = jnp.where(lo if half == 0 else jnp.logical_not(lo), q2, 0.0)
            s = _dot(qm, k2, NT_DIMS) * NAT_SCALE
            e = jnp.exp(s - jnp.max(s, axis=-1, keepdims=True))
            pr = e / jnp.sum(e, axis=-1, keepdims=True)
            halves.append(_dot(pr, v2))
        o_ref[:, sl] = jnp.where(lo, halves[0], halves[1]).astype(o_ref.dtype)


def _nat_ctx(qn, kn, p_main):
    spec = lambda cb: pl.BlockSpec((SEQ, NAT_W), lambda b: (b, cb))
    return pl.pallas_call(
        _nat_ctx_kernel,
        grid=(BATCH,),
        in_specs=[spec(0), spec(0), spec(NAT_QCOL + 2)],
        out_specs=spec(0),
        out_shape=jax.ShapeDtypeStruct((T_CTX, NAT_W), BF16),
        compiler_params=_cparams(("arbitrary",)),
        name="nat_ctx",
    )(qn, kn, p_main)


def _nat_bias_kernel(r_ref, e_ref, ok_ref, o_ref):
    r = r_ref[...]
    hi, lo = _split_bf16(r)
    lo2 = (r - hi.astype(F32) - lo.astype(F32)).astype(BF16)
    e = e_ref[...]
    t = (jnp.dot(hi, e, preferred_element_type=F32) + jnp.dot(lo, e, preferred_element_type=F32)
         + jnp.dot(lo2, e, preferred_element_type=F32))
    o_ref[...] = jnp.where(ok_ref[...] > 0.5, t, -jnp.inf)


def _nat_bias(rpb):
    n_dr, n_dc = 2 * WIN_R - 1, 2 * WIN_C - 1
    qc = np.arange(GRID_W)[:, None]
    kc = np.arange(GRID_W)[None, :]
    dc = (kc - qc + WIN_C - 1).reshape(-1)
    c0 = np.clip(qc - WIN_C // 2, 0, GRID_W - WIN_C)
    ok = ((kc >= c0) & (kc < c0 + WIN_C)).reshape(1, -1).astype(np.float32)
    onehot = (np.arange(128)[:, None] == dc[None, :]).astype(np.float32)
    rows = H_B * n_dr
    rp = jnp.zeros((rows, 128), F32).at[:, :n_dc].set(rpb.reshape(rows, n_dc))
    tab = pl.pallas_call(
        _nat_bias_kernel,
        out_shape=jax.ShapeDtypeStruct((rows, GRID_W * GRID_W), F32),
        name="nat_bias",
    )(rp, jnp.asarray(onehot, BF16), jnp.asarray(ok))
    tab = tab.reshape(H_B, n_dr, GRID_W, GRID_W)
    win = jnp.stack([tab[:, d0:d0 + WIN_R] for d0 in range(WIN_R)], axis=1)
    return win.transpose(0, 1, 3, 2, 4).reshape(H_B, WIN_R, GRID_W, WIN_R * GRID_W)


def _nat_lat_kernel(q_ref, k_ref, v_ref, kc_ref, vc_ref, bias_ref, o_ref):
    r = pl.program_id(1)
    rows = DEC_SEQ // GRID_W
    r0 = jnp.clip(r - WIN_R // 2, 0, rows - WIN_R)
    start = pl.multiple_of(r0 * GRID_W, GRID_W)
    n_loc = WIN_R * GRID_W
    lo = _pair_masks()
    for p in range(H_B // 2):
        sl = slice(p * 2 * DH_B, (p + 1) * 2 * DH_B)
        q2 = q_ref[:, sl]
        kw = k_ref[0, pl.ds(start, n_loc), sl]
        vw = v_ref[pl.ds(start, n_loc), sl]
        kc, vc = kc_ref[0, :, sl], vc_ref[0, :, sl]
        halves = []
        for half in range(2):
            qm = jnp.where(lo if half == 0 else jnp.logical_not(lo), q2, 0.0)
            s_loc = _dot(qm, kw, NT_DIMS) * NAT_SCALE + bias_ref[2 * p + half, 0]
            s_ctx = _dot(qm, kc, NT_DIMS) * NAT_SCALE
            m = jnp.maximum(jnp.max(s_loc, axis=-1, keepdims=True), jnp.max(s_ctx, axis=-1, keepdims=True))
            e_loc, e_ctx = jnp.exp(s_loc - m), jnp.exp(s_ctx - m)
            inv = 1.0 / (jnp.sum(e_loc, axis=-1, keepdims=True) + jnp.sum(e_ctx, axis=-1, keepdims=True))
            halves.append(_dot(e_loc * inv, vw) + _dot(e_ctx * inv, vc))
        o_ref[:, sl] = jnp.where(lo, halves[0], halves[1]).astype(o_ref.dtype)


def _nat_lat(qn, kn, p_main, kc, vc, bias):
    rows = DEC_SEQ // GRID_W
    lat_tile0 = T_CTX // GRID_W
    lat_seq0 = T_CTX // DEC_SEQ

    def bias_map(b, r):
        r0 = jnp.clip(r - WIN_R // 2, 0, rows - WIN_R)
        return (0, WIN_R - 1 + r0 - r, 0, 0)

    return pl.pallas_call(
        _nat_lat_kernel,
        grid=(DEC_BATCH, rows),
        in_specs=[pl.BlockSpec((GRID_W, NAT_W), lambda b, r: (lat_tile0 + b * rows + r, 0)),
                  pl.BlockSpec((1, DEC_SEQ, NAT_W), lambda b, r: (lat_seq0 + b, 0, 0)),
                  pl.BlockSpec((DEC_SEQ, NAT_W), lambda b, r: (lat_seq0 + b, NAT_QCOL + 2)),
                  pl.BlockSpec((1, PAST_LEN, NAT_W), lambda b, r: (b, 0, 0)),
                  pl.BlockSpec((1, PAST_LEN, NAT_W), lambda b, r: (b, 0, 0)),
                  pl.BlockSpec((H_B, 1, GRID_W, WIN_R * GRID_W), bias_map)],
        out_specs=pl.BlockSpec((GRID_W, NAT_W), lambda b, r: (b * rows + r, 0)),
        out_shape=jax.ShapeDtypeStruct((T_LAT, NAT_W), BF16),
        compiler_params=_cparams(("arbitrary", "arbitrary"), VMEM_LIMIT),
        name="nat_lat",
    )(qn, kn.reshape(T_ALL // DEC_SEQ, DEC_SEQ, NAT_W), p_main, kc, vc, bias)


def _heads_to_lanes(cache):
    b, h, l, dh = cache.shape
    return cache.transpose(0, 2, 1, 3).reshape(b, l, h * dh)


def _lanes_to_heads(x, b, l):
    return x.reshape(b, l, H_B, DH_B).transpose(0, 2, 1, 3)


def _even_mixer_layer(x, mod, l, i, w):
    h = _modulate(x, w["norm_mix_g"][l], mod, l, 0, BF16)
    w_in = w["even_w_in"][i]
    c_beta = A_QKV + A_GATE
    c_qkvb = c_beta + 4 * H_A
    w_main = jnp.concatenate([w_in[:, :c_beta], w_in[:, c_qkvb:]], axis=1)
    w_small = jnp.zeros((D, 128), F32).at[:, :4 * H_A].set(w_in[:, c_beta:c_qkvb])
    p_main = _mm(h, [w_main], tm=1024, tn=512, out_dtype=F32)
    p_small = _mm(h, [w_small], tm=1024, tn=128, out_dtype=F32)

    conv_w, a_log, dt_bias = w["gdn_conv_w"][i], w["gdn_a_log"][i], w["gdn_dt_bias"][i]
    ctx_tiles = T_CTX // ROW_TILE
    qkv_c, bg_c = _gdn_prep(p_main, p_small, conv_w, a_log, dt_bias, 0, T_CTX, SEQ)
    qkv_l, bg_l = _gdn_prep(p_main, p_small, conv_w, a_log, dt_bias, ctx_tiles, T_LAT, DEC_SEQ)
    of_c, ob_c, s_ctx = _gdn_scan(qkv_c, bg_c, None, i, BATCH, SEQ, 4)
    of_l, ob_l, _ = _gdn_scan(qkv_l, bg_l, w["state_gdn"], i, DEC_BATCH, DEC_SEQ, 2)
    oa_c = _gdn_post(of_c, ob_c, p_main, w["gdn_norm_g"][i], 0, T_CTX)
    oa_l = _gdn_post(of_l, ob_l, p_main, w["gdn_norm_g"][i], ctx_tiles, T_LAT)

    qn, kn = _nat_prep(p_main, w["nat_q_norm_g"][i], w["nat_k_norm_g"][i])
    ob_c = _nat_ctx(qn, kn, p_main)
    bias = _nat_bias(w["nat_rpb"][i])
    kc = _heads_to_lanes(w["cache_nat_k"][:, i])
    vc = _heads_to_lanes(w["cache_nat_v"][:, i])
    ob_l = _nat_lat(qn, kn, p_main, kc, vc, bias)

    mix = jnp.concatenate([jnp.concatenate([oa_c, ob_c], axis=1),
                           jnp.concatenate([oa_l, ob_l], axis=1)], axis=0)
    x = _mm(mix, [w["even_w_out"][i]], tm=1024, tn=512, out_dtype=F32, epilogue="residual",
            resid=x, mod=mod, layer=l, k_gate=2)

    h2 = _modulate(x, w["norm_ffn_g"][l], mod, l, 3, BF16)
    act = _mm(h2, [w["ffn_w_gate"][i], w["ffn_w_up"][i]], tm=512, tn=1408, out_dtype=BF16, epilogue="swiglu")
    x = _mm(act, [w["ffn_w_down"][i]], tm=512, tn=512, out_dtype=F32, epilogue="residual",
            resid=x, mod=mod, layer=l, k_gate=5)

    k_new = _lanes_to_heads(kn[:T_CTX], BATCH, SEQ)
    v_new = _lanes_to_heads(p_main[:T_CTX, (NAT_QCOL + 2) * NAT_W:], BATCH, SEQ)
    return x, s_ctx, k_new, v_new


def _dft_consts(seq):
    n = 2 * seq
    k = np.arange(seq)[:, None]
    s = np.arange(seq)[None, :]
    ang = 2.0 * np.pi * ((k * s) % n) / n
    fr = np.cos(ang)
    fi = -np.sin(ang)
    fi[0, :] = np.cos(np.pi * (np.arange(seq) % 2))
    fm = np.concatenate([fr, fi], axis=0)

    def split(a):
        a32 = a.astype(np.float32)
        hi = a32.astype(BF16)
        lo = (a32 - hi.astype(np.float32)).astype(BF16)
        return jnp.asarray(hi), jnp.asarray(lo)

    cw = np.full((n, 1), 2.0 / n)
    cw[0, 0] = cw[seq, 0] = 1.0 / n
    sg = np.ones((n, 1))
    sg[seq + 1:, 0] = -1.0
    cs = np.zeros((n, 128), np.float32)
    cs[:, 0:1] = cw
    cs[:, 1:2] = cw * sg
    return split(fm), split(fm.T.copy()), jnp.asarray(cs)


def _dft_apply(m_hi, m_lo, x):
    xh, xl = _split_bf16(x)
    r = jnp.dot(m_hi, xh, preferred_element_type=F32)
    r = r + jnp.dot(m_hi, xl, preferred_element_type=F32)
    return r + jnp.dot(m_lo, xh, preferred_element_type=F32)


def _hy_filter_kernel(z_ref, w1_ref, b1_ref, w2_ref, b2_ref, fq_ref, w3_ref, t_ref, dl_ref,
                      fh_ref, fl_ref, cs_ref, o_ref):
    d = pl.program_id(1)
    fq = fq_ref[...]
    hh = jnp.sin(fq * (_dot(z_ref[...], w1_ref[...], prec="bf16x3") + b1_ref[...]))
    hh = jnp.sin(fq * (_dot(hh, w2_ref[...], prec="bf16x3") + b2_ref[...]))
    filt = _dot(hh, w3_ref[...], prec="bf16x3") * jnp.exp(-t_ref[...] * dl_ref[...])
    rows = lax.broadcasted_iota(jnp.int32, filt.shape, 0)
    filt = jnp.where((d == 1) & (rows == 0), 0.0, filt)
    spec = _dft_apply(fh_ref[...], fl_ref[...], filt)

    @pl.when(d == 0)
    def _():
        o_ref[...] = spec * cs_ref[:, 0:1]

    @pl.when(d == 1)
    def _():
        o_ref[...] = o_ref[...] + spec * cs_ref[:, 1:2]


def _hy_filter_spectrum(seq, consts, w1, b1, w2, b2, w3, freq):
    (f_hi, f_lo), _, cs = consts
    bands = (HY_EMB - 1) // 2
    t = np.linspace(0.0, 1.0, seq, dtype=np.float32)[:, None]
    wv = (np.float32(2.0 * math.pi / seq) * np.arange(seq, dtype=np.float32))[:, None]
    f = np.linspace(1e-4, bands - 1, bands, dtype=np.float32)[None, :]
    z = np.zeros((seq, 128), np.float32)
    z[:, 0:1] = t
    z[:, 1:1 + bands] = np.cos(f * wv)
    z[:, 1 + bands:HY_EMB] = -np.sin(f * wv)
    deltas = np.abs(np.linspace(math.log(HY_TARGET) / HY_FAST, math.log(HY_TARGET) / HY_SLOW, D,
                                dtype=np.float32))[None, :]
    w1p = jnp.zeros((128, HY_HID), F32).at[:HY_EMB].set(w1)
    tc = 256
    n = 2 * seq
    full = lambda shape: pl.BlockSpec(shape, lambda c, d: tuple(0 for _ in shape))
    return pl.pallas_call(
        _hy_filter_kernel,
        grid=(D // tc, 2),
        in_specs=[full((seq, 128)), full((128, HY_HID)), full((1, HY_HID)), full((HY_HID, HY_HID)),
                  full((1, HY_HID)), full((1, HY_HID)),
                  pl.BlockSpec((HY_HID, tc), lambda c, d: (0, d * (D // tc) + c)),
                  full((seq, 1)), pl.BlockSpec((1, tc), lambda c, d: (0, c)),
                  full((n, seq)), full((n, seq)), full((n, 128))],
        out_specs=pl.BlockSpec((n, tc), lambda c, d: (0, c)),
        out_shape=jax.ShapeDtypeStruct((n, D), F32),
        compiler_params=_cparams(("arbitrary", "arbitrary"), VMEM_LIMIT),
        name="hy_filter",
    )(jnp.asarray(z), w1p, b1.reshape(1, -1), w2, b2.reshape(1, -1), freq.reshape(1, -1), w3,
      jnp.asarray(t), jnp.asarray(deltas), f_hi, f_lo, cs)


def _hy_prep_kernel(p_ref, prev_ref, next_ref, cw_ref, u_ref, x0_ref):
    x = _conv3(p_ref[...], prev_ref, next_ref, cw_ref[...], None)
    x0_ref[...] = x[:, :D]
    u_ref[...] = x[:, 2 * D:] * x[:, D:2 * D]


def _hy_prep(p, conv_w):
    main, prev, nxt = _halo_specs(3 * D, 0, 0)
    out = pl.BlockSpec((ROW_TILE, D), lambda i: (i, 0))
    return pl.pallas_call(
        _hy_prep_kernel,
        grid=(T_ALL // ROW_TILE,),
        in_specs=[main, prev, nxt, pl.BlockSpec((3, 3 * D), lambda i: (0, 0))],
        out_specs=[out, out],
        out_shape=[jax.ShapeDtypeStruct((T_ALL, D), F32)] * 2,
        compiler_params=_cparams(("arbitrary",), VMEM_LIMIT),
        name="hy_prep",
    )(p, p, p, conv_w)


def _hy_conv_kernel(u_ref, x0_ref, kf_ref, bias_ref, fh_ref, fl_ref, th_ref, tl_ref, o_ref, *, seq):
    u = u_ref[...]
    xs = _dft_apply(fh_ref[...], fl_ref[...], u)
    kf = kf_ref[...]
    xr, xi = xs[:seq], xs[seq:]
    kr, ki = kf[:seq], kf[seq:]
    row0 = lax.broadcasted_iota(jnp.int32, xr.shape, 0) == 0
    xiki = xi * ki
    yr = xr * kr - jnp.where(row0, 0.0, xiki)
    yi = jnp.where(row0, xiki, xr * ki + xi * kr)
    y = _dft_apply(th_ref[...], tl_ref[...], jnp.concatenate([yr, yi], axis=0))
    o_ref[...] = ((y + u * bias_ref[...]) * x0_ref[...]).astype(o_ref.dtype)


def _hy_conv(u, x0, kf, bias, consts, seq, n_seq, seq_blk0, tc):
    (f_hi, f_lo), (t_hi, t_lo), _ = consts
    n = 2 * seq
    full = lambda shape: pl.BlockSpec(shape, lambda b, c: (0, 0))
    return pl.pallas_call(
        functools.partial(_hy_conv_kernel, seq=seq),
        grid=(n_seq, D // tc),
        in_specs=[pl.BlockSpec((seq, tc), lambda b, c: (seq_blk0 + b, c)),
                  pl.BlockSpec((seq, tc), lambda b, c: (seq_blk0 + b, c)),
                  pl.BlockSpec((n, tc), lambda b, c: (0, c)),
                  pl.BlockSpec((1, tc), lambda b, c: (0, c)),
                  full((n, seq)), full((n, seq)), full((seq, n)), full((seq, n))],
        out_specs=pl.BlockSpec((seq, tc), lambda b, c: (b, c)),
        out_shape=jax.ShapeDtypeStruct((n_seq * seq, D), BF16),
        compiler_params=_cparams(("arbitrary", "arbitrary"), VMEM_LIMIT),
        name="hy_conv",
    )(u, x0, kf, bias.reshape(1, D), f_hi, f_lo, t_hi, t_lo)


MOE_TM = 256
MOE_TILES = 2 * T_ALL // MOE_TM + N_EXP
MOE_ROWS = MOE_TILES * MOE_TM
ROUTE_TM = 512
DMA_WINDOW = 32


def _router_kernel(h_ref, rw_ref, rb_ref, tri_ref, ei_ref, gf_ref, cnt_ref, carry):
    @pl.when(pl.program_id(0) == 0)
    def _():
        carry[...] = jnp.zeros(carry.shape, F32)

    logits = _dot(h_ref[...], rw_ref[...], prec="bf16x3") + rb_ref[...]
    lane = lax.broadcasted_iota(jnp.int32, logits.shape, 1)
    logits = jnp.where(lane < N_EXP, logits, -jnp.inf)
    m1 = jnp.max(logits, axis=-1, keepdims=True)
    i1 = jnp.min(jnp.where(logits == m1, lane, 128), axis=-1, keepdims=True)
    rest = jnp.where(lane == i1, -jnp.inf, logits)
    m2 = jnp.max(rest, axis=-1, keepdims=True)
    i2 = jnp.min(jnp.where(rest == m2, lane, 128), axis=-1, keepdims=True)
    e = jnp.exp(m2 - m1)
    g1 = 1.0 / (1.0 + e)
    g2 = e * g1
    pick = jnp.where((lane == i1) | (lane == i2), 1.0, 0.0)
    before = carry[...] + jnp.dot(tri_ref[...], pick.astype(BF16), preferred_element_type=F32)
    r1 = jnp.sum(jnp.where(lane == i1, before, 0.0), axis=-1, keepdims=True)
    r2 = jnp.sum(jnp.where(lane == i2, before, 0.0), axis=-1, keepdims=True)
    carry[...] = carry[...] + jnp.sum(pick, axis=0, keepdims=True)
    ints = jnp.where(lane == 0, i1, jnp.where(lane == 1, i2, 0))
    ranks = jnp.where(lane == 2, r1, jnp.where(lane == 3, r2, 0.0))
    ei_ref[...] = ints + ranks.astype(jnp.int32)
    gf_ref[...] = jnp.where(lane == 0, g1, jnp.where(lane == 1, g2, 0.0))
    cnt_ref[...] = carry[...].astype(jnp.int32)


def _router(h, router_w, router_b):
    rw = jnp.zeros((D, 128), F32).at[:, :N_EXP].set(router_w)
    rb = jnp.zeros((1, 128), F32).at[0, :N_EXP].set(router_b)
    tri = jnp.asarray(np.tril(np.ones((ROUTE_TM, ROUTE_TM), np.float32), -1), BF16)
    row = pl.BlockSpec((ROUTE_TM, 128), lambda i: (i, 0))
    return pl.pallas_call(
        _router_kernel,
        grid=(T_ALL // ROUTE_TM,),
        in_specs=[pl.BlockSpec((ROUTE_TM, D), lambda i: (i, 0)),
                  pl.BlockSpec((D, 128), lambda i: (0, 0)),
                  pl.BlockSpec((1, 128), lambda i: (0, 0)),
                  pl.BlockSpec((ROUTE_TM, ROUTE_TM), lambda i: (0, 0))],
        out_specs=[row, row, pl.BlockSpec((1, 128), lambda i: (0, 0))],
        out_shape=[jax.ShapeDtypeStruct((T_ALL, 128), jnp.int32),
                   jax.ShapeDtypeStruct((T_ALL, 128), F32),
                   jax.ShapeDtypeStruct((1, 128), jnp.int32)],
        scratch_shapes=[pltpu.VMEM((1, 128), F32)],
        compiler_params=_cparams(("arbitrary",)),
        name="moe_router",
    )(h, rw, rb, tri)


def _row_copy_kernel(src_ref, dst_ref, *refs, n, aliased):
    if aliased:
        in_ref, _, out_ref, sem = refs
    else:
        in_ref, out_ref, sem = refs

    def copy(i):
        return pltpu.make_async_copy(in_ref.at[pl.ds(src_ref[i], 1)], out_ref.at[pl.ds(dst_ref[i], 1)], sem)

    def prime(i, c):
        copy(i).start()
        return c

    def steady(i, c):
        copy(i - DMA_WINDOW).wait()
        copy(i).start()
        return c

    def drain(i, c):
        copy(i).wait()
        return c

    lax.fori_loop(0, DMA_WINDOW, prime, 0)
    lax.fori_loop(DMA_WINDOW, n, steady, 0)
    lax.fori_loop(n - DMA_WINDOW, n, drain, 0)


def _row_copy(x, src, dst, out_rows, init=None):
    n = src.shape[0]
    any_spec = pl.BlockSpec(memory_space=pl.ANY)
    args = [src, dst, x]
    in_specs = [any_spec]
    aliases = {}
    if init is not None:
        args.append(init)
        in_specs.append(any_spec)
        aliases = {3: 0}
    return pl.pallas_call(
        functools.partial(_row_copy_kernel, n=n, aliased=init is not None),
        grid_spec=pltpu.PrefetchScalarGridSpec(
            num_scalar_prefetch=2, grid=(1,), in_specs=in_specs, out_specs=any_spec,
            scratch_shapes=[pltpu.SemaphoreType.DMA(())]),
        out_shape=jax.ShapeDtypeStruct((out_rows, x.shape[1]), x.dtype),
        input_output_aliases=aliases,
        compiler_params=_cparams(("arbitrary",)),
        name="row_copy",
    )(*args)


def _moe_up_kernel(te_ref, nu_ref, g_ref, wg_ref, wu_ref, o_ref, wbf):
    i = pl.program_id(1)
    new_w = (i == 0) | (te_ref[i] != te_ref[jnp.maximum(i - 1, 0)])

    @pl.when(new_w)
    def _():
        wbf[0] = wg_ref[0].astype(BF16)
        wbf[1] = wu_ref[0].astype(BF16)

    @pl.when(i < nu_ref[0])
    def _():
        h = g_ref[...].astype(BF16)
        a = jnp.dot(h, wbf[0], preferred_element_type=F32)
        b = jnp.dot(h, wbf[1], preferred_element_type=F32)
        o_ref[...] = (a * _sigmoid(a) * b).astype(o_ref.dtype)

    @pl.when(i >= nu_ref[0])
    def _():
        o_ref[...] = jnp.zeros(o_ref.shape, o_ref.dtype)


def _moe_up(tile_expert, n_used, rows, wg, wu, tn):
    return pl.pallas_call(
        _moe_up_kernel,
        grid_spec=pltpu.PrefetchScalarGridSpec(
            num_scalar_prefetch=2, grid=(D_FF_E // tn, MOE_TILES),
            in_specs=[pl.BlockSpec((MOE_TM, D), lambda j, i, te, nu: (i, 0)),
                      pl.BlockSpec((1, D, tn), lambda j, i, te, nu: (te[i], 0, j)),
                      pl.BlockSpec((1, D, tn), lambda j, i, te, nu: (te[i], 0, j))],
            out_specs=pl.BlockSpec((MOE_TM, tn), lambda j, i, te, nu: (i, j)),
            scratch_shapes=[pltpu.VMEM((2, D, tn), BF16)]),
        out_shape=jax.ShapeDtypeStruct((MOE_ROWS, D_FF_E), BF16),
        compiler_params=_cparams(("arbitrary", "arbitrary"), VMEM_LIMIT),
        name="moe_up",
    )(tile_expert, n_used, rows, wg, wu)


def _moe_down_kernel(te_ref, nu_ref, a_ref, wd_ref, o_ref, wbf):
    i = pl.program_id(1)
    new_w = (i == 0) | (te_ref[i] != te_ref[jnp.maximum(i - 1, 0)])

    @pl.when(new_w)
    def _():
        wbf[...] = wd_ref[0].astype(BF16)

    @pl.when(i < nu_ref[0])
    def _():
        o_ref[...] = jnp.dot(a_ref[...], wbf[...], preferred_element_type=F32)

    @pl.when(i >= nu_ref[0])
    def _():
        o_ref[...] = jnp.zeros(o_ref.shape, o_ref.dtype)


def _moe_down(tile_expert, n_used, act, wd, tn):
    return pl.pallas_call(
        _moe_down_kernel,
        grid_spec=pltpu.PrefetchScalarGridSpec(
            num_scalar_prefetch=2, grid=(D // tn, MOE_TILES),
            in_specs=[pl.BlockSpec((MOE_TM, D_FF_E), lambda j, i, te, nu: (i, 0)),
                      pl.BlockSpec((1, D_FF_E, tn), lambda j, i, te, nu: (te[i], 0, j))],
            out_specs=pl.BlockSpec((MOE_TM, tn), lambda j, i, te, nu: (i, j)),
            scratch_shapes=[pltpu.VMEM((D_FF_E, tn), BF16)]),
        out_shape=jax.ShapeDtypeStruct((MOE_ROWS, D), F32),
        compiler_params=_cparams(("arbitrary", "arbitrary"), VMEM_LIMIT),
        name="moe_down",
    )(tile_expert, n_used, act, wd)


def _moe_combine_kernel(x_ref, y1_ref, y2_ref, gf_ref, gate_ref, o_ref):
    gf = gf_ref[...]
    f = gf[:, 0:1] * y1_ref[...] + gf[:, 1:2] * y2_ref[...]
    o_ref[...] = x_ref[...] + gate_ref[0] * f


def _moe_combine(x, y12, gf, mod, layer):
    tm = 512
    nt = T_ALL // tm
    return pl.pallas_call(
        _moe_combine_kernel,
        grid=(nt,),
        in_specs=[pl.BlockSpec((tm, D), lambda i: (i, 0)),
                  pl.BlockSpec((tm, D), lambda i: (i, 0)),
                  pl.BlockSpec((tm, D), lambda i: (i + nt, 0)),
                  pl.BlockSpec((tm, 128), lambda i: (i, 0)),
                  _mod_spec(layer, 5, tm, lambda i: i)],
        out_specs=pl.BlockSpec((tm, D), lambda i: (i, 0)),
        out_shape=jax.ShapeDtypeStruct((T_ALL, D), F32),
        compiler_params=_cparams(("arbitrary",)),
        name="moe_combine",
    )(x, y12, y12, gf, mod)


def _moe(x, h, mod, layer, router_w, router_b, wg, wu, wd):
    ei, gf, cnt = _router(h, router_w, router_b)
    experts, ranks, counts = ei[:, 0:2], ei[:, 2:4], cnt[0, :N_EXP]
    tiles = (counts + MOE_TM - 1) // MOE_TM
    tile_end = jnp.cumsum(tiles)
    row0 = (tile_end - tiles) * MOE_TM
    pos = (row0[experts] + ranks).astype(jnp.int32)
    n_used = tile_end[-1:].astype(jnp.int32)
    t_idx = jnp.minimum(jnp.arange(MOE_TILES, dtype=jnp.int32), n_used[0] - 1)
    tile_expert = jnp.sum(t_idx[:, None] >= tile_end[None, :], axis=1).astype(jnp.int32)
    tok = jnp.arange(T_ALL, dtype=jnp.int32)
    flat_pos = pos.T.reshape(-1)
    flat_tok = jnp.concatenate([tok, tok])
    sorted_rows = _row_copy(h, flat_tok, flat_pos, MOE_ROWS, init=jnp.zeros((MOE_ROWS, D), F32))
    act = _moe_up(tile_expert, n_used, sorted_rows, wg, wu, 1792)
    y = _moe_down(tile_expert, n_used, act, wd, 1024)
    y12 = _row_copy(y, flat_pos, jnp.arange(2 * T_ALL, dtype=jnp.int32), 2 * T_ALL)
    return _moe_combine(x, y12, gf, mod, layer)


def _odd_mixer_layer(x, mod, l, i, w, spectra, consts):
    h = _modulate(x, w["norm_mix_g"][l], mod, l, 0, BF16)
    p = _mm(h, [w["odd_w_in"][i]], tm=1024, tn=512, out_dtype=F32)
    u, x0 = _hy_prep(p, w["hy_conv_w"][i])
    z_c = _hy_conv(u, x0, spectra[SEQ][i], w["hy_bias"][i], consts[SEQ], SEQ, BATCH, 0, D)
    z_l = _hy_conv(u, x0, spectra[DEC_SEQ][i], w["hy_bias"][i], consts[DEC_SEQ], DEC_SEQ, DEC_BATCH,
                   T_CTX // DEC_SEQ, 256)
    z = jnp.concatenate([z_c, z_l], axis=0)
    x = _mm(z, [w["odd_w_out"][i]], tm=1024, tn=512, out_dtype=F32, epilogue="residual",
            resid=x, mod=mod, layer=l, k_gate=2)
    h2 = _modulate(x, w["norm_ffn_g"][l], mod, l, 3, F32)
    return _moe(x, h2, mod, l, w["moe_router_w"][i], w["moe_router_b"][i],
                w["moe_w_gate"][i], w["moe_w_up"][i], w["moe_w_down"][i])


def kernel(x_prompt, x_sample, state_gdn, cache_nat_k, cache_nat_v, c, c_ctx, ada_w, ada_b, norm_mix_g, norm_ffn_g, even_w_in, gdn_conv_w, gdn_a_log, gdn_dt_bias, gdn_norm_g, nat_q_norm_g, nat_k_norm_g, nat_rpb, even_w_out, ffn_w_gate, ffn_w_up, ffn_w_down, odd_w_in, hy_conv_w, hy_w1, hy_b1, hy_w2, hy_b2, hy_w3, hy_freq, hy_bias, odd_w_out, moe_router_w, moe_router_b, moe_w_gate, moe_w_up, moe_w_down):
    w = dict(state_gdn=state_gdn, cache_nat_k=cache_nat_k, cache_nat_v=cache_nat_v,
             norm_mix_g=norm_mix_g, norm_ffn_g=norm_ffn_g, even_w_in=even_w_in, gdn_conv_w=gdn_conv_w,
             gdn_a_log=gdn_a_log, gdn_dt_bias=gdn_dt_bias, gdn_norm_g=gdn_norm_g,
             nat_q_norm_g=nat_q_norm_g, nat_k_norm_g=nat_k_norm_g, nat_rpb=nat_rpb, even_w_out=even_w_out,
             ffn_w_gate=ffn_w_gate, ffn_w_up=ffn_w_up, ffn_w_down=ffn_w_down, odd_w_in=odd_w_in,
             hy_conv_w=hy_conv_w, hy_bias=hy_bias, odd_w_out=odd_w_out, moe_router_w=moe_router_w,
             moe_router_b=moe_router_b, moe_w_gate=moe_w_gate, moe_w_up=moe_w_up, moe_w_down=moe_w_down)
    assert SEQ == ROW_TILE and DEC_SEQ % ROW_TILE == 0
    mod = _ada_all(c, c_ctx, ada_w, ada_b)
    consts = {s: _dft_consts(s) for s in (SEQ, DEC_SEQ)}
    n_odd = DEPTH // 2
    spectra = {s: [_hy_filter_spectrum(s, consts[s], hy_w1[i], hy_b1[i], hy_w2[i], hy_b2[i], hy_w3[i], hy_freq[i])
                   for i in range(n_odd)] for s in (SEQ, DEC_SEQ)}
    x = jnp.concatenate([x_prompt.reshape(T_CTX, D), x_sample.reshape(T_LAT, D)], axis=0)
    states, ks, vs = [], [], []
    for l in range(DEPTH):
        i = l // 2
        if l % 2 == 0:
            x, s_ctx, k_new, v_new = _even_mixer_layer(x, mod, l, i, w)
            states.append(s_ctx)
            ks.append(k_new)
            vs.append(v_new)
        else:
            x = _odd_mixer_layer(x, mod, l, i, w, spectra, consts)
    y_prompt = x[:T_CTX].reshape(BATCH, SEQ, D)
    y_sample = x[T_CTX:].reshape(DEC_BATCH, DEC_SEQ, D)
    return (y_prompt, y_sample, jnp.stack(states, axis=1), jnp.stack(ks, axis=1), jnp.stack(vs, axis=1))
```

```python
import functools
import math

import jax
import jax.numpy as jnp
import numpy as np
from jax import lax
from jax.experimental import pallas as pl
from jax.experimental.pallas import tpu as pltpu

F32 = jnp.float32
BF16 = jnp.bfloat16

D = 1024
BATCH = 16
SEQ = 256
DEPTH = 4
DEC_BATCH = 2
DEC_SEQ = 1024
PAST_LEN = 512
GRID_W = 64
EPS = 1e-6
H_A = 4
DK_A = 128
DV_A = 128
CHUNK = 64
H_B = 8
DH_B = 64
WIN_R = 8
WIN_C = 16
A_QKV = 2 * H_A * DK_A + H_A * DV_A
A_GATE = H_A * DV_A
B_QKV = 3 * H_B * DH_B
HY_EMB = 33
HY_HID = 64
HY_FAST = 0.3
HY_SLOW = 1.5
HY_TARGET = 1e-2
D_FF = 2816
N_EXP = 8
D_FF_E = 3584

T_CTX = BATCH * SEQ
T_LAT = DEC_BATCH * DEC_SEQ
T_ALL = T_CTX + T_LAT
N_SEG = 8
ROW_TILE = 256

VMEM_LIMIT = 56 * 1024 * 1024


def _cparams(sem, vmem=None):
    return pltpu.CompilerParams(dimension_semantics=sem, vmem_limit_bytes=vmem)


def _sigmoid(x):
    return 1.0 / (1.0 + jnp.exp(-x))


def _seg_of_row(row):
    return jnp.where(row < T_CTX, 0, 1 + (row - T_CTX) // DEC_SEQ)


def _split_bf16(x):
    hi = x.astype(BF16)
    lo = (x - hi.astype(F32)).astype(BF16)
    return hi, lo


def _dot(a, b, dims=(((1,), (0,)), ((), ())), prec="bf16"):
    if prec == "bf16":
        return lax.dot_general(a.astype(BF16), b.astype(BF16), dims, preferred_element_type=F32)
    ah, al = _split_bf16(a.astype(F32))
    bh, bl = _split_bf16(b.astype(F32))
    r = lax.dot_general(ah, bh, dims, preferred_element_type=F32)
    r = r + lax.dot_general(ah, bl, dims, preferred_element_type=F32)
    r = r + lax.dot_general(al, bh, dims, preferred_element_type=F32)
    return r


NT_DIMS = (((1,), (1,)), ((), ()))
TN_DIMS = (((0,), (0,)), ((), ()))


def _ada_kernel(cv_ref, w_ref, b_ref, o_ref):
    cv = cv_ref[...]
    s = cv * _sigmoid(cv)
    o_ref[0] = _dot(s, w_ref[0]) + b_ref[0]


def _ada_all(c, c_ctx, ada_w, ada_b):
    cv = jnp.zeros((N_SEG, D), F32).at[0].set(c_ctx).at[1:1 + DEC_BATCH].set(c)
    tn = 1536
    out = pl.pallas_call(
        _ada_kernel,
        grid=(DEPTH, 6 * D // tn),
        in_specs=[
            pl.BlockSpec((N_SEG, D), lambda l, j: (0, 0)),
            pl.BlockSpec((1, D, tn), lambda l, j: (l, 0, j)),
            pl.BlockSpec((1, 1, tn), lambda l, j: (l, 0, j)),
        ],
        out_specs=pl.BlockSpec((1, N_SEG, tn), lambda l, j: (l, 0, j)),
        out_shape=jax.ShapeDtypeStruct((DEPTH, N_SEG, 6 * D), F32),
        compiler_params=_cparams(("arbitrary", "arbitrary"), VMEM_LIMIT),
        name="ada",
    )(cv, ada_w, ada_b.reshape(DEPTH, 1, 6 * D))
    return out.reshape(DEPTH, N_SEG, 6, D).transpose(0, 2, 1, 3).reshape(DEPTH * 6 * N_SEG, 1, D)


def _mod_spec(layer, k, tm, row_of_step):
    base = (layer * 6 + k) * N_SEG

    def imap(*ids):
        return (base + _seg_of_row(row_of_step(*ids) * tm), 0, 0)

    return pl.BlockSpec((1, 1, D), imap)


def _modulate_math(x, g, scale, shift):
    ms = jnp.mean(x * x, axis=-1, keepdims=True)
    y = x * lax.rsqrt(ms + EPS) * g
    return y * (1.0 + scale) + shift


def _modulate_kernel(x_ref, g_ref, sc_ref, sh_ref, o_ref):
    o_ref[...] = _modulate_math(x_ref[...], g_ref[...], sc_ref[0], sh_ref[0]).astype(o_ref.dtype)


def _modulate(x, g, mod, layer, k_shift, out_dtype):
    tm = 512
    return pl.pallas_call(
        _modulate_kernel,
        grid=(T_ALL // tm,),
        in_specs=[
            pl.BlockSpec((tm, D), lambda i: (i, 0)),
            pl.BlockSpec((1, D), lambda i: (0, 0)),
            _mod_spec(layer, k_shift + 1, tm, lambda i: i),
            _mod_spec(layer, k_shift, tm, lambda i: i),
        ],
        out_specs=pl.BlockSpec((tm, D), lambda i: (i, 0)),
        out_shape=jax.ShapeDtypeStruct((T_ALL, D), out_dtype),
        compiler_params=_cparams(("arbitrary",)),
        name="modulate",
    )(x, g.reshape(1, D), mod, mod)


def _mm_kernel(*refs, n_w, epilogue):
    h_ref = refs[0]
    w_refs = refs[1:1 + n_w]
    rest = refs[1 + n_w:]
    if epilogue == "residual":
        x_ref, gate_ref, o_ref, wbf = rest
    else:
        o_ref, wbf = rest

    @pl.when(pl.program_id(1) == 0)
    def _():
        for k in range(n_w):
            wbf[k] = w_refs[k][0].astype(BF16)

    h = h_ref[...].astype(BF16)
    a = jnp.dot(h, wbf[0], preferred_element_type=F32)
    if epilogue == "swiglu":
        b = jnp.dot(h, wbf[1], preferred_element_type=F32)
        r = a * _sigmoid(a) * b
    elif epilogue == "residual":
        r = x_ref[...] + gate_ref[0] * a
    else:
        r = a
    o_ref[...] = r.astype(o_ref.dtype)


def _mm(h, ws, *, tm, tn, out_dtype, epilogue="none", resid=None, mod=None, layer=None, k_gate=None, wi=0):
    m, kdim = h.shape
    n = ws[0].shape[2]
    n_w = len(ws)
    in_specs = [pl.BlockSpec((tm, kdim), lambda j, i: (i, 0))]
    in_specs += [pl.BlockSpec((1, kdim, tn), lambda j, i: (wi, 0, j)) for _ in ws]
    args = [h, *ws]
    if epilogue == "residual":
        base = (layer * 6 + k_gate) * N_SEG
        in_specs.append(pl.BlockSpec((tm, tn), lambda j, i: (i, j)))
        in_specs.append(pl.BlockSpec((1, 1, tn), lambda j, i: (base + _seg_of_row(i * tm), 0, j)))
        args += [resid, mod]
    return pl.pallas_call(
        functools.partial(_mm_kernel, n_w=n_w, epilogue=epilogue),
        grid=(n // tn, m // tm),
        in_specs=in_specs,
        out_specs=pl.BlockSpec((tm, tn), lambda j, i: (i, j)),
        out_shape=jax.ShapeDtypeStruct((m, n), out_dtype),
        scratch_shapes=[pltpu.VMEM((n_w, kdim, tn), BF16)],
        compiler_params=_cparams(("arbitrary", "arbitrary"), VMEM_LIMIT),
        name="mm_" + epilogue,
    )(*args)


def _conv3(x, prev_ref, next_ref, w, seq_len):
    i = pl.program_id(0)
    rows_n = x.shape[0]
    if seq_len is None:
        j = jnp.maximum(i - T_CTX // rows_n, 0)
        per = DEC_SEQ // rows_n
        is_ctx = i < T_CTX // rows_n
        first = is_ctx | (j % per == 0)
        last = is_ctx | (j % per == per - 1)
    else:
        first = (i * rows_n) % seq_len == 0
        last = ((i + 1) * rows_n) % seq_len == 0
    prev_row = jnp.where(first, 0.0, prev_ref[7:8, :])
    next_row = jnp.where(last, 0.0, next_ref[0:1, :])
    rows = lax.broadcasted_iota(jnp.int32, x.shape, 0)
    xm = jnp.where(rows == 0, prev_row, pltpu.roll(x, 1, 0))
    xp = jnp.where(rows == rows_n - 1, next_row, pltpu.roll(x, rows_n - 1, 0))
    return xm * w[0:1, :] + x * w[1:2, :] + xp * w[2:3, :]


def _halo_specs(width, col_block, row_off_tiles):
    per = ROW_TILE // 8
    last_blk = T_ALL // 8 - 1
    main = pl.BlockSpec((ROW_TILE, width), lambda i: (i + row_off_tiles, col_block))
    prev = pl.BlockSpec((8, width), lambda i: (jnp.maximum((i + row_off_tiles) * per - 1, 0), col_block))
    nxt = pl.BlockSpec((8, width), lambda i: (jnp.minimum((i + row_off_tiles + 1) * per, last_blk), col_block))
    return main, prev, nxt


def _cumsum_rows(x, reverse):
    n = x.shape[0]
    rows = lax.broadcasted_iota(jnp.int32, x.shape, 0)
    k = 1
    while k < n:
        if reverse:
            x = x + jnp.where(rows < n - k, pltpu.roll(x, n - k, 0), 0.0)
        else:
            x = x + jnp.where(rows >= k, pltpu.roll(x, k, 0), 0.0)
        k *= 2
    return x


GDN_PREC = "bf16x3"


def _gdn_prep_kernel(pm_ref, prev_ref, next_ref, ps_ref, cw_ref, par_ref, qkv_ref, bg_ref, *, seq_len):
    x = _conv3(pm_ref[...], prev_ref, next_ref, cw_ref[...], seq_len)
    x = x * _sigmoid(x)
    for h in range(H_A):
        sl = slice(h * DK_A, (h + 1) * DK_A)
        qh = x[:, sl]
        qkv_ref[:, sl] = qh * lax.rsqrt(jnp.sum(qh * qh, axis=-1, keepdims=True) + EPS) * (DK_A ** -0.5)
        sl = slice(H_A * DK_A + h * DK_A, H_A * DK_A + (h + 1) * DK_A)
        kh = x[:, sl]
        qkv_ref[:, sl] = kh * lax.rsqrt(jnp.sum(kh * kh, axis=-1, keepdims=True) + EPS)
    qkv_ref[:, 2 * H_A * DK_A:] = x[:, 2 * H_A * DK_A:]
    raw = ps_ref[...]
    lane = lax.broadcasted_iota(jnp.int32, raw.shape, 1)
    beta = _sigmoid(raw)
    z = raw + par_ref[1:2, :]
    softplus = jnp.maximum(z, 0.0) + jnp.log(1.0 + jnp.exp(-jnp.abs(z)))
    g = -jnp.exp(par_ref[0:1, :]) * softplus
    bg_ref[...] = jnp.where(lane < 2 * H_A, beta, jnp.where(lane < 4 * H_A, g, 0.0))


def _gdn_prep(p_main, p_small, conv_w, a_log, dt_bias, row_off_tiles, n_rows, seq_len):
    par = jnp.zeros((2, 128), F32)
    par = par.at[0, 2 * H_A:4 * H_A].set(a_log.reshape(-1)).at[1, 2 * H_A:4 * H_A].set(dt_bias.reshape(-1))
    main, prev, nxt = _halo_specs(A_QKV, 0, row_off_tiles)
    return pl.pallas_call(
        functools.partial(_gdn_prep_kernel, seq_len=seq_len),
        grid=(n_rows // ROW_TILE,),
        in_specs=[main, prev, nxt,
                  pl.BlockSpec((ROW_TILE, 128), lambda i: (i + row_off_tiles, 0)),
                  pl.BlockSpec((3, A_QKV), lambda i: (0, 0)),
                  pl.BlockSpec((2, 128), lambda i: (0, 0))],
        out_specs=[pl.BlockSpec((ROW_TILE, A_QKV), lambda i: (i, 0)),
                   pl.BlockSpec((ROW_TILE, 128), lambda i: (i, 0))],
        out_shape=[jax.ShapeDtypeStruct((n_rows, A_QKV), F32),
                   jax.ShapeDtypeStruct((n_rows, 128), F32)],
        compiler_params=_cparams(("arbitrary",)),
        name="gdn_prep",
    )(p_main, p_main, p_main, p_small, conv_w, par)


def _gdn_chunk(qkv, bg, state, d, h):
    c = CHUNK
    qh = qkv[:, h * DK_A:(h + 1) * DK_A]
    kh = qkv[:, H_A * DK_A + h * DK_A:H_A * DK_A + (h + 1) * DK_A]
    vh = qkv[:, 2 * H_A * DK_A + h * DV_A:2 * H_A * DK_A + (h + 1) * DV_A]
    col = d * H_A + h
    beta = bg[:, col:col + 1]
    gcum = _cumsum_rows(bg[:, 2 * H_A:4 * H_A], reverse=(d == 1))
    gc = gcum[:, col:col + 1]
    ri = lax.broadcasted_iota(jnp.int32, (c, c), 0)
    ci = lax.broadcasted_iota(jnp.int32, (c, c), 1)
    eye = ri == ci
    incl = (ri >= ci) if d == 0 else (ri <= ci)
    strict = (ri > ci) if d == 0 else (ri < ci)
    gc_row = jnp.sum(jnp.where(eye, gc, 0.0), axis=0, keepdims=True)
    decay = jnp.where(incl, jnp.exp(jnp.where(incl, gc - gc_row, 0.0)), 0.0)
    kb = kh * beta
    a_mat = jnp.where(strict, _dot(kb, kh, NT_DIMS, GDN_PREC) * decay, 0.0)
    t_inv = jnp.where(eye, 1.0, 0.0) - a_mat
    pw = a_mat
    for _ in range(5):
        pw = _dot(pw, pw, prec=GDN_PREC)
        t_inv = t_inv + _dot(t_inv, pw, prec=GDN_PREC)
    eg = jnp.exp(gc)
    rhs = jnp.concatenate([vh * beta, kb * eg], axis=1)
    sol = _dot(t_inv, rhs, prec=GDN_PREC)
    ub, wm = sol[:, :DV_A], sol[:, DV_A:]
    attn = jnp.where(incl, _dot(qh, kh, NT_DIMS, GDN_PREC) * decay, 0.0)
    g_end = gc[c - 1:c, :] if d == 0 else gc[0:1, :]
    qd = qh * eg
    kd = kh * jnp.exp(g_end - gc)
    u = ub - _dot(wm, state, prec=GDN_PREC)
    o = _dot(qd, state, prec=GDN_PREC) + _dot(attn, u, prec=GDN_PREC)
    new_state = state * jnp.exp(g_end) + _dot(kd, u, TN_DIMS, GDN_PREC)
    return o, new_state


def _gdn_scan_kernel(*refs, ns, has_s0):
    if has_s0:
        qf_ref, qb_ref, bf_ref, bb_ref, s0_ref, of_ref, ob_ref, sfin_ref, st = refs
    else:
        qf_ref, qb_ref, bf_ref, bb_ref, of_ref, ob_ref, sfin_ref, st = refs
    step = pl.program_id(1)

    @pl.when(step == 0)
    def _():
        if has_s0:
            st[...] = s0_ref[:, 0]
        else:
            st[...] = jnp.zeros(st.shape, F32)

    def per_seq(s, carry):
        for d in range(2):
            qkv = (qf_ref if d == 0 else qb_ref)[s, 0]
            bg = (bf_ref if d == 0 else bb_ref)[s, 0]
            o_ref = of_ref if d == 0 else ob_ref
            for h in range(H_A):
                o, new_state = _gdn_chunk(qkv, bg, st[s, d, h], d, h)
                o_ref[s, 0, :, h * DV_A:(h + 1) * DV_A] = o
                st[s, d, h] = new_state
        return carry

    lax.fori_loop(0, ns, per_seq, 0)

    @pl.when(step == pl.num_programs(1) - 1)
    def _():
        sfin_ref[...] = st[...]


def _gdn_scan(qkv, bg, s0, layer_i, n_seq, seq_len, ns):
    n = seq_len // CHUNK
    qkv4 = qkv.reshape(n_seq, n, CHUNK, A_QKV)
    bg4 = bg.reshape(n_seq, n, CHUNK, 128)
    fwd = lambda g, c: (g, c, 0, 0)
    bwd = lambda g, c: (g, n - 1 - c, 0, 0)
    in_specs = [pl.BlockSpec((ns, 1, CHUNK, A_QKV), fwd), pl.BlockSpec((ns, 1, CHUNK, A_QKV), bwd),
                pl.BlockSpec((ns, 1, CHUNK, 128), fwd), pl.BlockSpec((ns, 1, CHUNK, 128), bwd)]
    args = [qkv4, qkv4, bg4, bg4]
    if s0 is not None:
        in_specs.append(pl.BlockSpec((ns, 1, 2, H_A, DK_A, DV_A), lambda g, c: (g, layer_i, 0, 0, 0, 0)))
        args.append(s0)
    o_shape = jax.ShapeDtypeStruct((n_seq, n, CHUNK, H_A * DV_A), F32)
    of, ob, sfin = pl.pallas_call(
        functools.partial(_gdn_scan_kernel, ns=ns, has_s0=s0 is not None),
        grid=(n_seq // ns, n),
        in_specs=in_specs,
        out_specs=[pl.BlockSpec((ns, 1, CHUNK, H_A * DV_A), fwd),
                   pl.BlockSpec((ns, 1, CHUNK, H_A * DV_A), bwd),
                   pl.BlockSpec((ns, 2, H_A, DK_A, DV_A), lambda g, c: (g, 0, 0, 0, 0))],
        out_shape=[o_shape, o_shape, jax.ShapeDtypeStruct((n_seq, 2, H_A, DK_A, DV_A), F32)],
        scratch_shapes=[pltpu.VMEM((ns, 2, H_A, DK_A, DV_A), F32)],
        compiler_params=_cparams(("arbitrary", "arbitrary")),
        name="gdn_scan",
    )(*args)
    rows = n_seq * seq_len
    return of.reshape(rows, H_A * DV_A), ob.reshape(rows, H_A * DV_A), sfin


B_NN = (((2,), (1,)), ((0,), (0,)))
B_NT = (((2,), (2,)), ((0,), (0,)))
B_TN = (((1,), (1,)), ((0,), (0,)))


GDN_PRECISION = dict(qk="bf16", pw="bf16", sol="bf16", su="bf16", so="bf16", sk="bf16")


def _bmm(a, b, dims=B_NN, key=None):
    return _dot(a, b, dims, GDN_PRECISION[key] if key else "bf16")


def _gdn_scanb_kernel(*refs, ns, has_s0):
    if has_s0:
        qf_ref, qb_ref, bf_ref, bb_ref, s0_ref, of_ref, ob_ref, sfin_ref, st = refs
    else:
        qf_ref, qb_ref, bf_ref, bb_ref, of_ref, ob_ref, sfin_ref, st = refs
    step = pl.program_id(1)
    c = CHUNK
    nb = 2 * ns * H_A

    @pl.when(step == 0)
    def _():
        for d in range(2):
            st[d] = s0_ref[:, 0, d] if has_s0 else jnp.zeros(st.shape[1:], F32)

    q_l, k_l, v_l, beta_l, gc_l = [], [], [], [], []
    for d in range(2):
        for s in range(ns):
            qkv = (qf_ref if d == 0 else qb_ref)[s, 0]
            bg = (bf_ref if d == 0 else bb_ref)[s, 0]
            gcum = _cumsum_rows(bg, reverse=(d == 1))
            for h in range(H_A):
                q_l.append(qkv[:, h * DK_A:(h + 1) * DK_A])
                k_l.append(qkv[:, (H_A + h) * DK_A:(H_A + h + 1) * DK_A])
                v_l.append(qkv[:, 2 * H_A * DK_A + h * DV_A:2 * H_A * DK_A + (h + 1) * DV_A])
                col = d * H_A + h
                beta_l.append(bg[:, col:col + 1])
                gc_l.append(gcum[:, 2 * H_A + col:2 * H_A + col + 1])
    q, k, v = jnp.stack(q_l), jnp.stack(k_l), jnp.stack(v_l)
    beta, gc = jnp.stack(beta_l), jnp.stack(gc_l)

    bi = lax.broadcasted_iota(jnp.int32, (nb, c, c), 0)
    ri = lax.broadcasted_iota(jnp.int32, (nb, c, c), 1)
    ci = lax.broadcasted_iota(jnp.int32, (nb, c, c), 2)
    fwd = bi < nb // 2
    eye = ri == ci
    incl = (fwd & (ri >= ci)) | (jnp.logical_not(fwd) & (ri <= ci))
    strict = incl & jnp.logical_not(eye)
    gc_row = jnp.sum(jnp.where(eye, gc, 0.0), axis=1, keepdims=True)
    decay = jnp.where(incl, jnp.exp(jnp.where(incl, gc - gc_row, 0.0)), 0.0)
    kb = k * beta
    a_mat = jnp.where(strict, _bmm(kb, k, B_NT, "qk") * decay, 0.0)
    blk = 8
    diag = (ri // blk) == (ci // blk)
    pw = jnp.where(diag, a_mat, 0.0)
    r_mat = -pw
    for _ in range(2):
        pw = _bmm(pw, pw, B_NN, "pw")
        r_mat = r_mat + pw + _bmm(r_mat, pw, B_NN, "pw")
    while blk < c:
        off = ((ri // (2 * blk)) == (ci // (2 * blk))) & ((ri // blk) != (ci // blk))
        e_mat = jnp.where(off, a_mat, 0.0)
        x_mat = e_mat + _bmm(r_mat, e_mat, B_NN, "pw")
        r_mat = r_mat - (x_mat + _bmm(x_mat, r_mat, B_NN, "pw"))
        blk *= 2
    eg = jnp.exp(gc)
    rhs = jnp.concatenate([v * beta, kb * eg], axis=2)
    sol = rhs + _bmm(r_mat, rhs, B_NN, "sol")
    ub, wm = sol[:, :, :DV_A], sol[:, :, DV_A:]
    attn = jnp.where(incl, _bmm(q, k, B_NT, "qk") * decay, 0.0)
    fwd1 = lax.broadcasted_iota(jnp.int32, (nb, 1, 1), 0) < nb // 2
    g_end = jnp.where(fwd1, gc[:, c - 1:c, :], gc[:, 0:1, :])
    qd = q * eg
    kd = k * jnp.exp(g_end - gc)
    state = st[...].reshape(nb, DK_A, DV_A)
    u = ub - _bmm(wm, state, B_NN, "su")
    o = _bmm(qd, state, B_NN, "so") + _bmm(attn, u, B_NN, "so")
    state = state * jnp.exp(g_end) + _bmm(kd, u, B_TN, "sk")
    st[...] = state.reshape(st.shape)
    for d in range(2):
        o_ref = of_ref if d == 0 else ob_ref
        for s in range(ns):
            for h in range(H_A):
                o_ref[s, 0, :, h * DV_A:(h + 1) * DV_A] = o[(d * ns + s) * H_A + h]

    @pl.when(step == pl.num_programs(1) - 1)
    def _():
        for d in range(2):
            sfin_ref[:, d] = st[d]


def _gdn_scanb(qkv, bg, s0, layer_i, n_seq, seq_len, ns):
    n = seq_len // CHUNK
    qkv4 = qkv.reshape(n_seq, n, CHUNK, A_QKV)
    bg4 = bg.reshape(n_seq, n, CHUNK, 128)
    fwd = lambda g, c: (g, c, 0, 0)
    bwd = lambda g, c: (g, n - 1 - c, 0, 0)
    in_specs = [pl.BlockSpec((ns, 1, CHUNK, A_QKV), fwd), pl.BlockSpec((ns, 1, CHUNK, A_QKV), bwd),
                pl.BlockSpec((ns, 1, CHUNK, 128), fwd), pl.BlockSpec((ns, 1, CHUNK, 128), bwd)]
    args = [qkv4, qkv4, bg4, bg4]
    if s0 is not None:
        in_specs.append(pl.BlockSpec((ns, 1, 2, H_A, DK_A, DV_A), lambda g, c: (g, layer_i, 0, 0, 0, 0)))
        args.append(s0)
    o_shape = jax.ShapeDtypeStruct((n_seq, n, CHUNK, H_A * DV_A), F32)
    of, ob, sfin = pl.pallas_call(
        functools.partial(_gdn_scanb_kernel, ns=ns, has_s0=s0 is not None),
        grid=(n_seq // ns, n),
        in_specs=in_specs,
        out_specs=[pl.BlockSpec((ns, 1, CHUNK, H_A * DV_A), fwd),
                   pl.BlockSpec((ns, 1, CHUNK, H_A * DV_A), bwd),
                   pl.BlockSpec((ns, 2, H_A, DK_A, DV_A), lambda g, c: (g, 0, 0, 0, 0))],
        out_shape=[o_shape, o_shape, jax.ShapeDtypeStruct((n_seq, 2, H_A, DK_A, DV_A), F32)],
        scratch_shapes=[pltpu.VMEM((2, ns, H_A, DK_A, DV_A), F32)],
        compiler_params=_cparams(("arbitrary", "arbitrary"), VMEM_LIMIT),
        name="gdn_scan",
    )(*args)
    rows = n_seq * seq_len
    return of.reshape(rows, H_A * DV_A), ob.reshape(rows, H_A * DV_A), sfin


def _gdn_post_kernel(of_ref, ob_ref, gate_ref, g_ref, o_ref):
    o = of_ref[...] + ob_ref[...]
    gate = gate_ref[...]
    for h in range(H_A):
        sl = slice(h * DV_A, (h + 1) * DV_A)
        oh = o[:, sl]
        y = oh * lax.rsqrt(jnp.mean(oh * oh, axis=-1, keepdims=True) + EPS) * g_ref[...]
        gh = gate[:, sl]
        o_ref[:, sl] = (y * (gh * _sigmoid(gh))).astype(o_ref.dtype)


def _gdn_post(of, ob, p_main, gdn_g, row_off_tiles, n_rows):
    gate_blk = A_QKV // A_GATE
    return pl.pallas_call(
        _gdn_post_kernel,
        grid=(n_rows // ROW_TILE,),
        in_specs=[pl.BlockSpec((ROW_TILE, A_GATE), lambda i: (i, 0)),
                  pl.BlockSpec((ROW_TILE, A_GATE), lambda i: (i, 0)),
                  pl.BlockSpec((ROW_TILE, A_GATE), lambda i: (i + row_off_tiles, gate_blk)),
                  pl.BlockSpec((1, DV_A), lambda i: (0, 0))],
        out_specs=pl.BlockSpec((ROW_TILE, A_GATE), lambda i: (i, 0)),
        out_shape=jax.ShapeDtypeStruct((n_rows, A_GATE), BF16),
        compiler_params=_cparams(("arbitrary",)),
        name="gdn_post",
    )(of, ob, p_main, gdn_g.reshape(1, DV_A))


NAT_W = H_B * DH_B
NAT_QCOL = (A_QKV + A_GATE) // NAT_W
NAT_SCALE = DH_B ** -0.5


def _nat_prep_kernel(q_ref, k_ref, bd_ref, gq_ref, gk_ref, qn_ref, kn_ref):
    bd = bd_ref[...]
    for x_ref, g_ref, o_ref in ((q_ref, gq_ref, qn_ref), (k_ref, gk_ref, kn_ref)):
        x = x_ref[...]
        hi, lo = _split_bf16(x * x)
        ms = jnp.dot(hi, bd, preferred_element_type=F32) + jnp.dot(lo, bd, preferred_element_type=F32)
        o_ref[...] = x * lax.rsqrt(ms + EPS) * g_ref[...]


def _nat_prep(p_main, qn_g, kn_g):
    grp = np.arange(NAT_W) // DH_B
    bd = jnp.asarray((grp[:, None] == grp[None, :]).astype(np.float32) / DH_B, BF16)
    spec = lambda cb: pl.BlockSpec((ROW_TILE, NAT_W), lambda i: (i, cb))
    return pl.pallas_call(
        _nat_prep_kernel,
        grid=(T_ALL // ROW_TILE,),
        in_specs=[spec(NAT_QCOL), spec(NAT_QCOL + 1),
                  pl.BlockSpec((NAT_W, NAT_W), lambda i: (0, 0)),
                  pl.BlockSpec((1, NAT_W), lambda i: (0, 0)),
                  pl.BlockSpec((1, NAT_W), lambda i: (0, 0))],
        out_specs=[spec(0), spec(0)],
        out_shape=[jax.ShapeDtypeStruct((T_ALL, NAT_W), F32)] * 2,
        compiler_params=_cparams(("arbitrary",)),
        name="nat_prep",
    )(p_main, p_main, bd, jnp.tile(qn_g, H_B).reshape(1, NAT_W), jnp.tile(kn_g, H_B).reshape(1, NAT_W))


def _pair_masks():
    lane = lax.broadcasted_iota(jnp.int32, (1, 2 * DH_B), 1)
    return lane < DH_B


def _nat_ctx_kernel(q_ref, k_ref, v_ref, o_ref):
    lo = _pair_masks()
    for p in range(H_B // 2):
        sl = slice(p * 2 * DH_B, (p + 1) * 2 * DH_B)
        q2, k2, v2 = q_ref[:, sl], k_ref[:, sl], v_ref[:, sl]
        halves = []
        for half in range(2):
            qm = jnp.where(lo if half == 0 else jnp.logical_not(lo), q2, 0.0)
            s = _dot(qm, k2, NT_DIMS) * NAT_SCALE
            e = jnp.exp(s - jnp.max(s, axis=-1, keepdims=True))
            pr = e / jnp.sum(e, axis=-1, keepdims=True)
            halves.append(_dot(pr, v2))
        o_ref[:, sl] = jnp.where(lo, halves[0], halves[1]).astype(o_ref.dtype)


def _nat_ctx(qn, kn, p_main):
    spec = lambda cb: pl.BlockSpec((SEQ, NAT_W), lambda b: (b, cb))
    return pl.pallas_call(
        _nat_ctx_kernel,
        grid=(BATCH,),
        in_specs=[spec(0), spec(0), spec(NAT_QCOL + 2)],
        out_specs=spec(0),
        out_shape=jax.ShapeDtypeStruct((T_CTX, NAT_W), BF16),
        compiler_params=_cparams(("arbitrary",)),
        name="nat_ctx",
    )(qn, kn, p_main)


def _nat_bias_kernel(r_ref, e_ref, ok_ref, o_ref):
    r = r_ref[...]
    hi, lo = _split_bf16(r)
    lo2 = (r - hi.astype(F32) - lo.astype(F32)).astype(BF16)
    e = e_ref[...]
    t = (jnp.dot(hi, e, preferred_element_type=F32) + jnp.dot(lo, e, preferred_element_type=F32)
         + jnp.dot(lo2, e, preferred_element_type=F32))
    o_ref[...] = jnp.where(ok_ref[...] > 0.5, t, -jnp.inf)


def _nat_bias(rpb):
    n_dr, n_dc = 2 * WIN_R - 1, 2 * WIN_C - 1
    qc = np.arange(GRID_W)[:, None]
    kc = np.arange(GRID_W)[None, :]
    dc = (kc - qc + WIN_C - 1).reshape(-1)
    c0 = np.clip(qc - WIN_C // 2, 0, GRID_W - WIN_C)
    ok = ((kc >= c0) & (kc < c0 + WIN_C)).reshape(1, -1).astype(np.float32)
    onehot = (np.arange(128)[:, None] == dc[None, :]).astype(np.float32)
    rows = H_B * n_dr
    rp = jnp.zeros((rows, 128), F32).at[:, :n_dc].set(rpb.reshape(rows, n_dc))
    tab = pl.pallas_call(
        _nat_bias_kernel,
        out_shape=jax.ShapeDtypeStruct((rows, GRID_W * GRID_W), F32),
        name="nat_bias",
    )(rp, jnp.asarray(onehot, BF16), jnp.asarray(ok))
    tab = tab.reshape(H_B, n_dr, GRID_W, GRID_W)
    win = jnp.stack([tab[:, d0:d0 + WIN_R] for d0 in range(WIN_R)], axis=1)
    return win.transpose(0, 1, 3, 2, 4).reshape(H_B, WIN_R, GRID_W, WIN_R * GRID_W)


def _nat_lat_kernel(q_ref, k_ref, v_ref, kc_ref, vc_ref, bias_ref, o_ref):
    r = pl.program_id(1)
    rows = DEC_SEQ // GRID_W
    r0 = jnp.clip(r - WIN_R // 2, 0, rows - WIN_R)
    start = pl.multiple_of(r0 * GRID_W, GRID_W)
    n_loc = WIN_R * GRID_W
    lo = _pair_masks()
    for p in range(H_B // 2):
        sl = slice(p * 2 * DH_B, (p + 1) * 2 * DH_B)
        q2 = q_ref[:, sl]
        kw = k_ref[0, pl.ds(start, n_loc), sl]
        vw = v_ref[pl.ds(start, n_loc), sl]
        kc, vc = kc_ref[0, :, sl], vc_ref[0, :, sl]
        halves = []
        for half in range(2):
            qm = jnp.where(lo if half == 0 else jnp.logical_not(lo), q2, 0.0)
            s_loc = _dot(qm, kw, NT_DIMS) * NAT_SCALE + bias_ref[2 * p + half, 0]
            s_ctx = _dot(qm, kc, NT_DIMS) * NAT_SCALE
            m = jnp.maximum(jnp.max(s_loc, axis=-1, keepdims=True), jnp.max(s_ctx, axis=-1, keepdims=True))
            e_loc, e_ctx = jnp.exp(s_loc - m), jnp.exp(s_ctx - m)
            inv = 1.0 / (jnp.sum(e_loc, axis=-1, keepdims=True) + jnp.sum(e_ctx, axis=-1, keepdims=True))
            halves.append(_dot(e_loc * inv, vw) + _dot(e_ctx * inv, vc))
        o_ref[:, sl] = jnp.where(lo, halves[0], halves[1]).astype(o_ref.dtype)


def _nat_lat(qn, kn, p_main, kc, vc, bias):
    rows = DEC_SEQ // GRID_W
    lat_tile0 = T_CTX // GRID_W
    lat_seq0 = T_CTX // DEC_SEQ

    def bias_map(b, r):
        r0 = jnp.clip(r - WIN_R // 2, 0, rows - WIN_R)
        return (0, WIN_R - 1 + r0 - r, 0, 0)

    return pl.pallas_call(
        _nat_lat_kernel,
        grid=(DEC_BATCH, rows),
        in_specs=[pl.BlockSpec((GRID_W, NAT_W), lambda b, r: (lat_tile0 + b * rows + r, 0)),
                  pl.BlockSpec((1, DEC_SEQ, NAT_W), lambda b, r: (lat_seq0 + b, 0, 0)),
                  pl.BlockSpec((DEC_SEQ, NAT_W), lambda b, r: (lat_seq0 + b, NAT_QCOL + 2)),
                  pl.BlockSpec((1, PAST_LEN, NAT_W), lambda b, r: (b, 0, 0)),
                  pl.BlockSpec((1, PAST_LEN, NAT_W), lambda b, r: (b, 0, 0)),
                  pl.BlockSpec((H_B, 1, GRID_W, WIN_R * GRID_W), bias_map)],
        out_specs=pl.BlockSpec((GRID_W, NAT_W), lambda b, r: (b * rows + r, 0)),
        out_shape=jax.ShapeDtypeStruct((T_LAT, NAT_W), BF16),
        compiler_params=_cparams(("arbitrary", "arbitrary"), VMEM_LIMIT),
        name="nat_lat",
    )(qn, kn.reshape(T_ALL // DEC_SEQ, DEC_SEQ, NAT_W), p_main, kc, vc, bias)


def _heads_to_lanes(cache):
    b, h, l, dh = cache.shape
    return cache.transpose(0, 2, 1, 3).reshape(b, l, h * dh)


def _lanes_to_heads(x, b, l):
    return x.reshape(b, l, H_B, DH_B).transpose(0, 2, 1, 3)


def _even_mixer_layer(x, mod, l, i, w):
    h = _modulate(x, w["norm_mix_g"][l], mod, l, 0, BF16)
    w_in = w["even_w_in"][i]
    c_beta = A_QKV + A_GATE
    c_qkvb = c_beta + 4 * H_A
    w_main = jnp.concatenate([w_in[:, :c_beta], w_in[:, c_qkvb:]], axis=1)
    w_small = jnp.zeros((D, 128), F32).at[:, :4 * H_A].set(w_in[:, c_beta:c_qkvb])
    p_main = _mm(h, [w_main[None]], tm=1024, tn=512, out_dtype=F32)
    p_small = _mm(h, [w_small[None]], tm=1024, tn=128, out_dtype=F32)

    conv_w, a_log, dt_bias = w["gdn_conv_w"][i], w["gdn_a_log"][i], w["gdn_dt_bias"][i]
    ctx_tiles = T_CTX // ROW_TILE
    qkv_c, bg_c = _gdn_prep(p_main, p_small, conv_w, a_log, dt_bias, 0, T_CTX, SEQ)
    qkv_l, bg_l = _gdn_prep(p_main, p_small, conv_w, a_log, dt_bias, ctx_tiles, T_LAT, DEC_SEQ)
    of_c, ob_c, s_ctx = _gdn_scanb(qkv_c, bg_c, None, i, BATCH, SEQ, 8)
    of_l, ob_l, _ = _gdn_scanb(qkv_l, bg_l, w["state_gdn"], i, DEC_BATCH, DEC_SEQ, 2)
    oa_c = _gdn_post(of_c, ob_c, p_main, w["gdn_norm_g"][i], 0, T_CTX)
    oa_l = _gdn_post(of_l, ob_l, p_main, w["gdn_norm_g"][i], ctx_tiles, T_LAT)

    qn, kn = _nat_prep(p_main, w["nat_q_norm_g"][i], w["nat_k_norm_g"][i])
    ob_c = _nat_ctx(qn, kn, p_main)
    bias = _nat_bias(w["nat_rpb"][i])
    kc = _heads_to_lanes(w["cache_nat_k"][:, i])
    vc = _heads_to_lanes(w["cache_nat_v"][:, i])
    ob_l = _nat_lat(qn, kn, p_main, kc, vc, bias)

    mix = jnp.concatenate([jnp.concatenate([oa_c, ob_c], axis=1),
                           jnp.concatenate([oa_l, ob_l], axis=1)], axis=0)
    x = _mm(mix, [w["even_w_out"]], tm=1024, tn=512, out_dtype=F32, epilogue="residual",
            resid=x, mod=mod, layer=l, k_gate=2, wi=i)

    h2 = _modulate(x, w["norm_ffn_g"][l], mod, l, 3, BF16)
    act = _mm(h2, [w["ffn_w_gate"], w["ffn_w_up"]], tm=512, tn=1408, out_dtype=BF16, epilogue="swiglu", wi=i)
    x = _mm(act, [w["ffn_w_down"]], tm=512, tn=512, out_dtype=F32, epilogue="residual",
            resid=x, mod=mod, layer=l, k_gate=5, wi=i)

    k_new = _lanes_to_heads(kn[:T_CTX], BATCH, SEQ)
    v_new = _lanes_to_heads(p_main[:T_CTX, (NAT_QCOL + 2) * NAT_W:], BATCH, SEQ)
    return x, s_ctx, k_new, v_new


def _dft_consts(seq):
    n = 2 * seq
    k = np.arange(seq)[:, None]
    s = np.arange(seq)[None, :]
    ang = 2.0 * np.pi * ((k * s) % n) / n
    fr = np.cos(ang)
    fi = -np.sin(ang)
    fi[0, :] = np.cos(np.pi * (np.arange(seq) % 2))
    fm = np.concatenate([fr, fi], axis=0)

    def split(a):
        a32 = a.astype(np.float32)
        hi = a32.astype(BF16)
        lo = (a32 - hi.astype(np.float32)).astype(BF16)
        return jnp.asarray(hi), jnp.asarray(lo)

    cw = np.full((n, 1), 2.0 / n)
    cw[0, 0] = cw[seq, 0] = 1.0 / n
    sg = np.ones((n, 1))
    sg[seq + 1:, 0] = -1.0
    cs = np.zeros((n, 128), np.float32)
    cs[:, 0:1] = cw
    cs[:, 1:2] = cw * sg
    return split(fm), split(fm.T.copy()), jnp.asarray(cs)


def _dft_apply(m_hi, m_lo, x):
    xh, xl = _split_bf16(x)
    r = jnp.dot(m_hi, xh, preferred_element_type=F32)
    r = r + jnp.dot(m_hi, xl, preferred_element_type=F32)
    return r + jnp.dot(m_lo, xh, preferred_element_type=F32)


def _hy_filter_kernel(z_ref, w1_ref, b1_ref, w2_ref, b2_ref, fq_ref, w3_ref, t_ref, dl_ref,
                      fh_ref, fl_ref, cs_ref, o_ref):
    d = pl.program_id(1)
    fq = fq_ref[...]
    hh = jnp.sin(fq * (_dot(z_ref[...], w1_ref[...], prec="bf16x3") + b1_ref[...]))
    hh = jnp.sin(fq * (_dot(hh, w2_ref[...], prec="bf16x3") + b2_ref[...]))
    filt = _dot(hh, w3_ref[...], prec="bf16x3") * jnp.exp(-t_ref[...] * dl_ref[...])
    rows = lax.broadcasted_iota(jnp.int32, filt.shape, 0)
    filt = jnp.where((d == 1) & (rows == 0), 0.0, filt)
    spec = _dft_apply(fh_ref[...], fl_ref[...], filt)

    @pl.when(d == 0)
    def _():
        o_ref[...] = spec * cs_ref[:, 0:1]

    @pl.when(d == 1)
    def _():
        o_ref[...] = o_ref[...] + spec * cs_ref[:, 1:2]


def _hy_filter_spectrum(seq, consts, w1, b1, w2, b2, w3, freq):
    (f_hi, f_lo), _, cs = consts
    bands = (HY_EMB - 1) // 2
    t = np.linspace(0.0, 1.0, seq, dtype=np.float32)[:, None]
    wv = (np.float32(2.0 * math.pi / seq) * np.arange(seq, dtype=np.float32))[:, None]
    f = np.linspace(1e-4, bands - 1, bands, dtype=np.float32)[None, :]
    z = np.zeros((seq, 128), np.float32)
    z[:, 0:1] = t
    z[:, 1:1 + bands] = np.cos(f * wv)
    z[:, 1 + bands:HY_EMB] = -np.sin(f * wv)
    deltas = np.abs(np.linspace(math.log(HY_TARGET) / HY_FAST, math.log(HY_TARGET) / HY_SLOW, D,
                                dtype=np.float32))[None, :]
    w1p = jnp.zeros((128, HY_HID), F32).at[:HY_EMB].set(w1)
    tc = 256
    n = 2 * seq
    full = lambda shape: pl.BlockSpec(shape, lambda c, d: tuple(0 for _ in shape))
    return pl.pallas_call(
        _hy_filter_kernel,
        grid=(D // tc, 2),
        in_specs=[full((seq, 128)), full((128, HY_HID)), full((1, HY_HID)), full((HY_HID, HY_HID)),
                  full((1, HY_HID)), full((1, HY_HID)),
                  pl.BlockSpec((HY_HID, tc), lambda c, d: (0, d * (D // tc) + c)),
                  full((seq, 1)), pl.BlockSpec((1, tc), lambda c, d: (0, c)),
                  full((n, seq)), full((n, seq)), full((n, 128))],
        out_specs=pl.BlockSpec((n, tc), lambda c, d: (0, c)),
        out_shape=jax.ShapeDtypeStruct((n, D), F32),
        compiler_params=_cparams(("arbitrary", "arbitrary"), VMEM_LIMIT),
        name="hy_filter",
    )(jnp.asarray(z), w1p, b1.reshape(1, -1), w2, b2.reshape(1, -1), freq.reshape(1, -1), w3,
      jnp.asarray(t), jnp.asarray(deltas), f_hi, f_lo, cs)


def _hy_prep_kernel(p_ref, prev_ref, next_ref, cw_ref, u_ref, x0_ref):
    x = _conv3(p_ref[...], prev_ref, next_ref, cw_ref[...], None)
    x0_ref[...] = x[:, :D]
    u_ref[...] = x[:, 2 * D:] * x[:, D:2 * D]


def _hy_prep(p, conv_w):
    main, prev, nxt = _halo_specs(3 * D, 0, 0)
    out = pl.BlockSpec((ROW_TILE, D), lambda i: (i, 0))
    return pl.pallas_call(
        _hy_prep_kernel,
        grid=(T_ALL // ROW_TILE,),
        in_specs=[main, prev, nxt, pl.BlockSpec((3, 3 * D), lambda i: (0, 0))],
        out_specs=[out, out],
        out_shape=[jax.ShapeDtypeStruct((T_ALL, D), F32)] * 2,
        compiler_params=_cparams(("arbitrary",), VMEM_LIMIT),
        name="hy_prep",
    )(p, p, p, conv_w)


def _hy_conv_kernel(u_ref, x0_ref, kf_ref, bias_ref, fh_ref, fl_ref, th_ref, tl_ref, o_ref, *, seq):
    u = u_ref[...]
    xs = _dft_apply(fh_ref[...], fl_ref[...], u)
    kf = kf_ref[...]
    xr, xi = xs[:seq], xs[seq:]
    kr, ki = kf[:seq], kf[seq:]
    row0 = lax.broadcasted_iota(jnp.int32, xr.shape, 0) == 0
    xiki = xi * ki
    yr = xr * kr - jnp.where(row0, 0.0, xiki)
    yi = jnp.where(row0, xiki, xr * ki + xi * kr)
    y = _dft_apply(th_ref[...], tl_ref[...], jnp.concatenate([yr, yi], axis=0))
    o_ref[...] = ((y + u * bias_ref[...]) * x0_ref[...]).astype(o_ref.dtype)


def _hy_conv(u, x0, kf, bias, consts, seq, n_seq, seq_blk0, tc):
    (f_hi, f_lo), (t_hi, t_lo), _ = consts
    n = 2 * seq
    full = lambda shape: pl.BlockSpec(shape, lambda b, c: (0, 0))
    return pl.pallas_call(
        functools.partial(_hy_conv_kernel, seq=seq),
        grid=(n_seq, D // tc),
        in_specs=[pl.BlockSpec((seq, tc), lambda b, c: (seq_blk0 + b, c)),
                  pl.BlockSpec((seq, tc), lambda b, c: (seq_blk0 + b, c)),
                  pl.BlockSpec((n, tc), lambda b, c: (0, c)),
                  pl.BlockSpec((1, tc), lambda b, c: (0, c)),
                  full((n, seq)), full((n, seq)), full((seq, n)), full((seq, n))],
        out_specs=pl.BlockSpec((seq, tc), lambda b, c: (b, c)),
        out_shape=jax.ShapeDtypeStruct((n_seq * seq, D), BF16),
        compiler_params=_cparams(("arbitrary", "arbitrary"), VMEM_LIMIT),
        name="hy_conv",
    )(u, x0, kf, bias.reshape(1, D), f_hi, f_lo, t_hi, t_lo)


MOE_TM = 256
MOE_TILES = 2 * T_ALL // MOE_TM + N_EXP
MOE_ROWS = MOE_TILES * MOE_TM
ROUTE_TM = 512
DISPATCH_TM = 512
COMBINE_TM = 256


def _router_kernel(h_ref, rw_ref, rb_ref, tri_ref, ei_ref, gf_ref, cnt_ref, carry):
    @pl.when(pl.program_id(0) == 0)
    def _():
        carry[...] = jnp.zeros(carry.shape, F32)

    logits = _dot(h_ref[...], rw_ref[...], prec="bf16x3") + rb_ref[...]
    lane = lax.broadcasted_iota(jnp.int32, logits.shape, 1)
    logits = jnp.where(lane < N_EXP, logits, -jnp.inf)
    m1 = jnp.max(logits, axis=-1, keepdims=True)
    i1 = jnp.min(jnp.where(logits == m1, lane, 128), axis=-1, keepdims=True)
    rest = jnp.where(lane == i1, -jnp.inf, logits)
    m2 = jnp.max(rest, axis=-1, keepdims=True)
    i2 = jnp.min(jnp.where(rest == m2, lane, 128), axis=-1, keepdims=True)
    e = jnp.exp(m2 - m1)
    g1 = 1.0 / (1.0 + e)
    g2 = e * g1
    pick = jnp.where((lane == i1) | (lane == i2), 1.0, 0.0)
    before = carry[...] + jnp.dot(tri_ref[...], pick.astype(BF16), preferred_element_type=F32)
    r1 = jnp.sum(jnp.where(lane == i1, before, 0.0), axis=-1, keepdims=True)
    r2 = jnp.sum(jnp.where(lane == i2, before, 0.0), axis=-1, keepdims=True)
    carry[...] = carry[...] + jnp.sum(pick, axis=0, keepdims=True)
    ints = jnp.where(lane == 0, i1, jnp.where(lane == 1, i2, 0))
    ranks = jnp.where(lane == 2, r1, jnp.where(lane == 3, r2, 0.0))
    ei_ref[...] = ints + ranks.astype(jnp.int32)
    gf_ref[...] = jnp.where(lane == 0, g1, jnp.where(lane == 1, g2, 0.0))
    cnt_ref[...] = carry[...].astype(jnp.int32)


def _router(h, router_w, router_b):
    rw = jnp.zeros((D, 128), F32).at[:, :N_EXP].set(router_w)
    rb = jnp.zeros((1, 128), F32).at[0, :N_EXP].set(router_b)
    tri = jnp.asarray(np.tril(np.ones((ROUTE_TM, ROUTE_TM), np.float32), -1), BF16)
    row = pl.BlockSpec((ROUTE_TM, 128), lambda i: (i, 0))
    return pl.pallas_call(
        _router_kernel,
        grid=(T_ALL // ROUTE_TM,),
        in_specs=[pl.BlockSpec((ROUTE_TM, D), lambda i: (i, 0)),
                  pl.BlockSpec((D, 128), lambda i: (0, 0)),
                  pl.BlockSpec((1, 128), lambda i: (0, 0)),
                  pl.BlockSpec((ROUTE_TM, ROUTE_TM), lambda i: (0, 0))],
        out_specs=[row, row, pl.BlockSpec((1, 128), lambda i: (0, 0))],
        out_shape=[jax.ShapeDtypeStruct((T_ALL, 128), jnp.int32),
                   jax.ShapeDtypeStruct((T_ALL, 128), F32),
                   jax.ShapeDtypeStruct((1, 128), jnp.int32)],
        scratch_shapes=[pltpu.VMEM((1, 128), F32)],
        compiler_params=_cparams(("arbitrary",)),
        name="moe_router",
    )(h, rw, rb, tri)


def _moe_dispatch_kernel(pos_ref, h_ref, init_ref, out_ref, sem):
    del init_ref
    base = pl.program_id(0) * DISPATCH_TM

    def copy(r, p):
        return pltpu.make_async_copy(h_ref.at[pl.ds(r, 1)], out_ref.at[pl.ds(p, 1)], sem)

    def start(r, c):
        copy(r, pos_ref[base + r]).start()
        copy(r, pos_ref[T_ALL + base + r]).start()
        return c

    def wait(r, c):
        copy(0, 0).wait()
        copy(0, 0).wait()
        return c

    lax.fori_loop(0, DISPATCH_TM, start, 0, unroll=8)
    lax.fori_loop(0, DISPATCH_TM, wait, 0, unroll=8)


def _moe_dispatch(h, flat_pos):
    any_spec = pl.BlockSpec(memory_space=pl.ANY)
    return pl.pallas_call(
        _moe_dispatch_kernel,
        grid_spec=pltpu.PrefetchScalarGridSpec(
            num_scalar_prefetch=1, grid=(T_ALL // DISPATCH_TM,),
            in_specs=[pl.BlockSpec((DISPATCH_TM, D), lambda i, pos: (i, 0)), any_spec],
            out_specs=any_spec,
            scratch_shapes=[pltpu.SemaphoreType.DMA(())]),
        out_shape=jax.ShapeDtypeStruct((MOE_ROWS, D), F32),
        input_output_aliases={2: 0},
        compiler_params=_cparams(("arbitrary",)),
        name="moe_dispatch",
    )(flat_pos, h, jnp.zeros((MOE_ROWS, D), F32))


def _moe_up_kernel(te_ref, nu_ref, g_ref, wg_ref, wu_ref, o_ref, wbf):
    i = pl.program_id(1)
    new_w = (i == 0) | (te_ref[i] != te_ref[jnp.maximum(i - 1, 0)])

    @pl.when(new_w)
    def _():
        wbf[0] = wg_ref[0, 0].astype(BF16)
        wbf[1] = wu_ref[0, 0].astype(BF16)

    @pl.when(i < nu_ref[0])
    def _():
        h = g_ref[...].astype(BF16)
        a = jnp.dot(h, wbf[0], preferred_element_type=F32)
        b = jnp.dot(h, wbf[1], preferred_element_type=F32)
        o_ref[...] = (a * _sigmoid(a) * b).astype(o_ref.dtype)

    @pl.when(i >= nu_ref[0])
    def _():
        o_ref[...] = jnp.zeros(o_ref.shape, o_ref.dtype)


def _moe_up(tile_expert, n_used, rows, wg, wu, li, tn):
    wspec = pl.BlockSpec((1, 1, D, tn), lambda j, i, te, nu: (li, te[i], 0, j))
    return pl.pallas_call(
        _moe_up_kernel,
        grid_spec=pltpu.PrefetchScalarGridSpec(
            num_scalar_prefetch=2, grid=(D_FF_E // tn, MOE_TILES),
            in_specs=[pl.BlockSpec((MOE_TM, D), lambda j, i, te, nu: (i, 0)), wspec, wspec],
            out_specs=pl.BlockSpec((MOE_TM, tn), lambda j, i, te, nu: (i, j)),
            scratch_shapes=[pltpu.VMEM((2, D, tn), BF16)]),
        out_shape=jax.ShapeDtypeStruct((MOE_ROWS, D_FF_E), BF16),
        compiler_params=_cparams(("arbitrary", "arbitrary"), VMEM_LIMIT),
        name="moe_up",
    )(tile_expert, n_used, rows, wg, wu)


def _moe_down_kernel(te_ref, nu_ref, a_ref, wd_ref, o_ref, wbf):
    i = pl.program_id(1)
    new_w = (i == 0) | (te_ref[i] != te_ref[jnp.maximum(i - 1, 0)])

    @pl.when(new_w)
    def _():
        wbf[...] = wd_ref[0, 0].astype(BF16)

    @pl.when(i < nu_ref[0])
    def _():
        o_ref[...] = jnp.dot(a_ref[...], wbf[...], preferred_element_type=F32)

    @pl.when(i >= nu_ref[0])
    def _():
        o_ref[...] = jnp.zeros(o_ref.shape, o_ref.dtype)


def _moe_down(tile_expert, n_used, act, wd, li, tn):
    return pl.pallas_call(
        _moe_down_kernel,
        grid_spec=pltpu.PrefetchScalarGridSpec(
            num_scalar_prefetch=2, grid=(D // tn, MOE_TILES),
            in_specs=[pl.BlockSpec((MOE_TM, D_FF_E), lambda j, i, te, nu: (i, 0)),
                      pl.BlockSpec((1, 1, D_FF_E, tn), lambda j, i, te, nu: (li, te[i], 0, j))],
            out_specs=pl.BlockSpec((MOE_TM, tn), lambda j, i, te, nu: (i, j)),
            scratch_shapes=[pltpu.VMEM((D_FF_E, tn), BF16)]),
        out_shape=jax.ShapeDtypeStruct((MOE_ROWS, D), F32),
        compiler_params=_cparams(("arbitrary", "arbitrary"), VMEM_LIMIT),
        name="moe_down",
    )(tile_expert, n_used, act, wd)


def _moe_combine_kernel(pos_ref, x_ref, y_ref, gf_ref, gate_ref, o_ref, ybuf, sem):
    base = pl.program_id(0) * COMBINE_TM

    def copy(slot, r, p):
        return pltpu.make_async_copy(y_ref.at[pl.ds(p, 1)], ybuf.at[slot, pl.ds(r, 1)], sem)

    def start(r, c):
        copy(0, r, pos_ref[base + r]).start()
        copy(1, r, pos_ref[T_ALL + base + r]).start()
        return c

    def wait(r, c):
        copy(0, 0, 0).wait()
        copy(1, 0, 0).wait()
        return c

    lax.fori_loop(0, COMBINE_TM, start, 0, unroll=8)
    lax.fori_loop(0, COMBINE_TM, wait, 0, unroll=8)
    gf = gf_ref[...]
    f = gf[:, 0:1] * ybuf[0] + gf[:, 1:2] * ybuf[1]
    o_ref[...] = x_ref[...] + gate_ref[0] * f


def _moe_combine(x, y, flat_pos, gf, mod, layer):
    tm = COMBINE_TM
    base = (layer * 6 + 5) * N_SEG
    return pl.pallas_call(
        _moe_combine_kernel,
        grid_spec=pltpu.PrefetchScalarGridSpec(
            num_scalar_prefetch=1, grid=(T_ALL // tm,),
            in_specs=[pl.BlockSpec((tm, D), lambda i, pos: (i, 0)),
                      pl.BlockSpec(memory_space=pl.ANY),
                      pl.BlockSpec((tm, 128), lambda i, pos: (i, 0)),
                      pl.BlockSpec((1, 1, D), lambda i, pos: (base + _seg_of_row(i * tm), 0, 0))],
            out_specs=pl.BlockSpec((tm, D), lambda i, pos: (i, 0)),
            scratch_shapes=[pltpu.VMEM((2, tm, D), F32), pltpu.SemaphoreType.DMA(())]),
        out_shape=jax.ShapeDtypeStruct((T_ALL, D), F32),
        compiler_params=_cparams(("arbitrary",)),
        name="moe_combine",
    )(flat_pos, x, y, gf, mod)


def _moe(x, h, mod, layer, li, w):
    ei, gf, cnt = _router(h, w["moe_router_w"][li], w["moe_router_b"][li])
    experts, ranks, counts = ei[:, 0:2], ei[:, 2:4], cnt[0, :N_EXP]
    tiles = (counts + MOE_TM - 1) // MOE_TM
    tile_end = jnp.cumsum(tiles)
    row0 = (tile_end - tiles) * MOE_TM
    pos = (row0[experts] + ranks).astype(jnp.int32)
    n_used = tile_end[-1:].astype(jnp.int32)
    t_idx = jnp.minimum(jnp.arange(MOE_TILES, dtype=jnp.int32), n_used[0] - 1)
    tile_expert = jnp.sum(t_idx[:, None] >= tile_end[None, :], axis=1).astype(jnp.int32)
    flat_pos = pos.T.reshape(-1)
    sorted_rows = _moe_dispatch(h, flat_pos)
    act = _moe_up(tile_expert, n_used, sorted_rows, w["moe_w_gate"], w["moe_w_up"], li, 1792)
    y = _moe_down(tile_expert, n_used, act, w["moe_w_down"], li, 1024)
    return _moe_combine(x, y, flat_pos, gf, mod, layer)


def _odd_mixer_layer(x, mod, l, i, w, spectra, consts):
    h = _modulate(x, w["norm_mix_g"][l], mod, l, 0, BF16)
    p = _mm(h, [w["odd_w_in"]], tm=1024, tn=512, out_dtype=F32, wi=i)
    u, x0 = _hy_prep(p, w["hy_conv_w"][i])
    z_c = _hy_conv(u, x0, spectra[SEQ][i], w["hy_bias"][i], consts[SEQ], SEQ, BATCH, 0, D)
    z_l = _hy_conv(u, x0, spectra[DEC_SEQ][i], w["hy_bias"][i], consts[DEC_SEQ], DEC_SEQ, DEC_BATCH,
                   T_CTX // DEC_SEQ, 256)
    z = jnp.concatenate([z_c, z_l], axis=0)
    x = _mm(z, [w["odd_w_out"]], tm=1024, tn=512, out_dtype=F32, epilogue="residual",
            resid=x, mod=mod, layer=l, k_gate=2, wi=i)
    h2 = _modulate(x, w["norm_ffn_g"][l], mod, l, 3, F32)
    return _moe(x, h2, mod, l, i, w)


def kernel(x_prompt, x_sample, state_gdn, cache_nat_k, cache_nat_v, c, c_ctx, ada_w, ada_b, norm_mix_g, norm_ffn_g, even_w_in, gdn_conv_w, gdn_a_log, gdn_dt_bias, gdn_norm_g, nat_q_norm_g, nat_k_norm_g, nat_rpb, even_w_out, ffn_w_gate, ffn_w_up, ffn_w_down, odd_w_in, hy_conv_w, hy_w1, hy_b1, hy_w2, hy_b2, hy_w3, hy_freq, hy_bias, odd_w_out, moe_router_w, moe_router_b, moe_w_gate, moe_w_up, moe_w_down):
    w = dict(state_gdn=state_gdn, cache_nat_k=cache_nat_k, cache_nat_v=cache_nat_v,
             norm_mix_g=norm_mix_g, norm_ffn_g=norm_ffn_g, even_w_in=even_w_in, gdn_conv_w=gdn_conv_w,
             gdn_a_log=gdn_a_log, gdn_dt_bias=gdn_dt_bias, gdn_norm_g=gdn_norm_g,
             nat_q_norm_g=nat_q_norm_g, nat_k_norm_g=nat_k_norm_g, nat_rpb=nat_rpb, even_w_out=even_w_out,
             ffn_w_gate=ffn_w_gate, ffn_w_up=ffn_w_up, ffn_w_down=ffn_w_down, odd_w_in=odd_w_in,
             hy_conv_w=hy_conv_w, hy_bias=hy_bias, odd_w_out=odd_w_out, moe_router_w=moe_router_w,
             moe_router_b=moe_router_b, moe_w_gate=moe_w_gate, moe_w_up=moe_w_up, moe_w_down=moe_w_down)
    assert SEQ == ROW_TILE and DEC_SEQ % ROW_TILE == 0
    mod = _ada_all(c, c_ctx, ada_w, ada_b)
    consts = {s: _dft_consts(s) for s in (SEQ, DEC_SEQ)}
    n_odd = DEPTH // 2
    spectra = {s: [_hy_filter_spectrum(s, consts[s], hy_w1[i], hy_b1[i], hy_w2[i], hy_b2[i], hy_w3[i], hy_freq[i])
                   for i in range(n_odd)] for s in (SEQ, DEC_SEQ)}
    x = jnp.concatenate([x_prompt.reshape(T_CTX, D), x_sample.reshape(T_LAT, D)], axis=0)
    states, ks, vs = [], [], []
    for l in range(DEPTH):
        i = l // 2
        if l % 2 == 0:
            x, s_ctx, k_new, v_new = _even_mixer_layer(x, mod, l, i, w)
            states.append(s_ctx)
            ks.append(k_new)
            vs.append(v_new)
        else:
            x = _odd_mixer_layer(x, mod, l, i, w, spectra, consts)
    y_prompt = x[:T_CTX].reshape(BATCH, SEQ, D)
    y_sample = x[T_CTX:].reshape(DEC_BATCH, DEC_SEQ, D)
    return (y_prompt, y_sample, jnp.stack(states, axis=1), jnp.stack(ks, axis=1), jnp.stack(vs, axis=1))
```

```python
import functools
import math

import jax
import jax.numpy as jnp
import numpy as np
from jax import lax
from jax.experimental import pallas as pl
from jax.experimental.pallas import tpu as pltpu

F32 = jnp.float32
BF16 = jnp.bfloat16

D = 1024
BATCH = 16
SEQ = 256
DEPTH = 4
DEC_BATCH = 2
DEC_SEQ = 1024
PAST_LEN = 512
GRID_W = 64
EPS = 1e-6
H_A = 4
DK_A = 128
DV_A = 128
CHUNK = 64
H_B = 8
DH_B = 64
WIN_R = 8
WIN_C = 16
A_QKV = 2 * H_A * DK_A + H_A * DV_A
A_GATE = H_A * DV_A
B_QKV = 3 * H_B * DH_B
HY_EMB = 33
HY_HID = 64
HY_FAST = 0.3
HY_SLOW = 1.5
HY_TARGET = 1e-2
D_FF = 2816
N_EXP = 8
D_FF_E = 3584

T_CTX = BATCH * SEQ
T_LAT = DEC_BATCH * DEC_SEQ
T_ALL = T_CTX + T_LAT
N_SEG = 8
ROW_TILE = 256

VMEM_LIMIT = 56 * 1024 * 1024


def _cparams(sem, vmem=None):
    return pltpu.CompilerParams(dimension_semantics=sem, vmem_limit_bytes=vmem)


def _sigmoid(x):
    return 1.0 / (1.0 + jnp.exp(-x))


def _seg_of_row(row):
    return jnp.where(row < T_CTX, 0, 1 + (row - T_CTX) // DEC_SEQ)


def _split_bf16(x):
    hi = x.astype(BF16)
    lo = (x - hi.astype(F32)).astype(BF16)
    return hi, lo


def _dot(a, b, dims=(((1,), (0,)), ((), ())), prec="bf16"):
    if prec == "bf16":
        return lax.dot_general(a.astype(BF16), b.astype(BF16), dims, preferred_element_type=F32)
    ah, al = _split_bf16(a.astype(F32))
    bh, bl = _split_bf16(b.astype(F32))
    r = lax.dot_general(ah, bh, dims, preferred_element_type=F32)
    r = r + lax.dot_general(ah, bl, dims, preferred_element_type=F32)
    r = r + lax.dot_general(al, bh, dims, preferred_element_type=F32)
    return r


NT_DIMS = (((1,), (1,)), ((), ()))
TN_DIMS = (((0,), (0,)), ((), ()))


def _ada_kernel(cv_ref, w_ref, b_ref, o_ref):
    cv = cv_ref[...]
    s = cv * _sigmoid(cv)
    o_ref[0] = _dot(s, w_ref[0]) + b_ref[0]


def _ada_all(c, c_ctx, ada_w, ada_b):
    cv = jnp.zeros((N_SEG, D), F32).at[0].set(c_ctx).at[1:1 + DEC_BATCH].set(c)
    tn = 1536
    out = pl.pallas_call(
        _ada_kernel,
        grid=(DEPTH, 6 * D // tn),
        in_specs=[
            pl.BlockSpec((N_SEG, D), lambda l, j: (0, 0)),
            pl.BlockSpec((1, D, tn), lambda l, j: (l, 0, j)),
            pl.BlockSpec((1, 1, tn), lambda l, j: (l, 0, j)),
        ],
        out_specs=pl.BlockSpec((1, N_SEG, tn), lambda l, j: (l, 0, j)),
        out_shape=jax.ShapeDtypeStruct((DEPTH, N_SEG, 6 * D), F32),
        compiler_params=_cparams(("arbitrary", "arbitrary"), VMEM_LIMIT),
        name="ada",
    )(cv, ada_w, ada_b.reshape(DEPTH, 1, 6 * D))
    return out.reshape(DEPTH, N_SEG, 6, D).transpose(0, 2, 1, 3).reshape(DEPTH * 6 * N_SEG, 1, D)


def _mod_spec(layer, k, tm, row_of_step):
    base = (layer * 6 + k) * N_SEG

    def imap(*ids):
        return (base + _seg_of_row(row_of_step(*ids) * tm), 0, 0)

    return pl.BlockSpec((1, 1, D), imap)


def _modulate_math(x, g, scale, shift):
    ms = jnp.mean(x * x, axis=-1, keepdims=True)
    y = x * lax.rsqrt(ms + EPS) * g
    return y * (1.0 + scale) + shift


def _modulate_kernel(x_ref, g_ref, sc_ref, sh_ref, o_ref):
    o_ref[...] = _modulate_math(x_ref[...], g_ref[...], sc_ref[0], sh_ref[0]).astype(o_ref.dtype)


def _modulate(x, g, mod, layer, k_shift, out_dtype):
    tm = 512
    return pl.pallas_call(
        _modulate_kernel,
        grid=(T_ALL // tm,),
        in_specs=[
            pl.BlockSpec((tm, D), lambda i: (i, 0)),
            pl.BlockSpec((1, D), lambda i: (0, 0)),
            _mod_spec(layer, k_shift + 1, tm, lambda i: i),
            _mod_spec(layer, k_shift, tm, lambda i: i),
        ],
        out_specs=pl.BlockSpec((tm, D), lambda i: (i, 0)),
        out_shape=jax.ShapeDtypeStruct((T_ALL, D), out_dtype),
        compiler_params=_cparams(("arbitrary",)),
        name="modulate",
    )(x, g.reshape(1, D), mod, mod)


SWIGLU_CHUNK = 256


def _swiglu_chunks(h, wbf, o_ref):
    n = o_ref.shape[1]
    for c0 in range(0, n, SWIGLU_CHUNK):
        c1 = min(c0 + SWIGLU_CHUNK, n)
        a = jnp.dot(h, wbf[0, :, c0:c1], preferred_element_type=F32)
        b = jnp.dot(h, wbf[1, :, c0:c1], preferred_element_type=F32)
        o_ref[:, c0:c1] = (a * _sigmoid(a) * b).astype(o_ref.dtype)


EVEN_SHIFT = 4 * H_A
EVEN_SPLIT = A_QKV + A_GATE


def _mm_kernel(*refs, chunks, n_w, epilogue, shifted, n_ctx_tiles):
    pos = 0
    lhs = []
    for kw, paired in chunks:
        cnt = 2 if paired else 1
        lhs.append(refs[pos:pos + cnt])
        pos += cnt
    n_wrefs = n_w + (1 if shifted else 0)
    w_refs = refs[pos:pos + n_wrefs]
    rest = refs[pos + n_wrefs:]
    j, i = pl.program_id(0), pl.program_id(1)
    if epilogue == "residual":
        x_ref, gate_ref, o_ref, wbf = rest
    elif epilogue == "residual_mod":
        x_ref, gate_ref, g_ref, sc_ref, sh_ref, o_ref, h_ref, wbf = rest
    else:
        o_ref, wbf = rest

    @pl.when(i == 0)
    def _():
        if shifted:
            tn = wbf.shape[2]
            for jj in range(shifted):
                split = min(max(EVEN_SPLIT - jj * tn, 0), tn)

                @pl.when(j == jj)
                def _(split=split):
                    wa = w_refs[0][0]
                    if split == tn:
                        wbf[0] = wa.astype(BF16)
                    else:
                        parts = [wa[:, :split]] if split else []
                        parts += [wa[:, split + EVEN_SHIFT:], w_refs[1][0][:, :EVEN_SHIFT]]
                        wbf[0] = jnp.concatenate(parts, axis=1).astype(BF16)
        else:
            for k in range(n_w):
                wbf[k] = w_refs[k][0].astype(BF16)

    def lhs_chunk(c):
        r = lhs[c]
        if len(r) == 2:
            return jnp.where(i < n_ctx_tiles, r[0][...], r[1][...]).astype(BF16)
        return r[0][...].astype(BF16)

    def matmul(k):
        acc, off = None, 0
        for c, (kw, _) in enumerate(chunks):
            part = jnp.dot(lhs_chunk(c), wbf[k, off:off + kw, :], preferred_element_type=F32)
            acc = part if acc is None else acc + part
            off += kw
        return acc

    if epilogue == "swiglu":
        _swiglu_chunks(lhs_chunk(0), wbf, o_ref)
        return
    a = matmul(0)
    if epilogue == "residual":
        o_ref[...] = x_ref[...] + gate_ref[0] * a
    elif epilogue == "residual_mod":
        x_new = x_ref[...] + gate_ref[0] * a
        o_ref[...] = x_new
        h_ref[...] = _modulate_math(x_new, g_ref[...], sc_ref[0], sh_ref[0]).astype(h_ref.dtype)
    else:
        o_ref[...] = a.astype(o_ref.dtype)


def _mm(lhs, ws, *, tm, tn, out_dtype=F32, epilogue="none", wi=0, n_out=None, col0=0, shifted=False,
        resid=None, mod=None, layer=None, k_gate=None, next_norm=None):
    chunks, args, in_specs = [], [], []
    n_ctx_tiles = T_CTX // tm
    for part in lhs:
        if isinstance(part, tuple):
            kw = part[0].shape[1]
            chunks.append((kw, True))
            args += [part[0], part[1]]
            in_specs += [pl.BlockSpec((tm, kw), lambda j, i: (jnp.minimum(i, n_ctx_tiles - 1), 0)),
                         pl.BlockSpec((tm, kw), lambda j, i: (jnp.maximum(i - n_ctx_tiles, 0), 0))]
        else:
            kw = part.shape[1]
            chunks.append((kw, False))
            args.append(part)
            in_specs.append(pl.BlockSpec((tm, kw), lambda j, i: (i, 0)))
    kdim = sum(kw for kw, _ in chunks)
    n = n_out if n_out is not None else ws[0].shape[2]
    n_w = len(ws)
    if shifted:
        last = pl.cdiv(ws[0].shape[2], tn) - 1
        in_specs += [pl.BlockSpec((1, kdim, tn), lambda j, i: (wi, 0, j)),
                     pl.BlockSpec((1, kdim, tn), lambda j, i: (wi, 0, jnp.minimum(j + 1, last)))]
        args += [ws[0], ws[0]]
    else:
        in_specs += [pl.BlockSpec((1, kdim, tn), lambda j, i: (wi, 0, col0 + j)) for _ in ws]
        args += list(ws)
    out_specs = pl.BlockSpec((tm, tn), lambda j, i: (i, j))
    out_shape = jax.ShapeDtypeStruct((T_ALL, n), out_dtype)
    if epilogue in ("residual", "residual_mod"):
        base = (layer * 6 + k_gate) * N_SEG
        in_specs += [pl.BlockSpec((tm, tn), lambda j, i: (i, j)),
                     pl.BlockSpec((1, 1, tn), lambda j, i: (base + _seg_of_row(i * tm), 0, j))]
        args += [resid, mod]
    if epilogue == "residual_mod":
        g, n_layer, k_shift, h_dtype = next_norm
        assert tn == n == D
        in_specs += [pl.BlockSpec((1, D), lambda j, i: (0, 0)),
                     _mod_spec(n_layer, k_shift + 1, tm, lambda j, i: i),
                     _mod_spec(n_layer, k_shift, tm, lambda j, i: i)]
        args += [g.reshape(1, D), mod, mod]
        out_specs = [out_specs, pl.BlockSpec((tm, tn), lambda j, i: (i, j))]
        out_shape = [out_shape, jax.ShapeDtypeStruct((T_ALL, n), h_dtype)]
    return pl.pallas_call(
        functools.partial(_mm_kernel, chunks=tuple(chunks), n_w=n_w, epilogue=epilogue,
                          shifted=(n // tn if shifted else 0), n_ctx_tiles=n_ctx_tiles),
        grid=(n // tn, T_ALL // tm),
        in_specs=in_specs,
        out_specs=out_specs,
        out_shape=out_shape,
        scratch_shapes=[pltpu.VMEM((n_w, kdim, tn), BF16)],
        compiler_params=_cparams(("arbitrary", "arbitrary"), VMEM_LIMIT),
        name="mm_" + epilogue,
    )(*args)


def _conv3(x, prev_ref, next_ref, w, seq_len):
    i = pl.program_id(0)
    rows_n = x.shape[0]
    if seq_len is None:
        j = jnp.maximum(i - T_CTX // rows_n, 0)
        per = DEC_SEQ // rows_n
        is_ctx = i < T_CTX // rows_n
        first = is_ctx | (j % per == 0)
        last = is_ctx | (j % per == per - 1)
    else:
        first = (i * rows_n) % seq_len == 0
        last = ((i + 1) * rows_n) % seq_len == 0
    prev_row = jnp.where(first, 0.0, prev_ref[7:8, :])
    next_row = jnp.where(last, 0.0, next_ref[0:1, :])
    rows = lax.broadcasted_iota(jnp.int32, x.shape, 0)
    xm = jnp.where(rows == 0, prev_row, pltpu.roll(x, 1, 0))
    xp = jnp.where(rows == rows_n - 1, next_row, pltpu.roll(x, rows_n - 1, 0))
    return xm * w[0:1, :] + x * w[1:2, :] + xp * w[2:3, :]


def _halo_specs(width, col_block, row_off_tiles):
    per = ROW_TILE // 8
    last_blk = T_ALL // 8 - 1
    main = pl.BlockSpec((ROW_TILE, width), lambda i: (i + row_off_tiles, col_block))
    prev = pl.BlockSpec((8, width), lambda i: (jnp.maximum((i + row_off_tiles) * per - 1, 0), col_block))
    nxt = pl.BlockSpec((8, width), lambda i: (jnp.minimum((i + row_off_tiles + 1) * per, last_blk), col_block))
    return main, prev, nxt


def _cumsum_rows(x, reverse):
    n = x.shape[0]
    rows = lax.broadcasted_iota(jnp.int32, x.shape, 0)
    k = 1
    while k < n:
        if reverse:
            x = x + jnp.where(rows < n - k, pltpu.roll(x, n - k, 0), 0.0)
        else:
            x = x + jnp.where(rows >= k, pltpu.roll(x, k, 0), 0.0)
        k *= 2
    return x


GDN_PREC = "bf16x3"


def _gdn_prep_kernel(pm_ref, prev_ref, next_ref, ps_ref, cw_ref, par_ref, qkv_ref, bg_ref, *, seq_len):
    x = _conv3(pm_ref[...], prev_ref, next_ref, cw_ref[...], seq_len)
    x = x * _sigmoid(x)
    for h in range(H_A):
        sl = slice(h * DK_A, (h + 1) * DK_A)
        qh = x[:, sl]
        qkv_ref[:, sl] = qh * lax.rsqrt(jnp.sum(qh * qh, axis=-1, keepdims=True) + EPS) * (DK_A ** -0.5)
        sl = slice(H_A * DK_A + h * DK_A, H_A * DK_A + (h + 1) * DK_A)
        kh = x[:, sl]
        qkv_ref[:, sl] = kh * lax.rsqrt(jnp.sum(kh * kh, axis=-1, keepdims=True) + EPS)
    qkv_ref[:, 2 * H_A * DK_A:] = x[:, 2 * H_A * DK_A:]
    raw = ps_ref[...]
    lane = lax.broadcasted_iota(jnp.int32, raw.shape, 1)
    beta = _sigmoid(raw)
    z = raw + par_ref[1:2, :]
    softplus = jnp.maximum(z, 0.0) + jnp.log(1.0 + jnp.exp(-jnp.abs(z)))
    g = -jnp.exp(par_ref[0:1, :]) * softplus
    bg_ref[...] = jnp.where(lane < 2 * H_A, beta, jnp.where(lane < 4 * H_A, g, 0.0))


def _gdn_prep(p_main, p_small, conv_w, a_log, dt_bias, row_off_tiles, n_rows, seq_len):
    par = jnp.zeros((2, 128), F32)
    par = par.at[0, 2 * H_A:4 * H_A].set(a_log.reshape(-1)).at[1, 2 * H_A:4 * H_A].set(dt_bias.reshape(-1))
    main, prev, nxt = _halo_specs(A_QKV, 0, row_off_tiles)
    return pl.pallas_call(
        functools.partial(_gdn_prep_kernel, seq_len=seq_len),
        grid=(n_rows // ROW_TILE,),
        in_specs=[main, prev, nxt,
                  pl.BlockSpec((ROW_TILE, 128), lambda i: (i + row_off_tiles, 0)),
                  pl.BlockSpec((3, A_QKV), lambda i: (0, 0)),
                  pl.BlockSpec((2, 128), lambda i: (0, 0))],
        out_specs=[pl.BlockSpec((ROW_TILE, A_QKV), lambda i: (i, 0)),
                   pl.BlockSpec((ROW_TILE, 128), lambda i: (i, 0))],
        out_shape=[jax.ShapeDtypeStruct((n_rows, A_QKV), F32),
                   jax.ShapeDtypeStruct((n_rows, 128), F32)],
        compiler_params=_cparams(("arbitrary",)),
        name="gdn_prep",
    )(p_main, p_main, p_main, p_small, conv_w, par)


def _gdn_chunk(qkv, bg, state, d, h):
    c = CHUNK
    qh = qkv[:, h * DK_A:(h + 1) * DK_A]
    kh = qkv[:, H_A * DK_A + h * DK_A:H_A * DK_A + (h + 1) * DK_A]
    vh = qkv[:, 2 * H_A * DK_A + h * DV_A:2 * H_A * DK_A + (h + 1) * DV_A]
    col = d * H_A + h
    beta = bg[:, col:col + 1]
    gcum = _cumsum_rows(bg[:, 2 * H_A:4 * H_A], reverse=(d == 1))
    gc = gcum[:, col:col + 1]
    ri = lax.broadcasted_iota(jnp.int32, (c, c), 0)
    ci = lax.broadcasted_iota(jnp.int32, (c, c), 1)
    eye = ri == ci
    incl = (ri >= ci) if d == 0 else (ri <= ci)
    strict = (ri > ci) if d == 0 else (ri < ci)
    gc_row = jnp.sum(jnp.where(eye, gc, 0.0), axis=0, keepdims=True)
    decay = jnp.where(incl, jnp.exp(jnp.where(incl, gc - gc_row, 0.0)), 0.0)
    kb = kh * beta
    a_mat = jnp.where(strict, _dot(kb, kh, NT_DIMS, GDN_PREC) * decay, 0.0)
    t_inv = jnp.where(eye, 1.0, 0.0) - a_mat
    pw = a_mat
    for _ in range(5):
        pw = _dot(pw, pw, prec=GDN_PREC)
        t_inv = t_inv + _dot(t_inv, pw, prec=GDN_PREC)
    eg = jnp.exp(gc)
    rhs = jnp.concatenate([vh * beta, kb * eg], axis=1)
    sol = _dot(t_inv, rhs, prec=GDN_PREC)
    ub, wm = sol[:, :DV_A], sol[:, DV_A:]
    attn = jnp.where(incl, _dot(qh, kh, NT_DIMS, GDN_PREC) * decay, 0.0)
    g_end = gc[c - 1:c, :] if d == 0 else gc[0:1, :]
    qd = qh * eg
    kd = kh * jnp.exp(g_end - gc)
    u = ub - _dot(wm, state, prec=GDN_PREC)
    o = _dot(qd, state, prec=GDN_PREC) + _dot(attn, u, prec=GDN_PREC)
    new_state = state * jnp.exp(g_end) + _dot(kd, u, TN_DIMS, GDN_PREC)
    return o, new_state


def _gdn_scan_kernel(*refs, ns, has_s0):
    if has_s0:
        qf_ref, qb_ref, bf_ref, bb_ref, s0_ref, of_ref, ob_ref, sfin_ref, st = refs
    else:
        qf_ref, qb_ref, bf_ref, bb_ref, of_ref, ob_ref, sfin_ref, st = refs
    step = pl.program_id(1)

    @pl.when(step == 0)
    def _():
        if has_s0:
            st[...] = s0_ref[:, 0]
        else:
            st[...] = jnp.zeros(st.shape, F32)

    def per_seq(s, carry):
        for d in range(2):
            qkv = (qf_ref if d == 0 else qb_ref)[s, 0]
            bg = (bf_ref if d == 0 else bb_ref)[s, 0]
            o_ref = of_ref if d == 0 else ob_ref
            for h in range(H_A):
                o, new_state = _gdn_chunk(qkv, bg, st[s, d, h], d, h)
                o_ref[s, 0, :, h * DV_A:(h + 1) * DV_A] = o
                st[s, d, h] = new_state
        return carry

    lax.fori_loop(0, ns, per_seq, 0)

    @pl.when(step == pl.num_programs(1) - 1)
    def _():
        sfin_ref[...] = st[...]


def _gdn_scan(qkv, bg, s0, layer_i, n_seq, seq_len, ns):
    n = seq_len // CHUNK
    qkv4 = qkv.reshape(n_seq, n, CHUNK, A_QKV)
    bg4 = bg.reshape(n_seq, n, CHUNK, 128)
    fwd = lambda g, c: (g, c, 0, 0)
    bwd = lambda g, c: (g, n - 1 - c, 0, 0)
    in_specs = [pl.BlockSpec((ns, 1, CHUNK, A_QKV), fwd), pl.BlockSpec((ns, 1, CHUNK, A_QKV), bwd),
                pl.BlockSpec((ns, 1, CHUNK, 128), fwd), pl.BlockSpec((ns, 1, CHUNK, 128), bwd)]
    args = [qkv4, qkv4, bg4, bg4]
    if s0 is not None:
        in_specs.append(pl.BlockSpec((ns, 1, 2, H_A, DK_A, DV_A), lambda g, c: (g, layer_i, 0, 0, 0, 0)))
        args.append(s0)
    o_shape = jax.ShapeDtypeStruct((n_seq, n, CHUNK, H_A * DV_A), F32)
    of, ob, sfin = pl.pallas_call(
        functools.partial(_gdn_scan_kernel, ns=ns, has_s0=s0 is not None),
        grid=(n_seq // ns, n),
        in_specs=in_specs,
        out_specs=[pl.BlockSpec((ns, 1, CHUNK, H_A * DV_A), fwd),
                   pl.BlockSpec((ns, 1, CHUNK, H_A * DV_A), bwd),
                   pl.BlockSpec((ns, 2, H_A, DK_A, DV_A), lambda g, c: (g, 0, 0, 0, 0))],
        out_shape=[o_shape, o_shape, jax.ShapeDtypeStruct((n_seq, 2, H_A, DK_A, DV_A), F32)],
        scratch_shapes=[pltpu.VMEM((ns, 2, H_A, DK_A, DV_A), F32)],
        compiler_params=_cparams(("arbitrary", "arbitrary")),
        name="gdn_scan",
    )(*args)
    rows = n_seq * seq_len
    return of.reshape(rows, H_A * DV_A), ob.reshape(rows, H_A * DV_A), sfin


B_NN = (((2,), (1,)), ((0,), (0,)))
B_NT = (((2,), (2,)), ((0,), (0,)))
B_TN = (((1,), (1,)), ((0,), (0,)))


GDN_PRECISION = dict(qk="bf16", pw="bf16", sol="bf16", su="bf16", so="bf16", sk="bf16")


def _bmm(a, b, dims=B_NN, key=None):
    return _dot(a, b, dims, GDN_PRECISION[key] if key else "bf16")


def _gdn_scanb_kernel(*refs, ns, has_s0):
    if has_s0:
        qf_ref, qb_ref, bf_ref, bb_ref, s0_ref, of_ref, ob_ref, sfin_ref, st = refs
    else:
        qf_ref, qb_ref, bf_ref, bb_ref, of_ref, ob_ref, sfin_ref, st = refs
    step = pl.program_id(1)
    c = CHUNK
    nb = 2 * ns * H_A

    @pl.when(step == 0)
    def _():
        for d in range(2):
            st[d] = s0_ref[:, 0, d] if has_s0 else jnp.zeros(st.shape[1:], F32)

    q_l, k_l, v_l, beta_l, gc_l = [], [], [], [], []
    for d in range(2):
        for s in range(ns):
            qkv = (qf_ref if d == 0 else qb_ref)[s, 0]
            bg = (bf_ref if d == 0 else bb_ref)[s, 0]
            gcum = _cumsum_rows(bg, reverse=(d == 1))
            for h in range(H_A):
                q_l.append(qkv[:, h * DK_A:(h + 1) * DK_A])
                k_l.append(qkv[:, (H_A + h) * DK_A:(H_A + h + 1) * DK_A])
                v_l.append(qkv[:, 2 * H_A * DK_A + h * DV_A:2 * H_A * DK_A + (h + 1) * DV_A])
                col = d * H_A + h
                beta_l.append(bg[:, col:col + 1])
                gc_l.append(gcum[:, 2 * H_A + col:2 * H_A + col + 1])
    q, k, v = jnp.stack(q_l), jnp.stack(k_l), jnp.stack(v_l)
    beta, gc = jnp.stack(beta_l), jnp.stack(gc_l)

    bi = lax.broadcasted_iota(jnp.int32, (nb, c, c), 0)
    ri = lax.broadcasted_iota(jnp.int32, (nb, c, c), 1)
    ci = lax.broadcasted_iota(jnp.int32, (nb, c, c), 2)
    fwd = bi < nb // 2
    eye = ri == ci
    incl = (fwd & (ri >= ci)) | (jnp.logical_not(fwd) & (ri <= ci))
    strict = incl & jnp.logical_not(eye)
    gc_row = jnp.sum(jnp.where(eye, gc, 0.0), axis=1, keepdims=True)
    decay = jnp.where(incl, jnp.exp(jnp.where(incl, gc - gc_row, 0.0)), 0.0)
    kb = k * beta
    a_mat = jnp.where(strict, _bmm(kb, k, B_NT, "qk") * decay, 0.0)
    blk = 8
    diag = (ri // blk) == (ci // blk)
    pw = jnp.where(diag, a_mat, 0.0)
    r_mat = -pw
    for _ in range(2):
        pw = _bmm(pw, pw, B_NN, "pw")
        r_mat = r_mat + pw + _bmm(r_mat, pw, B_NN, "pw")
    while blk < c:
        off = ((ri // (2 * blk)) == (ci // (2 * blk))) & ((ri // blk) != (ci // blk))
        e_mat = jnp.where(off, a_mat, 0.0)
        x_mat = e_mat + _bmm(r_mat, e_mat, B_NN, "pw")
        r_mat = r_mat - (x_mat + _bmm(x_mat, r_mat, B_NN, "pw"))
        blk *= 2
    eg = jnp.exp(gc)
    rhs = jnp.concatenate([v * beta, kb * eg], axis=2)
    sol = rhs + _bmm(r_mat, rhs, B_NN, "sol")
    ub, wm = sol[:, :, :DV_A], sol[:, :, DV_A:]
    attn = jnp.where(incl, _bmm(q, k, B_NT, "qk") * decay, 0.0)
    fwd1 = lax.broadcasted_iota(jnp.int32, (nb, 1, 1), 0) < nb // 2
    g_end = jnp.where(fwd1, gc[:, c - 1:c, :], gc[:, 0:1, :])
    qd = q * eg
    kd = k * jnp.exp(g_end - gc)
    state = st[...].reshape(nb, DK_A, DV_A)
    u = ub - _bmm(wm, state, B_NN, "su")
    o = _bmm(qd, state, B_NN, "so") + _bmm(attn, u, B_NN, "so")
    state = state * jnp.exp(g_end) + _bmm(kd, u, B_TN, "sk")
    st[...] = state.reshape(st.shape)
    for d in range(2):
        o_ref = of_ref if d == 0 else ob_ref
        for s in range(ns):
            for h in range(H_A):
                o_ref[s, 0, :, h * DV_A:(h + 1) * DV_A] = o[(d * ns + s) * H_A + h]

    @pl.when(step == pl.num_programs(1) - 1)
    def _():
        for d in range(2):
            sfin_ref[:, d] = st[d]


def _gdn_scanb(qkv, bg, s0, layer_i, n_seq, seq_len, ns):
    n = seq_len // CHUNK
    qkv4 = qkv.reshape(n_seq, n, CHUNK, A_QKV)
    bg4 = bg.reshape(n_seq, n, CHUNK, 128)
    fwd = lambda g, c: (g, c, 0, 0)
    bwd = lambda g, c: (g, n - 1 - c, 0, 0)
    in_specs = [pl.BlockSpec((ns, 1, CHUNK, A_QKV), fwd), pl.BlockSpec((ns, 1, CHUNK, A_QKV), bwd),
                pl.BlockSpec((ns, 1, CHUNK, 128), fwd), pl.BlockSpec((ns, 1, CHUNK, 128), bwd)]
    args = [qkv4, qkv4, bg4, bg4]
    if s0 is not None:
        in_specs.append(pl.BlockSpec((ns, 1, 2, H_A, DK_A, DV_A), lambda g, c: (g, layer_i, 0, 0, 0, 0)))
        args.append(s0)
    o_shape = jax.ShapeDtypeStruct((n_seq, n, CHUNK, H_A * DV_A), F32)
    of, ob, sfin = pl.pallas_call(
        functools.partial(_gdn_scanb_kernel, ns=ns, has_s0=s0 is not None),
        grid=(n_seq // ns, n),
        in_specs=in_specs,
        out_specs=[pl.BlockSpec((ns, 1, CHUNK, H_A * DV_A), fwd),
                   pl.BlockSpec((ns, 1, CHUNK, H_A * DV_A), bwd),
                   pl.BlockSpec((ns, 2, H_A, DK_A, DV_A), lambda g, c: (g, 0, 0, 0, 0))],
        out_shape=[o_shape, o_shape, jax.ShapeDtypeStruct((n_seq, 2, H_A, DK_A, DV_A), F32)],
        scratch_shapes=[pltpu.VMEM((2, ns, H_A, DK_A, DV_A), F32)],
        compiler_params=_cparams(("arbitrary", "arbitrary"), VMEM_LIMIT),
        name="gdn_scan",
    )(*args)
    rows = n_seq * seq_len
    return of.reshape(rows, H_A * DV_A), ob.reshape(rows, H_A * DV_A), sfin


def _gdn_post_kernel(of_ref, ob_ref, gate_ref, g_ref, o_ref):
    o = of_ref[...] + ob_ref[...]
    gate = gate_ref[...]
    for h in range(H_A):
        sl = slice(h * DV_A, (h + 1) * DV_A)
        oh = o[:, sl]
        y = oh * lax.rsqrt(jnp.mean(oh * oh, axis=-1, keepdims=True) + EPS) * g_ref[...]
        gh = gate[:, sl]
        o_ref[:, sl] = (y * (gh * _sigmoid(gh))).astype(o_ref.dtype)


def _gdn_post(of, ob, p_main, gdn_g, row_off_tiles, n_rows):
    gate_blk = A_QKV // A_GATE
    return pl.pallas_call(
        _gdn_post_kernel,
        grid=(n_rows // ROW_TILE,),
        in_specs=[pl.BlockSpec((ROW_TILE, A_GATE), lambda i: (i, 0)),
                  pl.BlockSpec((ROW_TILE, A_GATE), lambda i: (i, 0)),
                  pl.BlockSpec((ROW_TILE, A_GATE), lambda i: (i + row_off_tiles, gate_blk)),
                  pl.BlockSpec((1, DV_A), lambda i: (0, 0))],
        out_specs=pl.BlockSpec((ROW_TILE, A_GATE), lambda i: (i, 0)),
        out_shape=jax.ShapeDtypeStruct((n_rows, A_GATE), BF16),
        compiler_params=_cparams(("arbitrary",)),
        name="gdn_post",
    )(of, ob, p_main, gdn_g.reshape(1, DV_A))


NAT_W = H_B * DH_B
NAT_QCOL = (A_QKV + A_GATE) // NAT_W
NAT_SCALE = DH_B ** -0.5


def _nat_prep_kernel(q_ref, k_ref, bd_ref, gq_ref, gk_ref, qn_ref, kn_ref):
    bd = bd_ref[...]
    for x_ref, g_ref, o_ref in ((q_ref, gq_ref, qn_ref), (k_ref, gk_ref, kn_ref)):
        x = x_ref[...]
        hi, lo = _split_bf16(x * x)
        ms = jnp.dot(hi, bd, preferred_element_type=F32) + jnp.dot(lo, bd, preferred_element_type=F32)
        o_ref[...] = x * lax.rsqrt(ms + EPS) * g_ref[...]


def _nat_prep(p_main, qn_g, kn_g):
    grp = np.arange(NAT_W) // DH_B
    bd = jnp.asarray((grp[:, None] == grp[None, :]).astype(np.float32) / DH_B, BF16)
    spec = lambda cb: pl.BlockSpec((ROW_TILE, NAT_W), lambda i: (i, cb))
    return pl.pallas_call(
        _nat_prep_kernel,
        grid=(T_ALL // ROW_TILE,),
        in_specs=[spec(NAT_QCOL), spec(NAT_QCOL + 1),
                  pl.BlockSpec((NAT_W, NAT_W), lambda i: (0, 0)),
                  pl.BlockSpec((1, NAT_W), lambda i: (0, 0)),
                  pl.BlockSpec((1, NAT_W), lambda i: (0, 0))],
        out_specs=[spec(0), spec(0)],
        out_shape=[jax.ShapeDtypeStruct((T_ALL, NAT_W), F32)] * 2,
        compiler_params=_cparams(("arbitrary",)),
        name="nat_prep",
    )(p_main, p_main, bd, jnp.tile(qn_g, H_B).reshape(1, NAT_W), jnp.tile(kn_g, H_B).reshape(1, NAT_W))


def _pair_masks():
    lane = lax.broadcasted_iota(jnp.int32, (1, 2 * DH_B), 1)
    return lane < DH_B


def _nat_ctx_kernel(q_ref, k_ref, v_ref, o_ref):
    lo = _pair_masks()
    for p in range(H_B // 2):
        sl = slice(p * 2 * DH_B, (p + 1) * 2 * DH_B)
        q2, k2, v2 = q_ref[:, sl], k_ref[:, sl], v_ref[:, sl]
        halves = []
        for half in range(2):
            qm = jnp.where(lo if half == 0 else jnp.logical_not(lo), q2, 0.0)
            s = _dot(qm, k2, NT_DIMS) * NAT_SCALE
            e = jnp.exp(s - jnp.max(s, axis=-1, keepdims=True))
            pr = e / jnp.sum(e, axis=-1, keepdims=True)
            halves.append(_dot(pr, v2))
        o_ref[:, sl] = jnp.where(lo, halves[0], halves[1]).astype(o_ref.dtype)


def _nat_ctx(qn, kn, p_main):
    spec = lambda cb: pl.BlockSpec((SEQ, NAT_W), lambda b: (b, cb))
    return pl.pallas_call(
        _nat_ctx_kernel,
        grid=(BATCH,),
        in_specs=[spec(0), spec(0), spec(NAT_QCOL + 2)],
        out_specs=spec(0),
        out_shape=jax.ShapeDtypeStruct((T_CTX, NAT_W), BF16),
        compiler_params=_cparams(("arbitrary",)),
        name="nat_ctx",
    )(qn, kn, p_main)


def _nat_bias_kernel(r_ref, e_ref, ok_ref, o_ref):
    r = r_ref[...]
    hi, lo = _split_bf16(r)
    lo2 = (r - hi.astype(F32) - lo.astype(F32)).astype(BF16)
    e = e_ref[...]
    t = (jnp.dot(hi, e, preferred_element_type=F32) + jnp.dot(lo, e, preferred_element_type=F32)
         + jnp.dot(lo2, e, preferred_element_type=F32))
    o_ref[...] = jnp.where(ok_ref[...] > 0.5, t, -jnp.inf)


def _nat_bias(rpb):
    n_dr, n_dc = 2 * WIN_R - 1, 2 * WIN_C - 1
    qc = np.arange(GRID_W)[:, None]
    kc = np.arange(GRID_W)[None, :]
    dc = (kc - qc + WIN_C - 1).reshape(-1)
    c0 = np.clip(qc - WIN_C // 2, 0, GRID_W - WIN_C)
    ok = ((kc >= c0) & (kc < c0 + WIN_C)).reshape(1, -1).astype(np.float32)
    onehot = (np.arange(128)[:, None] == dc[None, :]).astype(np.float32)
    rows = H_B * n_dr
    rp = jnp.zeros((rows, 128), F32).at[:, :n_dc].set(rpb.reshape(rows, n_dc))
    tab = pl.pallas_call(
        _nat_bias_kernel,
        out_shape=jax.ShapeDtypeStruct((rows, GRID_W * GRID_W), F32),
        name="nat_bias",
    )(rp, jnp.asarray(onehot, BF16), jnp.asarray(ok))
    tab = tab.reshape(H_B, n_dr, GRID_W, GRID_W)
    win = jnp.stack([tab[:, d0:d0 + WIN_R] for d0 in range(WIN_R)], axis=1)
    return win.transpose(0, 1, 3, 2, 4).reshape(H_B, WIN_R, GRID_W, WIN_R * GRID_W)


def _nat_lat_kernel(q_ref, k_ref, v_ref, kc_ref, vc_ref, bias_ref, o_ref):
    r = pl.program_id(1)
    rows = DEC_SEQ // GRID_W
    r0 = jnp.clip(r - WIN_R // 2, 0, rows - WIN_R)
    start = pl.multiple_of(r0 * GRID_W, GRID_W)
    n_loc = WIN_R * GRID_W
    lo = _pair_masks()
    for p in range(H_B // 2):
        sl = slice(p * 2 * DH_B, (p + 1) * 2 * DH_B)
        q2 = q_ref[:, sl]
        kw = k_ref[0, pl.ds(start, n_loc), sl]
        vw = v_ref[pl.ds(start, n_loc), sl]
        kc, vc = kc_ref[0, :, sl], vc_ref[0, :, sl]
        halves = []
        for half in range(2):
            qm = jnp.where(lo if half == 0 else jnp.logical_not(lo), q2, 0.0)
            s_loc = _dot(qm, kw, NT_DIMS) * NAT_SCALE + bias_ref[2 * p + half, 0]
            s_ctx = _dot(qm, kc, NT_DIMS) * NAT_SCALE
            m = jnp.maximum(jnp.max(s_loc, axis=-1, keepdims=True), jnp.max(s_ctx, axis=-1, keepdims=True))
            e_loc, e_ctx = jnp.exp(s_loc - m), jnp.exp(s_ctx - m)
            inv = 1.0 / (jnp.sum(e_loc, axis=-1, keepdims=True) + jnp.sum(e_ctx, axis=-1, keepdims=True))
            halves.append(_dot(e_loc * inv, vw) + _dot(e_ctx * inv, vc))
        o_ref[:, sl] = jnp.where(lo, halves[0], halves[1]).astype(o_ref.dtype)


def _nat_lat(qn, kn, p_main, kc, vc, bias):
    rows = DEC_SEQ // GRID_W
    lat_tile0 = T_CTX // GRID_W
    lat_seq0 = T_CTX // DEC_SEQ

    def bias_map(b, r):
        r0 = jnp.clip(r - WIN_R // 2, 0, rows - WIN_R)
        return (0, WIN_R - 1 + r0 - r, 0, 0)

    return pl.pallas_call(
        _nat_lat_kernel,
        grid=(DEC_BATCH, rows),
        in_specs=[pl.BlockSpec((GRID_W, NAT_W), lambda b, r: (lat_tile0 + b * rows + r, 0)),
                  pl.BlockSpec((1, DEC_SEQ, NAT_W), lambda b, r: (lat_seq0 + b, 0, 0)),
                  pl.BlockSpec((DEC_SEQ, NAT_W), lambda b, r: (lat_seq0 + b, NAT_QCOL + 2)),
                  pl.BlockSpec((1, PAST_LEN, NAT_W), lambda b, r: (b, 0, 0)),
                  pl.BlockSpec((1, PAST_LEN, NAT_W), lambda b, r: (b, 0, 0)),
                  pl.BlockSpec((H_B, 1, GRID_W, WIN_R * GRID_W), bias_map)],
        out_specs=pl.BlockSpec((GRID_W, NAT_W), lambda b, r: (b * rows + r, 0)),
        out_shape=jax.ShapeDtypeStruct((T_LAT, NAT_W), BF16),
        compiler_params=_cparams(("arbitrary", "arbitrary"), VMEM_LIMIT),
        name="nat_lat",
    )(qn, kn.reshape(T_ALL // DEC_SEQ, DEC_SEQ, NAT_W), p_main, kc, vc, bias)


def _heads_to_lanes(cache):
    b, h, l, dh = cache.shape
    return cache.transpose(0, 2, 1, 3).reshape(b, l, h * dh)


def _finalize_kernel(k0_ref, v0_ref, s0_ref, k1_ref, v1_ref, s1_ref, ko_ref, vo_ref, so_ref):
    layer = pl.program_id(0)
    for idx, (k_ref, v_ref, s_ref) in enumerate(((k0_ref, v0_ref, s0_ref), (k1_ref, v1_ref, s1_ref))):
        @pl.when(layer == idx)
        def _(k_ref=k_ref, v_ref=v_ref, s_ref=s_ref):
            for h in range(H_B):
                ko_ref[0, 0, h] = k_ref[:, h * DH_B:(h + 1) * DH_B]
                vo_ref[0, 0, h] = v_ref[:, h * DH_B:(h + 1) * DH_B]
            so_ref[0, 0] = s_ref[0]


def _finalize_caches(kns, pms, states):
    n_even = len(kns)
    assert n_even == 2
    tok = lambda cb: pl.BlockSpec((SEQ, NAT_W), lambda l, b: (b, cb))
    st_in = pl.BlockSpec((1, 2, H_A, DK_A, DV_A), lambda l, b: (b, 0, 0, 0, 0))
    cache_out = pl.BlockSpec((1, 1, H_B, SEQ, DH_B), lambda l, b: (b, l, 0, 0, 0))
    cache_shape = jax.ShapeDtypeStruct((BATCH, n_even, H_B, SEQ, DH_B), F32)
    return pl.pallas_call(
        _finalize_kernel,
        grid=(n_even, BATCH),
        in_specs=[tok(0), tok(NAT_QCOL + 2), st_in, tok(0), tok(NAT_QCOL + 2), st_in],
        out_specs=[cache_out, cache_out,
                   pl.BlockSpec((1, 1, 2, H_A, DK_A, DV_A), lambda l, b: (b, l, 0, 0, 0, 0))],
        out_shape=[cache_shape, cache_shape,
                   jax.ShapeDtypeStruct((BATCH, n_even, 2, H_A, DK_A, DV_A), F32)],
        compiler_params=_cparams(("arbitrary", "arbitrary")),
        name="finalize_caches",
    )(kns[0], pms[0], states[0], kns[1], pms[1], states[1])


def _even_mixer_layer(x, h, mod, l, i, w):
    p_main = _mm([h], [w["even_w_in"]], tm=1024, tn=1792, wi=i, n_out=A_QKV + A_GATE + B_QKV, shifted=True)
    p_small = _mm([h], [w["even_w_in"]], tm=1024, tn=128, wi=i, n_out=128, col0=(A_QKV + A_GATE) // 128)

    conv_w, a_log, dt_bias = w["gdn_conv_w"][i], w["gdn_a_log"][i], w["gdn_dt_bias"][i]
    ctx_tiles = T_CTX // ROW_TILE
    qkv_c, bg_c = _gdn_prep(p_main, p_small, conv_w, a_log, dt_bias, 0, T_CTX, SEQ)
    qkv_l, bg_l = _gdn_prep(p_main, p_small, conv_w, a_log, dt_bias, ctx_tiles, T_LAT, DEC_SEQ)
    of_c, ob_c, s_ctx = _gdn_scanb(qkv_c, bg_c, None, i, BATCH, SEQ, 8)
    of_l, ob_l, _ = _gdn_scanb(qkv_l, bg_l, w["state_gdn"], i, DEC_BATCH, DEC_SEQ, 2)
    oa_c = _gdn_post(of_c, ob_c, p_main, w["gdn_norm_g"][i], 0, T_CTX)
    oa_l = _gdn_post(of_l, ob_l, p_main, w["gdn_norm_g"][i], ctx_tiles, T_LAT)

    qn, kn = _nat_prep(p_main, w["nat_q_norm_g"][i], w["nat_k_norm_g"][i])
    ob_c = _nat_ctx(qn, kn, p_main)
    bias = _nat_bias(w["nat_rpb"][i])
    kc = _heads_to_lanes(w["cache_nat_k"][:, i])
    vc = _heads_to_lanes(w["cache_nat_v"][:, i])
    ob_l = _nat_lat(qn, kn, p_main, kc, vc, bias)

    x, h2 = _mm([(oa_c, oa_l), (ob_c, ob_l)], [w["even_w_out"]], tm=512, tn=D, epilogue="residual_mod", wi=i,
                resid=x, mod=mod, layer=l, k_gate=2, next_norm=(w["norm_ffn_g"][l], l, 3, BF16))
    act = _mm([h2], [w["ffn_w_gate"], w["ffn_w_up"]], tm=512, tn=1408, out_dtype=BF16, epilogue="swiglu", wi=i)
    if l + 1 < DEPTH:
        x, h_next = _mm([act], [w["ffn_w_down"]], tm=512, tn=D, epilogue="residual_mod", wi=i,
                        resid=x, mod=mod, layer=l, k_gate=5, next_norm=(w["norm_mix_g"][l + 1], l + 1, 0, BF16))
    else:
        x = _mm([act], [w["ffn_w_down"]], tm=512, tn=512, epilogue="residual", wi=i,
                resid=x, mod=mod, layer=l, k_gate=5)
        h_next = None
    return x, h_next, kn, p_main, s_ctx


def _dft_consts(seq):
    n = 2 * seq
    k = np.arange(seq)[:, None]
    s = np.arange(seq)[None, :]
    ang = 2.0 * np.pi * ((k * s) % n) / n
    fr = np.cos(ang)
    fi = -np.sin(ang)
    fi[0, :] = np.cos(np.pi * (np.arange(seq) % 2))
    fm = np.concatenate([fr, fi], axis=0)

    def to_bf16(a):
        return jnp.asarray(a.astype(np.float32)).astype(BF16)

    cw = np.full((n, 1), 2.0 / n)
    cw[0, 0] = cw[seq, 0] = 1.0 / n
    sg = np.ones((n, 1))
    sg[seq + 1:, 0] = -1.0
    cs = np.zeros((n, 128), np.float32)
    cs[:, 0:1] = cw
    cs[:, 1:2] = cw * sg
    return to_bf16(fm), to_bf16(fm.T.copy()), jnp.asarray(cs)


def _dft_apply(m, x):
    return jnp.dot(m, x.astype(BF16), preferred_element_type=F32)


def _hy_filter_kernel(z_ref, w1_ref, b1_ref, w2_ref, b2_ref, fq_ref, w3_ref, t_ref, dl_ref,
                      fm_ref, cs_ref, o_ref, hh_scr):
    c, d = pl.program_id(0), pl.program_id(1)

    @pl.when((c == 0) & (d == 0))
    def _():
        fq = fq_ref[...]
        hh = jnp.sin(fq * (_dot(z_ref[...], w1_ref[...], prec="bf16x3") + b1_ref[...]))
        hh_scr[...] = jnp.sin(fq * (_dot(hh, w2_ref[...], prec="bf16x3") + b2_ref[...]))

    filt = _dot(hh_scr[...], w3_ref[...], prec="bf16x3") * jnp.exp(-t_ref[...] * dl_ref[...])
    rows = lax.broadcasted_iota(jnp.int32, filt.shape, 0)
    filt = jnp.where((d == 1) & (rows == 0), 0.0, filt)
    spec = _dft_apply(fm_ref[...], filt)

    @pl.when(d == 0)
    def _():
        o_ref[...] = spec * cs_ref[:, 0:1]

    @pl.when(d == 1)
    def _():
        o_ref[...] = o_ref[...] + spec * cs_ref[:, 1:2]


def _hy_filter_spectrum(seq, consts, w1, b1, w2, b2, w3, freq):
    fm, _, cs = consts
    bands = (HY_EMB - 1) // 2
    t = np.linspace(0.0, 1.0, seq, dtype=np.float32)[:, None]
    wv = (np.float32(2.0 * math.pi / seq) * np.arange(seq, dtype=np.float32))[:, None]
    f = np.linspace(1e-4, bands - 1, bands, dtype=np.float32)[None, :]
    z = np.zeros((seq, 128), np.float32)
    z[:, 0:1] = t
    z[:, 1:1 + bands] = np.cos(f * wv)
    z[:, 1 + bands:HY_EMB] = -np.sin(f * wv)
    deltas = np.abs(np.linspace(math.log(HY_TARGET) / HY_FAST, math.log(HY_TARGET) / HY_SLOW, D,
                                dtype=np.float32))[None, :]
    w1p = jnp.zeros((128, HY_HID), F32).at[:HY_EMB].set(w1)
    tc = 256
    n = 2 * seq
    full = lambda shape: pl.BlockSpec(shape, lambda c, d: tuple(0 for _ in shape))
    return pl.pallas_call(
        _hy_filter_kernel,
        grid=(D // tc, 2),
        in_specs=[full((seq, 128)), full((128, HY_HID)), full((1, HY_HID)), full((HY_HID, HY_HID)),
                  full((1, HY_HID)), full((1, HY_HID)),
                  pl.BlockSpec((HY_HID, tc), lambda c, d: (0, d * (D // tc) + c)),
                  full((seq, 1)), pl.BlockSpec((1, tc), lambda c, d: (0, c)),
                  full((n, seq)), full((n, 128))],
        out_specs=pl.BlockSpec((n, tc), lambda c, d: (0, c)),
        out_shape=jax.ShapeDtypeStruct((n, D), F32),
        scratch_shapes=[pltpu.VMEM((seq, HY_HID), F32)],
        compiler_params=_cparams(("arbitrary", "arbitrary"), VMEM_LIMIT),
        name="hy_filter",
    )(jnp.asarray(z), w1p, b1.reshape(1, -1), w2, b2.reshape(1, -1), freq.reshape(1, -1), w3,
      jnp.asarray(t), jnp.asarray(deltas), fm, cs)


def _hy_conv_kernel(p0_ref, p1_ref, pv_ref, cw0_ref, cw1_ref, cwv_ref, kf_ref, bias_ref, fm_ref, ft_ref,
                    o_ref, *, seq):
    rows = lax.broadcasted_iota(jnp.int32, p0_ref.shape, 0)

    def conv3(x_ref, w_ref):
        x, w = x_ref[...], w_ref[...]
        xm = jnp.where(rows == 0, 0.0, pltpu.roll(x, 1, 0))
        xp = jnp.where(rows == seq - 1, 0.0, pltpu.roll(x, seq - 1, 0))
        return xm * w[0:1, :] + x * w[1:2, :] + xp * w[2:3, :]

    x0 = conv3(p0_ref, cw0_ref)
    u = conv3(pv_ref, cwv_ref) * conv3(p1_ref, cw1_ref)
    xs = _dft_apply(fm_ref[...], u)
    kf = kf_ref[...]
    xr, xi = xs[:seq], xs[seq:]
    kr, ki = kf[:seq], kf[seq:]
    row0 = lax.broadcasted_iota(jnp.int32, xr.shape, 0) == 0
    xiki = xi * ki
    yr = xr * kr - jnp.where(row0, 0.0, xiki)
    yi = jnp.where(row0, xiki, xr * ki + xi * kr)
    y = _dft_apply(ft_ref[...], jnp.concatenate([yr, yi], axis=0))
    o_ref[...] = ((y + u * bias_ref[...]) * x0).astype(o_ref.dtype)


def _hy_conv(p, conv_w, kf, bias, consts, seq, n_seq, seq_blk0, tc):
    fm, ft, _ = consts
    n = 2 * seq
    nc = D // tc
    full = lambda shape: pl.BlockSpec(shape, lambda b, c: (0, 0))
    grp = lambda g: pl.BlockSpec((seq, tc), lambda b, c: (seq_blk0 + b, g * nc + c))
    cwg = lambda g: pl.BlockSpec((3, tc), lambda b, c: (0, g * nc + c))
    return pl.pallas_call(
        functools.partial(_hy_conv_kernel, seq=seq),
        grid=(n_seq, nc),
        in_specs=[grp(0), grp(1), grp(2), cwg(0), cwg(1), cwg(2),
                  pl.BlockSpec((n, tc), lambda b, c: (0, c)),
                  pl.BlockSpec((1, tc), lambda b, c: (0, c)),
                  full((n, seq)), full((seq, n))],
        out_specs=pl.BlockSpec((seq, tc), lambda b, c: (b, c)),
        out_shape=jax.ShapeDtypeStruct((n_seq * seq, D), BF16),
        compiler_params=_cparams(("arbitrary", "arbitrary"), VMEM_LIMIT),
        name="hy_conv",
    )(p, p, p, conv_w, conv_w, conv_w, kf, bias.reshape(1, D), fm, ft)


MOE_TM = 256
MOE_TILES = 2 * T_ALL // MOE_TM + N_EXP
MOE_ROWS = MOE_TILES * MOE_TM
ROUTE_TM = 512
DISPATCH_TM = 512
COMBINE_TM = 256


def _router_kernel(h_ref, rw_ref, rb_ref, tri_ref, ei_ref, gf_ref, cnt_ref, carry):
    @pl.when(pl.program_id(0) == 0)
    def _():
        carry[...] = jnp.zeros(carry.shape, F32)

    logits = _dot(h_ref[...], rw_ref[...], prec="bf16x3") + rb_ref[...]
    lane = lax.broadcasted_iota(jnp.int32, logits.shape, 1)
    logits = jnp.where(lane < N_EXP, logits, -jnp.inf)
    m1 = jnp.max(logits, axis=-1, keepdims=True)
    i1 = jnp.min(jnp.where(logits == m1, lane, 128), axis=-1, keepdims=True)
    rest = jnp.where(lane == i1, -jnp.inf, logits)
    m2 = jnp.max(rest, axis=-1, keepdims=True)
    i2 = jnp.min(jnp.where(rest == m2, lane, 128), axis=-1, keepdims=True)
    e = jnp.exp(m2 - m1)
    g1 = 1.0 / (1.0 + e)
    g2 = e * g1
    pick = jnp.where((lane == i1) | (lane == i2), 1.0, 0.0)
    before = carry[...] + jnp.dot(tri_ref[...], pick.astype(BF16), preferred_element_type=F32)
    r1 = jnp.sum(jnp.where(lane == i1, before, 0.0), axis=-1, keepdims=True)
    r2 = jnp.sum(jnp.where(lane == i2, before, 0.0), axis=-1, keepdims=True)
    carry[...] = carry[...] + jnp.sum(pick, axis=0, keepdims=True)
    ints = jnp.where(lane == 0, i1, jnp.where(lane == 1, i2, 0))
    ranks = jnp.where(lane == 2, r1, jnp.where(lane == 3, r2, 0.0))
    ei_ref[...] = ints + ranks.astype(jnp.int32)
    gf_ref[...] = jnp.where(lane == 0, g1, jnp.where(lane == 1, g2, 0.0))
    cnt_ref[...] = carry[...].astype(jnp.int32)


def _router(h, router_w, router_b):
    rw = jnp.zeros((D, 128), F32).at[:, :N_EXP].set(router_w)
    rb = jnp.zeros((1, 128), F32).at[0, :N_EXP].set(router_b)
    tri = jnp.asarray(np.tril(np.ones((ROUTE_TM, ROUTE_TM), np.float32), -1), BF16)
    row = pl.BlockSpec((ROUTE_TM, 128), lambda i: (i, 0))
    return pl.pallas_call(
        _router_kernel,
        grid=(T_ALL // ROUTE_TM,),
        in_specs=[pl.BlockSpec((ROUTE_TM, D), lambda i: (i, 0)),
                  pl.BlockSpec((D, 128), lambda i: (0, 0)),
                  pl.BlockSpec((1, 128), lambda i: (0, 0)),
                  pl.BlockSpec((ROUTE_TM, ROUTE_TM), lambda i: (0, 0))],
        out_specs=[row, row, pl.BlockSpec((1, 128), lambda i: (0, 0))],
        out_shape=[jax.ShapeDtypeStruct((T_ALL, 128), jnp.int32),
                   jax.ShapeDtypeStruct((T_ALL, 128), F32),
                   jax.ShapeDtypeStruct((1, 128), jnp.int32)],
        scratch_shapes=[pltpu.VMEM((1, 128), F32)],
        compiler_params=_cparams(("arbitrary",)),
        name="moe_router",
    )(h, rw, rb, tri)


def _moe_dispatch_kernel(pos_ref, h_ref, init_ref, out_ref, sem):
    del init_ref
    base = pl.program_id(0) * DISPATCH_TM

    def copy(r, p):
        return pltpu.make_async_copy(h_ref.at[pl.ds(r, 1)], out_ref.at[pl.ds(p, 1)], sem)

    def start(r, c):
        copy(r, pos_ref[base + r]).start()
        copy(r, pos_ref[T_ALL + base + r]).start()
        return c

    def wait(r, c):
        copy(0, 0).wait()
        copy(0, 0).wait()
        return c

    lax.fori_loop(0, DISPATCH_TM, start, 0, unroll=8)
    lax.fori_loop(0, DISPATCH_TM, wait, 0, unroll=8)


def _moe_dispatch(h, flat_pos):
    any_spec = pl.BlockSpec(memory_space=pl.ANY)
    return pl.pallas_call(
        _moe_dispatch_kernel,
        grid_spec=pltpu.PrefetchScalarGridSpec(
            num_scalar_prefetch=1, grid=(T_ALL // DISPATCH_TM,),
            in_specs=[pl.BlockSpec((DISPATCH_TM, D), lambda i, pos: (i, 0)), any_spec],
            out_specs=any_spec,
            scratch_shapes=[pltpu.SemaphoreType.DMA(())]),
        out_shape=jax.ShapeDtypeStruct((MOE_ROWS, D), F32),
        input_output_aliases={2: 0},
        compiler_params=_cparams(("arbitrary",)),
        name="moe_dispatch",
    )(flat_pos, h, jnp.zeros((MOE_ROWS, D), F32))


def _moe_up_kernel(te_ref, nu_ref, g_ref, wg_ref, wu_ref, o_ref, wbf):
    i = pl.program_id(1)
    new_w = (i == 0) | (te_ref[i] != te_ref[jnp.maximum(i - 1, 0)])

    @pl.when(new_w)
    def _():
        wbf[0] = wg_ref[0, 0].astype(BF16)
        wbf[1] = wu_ref[0, 0].astype(BF16)

    @pl.when(i < nu_ref[0])
    def _():
        _swiglu_chunks(g_ref[...].astype(BF16), wbf, o_ref)

    @pl.when(i >= nu_ref[0])
    def _():
        o_ref[...] = jnp.zeros(o_ref.shape, o_ref.dtype)


def _moe_up(tile_expert, n_used, rows, wg, wu, li, tn):
    wspec = pl.BlockSpec((1, 1, D, tn), lambda j, i, te, nu: (li, te[i], 0, j))
    return pl.pallas_call(
        _moe_up_kernel,
        grid_spec=pltpu.PrefetchScalarGridSpec(
            num_scalar_prefetch=2, grid=(D_FF_E // tn, MOE_TILES),
            in_specs=[pl.BlockSpec((MOE_TM, D), lambda j, i, te, nu: (i, 0)), wspec, wspec],
            out_specs=pl.BlockSpec((MOE_TM, tn), lambda j, i, te, nu: (i, j)),
            scratch_shapes=[pltpu.VMEM((2, D, tn), BF16)]),
        out_shape=jax.ShapeDtypeStruct((MOE_ROWS, D_FF_E), BF16),
        compiler_params=_cparams(("arbitrary", "arbitrary"), VMEM_LIMIT),
        name="moe_up",
    )(tile_expert, n_used, rows, wg, wu)


def _moe_down_kernel(te_ref, nu_ref, a_ref, wd_ref, o_ref, wbf):
    i = pl.program_id(1)
    new_w = (i == 0) | (te_ref[i] != te_ref[jnp.maximum(i - 1, 0)])

    @pl.when(new_w)
    def _():
        wbf[...] = wd_ref[0, 0].astype(BF16)

    @pl.when(i < nu_ref[0])
    def _():
        o_ref[...] = jnp.dot(a_ref[...], wbf[...], preferred_element_type=F32)

    @pl.when(i >= nu_ref[0])
    def _():
        o_ref[...] = jnp.zeros(o_ref.shape, o_ref.dtype)


def _moe_down(tile_expert, n_used, act, wd, li, tn):
    return pl.pallas_call(
        _moe_down_kernel,
        grid_spec=pltpu.PrefetchScalarGridSpec(
            num_scalar_prefetch=2, grid=(D // tn, MOE_TILES),
            in_specs=[pl.BlockSpec((MOE_TM, D_FF_E), lambda j, i, te, nu: (i, 0)),
                      pl.BlockSpec((1, 1, D_FF_E, tn), lambda j, i, te, nu: (li, te[i], 0, j))],
            out_specs=pl.BlockSpec((MOE_TM, tn), lambda j, i, te, nu: (i, j)),
            scratch_shapes=[pltpu.VMEM((D_FF_E, tn), BF16)]),
        out_shape=jax.ShapeDtypeStruct((MOE_ROWS, D), F32),
        compiler_params=_cparams(("arbitrary", "arbitrary"), VMEM_LIMIT),
        name="moe_down",
    )(tile_expert, n_used, act, wd)


def _moe_combine_kernel(pos_ref, x_ref, y_ref, gf_ref, gate_ref, *rest, with_next):
    if with_next:
        g_ref, sc_ref, sh_ref, o_ref, h_ref, ybuf, sem = rest
    else:
        o_ref, ybuf, sem = rest
    base = pl.program_id(0) * COMBINE_TM

    def copy(slot, r, p):
        return pltpu.make_async_copy(y_ref.at[pl.ds(p, 1)], ybuf.at[slot, pl.ds(r, 1)], sem)

    def start(r, c):
        copy(0, r, pos_ref[base + r]).start()
        copy(1, r, pos_ref[T_ALL + base + r]).start()
        return c

    def wait(r, c):
        copy(0, 0, 0).wait()
        copy(1, 0, 0).wait()
        return c

    lax.fori_loop(0, COMBINE_TM, start, 0, unroll=8)
    lax.fori_loop(0, COMBINE_TM, wait, 0, unroll=8)
    gf = gf_ref[...]
    f = gf[:, 0:1] * ybuf[0] + gf[:, 1:2] * ybuf[1]
    x_new = x_ref[...] + gate_ref[0] * f
    o_ref[...] = x_new
    if with_next:
        h_ref[...] = _modulate_math(x_new, g_ref[...], sc_ref[0], sh_ref[0]).astype(h_ref.dtype)


def _moe_combine(x, y, flat_pos, gf, mod, layer, next_norm):
    tm = COMBINE_TM
    base = (layer * 6 + 5) * N_SEG
    row = pl.BlockSpec((tm, D), lambda i, pos: (i, 0))
    in_specs = [row, pl.BlockSpec(memory_space=pl.ANY),
                pl.BlockSpec((tm, 128), lambda i, pos: (i, 0)),
                pl.BlockSpec((1, 1, D), lambda i, pos: (base + _seg_of_row(i * tm), 0, 0))]
    args = [flat_pos, x, y, gf, mod]
    out_specs, out_shape = row, jax.ShapeDtypeStruct((T_ALL, D), F32)
    if next_norm is not None:
        g, n_layer, k_shift = next_norm
        in_specs += [pl.BlockSpec((1, D), lambda i, pos: (0, 0)),
                     _mod_spec(n_layer, k_shift + 1, tm, lambda i, pos: i),
                     _mod_spec(n_layer, k_shift, tm, lambda i, pos: i)]
        args += [g.reshape(1, D), mod, mod]
        out_specs = [row, row]
        out_shape = [out_shape, jax.ShapeDtypeStruct((T_ALL, D), BF16)]
    return pl.pallas_call(
        functools.partial(_moe_combine_kernel, with_next=next_norm is not None),
        grid_spec=pltpu.PrefetchScalarGridSpec(
            num_scalar_prefetch=1, grid=(T_ALL // tm,), in_specs=in_specs, out_specs=out_specs,
            scratch_shapes=[pltpu.VMEM((2, tm, D), F32), pltpu.SemaphoreType.DMA(())]),
        out_shape=out_shape,
        compiler_params=_cparams(("arbitrary",)),
        name="moe_combine",
    )(*args)


def _moe(x, h, mod, layer, li, w, next_norm):
    ei, gf, cnt = _router(h, w["moe_router_w"][li], w["moe_router_b"][li])
    experts, ranks, counts = ei[:, 0:2], ei[:, 2:4], cnt[0, :N_EXP]
    tiles = (counts + MOE_TM - 1) // MOE_TM
    tile_end = jnp.cumsum(tiles)
    row0 = (tile_end - tiles) * MOE_TM
    pos = (row0[experts] + ranks).astype(jnp.int32)
    n_used = tile_end[-1:].astype(jnp.int32)
    t_idx = jnp.minimum(jnp.arange(MOE_TILES, dtype=jnp.int32), n_used[0] - 1)
    tile_expert = jnp.sum(t_idx[:, None] >= tile_end[None, :], axis=1).astype(jnp.int32)
    flat_pos = pos.T.reshape(-1)
    sorted_rows = _moe_dispatch(h, flat_pos)
    act = _moe_up(tile_expert, n_used, sorted_rows, w["moe_w_gate"], w["moe_w_up"], li, 1792)
    y = _moe_down(tile_expert, n_used, act, w["moe_w_down"], li, 1024)
    return _moe_combine(x, y, flat_pos, gf, mod, layer, next_norm)


def _odd_mixer_layer(x, h, mod, l, i, w, spectra, consts):
    p = _mm([h], [w["odd_w_in"]], tm=1024, tn=1536, wi=i)
    cw = w["hy_conv_w"][i]
    z_c = _hy_conv(p, cw, spectra[SEQ][i], w["hy_bias"][i], consts[SEQ], SEQ, BATCH, 0, D)
    z_l = _hy_conv(p, cw, spectra[DEC_SEQ][i], w["hy_bias"][i], consts[DEC_SEQ], DEC_SEQ, DEC_BATCH,
                   T_CTX // DEC_SEQ, 512)
    x, h2 = _mm([(z_c, z_l)], [w["odd_w_out"]], tm=512, tn=D, epilogue="residual_mod", wi=i,
                resid=x, mod=mod, layer=l, k_gate=2, next_norm=(w["norm_ffn_g"][l], l, 3, F32))
    next_norm = (w["norm_mix_g"][l + 1], l + 1, 0) if l + 1 < DEPTH else None
    out = _moe(x, h2, mod, l, i, w, next_norm)
    return out if next_norm is not None else (out, None)


def kernel(x_prompt, x_sample, state_gdn, cache_nat_k, cache_nat_v, c, c_ctx, ada_w, ada_b, norm_mix_g, norm_ffn_g, even_w_in, gdn_conv_w, gdn_a_log, gdn_dt_bias, gdn_norm_g, nat_q_norm_g, nat_k_norm_g, nat_rpb, even_w_out, ffn_w_gate, ffn_w_up, ffn_w_down, odd_w_in, hy_conv_w, hy_w1, hy_b1, hy_w2, hy_b2, hy_w3, hy_freq, hy_bias, odd_w_out, moe_router_w, moe_router_b, moe_w_gate, moe_w_up, moe_w_down):
    w = dict(state_gdn=state_gdn, cache_nat_k=cache_nat_k, cache_nat_v=cache_nat_v,
             norm_mix_g=norm_mix_g, norm_ffn_g=norm_ffn_g, even_w_in=even_w_in, gdn_conv_w=gdn_conv_w,
             gdn_a_log=gdn_a_log, gdn_dt_bias=gdn_dt_bias, gdn_norm_g=gdn_norm_g,
             nat_q_norm_g=nat_q_norm_g, nat_k_norm_g=nat_k_norm_g, nat_rpb=nat_rpb, even_w_out=even_w_out,
             ffn_w_gate=ffn_w_gate, ffn_w_up=ffn_w_up, ffn_w_down=ffn_w_down, odd_w_in=odd_w_in,
             hy_conv_w=hy_conv_w, hy_bias=hy_bias, odd_w_out=odd_w_out, moe_router_w=moe_router_w,
             moe_router_b=moe_router_b, moe_w_gate=moe_w_gate, moe_w_up=moe_w_up, moe_w_down=moe_w_down)
    assert SEQ == ROW_TILE and DEC_SEQ % ROW_TILE == 0
    mod = _ada_all(c, c_ctx, ada_w, ada_b)
    consts = {s: _dft_consts(s) for s in (SEQ, DEC_SEQ)}
    n_odd = DEPTH // 2
    spectra = {s: [_hy_filter_spectrum(s, consts[s], hy_w1[i], hy_b1[i], hy_w2[i], hy_b2[i], hy_w3[i], hy_freq[i])
                   for i in range(n_odd)] for s in (SEQ, DEC_SEQ)}
    x = jnp.concatenate([x_prompt.reshape(T_CTX, D), x_sample.reshape(T_LAT, D)], axis=0)
    h = _modulate(x, norm_mix_g[0], mod, 0, 0, BF16)
    states, kns, pms = [], [], []
    for l in range(DEPTH):
        i = l // 2
        if l % 2 == 0:
            x, h, kn, p_main, s_ctx = _even_mixer_layer(x, h, mod, l, i, w)
            states.append(s_ctx)
            kns.append(kn)
            pms.append(p_main)
        else:
            x, h = _odd_mixer_layer(x, h, mod, l, i, w, spectra, consts)
    k_cache, v_cache, state_out = _finalize_caches(kns, pms, states)
    y_prompt = x[:T_CTX].reshape(BATCH, SEQ, D)
    y_sample = x[T_CTX:].reshape(DEC_BATCH, DEC_SEQ, D)
    return (y_prompt, y_sample, state_out, k_cache, v_cache)
```

```python
import functools
import math

import jax
import jax.numpy as jnp
import numpy as np
from jax import lax
from jax.experimental import pallas as pl
from jax.experimental.pallas import tpu as pltpu

F32 = jnp.float32
BF16 = jnp.bfloat16

D = 1024
BATCH = 16
SEQ = 256
DEPTH = 4
DEC_BATCH = 2
DEC_SEQ = 1024
PAST_LEN = 512
GRID_W = 64
EPS = 1e-6
H_A = 4
DK_A = 128
DV_A = 128
CHUNK = 64
H_B = 8
DH_B = 64
WIN_R = 8
WIN_C = 16
A_QKV = 2 * H_A * DK_A + H_A * DV_A
A_GATE = H_A * DV_A
B_QKV = 3 * H_B * DH_B
HY_EMB = 33
HY_HID = 64
HY_FAST = 0.3
HY_SLOW = 1.5
HY_TARGET = 1e-2
D_FF = 2816
N_EXP = 8
D_FF_E = 3584

T_CTX = BATCH * SEQ
T_LAT = DEC_BATCH * DEC_SEQ
T_ALL = T_CTX + T_LAT
N_SEG = 8
ROW_TILE = 256

VMEM_LIMIT = 56 * 1024 * 1024


def _cparams(sem, vmem=None):
    return pltpu.CompilerParams(dimension_semantics=sem, vmem_limit_bytes=vmem)


def _sigmoid(x):
    return 1.0 / (1.0 + jnp.exp(-x))


def _seg_of_row(row):
    return jnp.where(row < T_CTX, 0, 1 + (row - T_CTX) // DEC_SEQ)


def _split_bf16(x):
    hi = x.astype(BF16)
    lo = (x - hi.astype(F32)).astype(BF16)
    return hi, lo


def _dot(a, b, dims=(((1,), (0,)), ((), ())), prec="bf16"):
    if prec == "bf16":
        return lax.dot_general(a.astype(BF16), b.astype(BF16), dims, preferred_element_type=F32)
    ah, al = _split_bf16(a.astype(F32))
    bh, bl = _split_bf16(b.astype(F32))
    r = lax.dot_general(ah, bh, dims, preferred_element_type=F32)
    r = r + lax.dot_general(ah, bl, dims, preferred_element_type=F32)
    r = r + lax.dot_general(al, bh, dims, preferred_element_type=F32)
    return r


NT_DIMS = (((1,), (1,)), ((), ()))


def _ada_kernel(cv_ref, w_ref, b_ref, o_ref):
    cv = cv_ref[...]
    s = cv * _sigmoid(cv)
    o_ref[0] = _dot(s, w_ref[0]) + b_ref[0]


def _ada_all(c, c_ctx, ada_w, ada_b):
    cv = jnp.zeros((N_SEG, D), F32).at[0].set(c_ctx).at[1:1 + DEC_BATCH].set(c)
    tn = 1536
    out = pl.pallas_call(
        _ada_kernel,
        grid=(DEPTH, 6 * D // tn),
        in_specs=[
            pl.BlockSpec((N_SEG, D), lambda l, j: (0, 0)),
            pl.BlockSpec((1, D, tn), lambda l, j: (l, 0, j)),
            pl.BlockSpec((1, 1, tn), lambda l, j: (l, 0, j)),
        ],
        out_specs=pl.BlockSpec((1, N_SEG, tn), lambda l, j: (l, 0, j)),
        out_shape=jax.ShapeDtypeStruct((DEPTH, N_SEG, 6 * D), F32),
        compiler_params=_cparams(("arbitrary", "arbitrary"), VMEM_LIMIT),
        name="ada",
    )(cv, ada_w, ada_b.reshape(DEPTH, 1, 6 * D))
    return out.reshape(DEPTH, N_SEG, 6, D).transpose(0, 2, 1, 3).reshape(DEPTH * 6 * N_SEG, 1, D)


def _mod_spec(layer, k, tm, row_of_step):
    base = (layer * 6 + k) * N_SEG

    def imap(*ids):
        return (base + _seg_of_row(row_of_step(*ids) * tm), 0, 0)

    return pl.BlockSpec((1, 1, D), imap)


def _modulate_math(x, g, scale, shift):
    ms = jnp.mean(x * x, axis=-1, keepdims=True)
    y = x * lax.rsqrt(ms + EPS) * g
    return y * (1.0 + scale) + shift


def _modulate_kernel(x_ref, g_ref, sc_ref, sh_ref, o_ref):
    o_ref[...] = _modulate_math(x_ref[...], g_ref[...], sc_ref[0], sh_ref[0]).astype(o_ref.dtype)


def _modulate(x, g, mod, layer, k_shift, out_dtype):
    tm = 512
    return pl.pallas_call(
        _modulate_kernel,
        grid=(T_ALL // tm,),
        in_specs=[
            pl.BlockSpec((tm, D), lambda i: (i, 0)),
            pl.BlockSpec((1, D), lambda i: (0, 0)),
            _mod_spec(layer, k_shift + 1, tm, lambda i: i),
            _mod_spec(layer, k_shift, tm, lambda i: i),
        ],
        out_specs=pl.BlockSpec((tm, D), lambda i: (i, 0)),
        out_shape=jax.ShapeDtypeStruct((T_ALL, D), out_dtype),
        compiler_params=_cparams(("arbitrary",)),
        name="modulate",
    )(x, g.reshape(1, D), mod, mod)


SWIGLU_CHUNK = 256


def _swiglu_chunks(h, wbf, o_ref):
    n = o_ref.shape[1]
    for c0 in range(0, n, SWIGLU_CHUNK):
        c1 = min(c0 + SWIGLU_CHUNK, n)
        a = jnp.dot(h, wbf[0, :, c0:c1], preferred_element_type=F32)
        b = jnp.dot(h, wbf[1, :, c0:c1], preferred_element_type=F32)
        o_ref[:, c0:c1] = (a * _sigmoid(a) * b).astype(o_ref.dtype)


EVEN_SHIFT = 4 * H_A
EVEN_SPLIT = A_QKV + A_GATE


def _mm_kernel(*refs, chunks, n_w, epilogue, shifted, n_ctx_tiles):
    pos = 0
    lhs = []
    for kw, paired in chunks:
        cnt = 2 if paired else 1
        lhs.append(refs[pos:pos + cnt])
        pos += cnt
    n_wrefs = n_w + (1 if shifted else 0)
    w_refs = refs[pos:pos + n_wrefs]
    rest = refs[pos + n_wrefs:]
    j, i = pl.program_id(0), pl.program_id(1)
    if epilogue == "residual":
        x_ref, gate_ref, o_ref, wbf = rest
    elif epilogue == "residual_mod":
        x_ref, gate_ref, g_ref, sc_ref, sh_ref, o_ref, h_ref, wbf = rest
    else:
        o_ref, wbf = rest

    @pl.when(i == 0)
    def _():
        if shifted:
            tn = wbf.shape[2]
            for jj in range(shifted):
                split = min(max(EVEN_SPLIT - jj * tn, 0), tn)

                @pl.when(j == jj)
                def _(split=split):
                    wa = w_refs[0][0]
                    if split == tn:
                        wbf[0] = wa.astype(BF16)
                    else:
                        parts = [wa[:, :split]] if split else []
                        parts += [wa[:, split + EVEN_SHIFT:], w_refs[1][0][:, :EVEN_SHIFT]]
                        wbf[0] = jnp.concatenate(parts, axis=1).astype(BF16)
        else:
            for k in range(n_w):
                wbf[k] = w_refs[k][0].astype(BF16)

    def lhs_chunk(c):
        r = lhs[c]
        if len(r) == 2:
            return jnp.where(i < n_ctx_tiles, r[0][...], r[1][...]).astype(BF16)
        return r[0][...].astype(BF16)

    def matmul(k):
        acc, off = None, 0
        for c, (kw, _) in enumerate(chunks):
            part = jnp.dot(lhs_chunk(c), wbf[k, off:off + kw, :], preferred_element_type=F32)
            acc = part if acc is None else acc + part
            off += kw
        return acc

    if epilogue == "swiglu":
        _swiglu_chunks(lhs_chunk(0), wbf, o_ref)
        return
    a = matmul(0)
    if epilogue == "residual":
        o_ref[...] = x_ref[...] + gate_ref[0] * a
    elif epilogue == "residual_mod":
        x_new = x_ref[...] + gate_ref[0] * a
        o_ref[...] = x_new
        h_ref[...] = _modulate_math(x_new, g_ref[...], sc_ref[0], sh_ref[0]).astype(h_ref.dtype)
    else:
        o_ref[...] = a.astype(o_ref.dtype)


def _mm(lhs, ws, *, tm, tn, out_dtype=F32, epilogue="none", wi=0, n_out=None, col0=0, shifted=False,
        resid=None, mod=None, layer=None, k_gate=None, next_norm=None):
    chunks, args, in_specs = [], [], []
    n_ctx_tiles = T_CTX // tm
    for part in lhs:
        if isinstance(part, tuple):
            kw = part[0].shape[1]
            chunks.append((kw, True))
            args += [part[0], part[1]]
            in_specs += [pl.BlockSpec((tm, kw), lambda j, i: (jnp.minimum(i, n_ctx_tiles - 1), 0)),
                         pl.BlockSpec((tm, kw), lambda j, i: (jnp.maximum(i - n_ctx_tiles, 0), 0))]
        else:
            kw = part.shape[1]
            chunks.append((kw, False))
            args.append(part)
            in_specs.append(pl.BlockSpec((tm, kw), lambda j, i: (i, 0)))
    kdim = sum(kw for kw, _ in chunks)
    n = n_out if n_out is not None else ws[0].shape[2]
    n_w = len(ws)
    if shifted:
        last = pl.cdiv(ws[0].shape[2], tn) - 1
        in_specs += [pl.BlockSpec((1, kdim, tn), lambda j, i: (wi, 0, j)),
                     pl.BlockSpec((1, kdim, tn), lambda j, i: (wi, 0, jnp.minimum(j + 1, last)))]
        args += [ws[0], ws[0]]
    else:
        in_specs += [pl.BlockSpec((1, kdim, tn), lambda j, i: (wi, 0, col0 + j)) for _ in ws]
        args += list(ws)
    out_specs = pl.BlockSpec((tm, tn), lambda j, i: (i, j))
    out_shape = jax.ShapeDtypeStruct((T_ALL, n), out_dtype)
    if epilogue in ("residual", "residual_mod"):
        base = (layer * 6 + k_gate) * N_SEG
        in_specs += [pl.BlockSpec((tm, tn), lambda j, i: (i, j)),
                     pl.BlockSpec((1, 1, tn), lambda j, i: (base + _seg_of_row(i * tm), 0, j))]
        args += [resid, mod]
    if epilogue == "residual_mod":
        g, n_layer, k_shift, h_dtype = next_norm
        assert tn == n == D
        in_specs += [pl.BlockSpec((1, D), lambda j, i: (0, 0)),
                     _mod_spec(n_layer, k_shift + 1, tm, lambda j, i: i),
                     _mod_spec(n_layer, k_shift, tm, lambda j, i: i)]
        args += [g.reshape(1, D), mod, mod]
        out_specs = [out_specs, pl.BlockSpec((tm, tn), lambda j, i: (i, j))]
        out_shape = [out_shape, jax.ShapeDtypeStruct((T_ALL, n), h_dtype)]
    return pl.pallas_call(
        functools.partial(_mm_kernel, chunks=tuple(chunks), n_w=n_w, epilogue=epilogue,
                          shifted=(n // tn if shifted else 0), n_ctx_tiles=n_ctx_tiles),
        grid=(n // tn, T_ALL // tm),
        in_specs=in_specs,
        out_specs=out_specs,
        out_shape=out_shape,
        scratch_shapes=[pltpu.VMEM((n_w, kdim, tn), BF16)],
        compiler_params=_cparams(("arbitrary", "arbitrary"), VMEM_LIMIT),
        name="mm_" + epilogue,
    )(*args)


def _conv3(x, prev_ref, next_ref, w, seq_len):
    i = pl.program_id(0)
    rows_n = x.shape[0]
    if seq_len is None:
        j = jnp.maximum(i - T_CTX // rows_n, 0)
        per = DEC_SEQ // rows_n
        is_ctx = i < T_CTX // rows_n
        first = is_ctx | (j % per == 0)
        last = is_ctx | (j % per == per - 1)
    else:
        first = (i * rows_n) % seq_len == 0
        last = ((i + 1) * rows_n) % seq_len == 0
    prev_row = jnp.where(first, 0.0, prev_ref[7:8, :])
    next_row = jnp.where(last, 0.0, next_ref[0:1, :])
    rows = lax.broadcasted_iota(jnp.int32, x.shape, 0)
    xm = jnp.where(rows == 0, prev_row, pltpu.roll(x, 1, 0))
    xp = jnp.where(rows == rows_n - 1, next_row, pltpu.roll(x, rows_n - 1, 0))
    return xm * w[0:1, :] + x * w[1:2, :] + xp * w[2:3, :]


def _halo_specs(width, col_block, row_off_tiles):
    per = ROW_TILE // 8
    last_blk = T_ALL // 8 - 1
    main = pl.BlockSpec((ROW_TILE, width), lambda i: (i + row_off_tiles, col_block))
    prev = pl.BlockSpec((8, width), lambda i: (jnp.maximum((i + row_off_tiles) * per - 1, 0), col_block))
    nxt = pl.BlockSpec((8, width), lambda i: (jnp.minimum((i + row_off_tiles + 1) * per, last_blk), col_block))
    return main, prev, nxt


def _cumsum_rows(x, reverse):
    n = x.shape[0]
    rows = lax.broadcasted_iota(jnp.int32, x.shape, 0)
    k = 1
    while k < n:
        if reverse:
            x = x + jnp.where(rows < n - k, pltpu.roll(x, n - k, 0), 0.0)
        else:
            x = x + jnp.where(rows >= k, pltpu.roll(x, k, 0), 0.0)
        k *= 2
    return x


def _gdn_prep_kernel(pm_ref, prev_ref, next_ref, ps_ref, cw_ref, par_ref, qkv_ref, bg_ref, *, seq_len):
    x = _conv3(pm_ref[...], prev_ref, next_ref, cw_ref[...], seq_len)
    x = x * _sigmoid(x)
    for h in range(H_A):
        sl = slice(h * DK_A, (h + 1) * DK_A)
        qh = x[:, sl]
        qkv_ref[:, sl] = qh * lax.rsqrt(jnp.sum(qh * qh, axis=-1, keepdims=True) + EPS) * (DK_A ** -0.5)
        sl = slice(H_A * DK_A + h * DK_A, H_A * DK_A + (h + 1) * DK_A)
        kh = x[:, sl]
        qkv_ref[:, sl] = kh * lax.rsqrt(jnp.sum(kh * kh, axis=-1, keepdims=True) + EPS)
    qkv_ref[:, 2 * H_A * DK_A:] = x[:, 2 * H_A * DK_A:]
    raw = ps_ref[...]
    lane = lax.broadcasted_iota(jnp.int32, raw.shape, 1)
    beta = _sigmoid(raw)
    z = raw + par_ref[1:2, :]
    softplus = jnp.maximum(z, 0.0) + jnp.log(1.0 + jnp.exp(-jnp.abs(z)))
    g = -jnp.exp(par_ref[0:1, :]) * softplus
    bg_ref[...] = jnp.where(lane < 2 * H_A, beta, jnp.where(lane < 4 * H_A, g, 0.0))


def _gdn_prep(p_main, p_small, conv_w, a_log, dt_bias, row_off_tiles, n_rows, seq_len):
    par = jnp.zeros((2, 128), F32)
    par = par.at[0, 2 * H_A:4 * H_A].set(a_log.reshape(-1)).at[1, 2 * H_A:4 * H_A].set(dt_bias.reshape(-1))
    main, prev, nxt = _halo_specs(A_QKV, 0, row_off_tiles)
    return pl.pallas_call(
        functools.partial(_gdn_prep_kernel, seq_len=seq_len),
        grid=(n_rows // ROW_TILE,),
        in_specs=[main, prev, nxt,
                  pl.BlockSpec((ROW_TILE, 128), lambda i: (i + row_off_tiles, 0)),
                  pl.BlockSpec((3, A_QKV), lambda i: (0, 0)),
                  pl.BlockSpec((2, 128), lambda i: (0, 0))],
        out_specs=[pl.BlockSpec((ROW_TILE, A_QKV), lambda i: (i, 0)),
                   pl.BlockSpec((ROW_TILE, 128), lambda i: (i, 0))],
        out_shape=[jax.ShapeDtypeStruct((n_rows, A_QKV), F32),
                   jax.ShapeDtypeStruct((n_rows, 128), F32)],
        compiler_params=_cparams(("arbitrary",)),
        name="gdn_prep",
    )(p_main, p_main, p_main, p_small, conv_w, par)


B_NN = (((2,), (1,)), ((0,), (0,)))
B_NT = (((2,), (2,)), ((0,), (0,)))
B_TN = (((1,), (1,)), ((0,), (0,)))


def _bmm(a, b, dims=B_NN):
    return _dot(a, b, dims)


def _gdn_scanb_kernel(*refs, ns, has_s0):
    if has_s0:
        qf_ref, qb_ref, bf_ref, bb_ref, s0_ref, of_ref, ob_ref, sfin_ref, st = refs
    else:
        qf_ref, qb_ref, bf_ref, bb_ref, of_ref, ob_ref, sfin_ref, st = refs
    step = pl.program_id(1)
    c = CHUNK
    nb = 2 * ns * H_A

    @pl.when(step == 0)
    def _():
        for d in range(2):
            st[d] = s0_ref[:, 0, d] if has_s0 else jnp.zeros(st.shape[1:], F32)

    q_l, k_l, v_l, beta_l, gc_l = [], [], [], [], []
    for d in range(2):
        for s in range(ns):
            qkv = (qf_ref if d == 0 else qb_ref)[s, 0]
            bg = (bf_ref if d == 0 else bb_ref)[s, 0]
            gcum = _cumsum_rows(bg, reverse=(d == 1))
            for h in range(H_A):
                q_l.append(qkv[:, h * DK_A:(h + 1) * DK_A])
                k_l.append(qkv[:, (H_A + h) * DK_A:(H_A + h + 1) * DK_A])
                v_l.append(qkv[:, 2 * H_A * DK_A + h * DV_A:2 * H_A * DK_A + (h + 1) * DV_A])
                col = d * H_A + h
                beta_l.append(bg[:, col:col + 1])
                gc_l.append(gcum[:, 2 * H_A + col:2 * H_A + col + 1])
    q, k, v = jnp.stack(q_l), jnp.stack(k_l), jnp.stack(v_l)
    beta, gc = jnp.stack(beta_l), jnp.stack(gc_l)

    bi = lax.broadcasted_iota(jnp.int32, (nb, c, c), 0)
    ri = lax.broadcasted_iota(jnp.int32, (nb, c, c), 1)
    ci = lax.broadcasted_iota(jnp.int32, (nb, c, c), 2)
    fwd = bi < nb // 2
    eye = ri == ci
    incl = (fwd & (ri >= ci)) | (jnp.logical_not(fwd) & (ri <= ci))
    strict = incl & jnp.logical_not(eye)
    gc_row = jnp.sum(jnp.where(eye, gc, 0.0), axis=1, keepdims=True)
    decay = jnp.where(incl, jnp.exp(jnp.where(incl, gc - gc_row, 0.0)), 0.0)
    kb = k * beta
    a_mat = jnp.where(strict, _bmm(kb, k, B_NT) * decay, 0.0)
    blk = 8
    diag = (ri // blk) == (ci // blk)
    pw = jnp.where(diag, a_mat, 0.0)
    r_mat = -pw
    for _ in range(2):
        pw = _bmm(pw, pw)
        r_mat = r_mat + pw + _bmm(r_mat, pw)
    while blk < c:
        off = ((ri // (2 * blk)) == (ci // (2 * blk))) & ((ri // blk) != (ci // blk))
        e_mat = jnp.where(off, a_mat, 0.0)
        x_mat = e_mat + _bmm(r_mat, e_mat)
        r_mat = r_mat - (x_mat + _bmm(x_mat, r_mat))
        blk *= 2
    eg = jnp.exp(gc)
    rhs = jnp.concatenate([v * beta, kb * eg], axis=2)
    sol = rhs + _bmm(r_mat, rhs)
    ub, wm = sol[:, :, :DV_A], sol[:, :, DV_A:]
    attn = jnp.where(incl, _bmm(q, k, B_NT) * decay, 0.0)
    fwd1 = lax.broadcasted_iota(jnp.int32, (nb, 1, 1), 0) < nb // 2
    g_end = jnp.where(fwd1, gc[:, c - 1:c, :], gc[:, 0:1, :])
    qd = q * eg
    kd = k * jnp.exp(g_end - gc)
    state = st[...].reshape(nb, DK_A, DV_A)
    u = ub - _bmm(wm, state)
    o = _bmm(qd, state) + _bmm(attn, u)
    state = state * jnp.exp(g_end) + _bmm(kd, u, B_TN)
    st[...] = state.reshape(st.shape)
    for d in range(2):
        o_ref = of_ref if d == 0 else ob_ref
        for s in range(ns):
            for h in range(H_A):
                o_ref[s, 0, :, h * DV_A:(h + 1) * DV_A] = o[(d * ns + s) * H_A + h]

    @pl.when(step == pl.num_programs(1) - 1)
    def _():
        for d in range(2):
            sfin_ref[:, d] = st[d]


def _gdn_scanb(qkv, bg, s0, layer_i, n_seq, seq_len, ns):
    n = seq_len // CHUNK
    qkv4 = qkv.reshape(n_seq, n, CHUNK, A_QKV)
    bg4 = bg.reshape(n_seq, n, CHUNK, 128)
    fwd = lambda g, c: (g, c, 0, 0)
    bwd = lambda g, c: (g, n - 1 - c, 0, 0)
    in_specs = [pl.BlockSpec((ns, 1, CHUNK, A_QKV), fwd), pl.BlockSpec((ns, 1, CHUNK, A_QKV), bwd),
                pl.BlockSpec((ns, 1, CHUNK, 128), fwd), pl.BlockSpec((ns, 1, CHUNK, 128), bwd)]
    args = [qkv4, qkv4, bg4, bg4]
    if s0 is not None:
        in_specs.append(pl.BlockSpec((ns, 1, 2, H_A, DK_A, DV_A), lambda g, c: (g, layer_i, 0, 0, 0, 0)))
        args.append(s0)
    o_shape = jax.ShapeDtypeStruct((n_seq, n, CHUNK, H_A * DV_A), F32)
    of, ob, sfin = pl.pallas_call(
        functools.partial(_gdn_scanb_kernel, ns=ns, has_s0=s0 is not None),
        grid=(n_seq // ns, n),
        in_specs=in_specs,
        out_specs=[pl.BlockSpec((ns, 1, CHUNK, H_A * DV_A), fwd),
                   pl.BlockSpec((ns, 1, CHUNK, H_A * DV_A), bwd),
                   pl.BlockSpec((ns, 2, H_A, DK_A, DV_A), lambda g, c: (g, 0, 0, 0, 0))],
        out_shape=[o_shape, o_shape, jax.ShapeDtypeStruct((n_seq, 2, H_A, DK_A, DV_A), F32)],
        scratch_shapes=[pltpu.VMEM((2, ns, H_A, DK_A, DV_A), F32)],
        compiler_params=_cparams(("arbitrary", "arbitrary"), VMEM_LIMIT),
        name="gdn_scan",
    )(*args)
    rows = n_seq * seq_len
    return of.reshape(rows, H_A * DV_A), ob.reshape(rows, H_A * DV_A), sfin


def _gdn_post_kernel(of_ref, ob_ref, gate_ref, g_ref, o_ref):
    o = of_ref[...] + ob_ref[...]
    gate = gate_ref[...]
    for h in range(H_A):
        sl = slice(h * DV_A, (h + 1) * DV_A)
        oh = o[:, sl]
        y = oh * lax.rsqrt(jnp.mean(oh * oh, axis=-1, keepdims=True) + EPS) * g_ref[...]
        gh = gate[:, sl]
        o_ref[:, sl] = (y * (gh * _sigmoid(gh))).astype(o_ref.dtype)


def _gdn_post(of, ob, p_main, gdn_g, row_off_tiles, n_rows):
    gate_blk = A_QKV // A_GATE
    return pl.pallas_call(
        _gdn_post_kernel,
        grid=(n_rows // ROW_TILE,),
        in_specs=[pl.BlockSpec((ROW_TILE, A_GATE), lambda i: (i, 0)),
                  pl.BlockSpec((ROW_TILE, A_GATE), lambda i: (i, 0)),
                  pl.BlockSpec((ROW_TILE, A_GATE), lambda i: (i + row_off_tiles, gate_blk)),
                  pl.BlockSpec((1, DV_A), lambda i: (0, 0))],
        out_specs=pl.BlockSpec((ROW_TILE, A_GATE), lambda i: (i, 0)),
        out_shape=jax.ShapeDtypeStruct((n_rows, A_GATE), BF16),
        compiler_params=_cparams(("arbitrary",)),
        name="gdn_post",
    )(of, ob, p_main, gdn_g.reshape(1, DV_A))


NAT_W = H_B * DH_B
NAT_QCOL = (A_QKV + A_GATE) // NAT_W
NAT_SCALE = DH_B ** -0.5


def _nat_prep_kernel(q_ref, k_ref, bd_ref, gq_ref, gk_ref, qn_ref, kn_ref):
    bd = bd_ref[...]
    for x_ref, g_ref, o_ref in ((q_ref, gq_ref, qn_ref), (k_ref, gk_ref, kn_ref)):
        x = x_ref[...]
        hi, lo = _split_bf16(x * x)
        ms = jnp.dot(hi, bd, preferred_element_type=F32) + jnp.dot(lo, bd, preferred_element_type=F32)
        o_ref[...] = x * lax.rsqrt(ms + EPS) * g_ref[...]


def _nat_prep(p_main, qn_g, kn_g):
    grp = np.arange(NAT_W) // DH_B
    bd = jnp.asarray((grp[:, None] == grp[None, :]).astype(np.float32) / DH_B, BF16)
    spec = lambda cb: pl.BlockSpec((ROW_TILE, NAT_W), lambda i: (i, cb))
    return pl.pallas_call(
        _nat_prep_kernel,
        grid=(T_ALL // ROW_TILE,),
        in_specs=[spec(NAT_QCOL), spec(NAT_QCOL + 1),
                  pl.BlockSpec((NAT_W, NAT_W), lambda i: (0, 0)),
                  pl.BlockSpec((1, NAT_W), lambda i: (0, 0)),
                  pl.BlockSpec((1, NAT_W), lambda i: (0, 0))],
        out_specs=[spec(0), spec(0)],
        out_shape=[jax.ShapeDtypeStruct((T_ALL, NAT_W), F32)] * 2,
        compiler_params=_cparams(("arbitrary",)),
        name="nat_prep",
    )(p_main, p_main, bd, jnp.tile(qn_g, H_B).reshape(1, NAT_W), jnp.tile(kn_g, H_B).reshape(1, NAT_W))


def _pair_masks():
    lane = lax.broadcasted_iota(jnp.int32, (1, 2 * DH_B), 1)
    return lane < DH_B


def _nat_ctx_kernel(q_ref, k_ref, v_ref, o_ref):
    lo = _pair_masks()
    for p in range(H_B // 2):
        sl = slice(p * 2 * DH_B, (p + 1) * 2 * DH_B)
        q2, k2, v2 = q_ref[:, sl], k_ref[:, sl], v_ref[:, sl]
        halves = []
        for half in range(2):
            qm = jnp.where(lo if half == 0 else jnp.logical_not(lo), q2, 0.0)
            s = _dot(qm, k2, NT_DIMS) * NAT_SCALE
            e = jnp.exp(s - jnp.max(s, axis=-1, keepdims=True))
            pr = e / jnp.sum(e, axis=-1, keepdims=True)
            halves.append(_dot(pr, v2))
        o_ref[:, sl] = jnp.where(lo, halves[0], halves[1]).astype(o_ref.dtype)


def _nat_ctx(qn, kn, p_main):
    spec = lambda cb: pl.BlockSpec((SEQ, NAT_W), lambda b: (b, cb))
    return pl.pallas_call(
        _nat_ctx_kernel,
        grid=(BATCH,),
        in_specs=[spec(0), spec(0), spec(NAT_QCOL + 2)],
        out_specs=spec(0),
        out_shape=jax.ShapeDtypeStruct((T_CTX, NAT_W), BF16),
        compiler_params=_cparams(("arbitrary",)),
        name="nat_ctx",
    )(qn, kn, p_main)


def _nat_bias_kernel(r_ref, e_ref, ok_ref, o_ref):
    r = r_ref[...]
    hi, lo = _split_bf16(r)
    lo2 = (r - hi.astype(F32) - lo.astype(F32)).astype(BF16)
    e = e_ref[...]
    t = (jnp.dot(hi, e, preferred_element_type=F32) + jnp.dot(lo, e, preferred_element_type=F32)
         + jnp.dot(lo2, e, preferred_element_type=F32))
    o_ref[...] = jnp.where(ok_ref[...] > 0.5, t, -jnp.inf)


def _nat_bias(rpb):
    n_dr, n_dc = 2 * WIN_R - 1, 2 * WIN_C - 1
    qc = np.arange(GRID_W)[:, None]
    kc = np.arange(GRID_W)[None, :]
    dc = (kc - qc + WIN_C - 1).reshape(-1)
    c0 = np.clip(qc - WIN_C // 2, 0, GRID_W - WIN_C)
    ok = ((kc >= c0) & (kc < c0 + WIN_C)).reshape(1, -1).astype(np.float32)
    onehot = (np.arange(128)[:, None] == dc[None, :]).astype(np.float32)
    rows = H_B * n_dr
    rp = jnp.zeros((rows, 128), F32).at[:, :n_dc].set(rpb.reshape(rows, n_dc))
    tab = pl.pallas_call(
        _nat_bias_kernel,
        out_shape=jax.ShapeDtypeStruct((rows, GRID_W * GRID_W), F32),
        name="nat_bias",
    )(rp, jnp.asarray(onehot, BF16), jnp.asarray(ok))
    tab = tab.reshape(H_B, n_dr, GRID_W, GRID_W)
    pad = jnp.full((H_B, 1, GRID_W, GRID_W), -jnp.inf, F32)
    ext = jnp.concatenate([pad, tab, pad], axis=1)
    return jnp.concatenate([ext[:, :n_dr + 1], ext[:, 1:]], axis=-1)


NAT_QROWS = 2
NAT_KROWS = 10


def _nat_lat_kernel(q_ref, k_ref, v_ref, kc_ref, vc_ref, bias_ref, o_ref):
    rows = DEC_SEQ // GRID_W
    row_a = pl.program_id(1) * NAT_QROWS
    r0s = [jnp.clip(row_a + qi - WIN_R // 2, 0, rows - WIN_R) for qi in range(NAT_QROWS)]
    ws = jnp.minimum(r0s[0], rows - NAT_KROWS)
    start = pl.multiple_of(ws * GRID_W, GRID_W)
    n_loc = NAT_KROWS * GRID_W
    lo = _pair_masks()
    lane = lax.broadcasted_iota(jnp.int32, (GRID_W, 2 * GRID_W), 1)

    def bias_for(h):
        blocks = []
        for qi in range(NAT_QROWS):
            pieces = []
            for jp in range(NAT_KROWS // 2):
                rk = ws + 2 * jp
                ok0 = ((rk >= r0s[qi]) & (rk < r0s[qi] + WIN_R)).astype(jnp.int32)
                ok1 = ((rk + 1 >= r0s[qi]) & (rk + 1 < r0s[qi] + WIN_R)).astype(jnp.int32)
                d = jnp.clip(rk - (row_a + qi) + WIN_R - 1, -1, 2 * WIN_R - 2) + 1
                piece = bias_ref[h, d]
                pieces.append(jnp.where(jnp.where(lane < GRID_W, ok0, ok1) > 0, piece, -jnp.inf))
            blocks.append(jnp.concatenate(pieces, axis=1))
        return jnp.concatenate(blocks, axis=0)

    for p in range(H_B // 2):
        sl = slice(p * 2 * DH_B, (p + 1) * 2 * DH_B)
        q2 = q_ref[:, sl]
        kw = k_ref[0, pl.ds(start, n_loc), sl]
        vw = v_ref[pl.ds(start, n_loc), sl]
        kc, vc = kc_ref[0, :, sl], vc_ref[0, :, sl]
        halves = []
        for half in range(2):
            qm = jnp.where(lo if half == 0 else jnp.logical_not(lo), q2, 0.0)
            s_loc = _dot(qm, kw, NT_DIMS) * NAT_SCALE + bias_for(2 * p + half)
            s_ctx = _dot(qm, kc, NT_DIMS) * NAT_SCALE
            m = jnp.maximum(jnp.max(s_loc, axis=-1, keepdims=True), jnp.max(s_ctx, axis=-1, keepdims=True))
            e_loc, e_ctx = jnp.exp(s_loc - m), jnp.exp(s_ctx - m)
            inv = 1.0 / (jnp.sum(e_loc, axis=-1, keepdims=True) + jnp.sum(e_ctx, axis=-1, keepdims=True))
            halves.append(_dot(e_loc * inv, vw) + _dot(e_ctx * inv, vc))
        o_ref[:, sl] = jnp.where(lo, halves[0], halves[1]).astype(o_ref.dtype)


def _nat_lat(qn, kn, p_main, kc, vc, bias):
    steps = DEC_SEQ // GRID_W // NAT_QROWS
    tq = NAT_QROWS * GRID_W
    lat_tile0 = T_CTX // tq
    lat_seq0 = T_CTX // DEC_SEQ
    return pl.pallas_call(
        _nat_lat_kernel,
        grid=(DEC_BATCH, steps),
        in_specs=[pl.BlockSpec((tq, NAT_W), lambda b, r: (lat_tile0 + b * steps + r, 0)),
                  pl.BlockSpec((1, DEC_SEQ, NAT_W), lambda b, r: (lat_seq0 + b, 0, 0)),
                  pl.BlockSpec((DEC_SEQ, NAT_W), lambda b, r: (lat_seq0 + b, NAT_QCOL + 2)),
                  pl.BlockSpec((1, PAST_LEN, NAT_W), lambda b, r: (b, 0, 0)),
                  pl.BlockSpec((1, PAST_LEN, NAT_W), lambda b, r: (b, 0, 0)),
                  pl.BlockSpec(bias.shape, lambda b, r: (0, 0, 0, 0))],
        out_specs=pl.BlockSpec((tq, NAT_W), lambda b, r: (b * steps + r, 0)),
        out_shape=jax.ShapeDtypeStruct((T_LAT, NAT_W), BF16),
        compiler_params=_cparams(("arbitrary", "arbitrary"), VMEM_LIMIT),
        name="nat_lat",
    )(qn, kn.reshape(T_ALL // DEC_SEQ, DEC_SEQ, NAT_W), p_main, kc, vc, bias)


def _heads_to_lanes(cache):
    b, h, l, dh = cache.shape
    return cache.transpose(0, 2, 1, 3).reshape(b, l, h * dh)


def _finalize_kernel(k0_ref, v0_ref, s0_ref, k1_ref, v1_ref, s1_ref, ko_ref, vo_ref, so_ref):
    layer = pl.program_id(0)
    for idx, (k_ref, v_ref, s_ref) in enumerate(((k0_ref, v0_ref, s0_ref), (k1_ref, v1_ref, s1_ref))):
        @pl.when(layer == idx)
        def _(k_ref=k_ref, v_ref=v_ref, s_ref=s_ref):
            for h in range(H_B):
                ko_ref[0, 0, h] = k_ref[:, h * DH_B:(h + 1) * DH_B]
                vo_ref[0, 0, h] = v_ref[:, h * DH_B:(h + 1) * DH_B]
            so_ref[0, 0] = s_ref[0]


def _finalize_caches(kns, pms, states):
    n_even = len(kns)
    assert n_even == 2
    tok = lambda cb: pl.BlockSpec((SEQ, NAT_W), lambda l, b: (b, cb))
    st_in = pl.BlockSpec((1, 2, H_A, DK_A, DV_A), lambda l, b: (b, 0, 0, 0, 0))
    cache_out = pl.BlockSpec((1, 1, H_B, SEQ, DH_B), lambda l, b: (b, l, 0, 0, 0))
    cache_shape = jax.ShapeDtypeStruct((BATCH, n_even, H_B, SEQ, DH_B), F32)
    return pl.pallas_call(
        _finalize_kernel,
        grid=(n_even, BATCH),
        in_specs=[tok(0), tok(NAT_QCOL + 2), st_in, tok(0), tok(NAT_QCOL + 2), st_in],
        out_specs=[cache_out, cache_out,
                   pl.BlockSpec((1, 1, 2, H_A, DK_A, DV_A), lambda l, b: (b, l, 0, 0, 0, 0))],
        out_shape=[cache_shape, cache_shape,
                   jax.ShapeDtypeStruct((BATCH, n_even, 2, H_A, DK_A, DV_A), F32)],
        compiler_params=_cparams(("arbitrary", "arbitrary")),
        name="finalize_caches",
    )(kns[0], pms[0], states[0], kns[1], pms[1], states[1])


def _even_mixer_layer(x, h, mod, l, i, w):
    p_main = _mm([h], [w["even_w_in"]], tm=1024, tn=1792, wi=i, n_out=A_QKV + A_GATE + B_QKV, shifted=True)
    p_small = _mm([h], [w["even_w_in"]], tm=1024, tn=128, wi=i, n_out=128, col0=(A_QKV + A_GATE) // 128)

    conv_w, a_log, dt_bias = w["gdn_conv_w"][i], w["gdn_a_log"][i], w["gdn_dt_bias"][i]
    ctx_tiles = T_CTX // ROW_TILE
    qkv_c, bg_c = _gdn_prep(p_main, p_small, conv_w, a_log, dt_bias, 0, T_CTX, SEQ)
    qkv_l, bg_l = _gdn_prep(p_main, p_small, conv_w, a_log, dt_bias, ctx_tiles, T_LAT, DEC_SEQ)
    of_c, ob_c, s_ctx = _gdn_scanb(qkv_c, bg_c, None, i, BATCH, SEQ, 8)
    of_l, ob_l, _ = _gdn_scanb(qkv_l, bg_l, w["state_gdn"], i, DEC_BATCH, DEC_SEQ, 2)
    oa_c = _gdn_post(of_c, ob_c, p_main, w["gdn_norm_g"][i], 0, T_CTX)
    oa_l = _gdn_post(of_l, ob_l, p_main, w["gdn_norm_g"][i], ctx_tiles, T_LAT)

    qn, kn = _nat_prep(p_main, w["nat_q_norm_g"][i], w["nat_k_norm_g"][i])
    ob_c = _nat_ctx(qn, kn, p_main)
    bias = _nat_bias(w["nat_rpb"][i])
    kc = _heads_to_lanes(w["cache_nat_k"][:, i])
    vc = _heads_to_lanes(w["cache_nat_v"][:, i])
    ob_l = _nat_lat(qn, kn, p_main, kc, vc, bias)

    x, h2 = _mm([(oa_c, oa_l), (ob_c, ob_l)], [w["even_w_out"]], tm=512, tn=D, epilogue="residual_mod", wi=i,
                resid=x, mod=mod, layer=l, k_gate=2, next_norm=(w["norm_ffn_g"][l], l, 3, BF16))
    act = _mm([h2], [w["ffn_w_gate"], w["ffn_w_up"]], tm=512, tn=1408, out_dtype=BF16, epilogue="swiglu", wi=i)
    if l + 1 < DEPTH:
        x, h_next = _mm([act], [w["ffn_w_down"]], tm=512, tn=D, epilogue="residual_mod", wi=i,
                        resid=x, mod=mod, layer=l, k_gate=5, next_norm=(w["norm_mix_g"][l + 1], l + 1, 0, BF16))
    else:
        x = _mm([act], [w["ffn_w_down"]], tm=512, tn=512, epilogue="residual", wi=i,
                resid=x, mod=mod, layer=l, k_gate=5)
        h_next = None
    return x, h_next, kn, p_main, s_ctx


def _dft_consts(seq):
    n = 2 * seq
    k = np.arange(seq)[:, None]
    s = np.arange(seq)[None, :]
    ang = 2.0 * np.pi * ((k * s) % n) / n
    fr = np.cos(ang)
    fi = -np.sin(ang)
    fi[0, :] = np.cos(np.pi * (np.arange(seq) % 2))
    fm = np.concatenate([fr, fi], axis=0)

    def to_bf16(a):
        return jnp.asarray(a.astype(np.float32)).astype(BF16)

    cw = np.full((n, 1), 2.0 / n)
    cw[0, 0] = cw[seq, 0] = 1.0 / n
    sg = np.ones((n, 1))
    sg[seq + 1:, 0] = -1.0
    cs = np.zeros((n, 128), np.float32)
    cs[:, 0:1] = cw
    cs[:, 1:2] = cw * sg
    return to_bf16(fm), to_bf16(fm.T.copy()), jnp.asarray(cs)


def _dft_apply(m, x):
    return jnp.dot(m, x.astype(BF16), preferred_element_type=F32)


def _hy_filter_kernel(z_ref, w1_ref, b1_ref, w2_ref, b2_ref, fq_ref, w3_ref, t_ref, dl_ref,
                      fm_ref, cs_ref, o_ref, hh_scr):
    c, d = pl.program_id(0), pl.program_id(1)

    @pl.when((c == 0) & (d == 0))
    def _():
        fq = fq_ref[...]
        hh = jnp.sin(fq * (_dot(z_ref[...], w1_ref[...], prec="bf16x3") + b1_ref[...]))
        hh_scr[...] = jnp.sin(fq * (_dot(hh, w2_ref[...], prec="bf16x3") + b2_ref[...]))

    filt = _dot(hh_scr[...], w3_ref[...], prec="bf16x3") * jnp.exp(-t_ref[...] * dl_ref[...])
    rows = lax.broadcasted_iota(jnp.int32, filt.shape, 0)
    filt = jnp.where((d == 1) & (rows == 0), 0.0, filt)
    spec = _dft_apply(fm_ref[...], filt)

    @pl.when(d == 0)
    def _():
        o_ref[...] = spec * cs_ref[:, 0:1]

    @pl.when(d == 1)
    def _():
        o_ref[...] = o_ref[...] + spec * cs_ref[:, 1:2]


def _hy_filter_spectrum(seq, consts, w1, b1, w2, b2, w3, freq):
    fm, _, cs = consts
    bands = (HY_EMB - 1) // 2
    t = np.linspace(0.0, 1.0, seq, dtype=np.float32)[:, None]
    wv = (np.float32(2.0 * math.pi / seq) * np.arange(seq, dtype=np.float32))[:, None]
    f = np.linspace(1e-4, bands - 1, bands, dtype=np.float32)[None, :]
    z = np.zeros((seq, 128), np.float32)
    z[:, 0:1] = t
    z[:, 1:1 + bands] = np.cos(f * wv)
    z[:, 1 + bands:HY_EMB] = -np.sin(f * wv)
    deltas = np.abs(np.linspace(math.log(HY_TARGET) / HY_FAST, math.log(HY_TARGET) / HY_SLOW, D,
                                dtype=np.float32))[None, :]
    w1p = jnp.zeros((128, HY_HID), F32).at[:HY_EMB].set(w1)
    tc = 256
    n = 2 * seq
    full = lambda shape: pl.BlockSpec(shape, lambda c, d: tuple(0 for _ in shape))
    return pl.pallas_call(
        _hy_filter_kernel,
        grid=(D // tc, 2),
        in_specs=[full((seq, 128)), full((128, HY_HID)), full((1, HY_HID)), full((HY_HID, HY_HID)),
                  full((1, HY_HID)), full((1, HY_HID)),
                  pl.BlockSpec((HY_HID, tc), lambda c, d: (0, d * (D // tc) + c)),
                  full((seq, 1)), pl.BlockSpec((1, tc), lambda c, d: (0, c)),
                  full((n, seq)), full((n, 128))],
        out_specs=pl.BlockSpec((n, tc), lambda c, d: (0, c)),
        out_shape=jax.ShapeDtypeStruct((n, D), F32),
        scratch_shapes=[pltpu.VMEM((seq, HY_HID), F32)],
        compiler_params=_cparams(("arbitrary", "arbitrary"), VMEM_LIMIT),
        name="hy_filter",
    )(jnp.asarray(z), w1p, b1.reshape(1, -1), w2, b2.reshape(1, -1), freq.reshape(1, -1), w3,
      jnp.asarray(t), jnp.asarray(deltas), fm, cs)


def _hy_conv_kernel(p0_ref, p1_ref, pv_ref, cw0_ref, cw1_ref, cwv_ref, kf_ref, bias_ref, fm_ref, ft_ref,
                    o_ref, *, seq):
    rows = lax.broadcasted_iota(jnp.int32, p0_ref.shape, 0)

    def conv3(x_ref, w_ref):
        x, w = x_ref[...], w_ref[...]
        xm = jnp.where(rows == 0, 0.0, pltpu.roll(x, 1, 0))
        xp = jnp.where(rows == seq - 1, 0.0, pltpu.roll(x, seq - 1, 0))
        return xm * w[0:1, :] + x * w[1:2, :] + xp * w[2:3, :]

    x0 = conv3(p0_ref, cw0_ref)
    u = conv3(pv_ref, cwv_ref) * conv3(p1_ref, cw1_ref)
    xs = _dft_apply(fm_ref[...], u)
    kf = kf_ref[...]
    xr, xi = xs[:seq], xs[seq:]
    kr, ki = kf[:seq], kf[seq:]
    row0 = lax.broadcasted_iota(jnp.int32, xr.shape, 0) == 0
    xiki = xi * ki
    yr = xr * kr - jnp.where(row0, 0.0, xiki)
    yi = jnp.where(row0, xiki, xr * ki + xi * kr)
    y = _dft_apply(ft_ref[...], jnp.concatenate([yr, yi], axis=0))
    o_ref[...] = ((y + u * bias_ref[...]) * x0).astype(o_ref.dtype)


def _hy_conv(p, conv_w, kf, bias, consts, seq, n_seq, seq_blk0, tc):
    fm, ft, _ = consts
    n = 2 * seq
    nc = D // tc
    full = lambda shape: pl.BlockSpec(shape, lambda b, c: (0, 0))
    grp = lambda g: pl.BlockSpec((seq, tc), lambda b, c: (seq_blk0 + b, g * nc + c))
    cwg = lambda g: pl.BlockSpec((3, tc), lambda b, c: (0, g * nc + c))
    return pl.pallas_call(
        functools.partial(_hy_conv_kernel, seq=seq),
        grid=(n_seq, nc),
        in_specs=[grp(0), grp(1), grp(2), cwg(0), cwg(1), cwg(2),
                  pl.BlockSpec((n, tc), lambda b, c: (0, c)),
                  pl.BlockSpec((1, tc), lambda b, c: (0, c)),
                  full((n, seq)), full((seq, n))],
        out_specs=pl.BlockSpec((seq, tc), lambda b, c: (b, c)),
        out_shape=jax.ShapeDtypeStruct((n_seq * seq, D), BF16),
        compiler_params=_cparams(("arbitrary", "arbitrary"), VMEM_LIMIT),
        name="hy_conv",
    )(p, p, p, conv_w, conv_w, conv_w, kf, bias.reshape(1, D), fm, ft)


MOE_TM = 256
MOE_TILES = 2 * T_ALL // MOE_TM + N_EXP
MOE_ROWS = MOE_TILES * MOE_TM
ROUTE_TM = 512
DISPATCH_TM = 512
COMBINE_TM = 256


def _router_kernel(h_ref, rw_ref, rb_ref, tri_ref, ei_ref, gf_ref, cnt_ref, carry):
    @pl.when(pl.program_id(0) == 0)
    def _():
        carry[...] = jnp.zeros(carry.shape, F32)

    logits = _dot(h_ref[...], rw_ref[...], prec="bf16x3") + rb_ref[...]
    lane = lax.broadcasted_iota(jnp.int32, logits.shape, 1)
    logits = jnp.where(lane < N_EXP, logits, -jnp.inf)
    m1 = jnp.max(logits, axis=-1, keepdims=True)
    i1 = jnp.min(jnp.where(logits == m1, lane, 128), axis=-1, keepdims=True)
    rest = jnp.where(lane == i1, -jnp.inf, logits)
    m2 = jnp.max(rest, axis=-1, keepdims=True)
    i2 = jnp.min(jnp.where(rest == m2, lane, 128), axis=-1, keepdims=True)
    e = jnp.exp(m2 - m1)
    g1 = 1.0 / (1.0 + e)
    g2 = e * g1
    pick = jnp.where((lane == i1) | (lane == i2), 1.0, 0.0)
    before = carry[...] + jnp.dot(tri_ref[...], pick.astype(BF16), preferred_element_type=F32)
    r1 = jnp.sum(jnp.where(lane == i1, before, 0.0), axis=-1, keepdims=True)
    r2 = jnp.sum(jnp.where(lane == i2, before, 0.0), axis=-1, keepdims=True)
    carry[...] = carry[...] + jnp.sum(pick, axis=0, keepdims=True)
    ints = jnp.where(lane == 0, i1, jnp.where(lane == 1, i2, 0))
    ranks = jnp.where(lane == 2, r1, jnp.where(lane == 3, r2, 0.0))
    ei_ref[...] = ints + ranks.astype(jnp.int32)
    gf_ref[...] = jnp.where(lane == 0, g1, jnp.where(lane == 1, g2, 0.0))
    cnt_ref[...] = carry[...].astype(jnp.int32)


def _router(h, router_w, router_b):
    rw = jnp.zeros((D, 128), F32).at[:, :N_EXP].set(router_w)
    rb = jnp.zeros((1, 128), F32).at[0, :N_EXP].set(router_b)
    tri = jnp.asarray(np.tril(np.ones((ROUTE_TM, ROUTE_TM), np.float32), -1), BF16)
    row = pl.BlockSpec((ROUTE_TM, 128), lambda i: (i, 0))
    return pl.pallas_call(
        _router_kernel,
        grid=(T_ALL // ROUTE_TM,),
        in_specs=[pl.BlockSpec((ROUTE_TM, D), lambda i: (i, 0)),
                  pl.BlockSpec((D, 128), lambda i: (0, 0)),
                  pl.BlockSpec((1, 128), lambda i: (0, 0)),
                  pl.BlockSpec((ROUTE_TM, ROUTE_TM), lambda i: (0, 0))],
        out_specs=[row, row, pl.BlockSpec((1, 128), lambda i: (0, 0))],
        out_shape=[jax.ShapeDtypeStruct((T_ALL, 128), jnp.int32),
                   jax.ShapeDtypeStruct((T_ALL, 128), F32),
                   jax.ShapeDtypeStruct((1, 128), jnp.int32)],
        scratch_shapes=[pltpu.VMEM((1, 128), F32)],
        compiler_params=_cparams(("arbitrary",)),
        name="moe_router",
    )(h, rw, rb, tri)


def _moe_dispatch_kernel(pos_ref, h_ref, init_ref, out_ref, sem):
    del init_ref
    base = pl.program_id(0) * DISPATCH_TM

    def copy(r, p):
        return pltpu.make_async_copy(h_ref.at[pl.ds(r, 1)], out_ref.at[pl.ds(p, 1)], sem)

    def start(r, c):
        copy(r, pos_ref[base + r]).start()
        copy(r, pos_ref[T_ALL + base + r]).start()
        return c

    def wait(r, c):
        copy(0, 0).wait()
        copy(0, 0).wait()
        return c

    lax.fori_loop(0, DISPATCH_TM, start, 0, unroll=8)
    lax.fori_loop(0, DISPATCH_TM, wait, 0, unroll=8)


def _moe_dispatch(h, flat_pos):
    any_spec = pl.BlockSpec(memory_space=pl.ANY)
    return pl.pallas_call(
        _moe_dispatch_kernel,
        grid_spec=pltpu.PrefetchScalarGridSpec(
            num_scalar_prefetch=1, grid=(T_ALL // DISPATCH_TM,),
            in_specs=[pl.BlockSpec((DISPATCH_TM, D), lambda i, pos: (i, 0)), any_spec],
            out_specs=any_spec,
            scratch_shapes=[pltpu.SemaphoreType.DMA(())]),
        out_shape=jax.ShapeDtypeStruct((MOE_ROWS, D), F32),
        input_output_aliases={2: 0},
        compiler_params=_cparams(("arbitrary",)),
        name="moe_dispatch",
    )(flat_pos, h, jnp.zeros((MOE_ROWS, D), F32))


def _moe_up_kernel(te_ref, nu_ref, g_ref, wg_ref, wu_ref, o_ref, wbf):
    i = pl.program_id(1)
    new_w = (i == 0) | (te_ref[i] != te_ref[jnp.maximum(i - 1, 0)])

    @pl.when(new_w)
    def _():
        wbf[0] = wg_ref[0, 0].astype(BF16)
        wbf[1] = wu_ref[0, 0].astype(BF16)

    @pl.when(i < nu_ref[0])
    def _():
        _swiglu_chunks(g_ref[...].astype(BF16), wbf, o_ref)

    @pl.when(i >= nu_ref[0])
    def _():
        o_ref[...] = jnp.zeros(o_ref.shape, o_ref.dtype)


def _moe_up(tile_expert, n_used, rows, wg, wu, li, tn):
    wspec = pl.BlockSpec((1, 1, D, tn), lambda j, i, te, nu: (li, te[i], 0, j))
    return pl.pallas_call(
        _moe_up_kernel,
        grid_spec=pltpu.PrefetchScalarGridSpec(
            num_scalar_prefetch=2, grid=(D_FF_E // tn, MOE_TILES),
            in_specs=[pl.BlockSpec((MOE_TM, D), lambda j, i, te, nu: (i, 0)), wspec, wspec],
            out_specs=pl.BlockSpec((MOE_TM, tn), lambda j, i, te, nu: (i, j)),
            scratch_shapes=[pltpu.VMEM((2, D, tn), BF16)]),
        out_shape=jax.ShapeDtypeStruct((MOE_ROWS, D_FF_E), BF16),
        compiler_params=_cparams(("arbitrary", "arbitrary"), VMEM_LIMIT),
        name="moe_up",
    )(tile_expert, n_used, rows, wg, wu)


def _moe_down_kernel(te_ref, nu_ref, a_ref, wd_ref, o_ref, wbf):
    i = pl.program_id(1)
    new_w = (i == 0) | (te_ref[i] != te_ref[jnp.maximum(i - 1, 0)])

    @pl.when(new_w)
    def _():
        wbf[...] = wd_ref[0, 0].astype(BF16)

    @pl.when(i < nu_ref[0])
    def _():
        o_ref[...] = jnp.dot(a_ref[...], wbf[...], preferred_element_type=F32)

    @pl.when(i >= nu_ref[0])
    def _():
        o_ref[...] = jnp.zeros(o_ref.shape, o_ref.dtype)


def _moe_down(tile_expert, n_used, act, wd, li, tn):
    return pl.pallas_call(
        _moe_down_kernel,
        grid_spec=pltpu.PrefetchScalarGridSpec(
            num_scalar_prefetch=2, grid=(D // tn, MOE_TILES),
            in_specs=[pl.BlockSpec((MOE_TM, D_FF_E), lambda j, i, te, nu: (i, 0)),
                      pl.BlockSpec((1, 1, D_FF_E, tn), lambda j, i, te, nu: (li, te[i], 0, j))],
            out_specs=pl.BlockSpec((MOE_TM, tn), lambda j, i, te, nu: (i, j)),
            scratch_shapes=[pltpu.VMEM((D_FF_E, tn), BF16)]),
        out_shape=jax.ShapeDtypeStruct((MOE_ROWS, D), F32),
        compiler_params=_cparams(("arbitrary", "arbitrary"), VMEM_LIMIT),
        name="moe_down",
    )(tile_expert, n_used, act, wd)


def _moe_combine_kernel(pos_ref, x_ref, y_ref, gf_ref, gate_ref, *rest, with_next):
    if with_next:
        g_ref, sc_ref, sh_ref, o_ref, h_ref, ybuf, sem = rest
    else:
        o_ref, ybuf, sem = rest
    base = pl.program_id(0) * COMBINE_TM

    def copy(slot, r, p):
        return pltpu.make_async_copy(y_ref.at[pl.ds(p, 1)], ybuf.at[slot, pl.ds(r, 1)], sem)

    def start(r, c):
        copy(0, r, pos_ref[base + r]).start()
        copy(1, r, pos_ref[T_ALL + base + r]).start()
        return c

    def wait(r, c):
        copy(0, 0, 0).wait()
        copy(1, 0, 0).wait()
        return c

    lax.fori_loop(0, COMBINE_TM, start, 0, unroll=8)
    lax.fori_loop(0, COMBINE_TM, wait, 0, unroll=8)
    gf = gf_ref[...]
    f = gf[:, 0:1] * ybuf[0] + gf[:, 1:2] * ybuf[1]
    x_new = x_ref[...] + gate_ref[0] * f
    o_ref[...] = x_new
    if with_next:
        h_ref[...] = _modulate_math(x_new, g_ref[...], sc_ref[0], sh_ref[0]).astype(h_ref.dtype)


def _moe_combine(x, y, flat_pos, gf, mod, layer, next_norm):
    tm = COMBINE_TM
    base = (layer * 6 + 5) * N_SEG
    row = pl.BlockSpec((tm, D), lambda i, pos: (i, 0))
    in_specs = [row, pl.BlockSpec(memory_space=pl.ANY),
                pl.BlockSpec((tm, 128), lambda i, pos: (i, 0)),
                pl.BlockSpec((1, 1, D), lambda i, pos: (base + _seg_of_row(i * tm), 0, 0))]
    args = [flat_pos, x, y, gf, mod]
    out_specs, out_shape = row, jax.ShapeDtypeStruct((T_ALL, D), F32)
    if next_norm is not None:
        g, n_layer, k_shift = next_norm
        in_specs += [pl.BlockSpec((1, D), lambda i, pos: (0, 0)),
                     _mod_spec(n_layer, k_shift + 1, tm, lambda i, pos: i),
                     _mod_spec(n_layer, k_shift, tm, lambda i, pos: i)]
        args += [g.reshape(1, D), mod, mod]
        out_specs = [row, row]
        out_shape = [out_shape, jax.ShapeDtypeStruct((T_ALL, D), BF16)]
    return pl.pallas_call(
        functools.partial(_moe_combine_kernel, with_next=next_norm is not None),
        grid_spec=pltpu.PrefetchScalarGridSpec(
            num_scalar_prefetch=1, grid=(T_ALL // tm,), in_specs=in_specs, out_specs=out_specs,
            scratch_shapes=[pltpu.VMEM((2, tm, D), F32), pltpu.SemaphoreType.DMA(())]),
        out_shape=out_shape,
        compiler_params=_cparams(("arbitrary",)),
        name="moe_combine",
    )(*args)


def _moe(x, h, mod, layer, li, w, next_norm):
    ei, gf, cnt = _router(h, w["moe_router_w"][li], w["moe_router_b"][li])
    experts, ranks, counts = ei[:, 0:2], ei[:, 2:4], cnt[0, :N_EXP]
    tiles = (counts + MOE_TM - 1) // MOE_TM
    tile_end = jnp.cumsum(tiles)
    row0 = (tile_end - tiles) * MOE_TM
    pos = (row0[experts] + ranks).astype(jnp.int32)
    n_used = tile_end[-1:].astype(jnp.int32)
    t_idx = jnp.minimum(jnp.arange(MOE_TILES, dtype=jnp.int32), n_used[0] - 1)
    tile_expert = jnp.sum(t_idx[:, None] >= tile_end[None, :], axis=1).astype(jnp.int32)
    flat_pos = pos.T.reshape(-1)
    sorted_rows = _moe_dispatch(h, flat_pos)
    act = _moe_up(tile_expert, n_used, sorted_rows, w["moe_w_gate"], w["moe_w_up"], li, 1792)
    y = _moe_down(tile_expert, n_used, act, w["moe_w_down"], li, 1024)
    return _moe_combine(x, y, flat_pos, gf, mod, layer, next_norm)


def _odd_mixer_layer(x, h, mod, l, i, w, spectra, consts):
    p = _mm([h], [w["odd_w_in"]], tm=1024, tn=1536, wi=i)
    cw = w["hy_conv_w"][i]
    z_c = _hy_conv(p, cw, spectra[SEQ][i], w["hy_bias"][i], consts[SEQ], SEQ, BATCH, 0, D)
    z_l = _hy_conv(p, cw, spectra[DEC_SEQ][i], w["hy_bias"][i], consts[DEC_SEQ], DEC_SEQ, DEC_BATCH,
                   T_CTX // DEC_SEQ, 512)
    x, h2 = _mm([(z_c, z_l)], [w["odd_w_out"]], tm=512, tn=D, epilogue="residual_mod", wi=i,
                resid=x, mod=mod, layer=l, k_gate=2, next_norm=(w["norm_ffn_g"][l], l, 3, F32))
    next_norm = (w["norm_mix_g"][l + 1], l + 1, 0) if l + 1 < DEPTH else None
    out = _moe(x, h2, mod, l, i, w, next_norm)
    return out if next_norm is not None else (out, None)


def kernel(x_prompt, x_sample, state_gdn, cache_nat_k, cache_nat_v, c, c_ctx, ada_w, ada_b, norm_mix_g, norm_ffn_g, even_w_in, gdn_conv_w, gdn_a_log, gdn_dt_bias, gdn_norm_g, nat_q_norm_g, nat_k_norm_g, nat_rpb, even_w_out, ffn_w_gate, ffn_w_up, ffn_w_down, odd_w_in, hy_conv_w, hy_w1, hy_b1, hy_w2, hy_b2, hy_w3, hy_freq, hy_bias, odd_w_out, moe_router_w, moe_router_b, moe_w_gate, moe_w_up, moe_w_down):
    w = dict(state_gdn=state_gdn, cache_nat_k=cache_nat_k, cache_nat_v=cache_nat_v,
             norm_mix_g=norm_mix_g, norm_ffn_g=norm_ffn_g, even_w_in=even_w_in, gdn_conv_w=gdn_conv_w,
             gdn_a_log=gdn_a_log, gdn_dt_bias=gdn_dt_bias, gdn_norm_g=gdn_norm_g,
             nat_q_norm_g=nat_q_norm_g, nat_k_norm_g=nat_k_norm_g, nat_rpb=nat_rpb, even_w_out=even_w_out,
             ffn_w_gate=ffn_w_gate, ffn_w_up=ffn_w_up, ffn_w_down=ffn_w_down, odd_w_in=odd_w_in,
             hy_conv_w=hy_conv_w, hy_bias=hy_bias, odd_w_out=odd_w_out, moe_router_w=moe_router_w,
             moe_router_b=moe_router_b, moe_w_gate=moe_w_gate, moe_w_up=moe_w_up, moe_w_down=moe_w_down)
    assert SEQ == ROW_TILE and DEC_SEQ % ROW_TILE == 0
    mod = _ada_all(c, c_ctx, ada_w, ada_b)
    consts = {s: _dft_consts(s) for s in (SEQ, DEC_SEQ)}
    n_odd = DEPTH // 2
    spectra = {s: [_hy_filter_spectrum(s, consts[s], hy_w1[i], hy_b1[i], hy_w2[i], hy_b2[i], hy_w3[i], hy_freq[i])
                   for i in range(n_odd)] for s in (SEQ, DEC_SEQ)}
    x = jnp.concatenate([x_prompt.reshape(T_CTX, D), x_sample.reshape(T_LAT, D)], axis=0)
    h = _modulate(x, norm_mix_g[0], mod, 0, 0, BF16)
    states, kns, pms = [], [], []
    for l in range(DEPTH):
        i = l // 2
        if l % 2 == 0:
            x, h, kn, p_main, s_ctx = _even_mixer_layer(x, h, mod, l, i, w)
            states.append(s_ctx)
            kns.append(kn)
            pms.append(p_main)
        else:
            x, h = _odd_mixer_layer(x, h, mod, l, i, w, spectra, consts)
    k_cache, v_cache, state_out = _finalize_caches(kns, pms, states)
    y_prompt = x[:T_CTX].reshape(BATCH, SEQ, D)
    y_sample = x[T_CTX:].reshape(DEC_BATCH, DEC_SEQ, D)
    return (y_prompt, y_sample, state_out, k_cache, v_cache)
```

```python
import functools
import math

import jax
import jax.numpy as jnp
import numpy as np
from jax import lax
from jax.experimental import pallas as pl
from jax.experimental.pallas import tpu as pltpu

F32 = jnp.float32
BF16 = jnp.bfloat16

D = 1024
BATCH = 16
SEQ = 256
DEPTH = 4
DEC_BATCH = 2
DEC_SEQ = 1024
PAST_LEN = 512
GRID_W = 64
EPS = 1e-6
H_A = 4
DK_A = 128
DV_A = 128
CHUNK = 64
H_B = 8
DH_B = 64
WIN_R = 8
WIN_C = 16
A_QKV = 2 * H_A * DK_A + H_A * DV_A
A_GATE = H_A * DV_A
B_QKV = 3 * H_B * DH_B
HY_EMB = 33
HY_HID = 64
HY_FAST = 0.3
HY_SLOW = 1.5
HY_TARGET = 1e-2
D_FF = 2816
N_EXP = 8
D_FF_E = 3584

T_CTX = BATCH * SEQ
T_LAT = DEC_BATCH * DEC_SEQ
T_ALL = T_CTX + T_LAT
N_SEG = 8
ROW_TILE = 256

VMEM_LIMIT = 56 * 1024 * 1024


def _cparams(sem, vmem=None):
    return pltpu.CompilerParams(dimension_semantics=sem, vmem_limit_bytes=vmem)


def _sigmoid(x):
    return 1.0 / (1.0 + jnp.exp(-x))


def _seg_of_row(row):
    return jnp.where(row < T_CTX, 0, 1 + (row - T_CTX) // DEC_SEQ)


def _split_bf16(x):
    hi = x.astype(BF16)
    lo = (x - hi.astype(F32)).astype(BF16)
    return hi, lo


def _dot(a, b, dims=(((1,), (0,)), ((), ())), prec="bf16"):
    if prec == "bf16":
        return lax.dot_general(a.astype(BF16), b.astype(BF16), dims, preferred_element_type=F32)
    ah, al = _split_bf16(a.astype(F32))
    bh, bl = _split_bf16(b.astype(F32))
    r = lax.dot_general(ah, bh, dims, preferred_element_type=F32)
    r = r + lax.dot_general(ah, bl, dims, preferred_element_type=F32)
    r = r + lax.dot_general(al, bh, dims, preferred_element_type=F32)
    return r


NT_DIMS = (((1,), (1,)), ((), ()))


def _ada_kernel(cv_ref, w_ref, b_ref, o_ref):
    cv = cv_ref[...]
    s = cv * _sigmoid(cv)
    o_ref[0] = _dot(s, w_ref[0]) + b_ref[0]


def _ada_all(c, c_ctx, ada_w, ada_b):
    cv = jnp.zeros((N_SEG, D), F32).at[0].set(c_ctx).at[1:1 + DEC_BATCH].set(c)
    tn = 1536
    out = pl.pallas_call(
        _ada_kernel,
        grid=(DEPTH, 6 * D // tn),
        in_specs=[
            pl.BlockSpec((N_SEG, D), lambda l, j: (0, 0)),
            pl.BlockSpec((1, D, tn), lambda l, j: (l, 0, j)),
            pl.BlockSpec((1, 1, tn), lambda l, j: (l, 0, j)),
        ],
        out_specs=pl.BlockSpec((1, N_SEG, tn), lambda l, j: (l, 0, j)),
        out_shape=jax.ShapeDtypeStruct((DEPTH, N_SEG, 6 * D), F32),
        compiler_params=_cparams(("arbitrary", "arbitrary"), VMEM_LIMIT),
        name="ada",
    )(cv, ada_w, ada_b.reshape(DEPTH, 1, 6 * D))
    return out.reshape(DEPTH, N_SEG, 6, D).transpose(0, 2, 1, 3).reshape(DEPTH * 6 * N_SEG, 1, D)


def _mod_spec(layer, k, tm, row_of_step):
    base = (layer * 6 + k) * N_SEG

    def imap(*ids):
        return (base + _seg_of_row(row_of_step(*ids) * tm), 0, 0)

    return pl.BlockSpec((1, 1, D), imap)


def _modulate_math(x, g, scale, shift):
    ms = jnp.mean(x * x, axis=-1, keepdims=True)
    y = x * lax.rsqrt(ms + EPS) * g
    return y * (1.0 + scale) + shift


def _modulate_kernel(x_ref, g_ref, sc_ref, sh_ref, o_ref):
    o_ref[...] = _modulate_math(x_ref[...], g_ref[...], sc_ref[0], sh_ref[0]).astype(o_ref.dtype)


def _modulate(x, g, mod, layer, k_shift, out_dtype):
    tm = 512
    return pl.pallas_call(
        _modulate_kernel,
        grid=(T_ALL // tm,),
        in_specs=[
            pl.BlockSpec((tm, D), lambda i: (i, 0)),
            pl.BlockSpec((1, D), lambda i: (0, 0)),
            _mod_spec(layer, k_shift + 1, tm, lambda i: i),
            _mod_spec(layer, k_shift, tm, lambda i: i),
        ],
        out_specs=pl.BlockSpec((tm, D), lambda i: (i, 0)),
        out_shape=jax.ShapeDtypeStruct((T_ALL, D), out_dtype),
        compiler_params=_cparams(("arbitrary",)),
        name="modulate",
    )(x, g.reshape(1, D), mod, mod)


SWIGLU_CHUNK = 256


def _swiglu_chunks(h, wbf, o_ref):
    n = o_ref.shape[1]
    for c0 in range(0, n, SWIGLU_CHUNK):
        c1 = min(c0 + SWIGLU_CHUNK, n)
        a = jnp.dot(h, wbf[0, :, c0:c1], preferred_element_type=F32)
        b = jnp.dot(h, wbf[1, :, c0:c1], preferred_element_type=F32)
        o_ref[:, c0:c1] = (a * _sigmoid(a) * b).astype(o_ref.dtype)


EVEN_SHIFT = 4 * H_A
EVEN_SPLIT = A_QKV + A_GATE


def _mm_kernel(*refs, chunks, n_w, epilogue, shifted, n_ctx_tiles):
    pos = 0
    lhs = []
    for kw, paired in chunks:
        cnt = 2 if paired else 1
        lhs.append(refs[pos:pos + cnt])
        pos += cnt
    n_wrefs = n_w + (1 if shifted else 0)
    w_refs = refs[pos:pos + n_wrefs]
    rest = refs[pos + n_wrefs:]
    j, i = pl.program_id(0), pl.program_id(1)
    if epilogue == "residual":
        x_ref, gate_ref, o_ref, wbf = rest
    elif epilogue == "residual_mod":
        x_ref, gate_ref, g_ref, sc_ref, sh_ref, o_ref, h_ref, wbf = rest
    else:
        o_ref, wbf = rest

    @pl.when(i == 0)
    def _():
        if shifted:
            tn = wbf.shape[2]
            for jj in range(shifted):
                split = min(max(EVEN_SPLIT - jj * tn, 0), tn)

                @pl.when(j == jj)
                def _(split=split):
                    wa = w_refs[0][0]
                    if split == tn:
                        wbf[0] = wa.astype(BF16)
                    else:
                        parts = [wa[:, :split]] if split else []
                        parts += [wa[:, split + EVEN_SHIFT:], w_refs[1][0][:, :EVEN_SHIFT]]
                        wbf[0] = jnp.concatenate(parts, axis=1).astype(BF16)
        else:
            for k in range(n_w):
                wbf[k] = w_refs[k][0].astype(BF16)

    def lhs_chunk(c):
        r = lhs[c]
        if len(r) == 2:
            return jnp.where(i < n_ctx_tiles, r[0][...], r[1][...]).astype(BF16)
        return r[0][...].astype(BF16)

    def matmul(k):
        acc, off = None, 0
        for c, (kw, _) in enumerate(chunks):
            part = jnp.dot(lhs_chunk(c), wbf[k, off:off + kw, :], preferred_element_type=F32)
            acc = part if acc is None else acc + part
            off += kw
        return acc

    if epilogue == "swiglu":
        _swiglu_chunks(lhs_chunk(0), wbf, o_ref)
        return
    a = matmul(0)
    if epilogue == "residual":
        o_ref[...] = x_ref[...] + gate_ref[0] * a
    elif epilogue == "residual_mod":
        x_new = x_ref[...] + gate_ref[0] * a
        o_ref[...] = x_new
        h_ref[...] = _modulate_math(x_new, g_ref[...], sc_ref[0], sh_ref[0]).astype(h_ref.dtype)
    else:
        o_ref[...] = a.astype(o_ref.dtype)


def _mm(lhs, ws, *, tm, tn, out_dtype=F32, epilogue="none", wi=0, n_out=None, col0=0, shifted=False,
        resid=None, mod=None, layer=None, k_gate=None, next_norm=None):
    chunks, args, in_specs = [], [], []
    n_ctx_tiles = T_CTX // tm
    for part in lhs:
        if isinstance(part, tuple):
            kw = part[0].shape[1]
            chunks.append((kw, True))
            args += [part[0], part[1]]
            in_specs += [pl.BlockSpec((tm, kw), lambda j, i: (jnp.minimum(i, n_ctx_tiles - 1), 0)),
                         pl.BlockSpec((tm, kw), lambda j, i: (jnp.maximum(i - n_ctx_tiles, 0), 0))]
        else:
            kw = part.shape[1]
            chunks.append((kw, False))
            args.append(part)
            in_specs.append(pl.BlockSpec((tm, kw), lambda j, i: (i, 0)))
    kdim = sum(kw for kw, _ in chunks)
    n = n_out if n_out is not None else ws[0].shape[2]
    n_w = len(ws)
    if shifted:
        last = pl.cdiv(ws[0].shape[2], tn) - 1
        in_specs += [pl.BlockSpec((1, kdim, tn), lambda j, i: (wi, 0, j)),
                     pl.BlockSpec((1, kdim, tn), lambda j, i: (wi, 0, jnp.minimum(j + 1, last)))]
        args += [ws[0], ws[0]]
    else:
        in_specs += [pl.BlockSpec((1, kdim, tn), lambda j, i: (wi, 0, col0 + j)) for _ in ws]
        args += list(ws)
    out_specs = pl.BlockSpec((tm, tn), lambda j, i: (i, j))
    out_shape = jax.ShapeDtypeStruct((T_ALL, n), out_dtype)
    if epilogue in ("residual", "residual_mod"):
        base = (layer * 6 + k_gate) * N_SEG
        in_specs += [pl.BlockSpec((tm, tn), lambda j, i: (i, j)),
                     pl.BlockSpec((1, 1, tn), lambda j, i: (base + _seg_of_row(i * tm), 0, j))]
        args += [resid, mod]
    if epilogue == "residual_mod":
        g, n_layer, k_shift, h_dtype = next_norm
        assert tn == n == D
        in_specs += [pl.BlockSpec((1, D), lambda j, i: (0, 0)),
                     _mod_spec(n_layer, k_shift + 1, tm, lambda j, i: i),
                     _mod_spec(n_layer, k_shift, tm, lambda j, i: i)]
        args += [g.reshape(1, D), mod, mod]
        out_specs = [out_specs, pl.BlockSpec((tm, tn), lambda j, i: (i, j))]
        out_shape = [out_shape, jax.ShapeDtypeStruct((T_ALL, n), h_dtype)]
    return pl.pallas_call(
        functools.partial(_mm_kernel, chunks=tuple(chunks), n_w=n_w, epilogue=epilogue,
                          shifted=(n // tn if shifted else 0), n_ctx_tiles=n_ctx_tiles),
        grid=(n // tn, T_ALL // tm),
        in_specs=in_specs,
        out_specs=out_specs,
        out_shape=out_shape,
        scratch_shapes=[pltpu.VMEM((n_w, kdim, tn), BF16)],
        compiler_params=_cparams(("arbitrary", "arbitrary"), VMEM_LIMIT),
        name="mm_" + epilogue,
    )(*args)


def _conv3(x, prev_ref, next_ref, w, seq_len):
    i = pl.program_id(0)
    rows_n = x.shape[0]
    if seq_len is None:
        j = jnp.maximum(i - T_CTX // rows_n, 0)
        per = DEC_SEQ // rows_n
        is_ctx = i < T_CTX // rows_n
        first = is_ctx | (j % per == 0)
        last = is_ctx | (j % per == per - 1)
    else:
        first = (i * rows_n) % seq_len == 0
        last = ((i + 1) * rows_n) % seq_len == 0
    prev_row = jnp.where(first, 0.0, prev_ref[7:8, :])
    next_row = jnp.where(last, 0.0, next_ref[0:1, :])
    rows = lax.broadcasted_iota(jnp.int32, x.shape, 0)
    xm = jnp.where(rows == 0, prev_row, pltpu.roll(x, 1, 0))
    xp = jnp.where(rows == rows_n - 1, next_row, pltpu.roll(x, rows_n - 1, 0))
    return xm * w[0:1, :] + x * w[1:2, :] + xp * w[2:3, :]


def _halo_specs(width, col_block, row_off_tiles):
    per = ROW_TILE // 8
    last_blk = T_ALL // 8 - 1
    main = pl.BlockSpec((ROW_TILE, width), lambda i: (i + row_off_tiles, col_block))
    prev = pl.BlockSpec((8, width), lambda i: (jnp.maximum((i + row_off_tiles) * per - 1, 0), col_block))
    nxt = pl.BlockSpec((8, width), lambda i: (jnp.minimum((i + row_off_tiles + 1) * per, last_blk), col_block))
    return main, prev, nxt


def _cumsum_rows(x, reverse):
    n = x.shape[0]
    rows = lax.broadcasted_iota(jnp.int32, x.shape, 0)
    k = 1
    while k < n:
        if reverse:
            x = x + jnp.where(rows < n - k, pltpu.roll(x, n - k, 0), 0.0)
        else:
            x = x + jnp.where(rows >= k, pltpu.roll(x, k, 0), 0.0)
        k *= 2
    return x


def _gdn_prep_kernel(pm_ref, prev_ref, next_ref, ps_ref, cw_ref, par_ref, qkv_ref, bg_ref, *, seq_len):
    x = _conv3(pm_ref[...], prev_ref, next_ref, cw_ref[...], seq_len)
    x = x * _sigmoid(x)
    for h in range(H_A):
        sl = slice(h * DK_A, (h + 1) * DK_A)
        qh = x[:, sl]
        qkv_ref[:, sl] = qh * lax.rsqrt(jnp.sum(qh * qh, axis=-1, keepdims=True) + EPS) * (DK_A ** -0.5)
        sl = slice(H_A * DK_A + h * DK_A, H_A * DK_A + (h + 1) * DK_A)
        kh = x[:, sl]
        qkv_ref[:, sl] = kh * lax.rsqrt(jnp.sum(kh * kh, axis=-1, keepdims=True) + EPS)
    qkv_ref[:, 2 * H_A * DK_A:] = x[:, 2 * H_A * DK_A:]
    raw = ps_ref[...]
    lane = lax.broadcasted_iota(jnp.int32, raw.shape, 1)
    beta = _sigmoid(raw)
    z = raw + par_ref[1:2, :]
    softplus = jnp.maximum(z, 0.0) + jnp.log(1.0 + jnp.exp(-jnp.abs(z)))
    g = -jnp.exp(par_ref[0:1, :]) * softplus
    bg_ref[...] = jnp.where(lane < 2 * H_A, beta, jnp.where(lane < 4 * H_A, g, 0.0))


def _gdn_prep(p_main, p_small, conv_w, a_log, dt_bias, row_off_tiles, n_rows, seq_len):
    par = jnp.zeros((2, 128), F32)
    par = par.at[0, 2 * H_A:4 * H_A].set(a_log.reshape(-1)).at[1, 2 * H_A:4 * H_A].set(dt_bias.reshape(-1))
    main, prev, nxt = _halo_specs(A_QKV, 0, row_off_tiles)
    return pl.pallas_call(
        functools.partial(_gdn_prep_kernel, seq_len=seq_len),
        grid=(n_rows // ROW_TILE,),
        in_specs=[main, prev, nxt,
                  pl.BlockSpec((ROW_TILE, 128), lambda i: (i + row_off_tiles, 0)),
                  pl.BlockSpec((3, A_QKV), lambda i: (0, 0)),
                  pl.BlockSpec((2, 128), lambda i: (0, 0))],
        out_specs=[pl.BlockSpec((ROW_TILE, A_QKV), lambda i: (i, 0)),
                   pl.BlockSpec((ROW_TILE, 128), lambda i: (i, 0))],
        out_shape=[jax.ShapeDtypeStruct((n_rows, A_QKV), F32),
                   jax.ShapeDtypeStruct((n_rows, 128), F32)],
        compiler_params=_cparams(("arbitrary",)),
        name="gdn_prep",
    )(p_main, p_main, p_main, p_small, conv_w, par)


B_NN = (((2,), (1,)), ((0,), (0,)))
B_NT = (((2,), (2,)), ((0,), (0,)))
B_TN = (((1,), (1,)), ((0,), (0,)))


def _bmm(a, b, dims=B_NN):
    return _dot(a, b, dims)


def _gdn_scanb_kernel(*refs, ns, has_s0):
    if has_s0:
        qf_ref, qb_ref, bf_ref, bb_ref, s0_ref, of_ref, ob_ref, sfin_ref, st = refs
    else:
        qf_ref, qb_ref, bf_ref, bb_ref, of_ref, ob_ref, sfin_ref, st = refs
    step = pl.program_id(1)
    c = CHUNK
    nb = 2 * ns * H_A

    @pl.when(step == 0)
    def _():
        for d in range(2):
            st[d] = s0_ref[:, 0, d] if has_s0 else jnp.zeros(st.shape[1:], F32)

    q_l, k_l, v_l, beta_l, gc_l = [], [], [], [], []
    for d in range(2):
        for s in range(ns):
            qkv = (qf_ref if d == 0 else qb_ref)[s, 0]
            bg = (bf_ref if d == 0 else bb_ref)[s, 0]
            gcum = _cumsum_rows(bg, reverse=(d == 1))
            for h in range(H_A):
                q_l.append(qkv[:, h * DK_A:(h + 1) * DK_A])
                k_l.append(qkv[:, (H_A + h) * DK_A:(H_A + h + 1) * DK_A])
                v_l.append(qkv[:, 2 * H_A * DK_A + h * DV_A:2 * H_A * DK_A + (h + 1) * DV_A])
                col = d * H_A + h
                beta_l.append(bg[:, col:col + 1])
                gc_l.append(gcum[:, 2 * H_A + col:2 * H_A + col + 1])
    q, k, v = jnp.stack(q_l), jnp.stack(k_l), jnp.stack(v_l)
    beta, gc = jnp.stack(beta_l), jnp.stack(gc_l)

    ri = lax.broadcasted_iota(jnp.int32, (1, c, c), 1)
    ci = lax.broadcasted_iota(jnp.int32, (1, c, c), 2)
    eye = ri == ci
    half = nb // 2
    gc_row = jnp.sum(jnp.where(eye, gc, 0.0), axis=1, keepdims=True)
    dgc = gc - gc_row

    def masked_exp(x, keep):
        return jnp.where(keep, jnp.exp(jnp.where(keep, x, 0.0)), 0.0)

    decay = jnp.concatenate([masked_exp(dgc[:half], ri >= ci), masked_exp(dgc[half:], ri <= ci)], axis=0)
    kb = k * beta
    a_mat = jnp.where(eye, 0.0, _bmm(kb, k, B_NT) * decay)
    blk = 8
    diag = (ri // blk) == (ci // blk)
    pw = jnp.where(diag, a_mat, 0.0)
    r_mat = -pw
    for _ in range(2):
        pw = _bmm(pw, pw)
        r_mat = r_mat + pw + _bmm(r_mat, pw)
    while blk < c:
        off = ((ri // (2 * blk)) == (ci // (2 * blk))) & ((ri // blk) != (ci // blk))
        e_mat = jnp.where(off, a_mat, 0.0)
        x_mat = e_mat + _bmm(r_mat, e_mat)
        r_mat = r_mat - (x_mat + _bmm(x_mat, r_mat))
        blk *= 2
    eg = jnp.exp(gc)
    rhs = jnp.concatenate([v * beta, kb * eg], axis=2)
    sol = rhs + _bmm(r_mat, rhs)
    ub, wm = sol[:, :, :DV_A], sol[:, :, DV_A:]
    attn = _bmm(q, k, B_NT) * decay
    fwd1 = lax.broadcasted_iota(jnp.int32, (nb, 1, 1), 0) < nb // 2
    g_end = jnp.where(fwd1, gc[:, c - 1:c, :], gc[:, 0:1, :])
    qd = q * eg
    kd = k * jnp.exp(g_end - gc)
    state = st[...].reshape(nb, DK_A, DV_A)
    u = ub - _bmm(wm, state)
    o = _bmm(qd, state) + _bmm(attn, u)
    state = state * jnp.exp(g_end) + _bmm(kd, u, B_TN)
    st[...] = state.reshape(st.shape)
    for d in range(2):
        o_ref = of_ref if d == 0 else ob_ref
        for s in range(ns):
            for h in range(H_A):
                o_ref[s, 0, :, h * DV_A:(h + 1) * DV_A] = o[(d * ns + s) * H_A + h]

    @pl.when(step == pl.num_programs(1) - 1)
    def _():
        for d in range(2):
            sfin_ref[:, d] = st[d]


def _gdn_scanb(qkv, bg, s0, layer_i, n_seq, seq_len, ns):
    n = seq_len // CHUNK
    qkv4 = qkv.reshape(n_seq, n, CHUNK, A_QKV)
    bg4 = bg.reshape(n_seq, n, CHUNK, 128)
    fwd = lambda g, c: (g, c, 0, 0)
    bwd = lambda g, c: (g, n - 1 - c, 0, 0)
    in_specs = [pl.BlockSpec((ns, 1, CHUNK, A_QKV), fwd), pl.BlockSpec((ns, 1, CHUNK, A_QKV), bwd),
                pl.BlockSpec((ns, 1, CHUNK, 128), fwd), pl.BlockSpec((ns, 1, CHUNK, 128), bwd)]
    args = [qkv4, qkv4, bg4, bg4]
    if s0 is not None:
        in_specs.append(pl.BlockSpec((ns, 1, 2, H_A, DK_A, DV_A), lambda g, c: (g, layer_i, 0, 0, 0, 0)))
        args.append(s0)
    o_shape = jax.ShapeDtypeStruct((n_seq, n, CHUNK, H_A * DV_A), F32)
    of, ob, sfin = pl.pallas_call(
        functools.partial(_gdn_scanb_kernel, ns=ns, has_s0=s0 is not None),
        grid=(n_seq // ns, n),
        in_specs=in_specs,
        out_specs=[pl.BlockSpec((ns, 1, CHUNK, H_A * DV_A), fwd),
                   pl.BlockSpec((ns, 1, CHUNK, H_A * DV_A), bwd),
                   pl.BlockSpec((ns, 2, H_A, DK_A, DV_A), lambda g, c: (g, 0, 0, 0, 0))],
        out_shape=[o_shape, o_shape, jax.ShapeDtypeStruct((n_seq, 2, H_A, DK_A, DV_A), F32)],
        scratch_shapes=[pltpu.VMEM((2, ns, H_A, DK_A, DV_A), F32)],
        compiler_params=_cparams(("arbitrary", "arbitrary"), VMEM_LIMIT),
        name="gdn_scan",
    )(*args)
    rows = n_seq * seq_len
    return of.reshape(rows, H_A * DV_A), ob.reshape(rows, H_A * DV_A), sfin


def _gdn_post_kernel(of_ref, ob_ref, gate_ref, g_ref, o_ref):
    o = of_ref[...] + ob_ref[...]
    gate = gate_ref[...]
    for h in range(H_A):
        sl = slice(h * DV_A, (h + 1) * DV_A)
        oh = o[:, sl]
        y = oh * lax.rsqrt(jnp.mean(oh * oh, axis=-1, keepdims=True) + EPS) * g_ref[...]
        gh = gate[:, sl]
        o_ref[:, sl] = (y * (gh * _sigmoid(gh))).astype(o_ref.dtype)


def _gdn_post(of, ob, p_main, gdn_g, row_off_tiles, n_rows):
    gate_blk = A_QKV // A_GATE
    return pl.pallas_call(
        _gdn_post_kernel,
        grid=(n_rows // ROW_TILE,),
        in_specs=[pl.BlockSpec((ROW_TILE, A_GATE), lambda i: (i, 0)),
                  pl.BlockSpec((ROW_TILE, A_GATE), lambda i: (i, 0)),
                  pl.BlockSpec((ROW_TILE, A_GATE), lambda i: (i + row_off_tiles, gate_blk)),
                  pl.BlockSpec((1, DV_A), lambda i: (0, 0))],
        out_specs=pl.BlockSpec((ROW_TILE, A_GATE), lambda i: (i, 0)),
        out_shape=jax.ShapeDtypeStruct((n_rows, A_GATE), BF16),
        compiler_params=_cparams(("arbitrary",)),
        name="gdn_post",
    )(of, ob, p_main, gdn_g.reshape(1, DV_A))


NAT_W = H_B * DH_B
NAT_QCOL = (A_QKV + A_GATE) // NAT_W
NAT_SCALE = DH_B ** -0.5


def _nat_prep_kernel(q_ref, k_ref, bd_ref, gq_ref, gk_ref, qn_ref, kn_ref):
    bd = bd_ref[...]
    for x_ref, g_ref, o_ref in ((q_ref, gq_ref, qn_ref), (k_ref, gk_ref, kn_ref)):
        x = x_ref[...]
        hi, lo = _split_bf16(x * x)
        ms = jnp.dot(hi, bd, preferred_element_type=F32) + jnp.dot(lo, bd, preferred_element_type=F32)
        o_ref[...] = x * lax.rsqrt(ms + EPS) * g_ref[...]


def _nat_prep(p_main, qn_g, kn_g):
    grp = np.arange(NAT_W) // DH_B
    bd = jnp.asarray((grp[:, None] == grp[None, :]).astype(np.float32) / DH_B, BF16)
    spec = lambda cb: pl.BlockSpec((ROW_TILE, NAT_W), lambda i: (i, cb))
    return pl.pallas_call(
        _nat_prep_kernel,
        grid=(T_ALL // ROW_TILE,),
        in_specs=[spec(NAT_QCOL), spec(NAT_QCOL + 1),
                  pl.BlockSpec((NAT_W, NAT_W), lambda i: (0, 0)),
                  pl.BlockSpec((1, NAT_W), lambda i: (0, 0)),
                  pl.BlockSpec((1, NAT_W), lambda i: (0, 0))],
        out_specs=[spec(0), spec(0)],
        out_shape=[jax.ShapeDtypeStruct((T_ALL, NAT_W), F32)] * 2,
        compiler_params=_cparams(("arbitrary",)),
        name="nat_prep",
    )(p_main, p_main, bd, jnp.tile(qn_g, H_B).reshape(1, NAT_W), jnp.tile(kn_g, H_B).reshape(1, NAT_W))


def _pair_masks():
    lane = lax.broadcasted_iota(jnp.int32, (1, 2 * DH_B), 1)
    return lane < DH_B


def _nat_ctx_kernel(q_ref, k_ref, v_ref, o_ref):
    lo = _pair_masks()
    for p in range(H_B // 2):
        sl = slice(p * 2 * DH_B, (p + 1) * 2 * DH_B)
        q2, k2, v2 = q_ref[:, sl], k_ref[:, sl], v_ref[:, sl]
        halves = []
        for half in range(2):
            qm = jnp.where(lo if half == 0 else jnp.logical_not(lo), q2, 0.0)
            s = _dot(qm, k2, NT_DIMS) * NAT_SCALE
            e = jnp.exp(s - jnp.max(s, axis=-1, keepdims=True))
            pr = e / jnp.sum(e, axis=-1, keepdims=True)
            halves.append(_dot(pr, v2))
        o_ref[:, sl] = jnp.where(lo, halves[0], halves[1]).astype(o_ref.dtype)


def _nat_ctx(qn, kn, p_main):
    spec = lambda cb: pl.BlockSpec((SEQ, NAT_W), lambda b: (b, cb))
    return pl.pallas_call(
        _nat_ctx_kernel,
        grid=(BATCH,),
        in_specs=[spec(0), spec(0), spec(NAT_QCOL + 2)],
        out_specs=spec(0),
        out_shape=jax.ShapeDtypeStruct((T_CTX, NAT_W), BF16),
        compiler_params=_cparams(("arbitrary",)),
        name="nat_ctx",
    )(qn, kn, p_main)


def _nat_bias_kernel(r_ref, e_ref, ok_ref, o_ref):
    r = r_ref[...]
    hi, lo = _split_bf16(r)
    lo2 = (r - hi.astype(F32) - lo.astype(F32)).astype(BF16)
    e = e_ref[...]
    t = (jnp.dot(hi, e, preferred_element_type=F32) + jnp.dot(lo, e, preferred_element_type=F32)
         + jnp.dot(lo2, e, preferred_element_type=F32))
    o_ref[...] = jnp.where(ok_ref[...] > 0.5, t, -jnp.inf)


def _nat_bias(rpb):
    n_dr, n_dc = 2 * WIN_R - 1, 2 * WIN_C - 1
    qc = np.arange(GRID_W)[:, None]
    kc = np.arange(GRID_W)[None, :]
    dc = (kc - qc + WIN_C - 1).reshape(-1)
    c0 = np.clip(qc - WIN_C // 2, 0, GRID_W - WIN_C)
    ok = ((kc >= c0) & (kc < c0 + WIN_C)).reshape(1, -1).astype(np.float32)
    onehot = (np.arange(128)[:, None] == dc[None, :]).astype(np.float32)
    rows = H_B * n_dr
    rp = jnp.zeros((rows, 128), F32).at[:, :n_dc].set(rpb.reshape(rows, n_dc))
    tab = pl.pallas_call(
        _nat_bias_kernel,
        out_shape=jax.ShapeDtypeStruct((rows, GRID_W * GRID_W), F32),
        name="nat_bias",
    )(rp, jnp.asarray(onehot, BF16), jnp.asarray(ok))
    tab = tab.reshape(H_B, n_dr, GRID_W, GRID_W)
    pad = jnp.full((H_B, 1, GRID_W, GRID_W), -jnp.inf, F32)
    ext = jnp.concatenate([pad, tab, pad], axis=1)
    return jnp.concatenate([ext[:, :n_dr + 1], ext[:, 1:]], axis=-1)


NAT_QROWS = 2
NAT_KROWS = 10


def _nat_lat_kernel(q_ref, k_ref, v_ref, kc_ref, vc_ref, bias_ref, o_ref):
    rows = DEC_SEQ // GRID_W
    row_a = pl.program_id(1) * NAT_QROWS
    r0s = [jnp.clip(row_a + qi - WIN_R // 2, 0, rows - WIN_R) for qi in range(NAT_QROWS)]
    ws = jnp.minimum(r0s[0], rows - NAT_KROWS)
    start = pl.multiple_of(ws * GRID_W, GRID_W)
    n_loc = NAT_KROWS * GRID_W
    lo = _pair_masks()
    lane = lax.broadcasted_iota(jnp.int32, (GRID_W, 2 * GRID_W), 1)

    def bias_for(h):
        blocks = []
        for qi in range(NAT_QROWS):
            pieces = []
            for jp in range(NAT_KROWS // 2):
                rk = ws + 2 * jp
                ok0 = ((rk >= r0s[qi]) & (rk < r0s[qi] + WIN_R)).astype(jnp.int32)
                ok1 = ((rk + 1 >= r0s[qi]) & (rk + 1 < r0s[qi] + WIN_R)).astype(jnp.int32)
                d = jnp.clip(rk - (row_a + qi) + WIN_R - 1, -1, 2 * WIN_R - 2) + 1
                piece = bias_ref[h, d]
                pieces.append(jnp.where(jnp.where(lane < GRID_W, ok0, ok1) > 0, piece, -jnp.inf))
            blocks.append(jnp.concatenate(pieces, axis=1))
        return jnp.concatenate(blocks, axis=0)

    for p in range(H_B // 2):
        sl = slice(p * 2 * DH_B, (p + 1) * 2 * DH_B)
        q2 = q_ref[:, sl]
        kw = k_ref[0, pl.ds(start, n_loc), sl]
        vw = v_ref[pl.ds(start, n_loc), sl]
        kc, vc = kc_ref[0, :, sl], vc_ref[0, :, sl]
        halves = []
        for half in range(2):
            qm = jnp.where(lo if half == 0 else jnp.logical_not(lo), q2, 0.0)
            s_loc = _dot(qm, kw, NT_DIMS) * NAT_SCALE + bias_for(2 * p + half)
            s_ctx = _dot(qm, kc, NT_DIMS) * NAT_SCALE
            m = jnp.maximum(jnp.max(s_loc, axis=-1, keepdims=True), jnp.max(s_ctx, axis=-1, keepdims=True))
            e_loc, e_ctx = jnp.exp(s_loc - m), jnp.exp(s_ctx - m)
            inv = 1.0 / (jnp.sum(e_loc, axis=-1, keepdims=True) + jnp.sum(e_ctx, axis=-1, keepdims=True))
            halves.append(_dot(e_loc * inv, vw) + _dot(e_ctx * inv, vc))
        o_ref[:, sl] = jnp.where(lo, halves[0], halves[1]).astype(o_ref.dtype)


def _nat_lat(qn, kn, p_main, kc, vc, bias):
    steps = DEC_SEQ // GRID_W // NAT_QROWS
    tq = NAT_QROWS * GRID_W
    lat_tile0 = T_CTX // tq
    lat_seq0 = T_CTX // DEC_SEQ
    return pl.pallas_call(
        _nat_lat_kernel,
        grid=(DEC_BATCH, steps),
        in_specs=[pl.BlockSpec((tq, NAT_W), lambda b, r: (lat_tile0 + b * steps + r, 0)),
                  pl.BlockSpec((1, DEC_SEQ, NAT_W), lambda b, r: (lat_seq0 + b, 0, 0)),
                  pl.BlockSpec((DEC_SEQ, NAT_W), lambda b, r: (lat_seq0 + b, NAT_QCOL + 2)),
                  pl.BlockSpec((1, PAST_LEN, NAT_W), lambda b, r: (b, 0, 0)),
                  pl.BlockSpec((1, PAST_LEN, NAT_W), lambda b, r: (b, 0, 0)),
                  pl.BlockSpec(bias.shape, lambda b, r: (0, 0, 0, 0))],
        out_specs=pl.BlockSpec((tq, NAT_W), lambda b, r: (b * steps + r, 0)),
        out_shape=jax.ShapeDtypeStruct((T_LAT, NAT_W), BF16),
        compiler_params=_cparams(("arbitrary", "arbitrary"), VMEM_LIMIT),
        name="nat_lat",
    )(qn, kn.reshape(T_ALL // DEC_SEQ, DEC_SEQ, NAT_W), p_main, kc, vc, bias)


def _heads_to_lanes(cache):
    b, h, l, dh = cache.shape
    return cache.transpose(0, 2, 1, 3).reshape(b, l, h * dh)


def _finalize_kernel(k0_ref, v0_ref, s0_ref, k1_ref, v1_ref, s1_ref, ko_ref, vo_ref, so_ref):
    layer = pl.program_id(0)
    for idx, (k_ref, v_ref, s_ref) in enumerate(((k0_ref, v0_ref, s0_ref), (k1_ref, v1_ref, s1_ref))):
        @pl.when(layer == idx)
        def _(k_ref=k_ref, v_ref=v_ref, s_ref=s_ref):
            for h in range(H_B):
                ko_ref[0, 0, h] = k_ref[:, h * DH_B:(h + 1) * DH_B]
                vo_ref[0, 0, h] = v_ref[:, h * DH_B:(h + 1) * DH_B]
            so_ref[0, 0] = s_ref[0]


def _finalize_caches(kns, pms, states):
    n_even = len(kns)
    assert n_even == 2
    tok = lambda cb: pl.BlockSpec((SEQ, NAT_W), lambda l, b: (b, cb))
    st_in = pl.BlockSpec((1, 2, H_A, DK_A, DV_A), lambda l, b: (b, 0, 0, 0, 0))
    cache_out = pl.BlockSpec((1, 1, H_B, SEQ, DH_B), lambda l, b: (b, l, 0, 0, 0))
    cache_shape = jax.ShapeDtypeStruct((BATCH, n_even, H_B, SEQ, DH_B), F32)
    return pl.pallas_call(
        _finalize_kernel,
        grid=(n_even, BATCH),
        in_specs=[tok(0), tok(NAT_QCOL + 2), st_in, tok(0), tok(NAT_QCOL + 2), st_in],
        out_specs=[cache_out, cache_out,
                   pl.BlockSpec((1, 1, 2, H_A, DK_A, DV_A), lambda l, b: (b, l, 0, 0, 0, 0))],
        out_shape=[cache_shape, cache_shape,
                   jax.ShapeDtypeStruct((BATCH, n_even, 2, H_A, DK_A, DV_A), F32)],
        compiler_params=_cparams(("arbitrary", "arbitrary")),
        name="finalize_caches",
    )(kns[0], pms[0], states[0], kns[1], pms[1], states[1])


def _even_mixer_layer(x, h, mod, l, i, w):
    p_main = _mm([h], [w["even_w_in"]], tm=1024, tn=1792, wi=i, n_out=A_QKV + A_GATE + B_QKV, shifted=True)
    p_small = _mm([h], [w["even_w_in"]], tm=1024, tn=128, wi=i, n_out=128, col0=(A_QKV + A_GATE) // 128)

    conv_w, a_log, dt_bias = w["gdn_conv_w"][i], w["gdn_a_log"][i], w["gdn_dt_bias"][i]
    ctx_tiles = T_CTX // ROW_TILE
    qkv_c, bg_c = _gdn_prep(p_main, p_small, conv_w, a_log, dt_bias, 0, T_CTX, SEQ)
    qkv_l, bg_l = _gdn_prep(p_main, p_small, conv_w, a_log, dt_bias, ctx_tiles, T_LAT, DEC_SEQ)
    of_c, ob_c, s_ctx = _gdn_scanb(qkv_c, bg_c, None, i, BATCH, SEQ, 8)
    of_l, ob_l, _ = _gdn_scanb(qkv_l, bg_l, w["state_gdn"], i, DEC_BATCH, DEC_SEQ, 2)
    oa_c = _gdn_post(of_c, ob_c, p_main, w["gdn_norm_g"][i], 0, T_CTX)
    oa_l = _gdn_post(of_l, ob_l, p_main, w["gdn_norm_g"][i], ctx_tiles, T_LAT)

    qn, kn = _nat_prep(p_main, w["nat_q_norm_g"][i], w["nat_k_norm_g"][i])
    ob_c = _nat_ctx(qn, kn, p_main)
    bias = _nat_bias(w["nat_rpb"][i])
    kc = _heads_to_lanes(w["cache_nat_k"][:, i])
    vc = _heads_to_lanes(w["cache_nat_v"][:, i])
    ob_l = _nat_lat(qn, kn, p_main, kc, vc, bias)

    x, h2 = _mm([(oa_c, oa_l), (ob_c, ob_l)], [w["even_w_out"]], tm=512, tn=D, epilogue="residual_mod", wi=i,
                resid=x, mod=mod, layer=l, k_gate=2, next_norm=(w["norm_ffn_g"][l], l, 3, BF16))
    act = _mm([h2], [w["ffn_w_gate"], w["ffn_w_up"]], tm=512, tn=1408, out_dtype=BF16, epilogue="swiglu", wi=i)
    if l + 1 < DEPTH:
        x, h_next = _mm([act], [w["ffn_w_down"]], tm=512, tn=D, epilogue="residual_mod", wi=i,
                        resid=x, mod=mod, layer=l, k_gate=5, next_norm=(w["norm_mix_g"][l + 1], l + 1, 0, BF16))
    else:
        x = _mm([act], [w["ffn_w_down"]], tm=512, tn=512, epilogue="residual", wi=i,
                resid=x, mod=mod, layer=l, k_gate=5)
        h_next = None
    return x, h_next, kn, p_main, s_ctx


def _dft_consts(seq):
    n = 2 * seq
    k = np.arange(seq)[:, None]
    s = np.arange(seq)[None, :]
    ang = 2.0 * np.pi * ((k * s) % n) / n
    fr = np.cos(ang)
    fi = -np.sin(ang)
    fi[0, :] = np.cos(np.pi * (np.arange(seq) % 2))
    fm = np.concatenate([fr, fi], axis=0)

    def to_bf16(a):
        return jnp.asarray(a.astype(np.float32)).astype(BF16)

    cw = np.full((n, 1), 2.0 / n)
    cw[0, 0] = cw[seq, 0] = 1.0 / n
    sg = np.ones((n, 1))
    sg[seq + 1:, 0] = -1.0
    cs = np.zeros((n, 128), np.float32)
    cs[:, 0:1] = cw
    cs[:, 1:2] = cw * sg
    return to_bf16(fm), to_bf16(fm.T.copy()), jnp.asarray(cs)


def _dft_apply(m, x):
    return jnp.dot(m, x.astype(BF16), preferred_element_type=F32)


def _hy_filter_kernel(z_ref, w1_ref, b1_ref, w2_ref, b2_ref, fq_ref, w3_ref, t_ref, dl_ref,
                      fm_ref, cs_ref, o_ref, hh_scr):
    c, d = pl.program_id(0), pl.program_id(1)

    @pl.when((c == 0) & (d == 0))
    def _():
        fq = fq_ref[...]
        hh = jnp.sin(fq * (_dot(z_ref[...], w1_ref[...], prec="bf16x3") + b1_ref[...]))
        hh_scr[...] = jnp.sin(fq * (_dot(hh, w2_ref[...], prec="bf16x3") + b2_ref[...]))

    filt = _dot(hh_scr[...], w3_ref[...], prec="bf16x3") * jnp.exp(-t_ref[...] * dl_ref[...])
    rows = lax.broadcasted_iota(jnp.int32, filt.shape, 0)
    filt = jnp.where((d == 1) & (rows == 0), 0.0, filt)
    spec = _dft_apply(fm_ref[...], filt)

    @pl.when(d == 0)
    def _():
        o_ref[...] = spec * cs_ref[:, 0:1]

    @pl.when(d == 1)
    def _():
        o_ref[...] = o_ref[...] + spec * cs_ref[:, 1:2]


def _hy_filter_spectrum(seq, consts, w1, b1, w2, b2, w3, freq):
    fm, _, cs = consts
    bands = (HY_EMB - 1) // 2
    t = np.linspace(0.0, 1.0, seq, dtype=np.float32)[:, None]
    wv = (np.float32(2.0 * math.pi / seq) * np.arange(seq, dtype=np.float32))[:, None]
    f = np.linspace(1e-4, bands - 1, bands, dtype=np.float32)[None, :]
    z = np.zeros((seq, 128), np.float32)
    z[:, 0:1] = t
    z[:, 1:1 + bands] = np.cos(f * wv)
    z[:, 1 + bands:HY_EMB] = -np.sin(f * wv)
    deltas = np.abs(np.linspace(math.log(HY_TARGET) / HY_FAST, math.log(HY_TARGET) / HY_SLOW, D,
                                dtype=np.float32))[None, :]
    w1p = jnp.zeros((128, HY_HID), F32).at[:HY_EMB].set(w1)
    tc = 256
    n = 2 * seq
    full = lambda shape: pl.BlockSpec(shape, lambda c, d: tuple(0 for _ in shape))
    return pl.pallas_call(
        _hy_filter_kernel,
        grid=(D // tc, 2),
        in_specs=[full((seq, 128)), full((128, HY_HID)), full((1, HY_HID)), full((HY_HID, HY_HID)),
                  full((1, HY_HID)), full((1, HY_HID)),
                  pl.BlockSpec((HY_HID, tc), lambda c, d: (0, d * (D // tc) + c)),
                  full((seq, 1)), pl.BlockSpec((1, tc), lambda c, d: (0, c)),
                  full((n, seq)), full((n, 128))],
        out_specs=pl.BlockSpec((n, tc), lambda c, d: (0, c)),
        out_shape=jax.ShapeDtypeStruct((n, D), F32),
        scratch_shapes=[pltpu.VMEM((seq, HY_HID), F32)],
        compiler_params=_cparams(("arbitrary", "arbitrary"), VMEM_LIMIT),
        name="hy_filter",
    )(jnp.asarray(z), w1p, b1.reshape(1, -1), w2, b2.reshape(1, -1), freq.reshape(1, -1), w3,
      jnp.asarray(t), jnp.asarray(deltas), fm, cs)


def _hy_conv_kernel(p0_ref, p1_ref, pv_ref, cw0_ref, cw1_ref, cwv_ref, kf_ref, bias_ref, fm_ref, ft_ref,
                    o_ref, *, seq):
    rows = lax.broadcasted_iota(jnp.int32, p0_ref.shape, 0)

    def conv3(x_ref, w_ref):
        x, w = x_ref[...], w_ref[...]
        xm = jnp.where(rows == 0, 0.0, pltpu.roll(x, 1, 0))
        xp = jnp.where(rows == seq - 1, 0.0, pltpu.roll(x, seq - 1, 0))
        return xm * w[0:1, :] + x * w[1:2, :] + xp * w[2:3, :]

    x0 = conv3(p0_ref, cw0_ref)
    u = conv3(pv_ref, cwv_ref) * conv3(p1_ref, cw1_ref)
    xs = _dft_apply(fm_ref[...], u)
    kf = kf_ref[...]
    xr, xi = xs[:seq], xs[seq:]
    kr, ki = kf[:seq], kf[seq:]
    row0 = lax.broadcasted_iota(jnp.int32, xr.shape, 0) == 0
    xiki = xi * ki
    yr = xr * kr - jnp.where(row0, 0.0, xiki)
    yi = jnp.where(row0, xiki, xr * ki + xi * kr)
    y = _dft_apply(ft_ref[...], jnp.concatenate([yr, yi], axis=0))
    o_ref[...] = ((y + u * bias_ref[...]) * x0).astype(o_ref.dtype)


def _hy_conv(p, conv_w, kf, bias, consts, seq, n_seq, seq_blk0, tc):
    fm, ft, _ = consts
    n = 2 * seq
    nc = D // tc
    full = lambda shape: pl.BlockSpec(shape, lambda b, c: (0, 0))
    grp = lambda g: pl.BlockSpec((seq, tc), lambda b, c: (seq_blk0 + b, g * nc + c))
    cwg = lambda g: pl.BlockSpec((3, tc), lambda b, c: (0, g * nc + c))
    return pl.pallas_call(
        functools.partial(_hy_conv_kernel, seq=seq),
        grid=(n_seq, nc),
        in_specs=[grp(0), grp(1), grp(2), cwg(0), cwg(1), cwg(2),
                  pl.BlockSpec((n, tc), lambda b, c: (0, c)),
                  pl.BlockSpec((1, tc), lambda b, c: (0, c)),
                  full((n, seq)), full((seq, n))],
        out_specs=pl.BlockSpec((seq, tc), lambda b, c: (b, c)),
        out_shape=jax.ShapeDtypeStruct((n_seq * seq, D), BF16),
        compiler_params=_cparams(("arbitrary", "arbitrary"), VMEM_LIMIT),
        name="hy_conv",
    )(p, p, p, conv_w, conv_w, conv_w, kf, bias.reshape(1, D), fm, ft)


MOE_TM = 256
MOE_TILES = 2 * T_ALL // MOE_TM + N_EXP
MOE_ROWS = MOE_TILES * MOE_TM
ROUTE_TM = 512
DISPATCH_TM = 512
COMBINE_TM = 256


def _router_kernel(h_ref, rw_ref, rb_ref, tri_ref, ei_ref, gf_ref, cnt_ref, carry):
    @pl.when(pl.program_id(0) == 0)
    def _():
        carry[...] = jnp.zeros(carry.shape, F32)

    logits = _dot(h_ref[...], rw_ref[...], prec="bf16x3") + rb_ref[...]
    lane = lax.broadcasted_iota(jnp.int32, logits.shape, 1)
    logits = jnp.where(lane < N_EXP, logits, -jnp.inf)
    m1 = jnp.max(logits, axis=-1, keepdims=True)
    i1 = jnp.min(jnp.where(logits == m1, lane, 128), axis=-1, keepdims=True)
    rest = jnp.where(lane == i1, -jnp.inf, logits)
    m2 = jnp.max(rest, axis=-1, keepdims=True)
    i2 = jnp.min(jnp.where(rest == m2, lane, 128), axis=-1, keepdims=True)
    e = jnp.exp(m2 - m1)
    g1 = 1.0 / (1.0 + e)
    g2 = e * g1
    pick = jnp.where((lane == i1) | (lane == i2), 1.0, 0.0)
    before = carry[...] + jnp.dot(tri_ref[...], pick.astype(BF16), preferred_element_type=F32)
    r1 = jnp.sum(jnp.where(lane == i1, before, 0.0), axis=-1, keepdims=True)
    r2 = jnp.sum(jnp.where(lane == i2, before, 0.0), axis=-1, keepdims=True)
    carry[...] = carry[...] + jnp.sum(pick, axis=0, keepdims=True)
    ints = jnp.where(lane == 0, i1, jnp.where(lane == 1, i2, 0))
    ranks = jnp.where(lane == 2, r1, jnp.where(lane == 3, r2, 0.0))
    ei_ref[...] = ints + ranks.astype(jnp.int32)
    gf_ref[...] = jnp.where(lane == 0, g1, jnp.where(lane == 1, g2, 0.0))
    cnt_ref[...] = carry[...].astype(jnp.int32)


def _router(h, router_w, router_b):
    rw = jnp.zeros((D, 128), F32).at[:, :N_EXP].set(router_w)
    rb = jnp.zeros((1, 128), F32).at[0, :N_EXP].set(router_b)
    tri = jnp.asarray(np.tril(np.ones((ROUTE_TM, ROUTE_TM), np.float32), -1), BF16)
    row = pl.BlockSpec((ROUTE_TM, 128), lambda i: (i, 0))
    return pl.pallas_call(
        _router_kernel,
        grid=(T_ALL // ROUTE_TM,),
        in_specs=[pl.BlockSpec((ROUTE_TM, D), lambda i: (i, 0)),
                  pl.BlockSpec((D, 128), lambda i: (0, 0)),
                  pl.BlockSpec((1, 128), lambda i: (0, 0)),
                  pl.BlockSpec((ROUTE_TM, ROUTE_TM), lambda i: (0, 0))],
        out_specs=[row, row, pl.BlockSpec((1, 128), lambda i: (0, 0))],
        out_shape=[jax.ShapeDtypeStruct((T_ALL, 128), jnp.int32),
                   jax.ShapeDtypeStruct((T_ALL, 128), F32),
                   jax.ShapeDtypeStruct((1, 128), jnp.int32)],
        scratch_shapes=[pltpu.VMEM((1, 128), F32)],
        compiler_params=_cparams(("arbitrary",)),
        name="moe_router",
    )(h, rw, rb, tri)


def _moe_dispatch_kernel(pos_ref, h_ref, init_ref, out_ref, sem):
    del init_ref
    base = pl.program_id(0) * DISPATCH_TM

    def copy(r, p):
        return pltpu.make_async_copy(h_ref.at[pl.ds(r, 1)], out_ref.at[pl.ds(p, 1)], sem)

    def start(r, c):
        copy(r, pos_ref[base + r]).start(priority=0)
        copy(r, pos_ref[T_ALL + base + r]).start(priority=1)
        return c

    def wait(r, c):
        copy(0, 0).wait()
        copy(0, 0).wait()
        return c

    lax.fori_loop(0, DISPATCH_TM, start, 0, unroll=8)
    lax.fori_loop(0, DISPATCH_TM, wait, 0, unroll=8)


def _moe_dispatch(h, flat_pos):
    any_spec = pl.BlockSpec(memory_space=pl.ANY)
    return pl.pallas_call(
        _moe_dispatch_kernel,
        grid_spec=pltpu.PrefetchScalarGridSpec(
            num_scalar_prefetch=1, grid=(T_ALL // DISPATCH_TM,),
            in_specs=[pl.BlockSpec((DISPATCH_TM, D), lambda i, pos: (i, 0)), any_spec],
            out_specs=any_spec,
            scratch_shapes=[pltpu.SemaphoreType.DMA(())]),
        out_shape=jax.ShapeDtypeStruct((MOE_ROWS, D), F32),
        input_output_aliases={2: 0},
        compiler_params=_cparams(("arbitrary",)),
        name="moe_dispatch",
    )(flat_pos, h, jnp.zeros((MOE_ROWS, D), F32))


def _moe_up_kernel(te_ref, nu_ref, g_ref, wg_ref, wu_ref, o_ref, wbf):
    i = pl.program_id(1)
    new_w = (i == 0) | (te_ref[i] != te_ref[jnp.maximum(i - 1, 0)])

    @pl.when(new_w)
    def _():
        wbf[0] = wg_ref[0, 0].astype(BF16)
        wbf[1] = wu_ref[0, 0].astype(BF16)

    @pl.when(i < nu_ref[0])
    def _():
        _swiglu_chunks(g_ref[...].astype(BF16), wbf, o_ref)

    @pl.when(i >= nu_ref[0])
    def _():
        o_ref[...] = jnp.zeros(o_ref.shape, o_ref.dtype)


def _moe_up(tile_expert, n_used, rows, wg, wu, li, tn):
    wspec = pl.BlockSpec((1, 1, D, tn), lambda j, i, te, nu: (li, te[i], 0, j))
    return pl.pallas_call(
        _moe_up_kernel,
        grid_spec=pltpu.PrefetchScalarGridSpec(
            num_scalar_prefetch=2, grid=(D_FF_E // tn, MOE_TILES),
            in_specs=[pl.BlockSpec((MOE_TM, D), lambda j, i, te, nu: (i, 0)), wspec, wspec],
            out_specs=pl.BlockSpec((MOE_TM, tn), lambda j, i, te, nu: (i, j)),
            scratch_shapes=[pltpu.VMEM((2, D, tn), BF16)]),
        out_shape=jax.ShapeDtypeStruct((MOE_ROWS, D_FF_E), BF16),
        compiler_params=_cparams(("arbitrary", "arbitrary"), VMEM_LIMIT),
        name="moe_up",
    )(tile_expert, n_used, rows, wg, wu)


def _moe_down_kernel(te_ref, nu_ref, a_ref, wd_ref, o_ref, wbf):
    i = pl.program_id(1)
    new_w = (i == 0) | (te_ref[i] != te_ref[jnp.maximum(i - 1, 0)])

    @pl.when(new_w)
    def _():
        wbf[...] = wd_ref[0, 0].astype(BF16)

    @pl.when(i < nu_ref[0])
    def _():
        o_ref[...] = jnp.dot(a_ref[...], wbf[...], preferred_element_type=F32)

    @pl.when(i >= nu_ref[0])
    def _():
        o_ref[...] = jnp.zeros(o_ref.shape, o_ref.dtype)


def _moe_down(tile_expert, n_used, act, wd, li, tn):
    return pl.pallas_call(
        _moe_down_kernel,
        grid_spec=pltpu.PrefetchScalarGridSpec(
            num_scalar_prefetch=2, grid=(D // tn, MOE_TILES),
            in_specs=[pl.BlockSpec((MOE_TM, D_FF_E), lambda j, i, te, nu: (i, 0)),
                      pl.BlockSpec((1, 1, D_FF_E, tn), lambda j, i, te, nu: (li, te[i], 0, j))],
            out_specs=pl.BlockSpec((MOE_TM, tn), lambda j, i, te, nu: (i, j)),
            scratch_shapes=[pltpu.VMEM((D_FF_E, tn), BF16)]),
        out_shape=jax.ShapeDtypeStruct((MOE_ROWS, D), F32),
        compiler_params=_cparams(("arbitrary", "arbitrary"), VMEM_LIMIT),
        name="moe_down",
    )(tile_expert, n_used, act, wd)


def _moe_combine_kernel(pos_ref, x_ref, y_ref, gf_ref, gate_ref, *rest, with_next):
    if with_next:
        g_ref, sc_ref, sh_ref, o_ref, h_ref, ybuf, sems = rest
    else:
        o_ref, ybuf, sems = rest
    i = pl.program_id(0)
    buf = i % 2

    def copy(b, which, r, p):
        return pltpu.make_async_copy(y_ref.at[pl.ds(p, 1)], ybuf.at[b, which, pl.ds(r, 1)], sems.at[b])

    def gather(tile, b):
        base = tile * COMBINE_TM

        def start(r, c):
            copy(b, 0, r, pos_ref[base + r]).start(priority=0)
            copy(b, 1, r, pos_ref[T_ALL + base + r]).start(priority=1)
            return c
        lax.fori_loop(0, COMBINE_TM, start, 0, unroll=8)

    @pl.when(i == 0)
    def _():
        gather(0, 0)

    @pl.when(i + 1 < pl.num_programs(0))
    def _():
        gather(i + 1, 1 - buf)

    def wait(r, c):
        copy(buf, 0, 0, 0).wait()
        copy(buf, 1, 0, 0).wait()
        return c

    lax.fori_loop(0, COMBINE_TM, wait, 0, unroll=8)
    gf = gf_ref[...]
    f = gf[:, 0:1] * ybuf[buf, 0] + gf[:, 1:2] * ybuf[buf, 1]
    x_new = x_ref[...] + gate_ref[0] * f
    o_ref[...] = x_new
    if with_next:
        h_ref[...] = _modulate_math(x_new, g_ref[...], sc_ref[0], sh_ref[0]).astype(h_ref.dtype)


def _moe_combine(x, y, flat_pos, gf, mod, layer, next_norm):
    tm = COMBINE_TM
    base = (layer * 6 + 5) * N_SEG
    row = pl.BlockSpec((tm, D), lambda i, pos: (i, 0))
    in_specs = [row, pl.BlockSpec(memory_space=pl.ANY),
                pl.BlockSpec((tm, 128), lambda i, pos: (i, 0)),
                pl.BlockSpec((1, 1, D), lambda i, pos: (base + _seg_of_row(i * tm), 0, 0))]
    args = [flat_pos, x, y, gf, mod]
    out_specs, out_shape = row, jax.ShapeDtypeStruct((T_ALL, D), F32)
    if next_norm is not None:
        g, n_layer, k_shift = next_norm
        in_specs += [pl.BlockSpec((1, D), lambda i, pos: (0, 0)),
                     _mod_spec(n_layer, k_shift + 1, tm, lambda i, pos: i),
                     _mod_spec(n_layer, k_shift, tm, lambda i, pos: i)]
        args += [g.reshape(1, D), mod, mod]
        out_specs = [row, row]
        out_shape = [out_shape, jax.ShapeDtypeStruct((T_ALL, D), BF16)]
    return pl.pallas_call(
        functools.partial(_moe_combine_kernel, with_next=next_norm is not None),
        grid_spec=pltpu.PrefetchScalarGridSpec(
            num_scalar_prefetch=1, grid=(T_ALL // tm,), in_specs=in_specs, out_specs=out_specs,
            scratch_shapes=[pltpu.VMEM((2, 2, tm, D), F32), pltpu.SemaphoreType.DMA((2,))]),
        out_shape=out_shape,
        compiler_params=_cparams(("arbitrary",)),
        name="moe_combine",
    )(*args)


def _moe(x, h, mod, layer, li, w, next_norm):
    ei, gf, cnt = _router(h, w["moe_router_w"][li], w["moe_router_b"][li])
    experts, ranks, counts = ei[:, 0:2], ei[:, 2:4], cnt[0, :N_EXP]
    tiles = (counts + MOE_TM - 1) // MOE_TM
    tile_end = jnp.cumsum(tiles)
    row0 = (tile_end - tiles) * MOE_TM
    pos = (row0[experts] + ranks).astype(jnp.int32)
    n_used = tile_end[-1:].astype(jnp.int32)
    t_idx = jnp.minimum(jnp.arange(MOE_TILES, dtype=jnp.int32), n_used[0] - 1)
    tile_expert = jnp.sum(t_idx[:, None] >= tile_end[None, :], axis=1).astype(jnp.int32)
    flat_pos = pos.T.reshape(-1)
    sorted_rows = _moe_dispatch(h, flat_pos)
    act = _moe_up(tile_expert, n_used, sorted_rows, w["moe_w_gate"], w["moe_w_up"], li, 1792)
    y = _moe_down(tile_expert, n_used, act, w["moe_w_down"], li, 1024)
    return _moe_combine(x, y, flat_pos, gf, mod, layer, next_norm)


def _odd_mixer_layer(x, h, mod, l, i, w, spectra, consts):
    p = _mm([h], [w["odd_w_in"]], tm=1024, tn=1536, wi=i)
    cw = w["hy_conv_w"][i]
    z_c = _hy_conv(p, cw, spectra[SEQ][i], w["hy_bias"][i], consts[SEQ], SEQ, BATCH, 0, D)
    z_l = _hy_conv(p, cw, spectra[DEC_SEQ][i], w["hy_bias"][i], consts[DEC_SEQ], DEC_SEQ, DEC_BATCH,
                   T_CTX // DEC_SEQ, 512)
    x, h2 = _mm([(z_c, z_l)], [w["odd_w_out"]], tm=512, tn=D, epilogue="residual_mod", wi=i,
                resid=x, mod=mod, layer=l, k_gate=2, next_norm=(w["norm_ffn_g"][l], l, 3, F32))
    next_norm = (w["norm_mix_g"][l + 1], l + 1, 0) if l + 1 < DEPTH else None
    out = _moe(x, h2, mod, l, i, w, next_norm)
    return out if next_norm is not None else (out, None)


def kernel(x_prompt, x_sample, state_gdn, cache_nat_k, cache_nat_v, c, c_ctx, ada_w, ada_b, norm_mix_g, norm_ffn_g, even_w_in, gdn_conv_w, gdn_a_log, gdn_dt_bias, gdn_norm_g, nat_q_norm_g, nat_k_norm_g, nat_rpb, even_w_out, ffn_w_gate, ffn_w_up, ffn_w_down, odd_w_in, hy_conv_w, hy_w1, hy_b1, hy_w2, hy_b2, hy_w3, hy_freq, hy_bias, odd_w_out, moe_router_w, moe_router_b, moe_w_gate, moe_w_up, moe_w_down):
    w = dict(state_gdn=state_gdn, cache_nat_k=cache_nat_k, cache_nat_v=cache_nat_v,
             norm_mix_g=norm_mix_g, norm_ffn_g=norm_ffn_g, even_w_in=even_w_in, gdn_conv_w=gdn_conv_w,
             gdn_a_log=gdn_a_log, gdn_dt_bias=gdn_dt_bias, gdn_norm_g=gdn_norm_g,
             nat_q_norm_g=nat_q_norm_g, nat_k_norm_g=nat_k_norm_g, nat_rpb=nat_rpb, even_w_out=even_w_out,
             ffn_w_gate=ffn_w_gate, ffn_w_up=ffn_w_up, ffn_w_down=ffn_w_down, odd_w_in=odd_w_in,
             hy_conv_w=hy_conv_w, hy_bias=hy_bias, odd_w_out=odd_w_out, moe_router_w=moe_router_w,
             moe_router_b=moe_router_b, moe_w_gate=moe_w_gate, moe_w_up=moe_w_up, moe_w_down=moe_w_down)
    assert SEQ == ROW_TILE and DEC_SEQ % ROW_TILE == 0
    mod = _ada_all(c, c_ctx, ada_w, ada_b)
    consts = {s: _dft_consts(s) for s in (SEQ, DEC_SEQ)}
    n_odd = DEPTH // 2
    spectra = {s: [_hy_filter_spectrum(s, consts[s], hy_w1[i], hy_b1[i], hy_w2[i], hy_b2[i], hy_w3[i], hy_freq[i])
                   for i in range(n_odd)] for s in (SEQ, DEC_SEQ)}
    x = jnp.concatenate([x_prompt.reshape(T_CTX, D), x_sample.reshape(T_LAT, D)], axis=0)
    h = _modulate(x, norm_mix_g[0], mod, 0, 0, BF16)
    states, kns, pms = [], [], []
    for l in range(DEPTH):
        i = l // 2
        if l % 2 == 0:
            x, h, kn, p_main, s_ctx = _even_mixer_layer(x, h, mod, l, i, w)
            states.append(s_ctx)
            kns.append(kn)
            pms.append(p_main)
        else:
            x, h = _odd_mixer_layer(x, h, mod, l, i, w, spectra, consts)
    k_cache, v_cache, state_out = _finalize_caches(kns, pms, states)
    y_prompt = x[:T_CTX].reshape(BATCH, SEQ, D)
    y_sample = x[T_CTX:].reshape(DEC_BATCH, DEC_SEQ, D)
    return (y_prompt, y_sample, state_out, k_cache, v_cache)
```

```python
import functools
import math

import jax
import jax.numpy as jnp
import numpy as np
from jax import lax
from jax.experimental import pallas as pl
from jax.experimental.pallas import tpu as pltpu

F32 = jnp.float32
BF16 = jnp.bfloat16

D = 1024
BATCH = 16
SEQ = 256
DEPTH = 4
DEC_BATCH = 2
DEC_SEQ = 1024
PAST_LEN = 512
GRID_W = 64
EPS = 1e-6
H_A = 4
DK_A = 128
DV_A = 128
CHUNK = 64
H_B = 8
DH_B = 64
WIN_R = 8
WIN_C = 16
A_QKV = 2 * H_A * DK_A + H_A * DV_A
A_GATE = H_A * DV_A
B_QKV = 3 * H_B * DH_B
HY_EMB = 33
HY_HID = 64
HY_FAST = 0.3
HY_SLOW = 1.5
HY_TARGET = 1e-2
D_FF = 2816
N_EXP = 8
D_FF_E = 3584

T_CTX = BATCH * SEQ
T_LAT = DEC_BATCH * DEC_SEQ
T_ALL = T_CTX + T_LAT
N_SEG = 8
ROW_TILE = 256

VMEM_LIMIT = 56 * 1024 * 1024
LANES = 128

TM_PROJ = 1024
TN_EVEN_IN = 1792
TN_ODD_IN = 1536
TM_EPI = 512
TN_FFN_UP = D_FF // 2
MOE_UP_TN = D_FF_E // 2
HY_TC_LAT = 512
GDN_NS_CTX = 8
GDN_NS_LAT = 2


def _cparams(sem, vmem=None):
    return pltpu.CompilerParams(dimension_semantics=sem, vmem_limit_bytes=vmem)


def _sigmoid(x):
    return 1.0 / (1.0 + jnp.exp(-x))


def _seg_of_row(row):
    return jnp.where(row < T_CTX, 0, 1 + (row - T_CTX) // DEC_SEQ)


def _split_bf16(x):
    hi = x.astype(BF16)
    lo = (x - hi.astype(F32)).astype(BF16)
    return hi, lo


def _dot(a, b, dims=(((1,), (0,)), ((), ())), prec="bf16"):
    if prec == "bf16":
        return lax.dot_general(a.astype(BF16), b.astype(BF16), dims, preferred_element_type=F32)
    ah, al = _split_bf16(a.astype(F32))
    bh, bl = _split_bf16(b.astype(F32))
    r = lax.dot_general(ah, bh, dims, preferred_element_type=F32)
    r = r + lax.dot_general(ah, bl, dims, preferred_element_type=F32)
    r = r + lax.dot_general(al, bh, dims, preferred_element_type=F32)
    return r


NT_DIMS = (((1,), (1,)), ((), ()))


def _ada_kernel(cv_ref, w_ref, b_ref, o_ref):
    cv = cv_ref[...]
    s = cv * _sigmoid(cv)
    o_ref[0] = _dot(s, w_ref[0]) + b_ref[0]


def _ada_all(c, c_ctx, ada_w, ada_b):
    cv = jnp.zeros((N_SEG, D), F32).at[0].set(c_ctx).at[1:1 + DEC_BATCH].set(c)
    tn = 1536
    out = pl.pallas_call(
        _ada_kernel,
        grid=(DEPTH, 6 * D // tn),
        in_specs=[
            pl.BlockSpec((N_SEG, D), lambda l, j: (0, 0)),
            pl.BlockSpec((1, D, tn), lambda l, j: (l, 0, j)),
            pl.BlockSpec((1, 1, tn), lambda l, j: (l, 0, j)),
        ],
        out_specs=pl.BlockSpec((1, N_SEG, tn), lambda l, j: (l, 0, j)),
        out_shape=jax.ShapeDtypeStruct((DEPTH, N_SEG, 6 * D), F32),
        compiler_params=_cparams(("arbitrary", "arbitrary"), VMEM_LIMIT),
        name="ada",
    )(cv, ada_w, ada_b.reshape(DEPTH, 1, 6 * D))
    return out.reshape(DEPTH, N_SEG, 6, D).transpose(0, 2, 1, 3).reshape(DEPTH * 6 * N_SEG, 1, D)


def _mod_spec(layer, k, tm, row_of_step):
    base = (layer * 6 + k) * N_SEG

    def imap(*ids):
        return (base + _seg_of_row(row_of_step(*ids) * tm), 0, 0)

    return pl.BlockSpec((1, 1, D), imap)


def _modulate_math(x, g, scale, shift):
    ms = jnp.mean(x * x, axis=-1, keepdims=True)
    y = x * lax.rsqrt(ms + EPS) * g
    return y * (1.0 + scale) + shift


def _modulate_kernel(x_ref, g_ref, sc_ref, sh_ref, o_ref):
    o_ref[...] = _modulate_math(x_ref[...], g_ref[...], sc_ref[0], sh_ref[0]).astype(o_ref.dtype)


def _modulate(x, g, mod, layer, k_shift, out_dtype):
    tm = 512
    return pl.pallas_call(
        _modulate_kernel,
        grid=(T_ALL // tm,),
        in_specs=[
            pl.BlockSpec((tm, D), lambda i: (i, 0)),
            pl.BlockSpec((1, D), lambda i: (0, 0)),
            _mod_spec(layer, k_shift + 1, tm, lambda i: i),
            _mod_spec(layer, k_shift, tm, lambda i: i),
        ],
        out_specs=pl.BlockSpec((tm, D), lambda i: (i, 0)),
        out_shape=jax.ShapeDtypeStruct((T_ALL, D), out_dtype),
        compiler_params=_cparams(("arbitrary",)),
        name="modulate",
    )(x, g.reshape(1, D), mod, mod)


SWIGLU_CHUNK = 256


def _swiglu_chunks(h, wbf, o_ref):
    n = o_ref.shape[1]
    for c0 in range(0, n, SWIGLU_CHUNK):
        c1 = min(c0 + SWIGLU_CHUNK, n)
        a = jnp.dot(h, wbf[0, :, c0:c1], preferred_element_type=F32)
        b = jnp.dot(h, wbf[1, :, c0:c1], preferred_element_type=F32)
        o_ref[:, c0:c1] = (a * _sigmoid(a) * b).astype(o_ref.dtype)


EVEN_SHIFT = 4 * H_A
EVEN_SPLIT = A_QKV + A_GATE


def _mm_kernel(*refs, chunks, n_w, epilogue, shifted, n_ctx_tiles):
    pos = 0
    lhs = []
    for kw, paired in chunks:
        cnt = 2 if paired else 1
        lhs.append(refs[pos:pos + cnt])
        pos += cnt
    n_wrefs = n_w + (1 if shifted else 0)
    w_refs = refs[pos:pos + n_wrefs]
    rest = refs[pos + n_wrefs:]
    j, i = pl.program_id(0), pl.program_id(1)
    if epilogue == "residual":
        x_ref, gate_ref, o_ref, wbf = rest
    elif epilogue == "residual_mod":
        x_ref, gate_ref, g_ref, sc_ref, sh_ref, o_ref, h_ref, wbf = rest
    else:
        o_ref, wbf = rest

    @pl.when(i == 0)
    def _():
        if shifted:
            tn = wbf.shape[2]
            for jj in range(shifted):
                split = min(max(EVEN_SPLIT - jj * tn, 0), tn)

                @pl.when(j == jj)
                def _(split=split):
                    wa = w_refs[0][0]
                    if split == tn:
                        wbf[0] = wa.astype(BF16)
                    else:
                        parts = [wa[:, :split]] if split else []
                        parts += [wa[:, split + EVEN_SHIFT:], w_refs[1][0][:, :EVEN_SHIFT]]
                        wbf[0] = jnp.concatenate(parts, axis=1).astype(BF16)
        else:
            for k in range(n_w):
                wbf[k] = w_refs[k][0].astype(BF16)

    def lhs_chunk(c):
        r = lhs[c]
        if len(r) == 2:
            return jnp.where(i < n_ctx_tiles, r[0][...], r[1][...]).astype(BF16)
        return r[0][...].astype(BF16)

    def matmul(k):
        acc, off = None, 0
        for c, (kw, _) in enumerate(chunks):
            part = jnp.dot(lhs_chunk(c), wbf[k, off:off + kw, :], preferred_element_type=F32)
            acc = part if acc is None else acc + part
            off += kw
        return acc

    if epilogue == "swiglu":
        _swiglu_chunks(lhs_chunk(0), wbf, o_ref)
        return
    a = matmul(0)
    if epilogue == "residual":
        o_ref[...] = x_ref[...] + gate_ref[0] * a
    elif epilogue == "residual_mod":
        x_new = x_ref[...] + gate_ref[0] * a
        o_ref[...] = x_new
        h_ref[...] = _modulate_math(x_new, g_ref[...], sc_ref[0], sh_ref[0]).astype(h_ref.dtype)
    else:
        o_ref[...] = a.astype(o_ref.dtype)


def _mm(lhs, ws, *, tm, tn, out_dtype=F32, epilogue="none", wi=0, n_out=None, col0=0, shifted=False,
        resid=None, mod=None, layer=None, k_gate=None, next_norm=None):
    chunks, args, in_specs = [], [], []
    n_ctx_tiles = T_CTX // tm
    for part in lhs:
        if isinstance(part, tuple):
            kw = part[0].shape[1]
            chunks.append((kw, True))
            args += [part[0], part[1]]
            in_specs += [pl.BlockSpec((tm, kw), lambda j, i: (jnp.minimum(i, n_ctx_tiles - 1), 0)),
                         pl.BlockSpec((tm, kw), lambda j, i: (jnp.maximum(i - n_ctx_tiles, 0), 0))]
        else:
            kw = part.shape[1]
            chunks.append((kw, False))
            args.append(part)
            in_specs.append(pl.BlockSpec((tm, kw), lambda j, i: (i, 0)))
    kdim = sum(kw for kw, _ in chunks)
    n = n_out if n_out is not None else ws[0].shape[2]
    n_w = len(ws)
    if shifted:
        last = pl.cdiv(ws[0].shape[2], tn) - 1
        in_specs += [pl.BlockSpec((1, kdim, tn), lambda j, i: (wi, 0, j)),
                     pl.BlockSpec((1, kdim, tn), lambda j, i: (wi, 0, jnp.minimum(j + 1, last)))]
        args += [ws[0], ws[0]]
    else:
        in_specs += [pl.BlockSpec((1, kdim, tn), lambda j, i: (wi, 0, col0 + j)) for _ in ws]
        args += list(ws)
    out_specs = pl.BlockSpec((tm, tn), lambda j, i: (i, j))
    out_shape = jax.ShapeDtypeStruct((T_ALL, n), out_dtype)
    if epilogue in ("residual", "residual_mod"):
        base = (layer * 6 + k_gate) * N_SEG
        in_specs += [pl.BlockSpec((tm, tn), lambda j, i: (i, j)),
                     pl.BlockSpec((1, 1, tn), lambda j, i: (base + _seg_of_row(i * tm), 0, j))]
        args += [resid, mod]
    if epilogue == "residual_mod":
        g, n_layer, k_shift, h_dtype = next_norm
        assert tn == n == D
        in_specs += [pl.BlockSpec((1, D), lambda j, i: (0, 0)),
                     _mod_spec(n_layer, k_shift + 1, tm, lambda j, i: i),
                     _mod_spec(n_layer, k_shift, tm, lambda j, i: i)]
        args += [g.reshape(1, D), mod, mod]
        out_specs = [out_specs, pl.BlockSpec((tm, tn), lambda j, i: (i, j))]
        out_shape = [out_shape, jax.ShapeDtypeStruct((T_ALL, n), h_dtype)]
    return pl.pallas_call(
        functools.partial(_mm_kernel, chunks=tuple(chunks), n_w=n_w, epilogue=epilogue,
                          shifted=(n // tn if shifted else 0), n_ctx_tiles=n_ctx_tiles),
        grid=(n // tn, T_ALL // tm),
        in_specs=in_specs,
        out_specs=out_specs,
        out_shape=out_shape,
        scratch_shapes=[pltpu.VMEM((n_w, kdim, tn), BF16)],
        compiler_params=_cparams(("arbitrary", "arbitrary"), VMEM_LIMIT),
        name="mm_" + epilogue,
    )(*args)


def _conv3(x, prev_ref, next_ref, w, seq_len):
    i = pl.program_id(0)
    rows_n = x.shape[0]
    if seq_len is None:
        j = jnp.maximum(i - T_CTX // rows_n, 0)
        per = DEC_SEQ // rows_n
        is_ctx = i < T_CTX // rows_n
        first = is_ctx | (j % per == 0)
        last = is_ctx | (j % per == per - 1)
    else:
        first = (i * rows_n) % seq_len == 0
        last = ((i + 1) * rows_n) % seq_len == 0
    prev_row = jnp.where(first, 0.0, prev_ref[7:8, :])
    next_row = jnp.where(last, 0.0, next_ref[0:1, :])
    rows = lax.broadcasted_iota(jnp.int32, x.shape, 0)
    xm = jnp.where(rows == 0, prev_row, pltpu.roll(x, 1, 0))
    xp = jnp.where(rows == rows_n - 1, next_row, pltpu.roll(x, rows_n - 1, 0))
    return xm * w[0:1, :] + x * w[1:2, :] + xp * w[2:3, :]


def _halo_specs(width, col_block, row_off_tiles):
    per = ROW_TILE // 8
    last_blk = T_ALL // 8 - 1
    main = pl.BlockSpec((ROW_TILE, width), lambda i: (i + row_off_tiles, col_block))
    prev = pl.BlockSpec((8, width), lambda i: (jnp.maximum((i + row_off_tiles) * per - 1, 0), col_block))
    nxt = pl.BlockSpec((8, width), lambda i: (jnp.minimum((i + row_off_tiles + 1) * per, last_blk), col_block))
    return main, prev, nxt


def _cumsum_rows(x, reverse):
    n = x.shape[0]
    rows = lax.broadcasted_iota(jnp.int32, x.shape, 0)
    k = 1
    while k < n:
        if reverse:
            x = x + jnp.where(rows < n - k, pltpu.roll(x, n - k, 0), 0.0)
        else:
            x = x + jnp.where(rows >= k, pltpu.roll(x, k, 0), 0.0)
        k *= 2
    return x


def _gdn_prep_kernel(pm_ref, prev_ref, next_ref, ps_ref, cw_ref, par_ref, qkv_ref, bg_ref, *, seq_len):
    x = _conv3(pm_ref[...], prev_ref, next_ref, cw_ref[...], seq_len)
    x = x * _sigmoid(x)
    for h in range(H_A):
        sl = slice(h * DK_A, (h + 1) * DK_A)
        qh = x[:, sl]
        qkv_ref[:, sl] = qh * lax.rsqrt(jnp.sum(qh * qh, axis=-1, keepdims=True) + EPS) * (DK_A ** -0.5)
        sl = slice(H_A * DK_A + h * DK_A, H_A * DK_A + (h + 1) * DK_A)
        kh = x[:, sl]
        qkv_ref[:, sl] = kh * lax.rsqrt(jnp.sum(kh * kh, axis=-1, keepdims=True) + EPS)
    qkv_ref[:, 2 * H_A * DK_A:] = x[:, 2 * H_A * DK_A:]
    raw = ps_ref[...]
    lane = lax.broadcasted_iota(jnp.int32, raw.shape, 1)
    beta = _sigmoid(raw)
    z = raw + par_ref[1:2, :]
    softplus = jnp.maximum(z, 0.0) + jnp.log(1.0 + jnp.exp(-jnp.abs(z)))
    g = -jnp.exp(par_ref[0:1, :]) * softplus
    bg_ref[...] = jnp.where(lane < 2 * H_A, beta, jnp.where(lane < 4 * H_A, g, 0.0))


def _gdn_prep(p_main, p_small, conv_w, a_log, dt_bias, row_off_tiles, n_rows, seq_len):
    par = jnp.zeros((2, 128), F32)
    par = par.at[0, 2 * H_A:4 * H_A].set(a_log.reshape(-1)).at[1, 2 * H_A:4 * H_A].set(dt_bias.reshape(-1))
    main, prev, nxt = _halo_specs(A_QKV, 0, row_off_tiles)
    return pl.pallas_call(
        functools.partial(_gdn_prep_kernel, seq_len=seq_len),
        grid=(n_rows // ROW_TILE,),
        in_specs=[main, prev, nxt,
                  pl.BlockSpec((ROW_TILE, 128), lambda i: (i + row_off_tiles, 0)),
                  pl.BlockSpec((3, A_QKV), lambda i: (0, 0)),
                  pl.BlockSpec((2, 128), lambda i: (0, 0))],
        out_specs=[pl.BlockSpec((ROW_TILE, A_QKV), lambda i: (i, 0)),
                   pl.BlockSpec((ROW_TILE, 128), lambda i: (i, 0))],
        out_shape=[jax.ShapeDtypeStruct((n_rows, A_QKV), F32),
                   jax.ShapeDtypeStruct((n_rows, 128), F32)],
        compiler_params=_cparams(("arbitrary",)),
        name="gdn_prep",
    )(p_main, p_main, p_main, p_small, conv_w, par)


B_NN = (((2,), (1,)), ((0,), (0,)))
B_NT = (((2,), (2,)), ((0,), (0,)))
B_TN = (((1,), (1,)), ((0,), (0,)))


def _bmm(a, b, dims=B_NN):
    return _dot(a, b, dims)


def _gdn_scanb_kernel(*refs, ns, has_s0):
    if has_s0:
        qf_ref, qb_ref, bf_ref, bb_ref, s0_ref, of_ref, ob_ref, sfin_ref, st = refs
    else:
        qf_ref, qb_ref, bf_ref, bb_ref, of_ref, ob_ref, sfin_ref, st = refs
    step = pl.program_id(1)
    c = CHUNK
    nb = 2 * ns * H_A

    @pl.when(step == 0)
    def _():
        for d in range(2):
            st[d] = s0_ref[:, 0, d] if has_s0 else jnp.zeros(st.shape[1:], F32)

    q_l, k_l, v_l, beta_l, gc_l = [], [], [], [], []
    for d in range(2):
        for s in range(ns):
            qkv = (qf_ref if d == 0 else qb_ref)[s, 0]
            bg = (bf_ref if d == 0 else bb_ref)[s, 0]
            gcum = _cumsum_rows(bg, reverse=(d == 1))
            for h in range(H_A):
                q_l.append(qkv[:, h * DK_A:(h + 1) * DK_A])
                k_l.append(qkv[:, (H_A + h) * DK_A:(H_A + h + 1) * DK_A])
                v_l.append(qkv[:, 2 * H_A * DK_A + h * DV_A:2 * H_A * DK_A + (h + 1) * DV_A])
                col = d * H_A + h
                beta_l.append(bg[:, col:col + 1])
                gc_l.append(gcum[:, 2 * H_A + col:2 * H_A + col + 1])
    q, k, v = jnp.stack(q_l), jnp.stack(k_l), jnp.stack(v_l)
    beta, gc = jnp.stack(beta_l), jnp.stack(gc_l)

    ri = lax.broadcasted_iota(jnp.int32, (1, c, c), 1)
    ci = lax.broadcasted_iota(jnp.int32, (1, c, c), 2)
    eye = ri == ci
    half = nb // 2
    gc_row = jnp.sum(jnp.where(eye, gc, 0.0), axis=1, keepdims=True)
    dgc = gc - gc_row

    def masked_exp(x, keep):
        return jnp.where(keep, jnp.exp(jnp.where(keep, x, 0.0)), 0.0)

    decay = jnp.concatenate([masked_exp(dgc[:half], ri >= ci), masked_exp(dgc[half:], ri <= ci)], axis=0)
    kb = k * beta
    a_mat = jnp.where(eye, 0.0, _bmm(kb, k, B_NT) * decay)
    blk = 8
    diag = (ri // blk) == (ci // blk)
    pw = jnp.where(diag, a_mat, 0.0)
    r_mat = -pw
    for _ in range(2):
        pw = _bmm(pw, pw)
        r_mat = r_mat + pw + _bmm(r_mat, pw)
    while blk < c:
        off = ((ri // (2 * blk)) == (ci // (2 * blk))) & ((ri // blk) != (ci // blk))
        e_mat = jnp.where(off, a_mat, 0.0)
        x_mat = e_mat + _bmm(r_mat, e_mat)
        r_mat = r_mat - (x_mat + _bmm(x_mat, r_mat))
        blk *= 2
    eg = jnp.exp(gc)
    rhs = jnp.concatenate([v * beta, kb * eg], axis=2)
    sol = rhs + _bmm(r_mat, rhs)
    ub, wm = sol[:, :, :DV_A], sol[:, :, DV_A:]
    attn = _bmm(q, k, B_NT) * decay
    fwd1 = lax.broadcasted_iota(jnp.int32, (nb, 1, 1), 0) < nb // 2
    g_end = jnp.where(fwd1, gc[:, c - 1:c, :], gc[:, 0:1, :])
    qd = q * eg
    kd = k * jnp.exp(g_end - gc)
    state = st[...].reshape(nb, DK_A, DV_A)
    u = ub - _bmm(wm, state)
    o = _bmm(qd, state) + _bmm(attn, u)
    state = state * jnp.exp(g_end) + _bmm(kd, u, B_TN)
    st[...] = state.reshape(st.shape)
    for d in range(2):
        o_ref = of_ref if d == 0 else ob_ref
        for s in range(ns):
            for h in range(H_A):
                o_ref[s, 0, :, h * DV_A:(h + 1) * DV_A] = o[(d * ns + s) * H_A + h]

    @pl.when(step == pl.num_programs(1) - 1)
    def _():
        for d in range(2):
            sfin_ref[:, d] = st[d]


def _gdn_scanb(qkv, bg, s0, layer_i, n_seq, seq_len, ns):
    n = seq_len // CHUNK
    qkv4 = qkv.reshape(n_seq, n, CHUNK, A_QKV)
    bg4 = bg.reshape(n_seq, n, CHUNK, 128)
    fwd = lambda g, c: (g, c, 0, 0)
    bwd = lambda g, c: (g, n - 1 - c, 0, 0)
    in_specs = [pl.BlockSpec((ns, 1, CHUNK, A_QKV), fwd), pl.BlockSpec((ns, 1, CHUNK, A_QKV), bwd),
                pl.BlockSpec((ns, 1, CHUNK, 128), fwd), pl.BlockSpec((ns, 1, CHUNK, 128), bwd)]
    args = [qkv4, qkv4, bg4, bg4]
    if s0 is not None:
        in_specs.append(pl.BlockSpec((ns, 1, 2, H_A, DK_A, DV_A), lambda g, c: (g, layer_i, 0, 0, 0, 0)))
        args.append(s0)
    o_shape = jax.ShapeDtypeStruct((n_seq, n, CHUNK, H_A * DV_A), F32)
    of, ob, sfin = pl.pallas_call(
        functools.partial(_gdn_scanb_kernel, ns=ns, has_s0=s0 is not None),
        grid=(n_seq // ns, n),
        in_specs=in_specs,
        out_specs=[pl.BlockSpec((ns, 1, CHUNK, H_A * DV_A), fwd),
                   pl.BlockSpec((ns, 1, CHUNK, H_A * DV_A), bwd),
                   pl.BlockSpec((ns, 2, H_A, DK_A, DV_A), lambda g, c: (g, 0, 0, 0, 0))],
        out_shape=[o_shape, o_shape, jax.ShapeDtypeStruct((n_seq, 2, H_A, DK_A, DV_A), F32)],
        scratch_shapes=[pltpu.VMEM((2, ns, H_A, DK_A, DV_A), F32)],
        compiler_params=_cparams(("arbitrary", "arbitrary"), VMEM_LIMIT),
        name="gdn_scan",
    )(*args)
    rows = n_seq * seq_len
    return of.reshape(rows, H_A * DV_A), ob.reshape(rows, H_A * DV_A), sfin


def _gdn_post_kernel(of_ref, ob_ref, gate_ref, g_ref, o_ref):
    o = of_ref[...] + ob_ref[...]
    gate = gate_ref[...]
    for h in range(H_A):
        sl = slice(h * DV_A, (h + 1) * DV_A)
        oh = o[:, sl]
        y = oh * lax.rsqrt(jnp.mean(oh * oh, axis=-1, keepdims=True) + EPS) * g_ref[...]
        gh = gate[:, sl]
        o_ref[:, sl] = (y * (gh * _sigmoid(gh))).astype(o_ref.dtype)


def _gdn_post(of, ob, p_main, gdn_g, row_off_tiles, n_rows):
    gate_blk = A_QKV // A_GATE
    return pl.pallas_call(
        _gdn_post_kernel,
        grid=(n_rows // ROW_TILE,),
        in_specs=[pl.BlockSpec((ROW_TILE, A_GATE), lambda i: (i, 0)),
                  pl.BlockSpec((ROW_TILE, A_GATE), lambda i: (i, 0)),
                  pl.BlockSpec((ROW_TILE, A_GATE), lambda i: (i + row_off_tiles, gate_blk)),
                  pl.BlockSpec((1, DV_A), lambda i: (0, 0))],
        out_specs=pl.BlockSpec((ROW_TILE, A_GATE), lambda i: (i, 0)),
        out_shape=jax.ShapeDtypeStruct((n_rows, A_GATE), BF16),
        compiler_params=_cparams(("arbitrary",)),
        name="gdn_post",
    )(of, ob, p_main, gdn_g.reshape(1, DV_A))


NAT_W = H_B * DH_B
NAT_QCOL = (A_QKV + A_GATE) // NAT_W
NAT_SCALE = DH_B ** -0.5


def _nat_prep_kernel(q_ref, k_ref, bd_ref, gq_ref, gk_ref, qn_ref, kn_ref):
    bd = bd_ref[...]
    for x_ref, g_ref, o_ref in ((q_ref, gq_ref, qn_ref), (k_ref, gk_ref, kn_ref)):
        x = x_ref[...]
        ms = jnp.dot((x * x).astype(BF16), bd, preferred_element_type=F32)
        o_ref[...] = x * lax.rsqrt(ms + EPS) * g_ref[...]


def _nat_prep(p_main, qn_g, kn_g):
    grp = np.arange(NAT_W) // DH_B
    bd = jnp.asarray((grp[:, None] == grp[None, :]).astype(np.float32) / DH_B, BF16)
    spec = lambda cb: pl.BlockSpec((ROW_TILE, NAT_W), lambda i: (i, cb))
    return pl.pallas_call(
        _nat_prep_kernel,
        grid=(T_ALL // ROW_TILE,),
        in_specs=[spec(NAT_QCOL), spec(NAT_QCOL + 1),
                  pl.BlockSpec((NAT_W, NAT_W), lambda i: (0, 0)),
                  pl.BlockSpec((1, NAT_W), lambda i: (0, 0)),
                  pl.BlockSpec((1, NAT_W), lambda i: (0, 0))],
        out_specs=[spec(0), spec(0)],
        out_shape=[jax.ShapeDtypeStruct((T_ALL, NAT_W), F32)] * 2,
        compiler_params=_cparams(("arbitrary",)),
        name="nat_prep",
    )(p_main, p_main, bd, jnp.tile(qn_g, H_B).reshape(1, NAT_W), jnp.tile(kn_g, H_B).reshape(1, NAT_W))


def _pair_masks():
    lane = lax.broadcasted_iota(jnp.int32, (1, 2 * DH_B), 1)
    return lane < DH_B


def _nat_ctx_kernel(q_ref, k_ref, v_ref, o_ref):
    lo = _pair_masks()
    for p in range(H_B // 2):
        sl = slice(p * 2 * DH_B, (p + 1) * 2 * DH_B)
        q2, k2, v2 = q_ref[:, sl], k_ref[:, sl], v_ref[:, sl]
        halves = []
        for half in range(2):
            qm = jnp.where(lo if half == 0 else jnp.logical_not(lo), q2, 0.0)
            s = _dot(qm, k2, NT_DIMS) * NAT_SCALE
            e = jnp.exp(s - jnp.max(s, axis=-1, keepdims=True))
            pr = e / jnp.sum(e, axis=-1, keepdims=True)
            halves.append(_dot(pr, v2))
        o_ref[:, sl] = jnp.where(lo, halves[0], halves[1]).astype(o_ref.dtype)


def _nat_ctx(qn, kn, p_main):
    spec = lambda cb: pl.BlockSpec((SEQ, NAT_W), lambda b: (b, cb))
    return pl.pallas_call(
        _nat_ctx_kernel,
        grid=(BATCH,),
        in_specs=[spec(0), spec(0), spec(NAT_QCOL + 2)],
        out_specs=spec(0),
        out_shape=jax.ShapeDtypeStruct((T_CTX, NAT_W), BF16),
        compiler_params=_cparams(("arbitrary",)),
        name="nat_ctx",
    )(qn, kn, p_main)


def _nat_bias_kernel(r_ref, e_ref, ok_ref, o_ref):
    r = r_ref[...]
    hi, lo = _split_bf16(r)
    lo2 = (r - hi.astype(F32) - lo.astype(F32)).astype(BF16)
    e = e_ref[...]
    t = (jnp.dot(hi, e, preferred_element_type=F32) + jnp.dot(lo, e, preferred_element_type=F32)
         + jnp.dot(lo2, e, preferred_element_type=F32))
    o_ref[...] = jnp.where(ok_ref[...] > 0.5, t, -jnp.inf)


def _nat_bias(rpb):
    n_dr, n_dc = 2 * WIN_R - 1, 2 * WIN_C - 1
    qc = np.arange(GRID_W)[:, None]
    kc = np.arange(GRID_W)[None, :]
    dc = (kc - qc + WIN_C - 1).reshape(-1)
    c0 = np.clip(qc - WIN_C // 2, 0, GRID_W - WIN_C)
    ok = ((kc >= c0) & (kc < c0 + WIN_C)).reshape(1, -1).astype(np.float32)
    onehot = (np.arange(128)[:, None] == dc[None, :]).astype(np.float32)
    rows = H_B * n_dr
    rp = jnp.zeros((rows, 128), F32).at[:, :n_dc].set(rpb.reshape(rows, n_dc))
    tab = pl.pallas_call(
        _nat_bias_kernel,
        out_shape=jax.ShapeDtypeStruct((rows, GRID_W * GRID_W), F32),
        name="nat_bias",
    )(rp, jnp.asarray(onehot, BF16), jnp.asarray(ok))
    tab = tab.reshape(H_B, n_dr, GRID_W, GRID_W)
    pad = jnp.full((H_B, 1, GRID_W, GRID_W), -jnp.inf, F32)
    ext = jnp.concatenate([pad, tab, pad], axis=1)
    return jnp.concatenate([ext[:, :n_dr + 1], ext[:, 1:]], axis=-1)


NAT_QROWS = 2
NAT_KROWS = 10


def _nat_lat_kernel(q_ref, k_ref, v_ref, kc_ref, vc_ref, bias_ref, o_ref):
    rows = DEC_SEQ // GRID_W
    row_a = pl.program_id(1) * NAT_QROWS
    r0s = [jnp.clip(row_a + qi - WIN_R // 2, 0, rows - WIN_R) for qi in range(NAT_QROWS)]
    ws = jnp.minimum(r0s[0], rows - NAT_KROWS)
    start = pl.multiple_of(ws * GRID_W, GRID_W)
    n_loc = NAT_KROWS * GRID_W
    lo = _pair_masks()
    lane = lax.broadcasted_iota(jnp.int32, (GRID_W, 2 * GRID_W), 1)

    def bias_for(h):
        blocks = []
        for qi in range(NAT_QROWS):
            pieces = []
            for jp in range(NAT_KROWS // 2):
                rk = ws + 2 * jp
                ok0 = ((rk >= r0s[qi]) & (rk < r0s[qi] + WIN_R)).astype(jnp.int32)
                ok1 = ((rk + 1 >= r0s[qi]) & (rk + 1 < r0s[qi] + WIN_R)).astype(jnp.int32)
                d = jnp.clip(rk - (row_a + qi) + WIN_R - 1, -1, 2 * WIN_R - 2) + 1
                piece = bias_ref[h, d]
                pieces.append(jnp.where(jnp.where(lane < GRID_W, ok0, ok1) > 0, piece, -jnp.inf))
            blocks.append(jnp.concatenate(pieces, axis=1))
        return jnp.concatenate(blocks, axis=0)

    for p in range(H_B // 2):
        sl = slice(p * 2 * DH_B, (p + 1) * 2 * DH_B)
        q2 = q_ref[:, sl]
        kw = k_ref[0, pl.ds(start, n_loc), sl]
        vw = v_ref[pl.ds(start, n_loc), sl]
        kc, vc = kc_ref[0, :, sl], vc_ref[0, :, sl]
        halves = []
        for half in range(2):
            qm = jnp.where(lo if half == 0 else jnp.logical_not(lo), q2, 0.0)
            s_loc = _dot(qm, kw, NT_DIMS) * NAT_SCALE + bias_for(2 * p + half)
            s_ctx = _dot(qm, kc, NT_DIMS) * NAT_SCALE
            m = jnp.maximum(jnp.max(s_loc, axis=-1, keepdims=True), jnp.max(s_ctx, axis=-1, keepdims=True))
            e_loc, e_ctx = jnp.exp(s_loc - m), jnp.exp(s_ctx - m)
            inv = 1.0 / (jnp.sum(e_loc, axis=-1, keepdims=True) + jnp.sum(e_ctx, axis=-1, keepdims=True))
            halves.append(_dot(e_loc * inv, vw) + _dot(e_ctx * inv, vc))
        o_ref[:, sl] = jnp.where(lo, halves[0], halves[1]).astype(o_ref.dtype)


def _nat_lat(qn, kn, p_main, kc, vc, bias):
    steps = DEC_SEQ // GRID_W // NAT_QROWS
    tq = NAT_QROWS * GRID_W
    lat_tile0 = T_CTX // tq
    lat_seq0 = T_CTX // DEC_SEQ
    return pl.pallas_call(
        _nat_lat_kernel,
        grid=(DEC_BATCH, steps),
        in_specs=[pl.BlockSpec((tq, NAT_W), lambda b, r: (lat_tile0 + b * steps + r, 0)),
                  pl.BlockSpec((1, DEC_SEQ, NAT_W), lambda b, r: (lat_seq0 + b, 0, 0)),
                  pl.BlockSpec((DEC_SEQ, NAT_W), lambda b, r: (lat_seq0 + b, NAT_QCOL + 2)),
                  pl.BlockSpec((1, PAST_LEN, NAT_W), lambda b, r: (b, 0, 0)),
                  pl.BlockSpec((1, PAST_LEN, NAT_W), lambda b, r: (b, 0, 0)),
                  pl.BlockSpec(bias.shape, lambda b, r: (0, 0, 0, 0))],
        out_specs=pl.BlockSpec((tq, NAT_W), lambda b, r: (b * steps + r, 0)),
        out_shape=jax.ShapeDtypeStruct((T_LAT, NAT_W), BF16),
        compiler_params=_cparams(("arbitrary", "arbitrary"), VMEM_LIMIT),
        name="nat_lat",
    )(qn, kn.reshape(T_ALL // DEC_SEQ, DEC_SEQ, NAT_W), p_main, kc, vc, bias)


def _heads_to_lanes(cache):
    b, h, l, dh = cache.shape
    return cache.transpose(0, 2, 1, 3).reshape(b, l, h * dh)


def _finalize_kernel(k0_ref, v0_ref, s0_ref, k1_ref, v1_ref, s1_ref, ko_ref, vo_ref, so_ref):
    layer = pl.program_id(0)
    for idx, (k_ref, v_ref, s_ref) in enumerate(((k0_ref, v0_ref, s0_ref), (k1_ref, v1_ref, s1_ref))):
        @pl.when(layer == idx)
        def _(k_ref=k_ref, v_ref=v_ref, s_ref=s_ref):
            for h in range(H_B):
                ko_ref[0, 0, h] = k_ref[:, h * DH_B:(h + 1) * DH_B]
                vo_ref[0, 0, h] = v_ref[:, h * DH_B:(h + 1) * DH_B]
            so_ref[0, 0] = s_ref[0]


def _finalize_caches(kns, pms, states):
    n_even = len(kns)
    assert n_even == 2
    pick = lambda idx, l, b: b * (l if idx else 1 - l)
    tok = lambda idx, cb: pl.BlockSpec((SEQ, NAT_W), lambda l, b: (pick(idx, l, b), cb))
    st_in = lambda idx: pl.BlockSpec((1, 2, H_A, DK_A, DV_A), lambda l, b: (pick(idx, l, b), 0, 0, 0, 0))
    cache_out = pl.BlockSpec((1, 1, H_B, SEQ, DH_B), lambda l, b: (b, l, 0, 0, 0))
    cache_shape = jax.ShapeDtypeStruct((BATCH, n_even, H_B, SEQ, DH_B), F32)
    return pl.pallas_call(
        _finalize_kernel,
        grid=(n_even, BATCH),
        in_specs=[tok(0, 0), tok(0, NAT_QCOL + 2), st_in(0), tok(1, 0), tok(1, NAT_QCOL + 2), st_in(1)],
        out_specs=[cache_out, cache_out,
                   pl.BlockSpec((1, 1, 2, H_A, DK_A, DV_A), lambda l, b: (b, l, 0, 0, 0, 0))],
        out_shape=[cache_shape, cache_shape,
                   jax.ShapeDtypeStruct((BATCH, n_even, 2, H_A, DK_A, DV_A), F32)],
        compiler_params=_cparams(("arbitrary", "arbitrary")),
        name="finalize_caches",
    )(kns[0], pms[0], states[0], kns[1], pms[1], states[1])


def _even_mixer_layer(x, h, mod, l, i, w):
    p_main = _mm([h], [w["even_w_in"]], tm=TM_PROJ, tn=TN_EVEN_IN, wi=i, n_out=A_QKV + A_GATE + B_QKV,
                 shifted=True)
    p_small = _mm([h], [w["even_w_in"]], tm=TM_PROJ, tn=LANES, wi=i, n_out=LANES, col0=(A_QKV + A_GATE) // LANES)

    conv_w, a_log, dt_bias = w["gdn_conv_w"][i], w["gdn_a_log"][i], w["gdn_dt_bias"][i]
    ctx_tiles = T_CTX // ROW_TILE
    qkv_c, bg_c = _gdn_prep(p_main, p_small, conv_w, a_log, dt_bias, 0, T_CTX, SEQ)
    qkv_l, bg_l = _gdn_prep(p_main, p_small, conv_w, a_log, dt_bias, ctx_tiles, T_LAT, DEC_SEQ)
    of_c, ob_c, s_ctx = _gdn_scanb(qkv_c, bg_c, None, i, BATCH, SEQ, GDN_NS_CTX)
    of_l, ob_l, _ = _gdn_scanb(qkv_l, bg_l, w["state_gdn"], i, DEC_BATCH, DEC_SEQ, GDN_NS_LAT)
    oa_c = _gdn_post(of_c, ob_c, p_main, w["gdn_norm_g"][i], 0, T_CTX)
    oa_l = _gdn_post(of_l, ob_l, p_main, w["gdn_norm_g"][i], ctx_tiles, T_LAT)

    qn, kn = _nat_prep(p_main, w["nat_q_norm_g"][i], w["nat_k_norm_g"][i])
    ob_c = _nat_ctx(qn, kn, p_main)
    bias = _nat_bias(w["nat_rpb"][i])
    kc = _heads_to_lanes(w["cache_nat_k"][:, i])
    vc = _heads_to_lanes(w["cache_nat_v"][:, i])
    ob_l = _nat_lat(qn, kn, p_main, kc, vc, bias)

    x, h2 = _mm([(oa_c, oa_l), (ob_c, ob_l)], [w["even_w_out"]], tm=TM_EPI, tn=D, epilogue="residual_mod", wi=i,
                resid=x, mod=mod, layer=l, k_gate=2, next_norm=(w["norm_ffn_g"][l], l, 3, BF16))
    act = _mm([h2], [w["ffn_w_gate"], w["ffn_w_up"]], tm=TM_EPI, tn=TN_FFN_UP, out_dtype=BF16,
              epilogue="swiglu", wi=i)
    if l + 1 < DEPTH:
        x, h_next = _mm([act], [w["ffn_w_down"]], tm=TM_EPI, tn=D, epilogue="residual_mod", wi=i,
                        resid=x, mod=mod, layer=l, k_gate=5, next_norm=(w["norm_mix_g"][l + 1], l + 1, 0, BF16))
    else:
        x = _mm([act], [w["ffn_w_down"]], tm=TM_EPI, tn=D // 2, epilogue="residual", wi=i,
                resid=x, mod=mod, layer=l, k_gate=5)
        h_next = None
    return x, h_next, kn, p_main, s_ctx


def _dft_consts(seq):
    n = 2 * seq
    k = np.arange(seq)[:, None]
    s = np.arange(seq)[None, :]
    ang = 2.0 * np.pi * ((k * s) % n) / n
    fr = np.cos(ang)
    fi = -np.sin(ang)
    fi[0, :] = np.cos(np.pi * (np.arange(seq) % 2))
    fm = np.concatenate([fr, fi], axis=0)

    def to_bf16(a):
        return jnp.asarray(a.astype(np.float32)).astype(BF16)

    cw = np.full((n, 1), 2.0 / n)
    cw[0, 0] = cw[seq, 0] = 1.0 / n
    sg = np.ones((n, 1))
    sg[seq + 1:, 0] = -1.0
    cs = np.zeros((n, 128), np.float32)
    cs[:, 0:1] = cw
    cs[:, 1:2] = cw * sg
    return to_bf16(fm), to_bf16(fm.T.copy()), jnp.asarray(cs)


def _dft_apply(m, x):
    return jnp.dot(m, x.astype(BF16), preferred_element_type=F32)


def _hy_filter_kernel(z_ref, w1_ref, b1_ref, w2_ref, b2_ref, fq_ref, w3_ref, t_ref, dl_ref,
                      fm_ref, cs_ref, o_ref, hh_scr):
    c, d = pl.program_id(0), pl.program_id(1)

    @pl.when((c == 0) & (d == 0))
    def _():
        fq = fq_ref[...]
        hh = jnp.sin(fq * (_dot(z_ref[...], w1_ref[...], prec="bf16x3") + b1_ref[...]))
        hh_scr[...] = jnp.sin(fq * (_dot(hh, w2_ref[...], prec="bf16x3") + b2_ref[...]))

    filt = _dot(hh_scr[...], w3_ref[...], prec="bf16x3") * jnp.exp(-t_ref[...] * dl_ref[...])
    rows = lax.broadcasted_iota(jnp.int32, filt.shape, 0)
    filt = jnp.where((d == 1) & (rows == 0), 0.0, filt)
    spec = _dft_apply(fm_ref[...], filt)

    @pl.when(d == 0)
    def _():
        o_ref[...] = spec * cs_ref[:, 0:1]

    @pl.when(d == 1)
    def _():
        o_ref[...] = o_ref[...] + spec * cs_ref[:, 1:2]


def _hy_filter_spectrum(seq, consts, w1, b1, w2, b2, w3, freq):
    fm, _, cs = consts
    bands = (HY_EMB - 1) // 2
    t = np.linspace(0.0, 1.0, seq, dtype=np.float32)[:, None]
    wv = (np.float32(2.0 * math.pi / seq) * np.arange(seq, dtype=np.float32))[:, None]
    f = np.linspace(1e-4, bands - 1, bands, dtype=np.float32)[None, :]
    z = np.zeros((seq, 128), np.float32)
    z[:, 0:1] = t
    z[:, 1:1 + bands] = np.cos(f * wv)
    z[:, 1 + bands:HY_EMB] = -np.sin(f * wv)
    deltas = np.abs(np.linspace(math.log(HY_TARGET) / HY_FAST, math.log(HY_TARGET) / HY_SLOW, D,
                                dtype=np.float32))[None, :]
    w1p = jnp.zeros((128, HY_HID), F32).at[:HY_EMB].set(w1)
    tc = 256
    n = 2 * seq
    full = lambda shape: pl.BlockSpec(shape, lambda c, d: tuple(0 for _ in shape))
    return pl.pallas_call(
        _hy_filter_kernel,
        grid=(D // tc, 2),
        in_specs=[full((seq, 128)), full((128, HY_HID)), full((1, HY_HID)), full((HY_HID, HY_HID)),
                  full((1, HY_HID)), full((1, HY_HID)),
                  pl.BlockSpec((HY_HID, tc), lambda c, d: (0, d * (D // tc) + c)),
                  full((seq, 1)), pl.BlockSpec((1, tc), lambda c, d: (0, c)),
                  full((n, seq)), full((n, 128))],
        out_specs=pl.BlockSpec((n, tc), lambda c, d: (0, c)),
        out_shape=jax.ShapeDtypeStruct((n, D), F32),
        scratch_shapes=[pltpu.VMEM((seq, HY_HID), F32)],
        compiler_params=_cparams(("arbitrary", "arbitrary"), VMEM_LIMIT),
        name="hy_filter",
    )(jnp.asarray(z), w1p, b1.reshape(1, -1), w2, b2.reshape(1, -1), freq.reshape(1, -1), w3,
      jnp.asarray(t), jnp.asarray(deltas), fm, cs)


def _hy_conv_kernel(p0_ref, p1_ref, pv_ref, cw0_ref, cw1_ref, cwv_ref, kf_ref, bias_ref, fm_ref, ft_ref,
                    o_ref, *, seq):
    rows = lax.broadcasted_iota(jnp.int32, p0_ref.shape, 0)

    def conv3(x_ref, w_ref):
        x, w = x_ref[...], w_ref[...]
        xm = jnp.where(rows == 0, 0.0, pltpu.roll(x, 1, 0))
        xp = jnp.where(rows == seq - 1, 0.0, pltpu.roll(x, seq - 1, 0))
        return xm * w[0:1, :] + x * w[1:2, :] + xp * w[2:3, :]

    x0 = conv3(p0_ref, cw0_ref)
    u = conv3(pv_ref, cwv_ref) * conv3(p1_ref, cw1_ref)
    xs = _dft_apply(fm_ref[...], u)
    kf = kf_ref[...]
    xr, xi = xs[:seq], xs[seq:]
    kr, ki = kf[:seq], kf[seq:]
    row0 = lax.broadcasted_iota(jnp.int32, xr.shape, 0) == 0
    xiki = xi * ki
    yr = xr * kr - jnp.where(row0, 0.0, xiki)
    yi = jnp.where(row0, xiki, xr * ki + xi * kr)
    y = _dft_apply(ft_ref[...], jnp.concatenate([yr, yi], axis=0))
    o_ref[...] = ((y + u * bias_ref[...]) * x0).astype(o_ref.dtype)


def _hy_conv(p, conv_w, kf, bias, consts, seq, n_seq, seq_blk0, tc):
    fm, ft, _ = consts
    n = 2 * seq
    nc = D // tc
    full = lambda shape: pl.BlockSpec(shape, lambda b, c: (0, 0))
    grp = lambda g: pl.BlockSpec((seq, tc), lambda b, c: (seq_blk0 + b, g * nc + c))
    cwg = lambda g: pl.BlockSpec((3, tc), lambda b, c: (0, g * nc + c))
    return pl.pallas_call(
        functools.partial(_hy_conv_kernel, seq=seq),
        grid=(n_seq, nc),
        in_specs=[grp(0), grp(1), grp(2), cwg(0), cwg(1), cwg(2),
                  pl.BlockSpec((n, tc), lambda b, c: (0, c)),
                  pl.BlockSpec((1, tc), lambda b, c: (0, c)),
                  full((n, seq)), full((seq, n))],
        out_specs=pl.BlockSpec((seq, tc), lambda b, c: (b, c)),
        out_shape=jax.ShapeDtypeStruct((n_seq * seq, D), BF16),
        compiler_params=_cparams(("arbitrary", "arbitrary"), VMEM_LIMIT),
        name="hy_conv",
    )(p, p, p, conv_w, conv_w, conv_w, kf, bias.reshape(1, D), fm, ft)


MOE_TM = 256
MOE_TILES = 2 * T_ALL // MOE_TM + N_EXP
MOE_ROWS = MOE_TILES * MOE_TM
ROUTE_TM = 512
DISPATCH_TM = 512
COMBINE_TM = 256


def _router_kernel(h_ref, rw_ref, rb_ref, tri_ref, ei_ref, gf_ref, cnt_ref, carry):
    @pl.when(pl.program_id(0) == 0)
    def _():
        carry[...] = jnp.zeros(carry.shape, F32)

    logits = _dot(h_ref[...], rw_ref[...], prec="bf16x3") + rb_ref[...]
    lane = lax.broadcasted_iota(jnp.int32, logits.shape, 1)
    logits = jnp.where(lane < N_EXP, logits, -jnp.inf)
    m1 = jnp.max(logits, axis=-1, keepdims=True)
    i1 = jnp.min(jnp.where(logits == m1, lane, 128), axis=-1, keepdims=True)
    rest = jnp.where(lane == i1, -jnp.inf, logits)
    m2 = jnp.max(rest, axis=-1, keepdims=True)
    i2 = jnp.min(jnp.where(rest == m2, lane, 128), axis=-1, keepdims=True)
    e = jnp.exp(m2 - m1)
    g1 = 1.0 / (1.0 + e)
    g2 = e * g1
    pick = jnp.where((lane == i1) | (lane == i2), 1.0, 0.0)
    before = carry[...] + jnp.dot(tri_ref[...], pick.astype(BF16), preferred_element_type=F32)
    r1 = jnp.sum(jnp.where(lane == i1, before, 0.0), axis=-1, keepdims=True)
    r2 = jnp.sum(jnp.where(lane == i2, before, 0.0), axis=-1, keepdims=True)
    carry[...] = carry[...] + jnp.sum(pick, axis=0, keepdims=True)
    ints = jnp.where(lane == 0, i1, jnp.where(lane == 1, i2, 0))
    ranks = jnp.where(lane == 2, r1, jnp.where(lane == 3, r2, 0.0))
    ei_ref[...] = ints + ranks.astype(jnp.int32)
    gf_ref[...] = jnp.where(lane == 0, g1, jnp.where(lane == 1, g2, 0.0))
    cnt_ref[...] = carry[...].astype(jnp.int32)


def _router(h, router_w, router_b):
    rw = jnp.zeros((D, 128), F32).at[:, :N_EXP].set(router_w)
    rb = jnp.zeros((1, 128), F32).at[0, :N_EXP].set(router_b)
    tri = jnp.asarray(np.tril(np.ones((ROUTE_TM, ROUTE_TM), np.float32), -1), BF16)
    row = pl.BlockSpec((ROUTE_TM, 128), lambda i: (i, 0))
    return pl.pallas_call(
        _router_kernel,
        grid=(T_ALL // ROUTE_TM,),
        in_specs=[pl.BlockSpec((ROUTE_TM, D), lambda i: (i, 0)),
                  pl.BlockSpec((D, 128), lambda i: (0, 0)),
                  pl.BlockSpec((1, 128), lambda i: (0, 0)),
                  pl.BlockSpec((ROUTE_TM, ROUTE_TM), lambda i: (0, 0))],
        out_specs=[row, row, pl.BlockSpec((1, 128), lambda i: (0, 0))],
        out_shape=[jax.ShapeDtypeStruct((T_ALL, 128), jnp.int32),
                   jax.ShapeDtypeStruct((T_ALL, 128), F32),
                   jax.ShapeDtypeStruct((1, 128), jnp.int32)],
        scratch_shapes=[pltpu.VMEM((1, 128), F32)],
        compiler_params=_cparams(("arbitrary",)),
        name="moe_router",
    )(h, rw, rb, tri)


def _moe_dispatch_kernel(pos_ref, h_ref, init_ref, out_ref, sem):
    del init_ref
    base = pl.program_id(0) * DISPATCH_TM

    def copy(r, p):
        return pltpu.make_async_copy(h_ref.at[pl.ds(r, 1)], out_ref.at[pl.ds(p, 1)], sem)

    def start(r, c):
        copy(r, pos_ref[base + r]).start(priority=0)
        copy(r, pos_ref[T_ALL + base + r]).start(priority=1)
        return c

    def wait(r, c):
        copy(0, 0).wait()
        copy(0, 0).wait()
        return c

    lax.fori_loop(0, DISPATCH_TM, start, 0, unroll=8)
    lax.fori_loop(0, DISPATCH_TM, wait, 0, unroll=8)


def _moe_dispatch(h, flat_pos, init):
    if init is None:
        init = jnp.zeros((MOE_ROWS, D), F32)
    any_spec = pl.BlockSpec(memory_space=pl.ANY)
    return pl.pallas_call(
        _moe_dispatch_kernel,
        grid_spec=pltpu.PrefetchScalarGridSpec(
            num_scalar_prefetch=1, grid=(T_ALL // DISPATCH_TM,),
            in_specs=[pl.BlockSpec((DISPATCH_TM, D), lambda i, pos: (i, 0)), any_spec],
            out_specs=any_spec,
            scratch_shapes=[pltpu.SemaphoreType.DMA(())]),
        out_shape=jax.ShapeDtypeStruct((MOE_ROWS, D), F32),
        input_output_aliases={2: 0},
        compiler_params=_cparams(("arbitrary",)),
        name="moe_dispatch",
    )(flat_pos, h, init)


def _moe_up_kernel(te_ref, nu_ref, g_ref, wg_ref, wu_ref, o_ref, wbf):
    i = pl.program_id(1)
    new_w = (i == 0) | (te_ref[i] != te_ref[jnp.maximum(i - 1, 0)])

    @pl.when(new_w)
    def _():
        wbf[0] = wg_ref[0, 0].astype(BF16)
        wbf[1] = wu_ref[0, 0].astype(BF16)

    @pl.when(i < nu_ref[0])
    def _():
        _swiglu_chunks(g_ref[...].astype(BF16), wbf, o_ref)

    @pl.when(i >= nu_ref[0])
    def _():
        o_ref[...] = jnp.zeros(o_ref.shape, o_ref.dtype)


def _moe_up(tile_expert, n_used, rows, wg, wu, li, tn):
    wspec = pl.BlockSpec((1, 1, D, tn), lambda j, i, te, nu: (li, te[i], 0, j))
    return pl.pallas_call(
        _moe_up_kernel,
        grid_spec=pltpu.PrefetchScalarGridSpec(
            num_scalar_prefetch=2, grid=(D_FF_E // tn, MOE_TILES),
            in_specs=[pl.BlockSpec((MOE_TM, D), lambda j, i, te, nu: (i, 0)), wspec, wspec],
            out_specs=pl.BlockSpec((MOE_TM, tn), lambda j, i, te, nu: (i, j)),
            scratch_shapes=[pltpu.VMEM((2, D, tn), BF16)]),
        out_shape=jax.ShapeDtypeStruct((MOE_ROWS, D_FF_E), BF16),
        compiler_params=_cparams(("arbitrary", "arbitrary"), VMEM_LIMIT),
        name="moe_up",
    )(tile_expert, n_used, rows, wg, wu)


def _moe_down_kernel(te_ref, nu_ref, a_ref, wd_ref, o_ref, wbf):
    i = pl.program_id(1)
    new_w = (i == 0) | (te_ref[i] != te_ref[jnp.maximum(i - 1, 0)])

    @pl.when(new_w)
    def _():
        wbf[...] = wd_ref[0, 0].astype(BF16)

    @pl.when(i < nu_ref[0])
    def _():
        o_ref[...] = jnp.dot(a_ref[...], wbf[...], preferred_element_type=F32)

    @pl.when(i >= nu_ref[0])
    def _():
        o_ref[...] = jnp.zeros(o_ref.shape, o_ref.dtype)


def _moe_down(tile_expert, n_used, act, wd, li, tn):
    return pl.pallas_call(
        _moe_down_kernel,
        grid_spec=pltpu.PrefetchScalarGridSpec(
            num_scalar_prefetch=2, grid=(D // tn, MOE_TILES),
            in_specs=[pl.BlockSpec((MOE_TM, D_FF_E), lambda j, i, te, nu: (i, 0)),
                      pl.BlockSpec((1, 1, D_FF_E, tn), lambda j, i, te, nu: (li, te[i], 0, j))],
            out_specs=pl.BlockSpec((MOE_TM, tn), lambda j, i, te, nu: (i, j)),
            scratch_shapes=[pltpu.VMEM((D_FF_E, tn), BF16)]),
        out_shape=jax.ShapeDtypeStruct((MOE_ROWS, D), F32),
        compiler_params=_cparams(("arbitrary", "arbitrary"), VMEM_LIMIT),
        name="moe_down",
    )(tile_expert, n_used, act, wd)


def _moe_combine_kernel(pos_ref, x_ref, y_ref, gf_ref, gate_ref, *rest, with_next):
    if with_next:
        g_ref, sc_ref, sh_ref, o_ref, h_ref, ybuf, sems = rest
    else:
        o_ref, ybuf, sems = rest
    i = pl.program_id(0)
    buf = i % 2

    def copy(b, which, r, p):
        return pltpu.make_async_copy(y_ref.at[pl.ds(p, 1)], ybuf.at[b, which, pl.ds(r, 1)], sems.at[b])

    def gather(tile, b):
        base = tile * COMBINE_TM

        def start(r, c):
            copy(b, 0, r, pos_ref[base + r]).start(priority=0)
            copy(b, 1, r, pos_ref[T_ALL + base + r]).start(priority=1)
            return c
        lax.fori_loop(0, COMBINE_TM, start, 0, unroll=8)

    @pl.when(i == 0)
    def _():
        gather(0, 0)

    @pl.when(i + 1 < pl.num_programs(0))
    def _():
        gather(i + 1, 1 - buf)

    def wait(r, c):
        copy(buf, 0, 0, 0).wait()
        copy(buf, 1, 0, 0).wait()
        return c

    lax.fori_loop(0, COMBINE_TM, wait, 0, unroll=8)
    gf = gf_ref[...]
    f = gf[:, 0:1] * ybuf[buf, 0] + gf[:, 1:2] * ybuf[buf, 1]
    x_new = x_ref[...] + gate_ref[0] * f
    o_ref[...] = x_new
    if with_next:
        h_ref[...] = _modulate_math(x_new, g_ref[...], sc_ref[0], sh_ref[0]).astype(h_ref.dtype)


def _moe_combine(x, y, flat_pos, gf, mod, layer, next_norm):
    tm = COMBINE_TM
    base = (layer * 6 + 5) * N_SEG
    row = pl.BlockSpec((tm, D), lambda i, pos: (i, 0))
    in_specs = [row, pl.BlockSpec(memory_space=pl.ANY),
                pl.BlockSpec((tm, 128), lambda i, pos: (i, 0)),
                pl.BlockSpec((1, 1, D), lambda i, pos: (base + _seg_of_row(i * tm), 0, 0))]
    args = [flat_pos, x, y, gf, mod]
    out_specs, out_shape = row, jax.ShapeDtypeStruct((T_ALL, D), F32)
    if next_norm is not None:
        g, n_layer, k_shift = next_norm
        in_specs += [pl.BlockSpec((1, D), lambda i, pos: (0, 0)),
                     _mod_spec(n_layer, k_shift + 1, tm, lambda i, pos: i),
                     _mod_spec(n_layer, k_shift, tm, lambda i, pos: i)]
        args += [g.reshape(1, D), mod, mod]
        out_specs = [row, row]
        out_shape = [out_shape, jax.ShapeDtypeStruct((T_ALL, D), BF16)]
    return pl.pallas_call(
        functools.partial(_moe_combine_kernel, with_next=next_norm is not None),
        grid_spec=pltpu.PrefetchScalarGridSpec(
            num_scalar_prefetch=1, grid=(T_ALL // tm,), in_specs=in_specs, out_specs=out_specs,
            scratch_shapes=[pltpu.VMEM((2, 2, tm, D), F32), pltpu.SemaphoreType.DMA((2,))]),
        out_shape=out_shape,
        compiler_params=_cparams(("arbitrary",)),
        name="moe_combine",
    )(*args)


def _moe(x, h, mod, layer, li, w, next_norm, sorted_init):
    ei, gf, cnt = _router(h, w["moe_router_w"][li], w["moe_router_b"][li])
    experts, ranks, counts = ei[:, 0:2], ei[:, 2:4], cnt[0, :N_EXP]
    tiles = (counts + MOE_TM - 1) // MOE_TM
    tile_end = jnp.cumsum(tiles)
    row0 = (tile_end - tiles) * MOE_TM
    pos = (row0[experts] + ranks).astype(jnp.int32)
    n_used = tile_end[-1:].astype(jnp.int32)
    t_idx = jnp.minimum(jnp.arange(MOE_TILES, dtype=jnp.int32), n_used[0] - 1)
    tile_expert = jnp.sum(t_idx[:, None] >= tile_end[None, :], axis=1).astype(jnp.int32)
    flat_pos = pos.T.reshape(-1)
    sorted_rows = _moe_dispatch(h, flat_pos, sorted_init)
    act = _moe_up(tile_expert, n_used, sorted_rows, w["moe_w_gate"], w["moe_w_up"], li, MOE_UP_TN)
    y = _moe_down(tile_expert, n_used, act, w["moe_w_down"], li, D)
    return _moe_combine(x, y, flat_pos, gf, mod, layer, next_norm), sorted_rows


def _odd_mixer_layer(x, h, mod, l, i, w, spectra, consts, sorted_init):
    p = _mm([h], [w["odd_w_in"]], tm=TM_PROJ, tn=TN_ODD_IN, wi=i)
    cw = w["hy_conv_w"][i]
    z_c = _hy_conv(p, cw, spectra[SEQ][i], w["hy_bias"][i], consts[SEQ], SEQ, BATCH, 0, D)
    z_l = _hy_conv(p, cw, spectra[DEC_SEQ][i], w["hy_bias"][i], consts[DEC_SEQ], DEC_SEQ, DEC_BATCH,
                   T_CTX // DEC_SEQ, HY_TC_LAT)
    x, h2 = _mm([(z_c, z_l)], [w["odd_w_out"]], tm=TM_EPI, tn=D, epilogue="residual_mod", wi=i,
                resid=x, mod=mod, layer=l, k_gate=2, next_norm=(w["norm_ffn_g"][l], l, 3, F32))
    next_norm = (w["norm_mix_g"][l + 1], l + 1, 0) if l + 1 < DEPTH else None
    out, sorted_rows = _moe(x, h2, mod, l, i, w, next_norm, sorted_init)
    x, h_next = out if next_norm is not None else (out, None)
    return x, h_next, sorted_rows


def kernel(x_prompt, x_sample, state_gdn, cache_nat_k, cache_nat_v, c, c_ctx, ada_w, ada_b, norm_mix_g, norm_ffn_g, even_w_in, gdn_conv_w, gdn_a_log, gdn_dt_bias, gdn_norm_g, nat_q_norm_g, nat_k_norm_g, nat_rpb, even_w_out, ffn_w_gate, ffn_w_up, ffn_w_down, odd_w_in, hy_conv_w, hy_w1, hy_b1, hy_w2, hy_b2, hy_w3, hy_freq, hy_bias, odd_w_out, moe_router_w, moe_router_b, moe_w_gate, moe_w_up, moe_w_down):
    w = dict(state_gdn=state_gdn, cache_nat_k=cache_nat_k, cache_nat_v=cache_nat_v,
             norm_mix_g=norm_mix_g, norm_ffn_g=norm_ffn_g, even_w_in=even_w_in, gdn_conv_w=gdn_conv_w,
             gdn_a_log=gdn_a_log, gdn_dt_bias=gdn_dt_bias, gdn_norm_g=gdn_norm_g,
             nat_q_norm_g=nat_q_norm_g, nat_k_norm_g=nat_k_norm_g, nat_rpb=nat_rpb, even_w_out=even_w_out,
             ffn_w_gate=ffn_w_gate, ffn_w_up=ffn_w_up, ffn_w_down=ffn_w_down, odd_w_in=odd_w_in,
             hy_conv_w=hy_conv_w, hy_bias=hy_bias, odd_w_out=odd_w_out, moe_router_w=moe_router_w,
             moe_router_b=moe_router_b, moe_w_gate=moe_w_gate, moe_w_up=moe_w_up, moe_w_down=moe_w_down)
    assert SEQ == ROW_TILE and DEC_SEQ % ROW_TILE == 0
    mod = _ada_all(c, c_ctx, ada_w, ada_b)
    consts = {s: _dft_consts(s) for s in (SEQ, DEC_SEQ)}
    n_odd = DEPTH // 2
    spectra = {s: [_hy_filter_spectrum(s, consts[s], hy_w1[i], hy_b1[i], hy_w2[i], hy_b2[i], hy_w3[i], hy_freq[i])
                   for i in range(n_odd)] for s in (SEQ, DEC_SEQ)}
    x = jnp.concatenate([x_prompt.reshape(T_CTX, D), x_sample.reshape(T_LAT, D)], axis=0)
    h = _modulate(x, norm_mix_g[0], mod, 0, 0, BF16)
    states, kns, pms = [], [], []
    sorted_rows = None
    for l in range(DEPTH):
        i = l // 2
        if l % 2 == 0:
            x, h, kn, p_main, s_ctx = _even_mixer_layer(x, h, mod, l, i, w)
            states.append(s_ctx)
            kns.append(kn)
            pms.append(p_main)
        else:
            x, h, sorted_rows = _odd_mixer_layer(x, h, mod, l, i, w, spectra, consts, sorted_rows)
    k_cache, v_cache, state_out = _finalize_caches(kns, pms, states)
    y_prompt = x[:T_CTX].reshape(BATCH, SEQ, D)
    y_sample = x[T_CTX:].reshape(DEC_BATCH, DEC_SEQ, D)
    return (y_prompt, y_sample, state_out, k_cache, v_cache)
```

```python
import functools
import math

import jax
import jax.numpy as jnp
import numpy as np
from jax import lax
from jax.experimental import pallas as pl
from jax.experimental.pallas import tpu as pltpu

F32 = jnp.float32
BF16 = jnp.bfloat16

D = 1024
BATCH = 16
SEQ = 256
DEPTH = 4
DEC_BATCH = 2
DEC_SEQ = 1024
PAST_LEN = 512
GRID_W = 64
EPS = 1e-6
H_A = 4
DK_A = 128
DV_A = 128
CHUNK = 64
H_B = 8
DH_B = 64
WIN_R = 8
WIN_C = 16
A_QKV = 2 * H_A * DK_A + H_A * DV_A
A_GATE = H_A * DV_A
B_QKV = 3 * H_B * DH_B
HY_EMB = 33
HY_HID = 64
HY_FAST = 0.3
HY_SLOW = 1.5
HY_TARGET = 1e-2
D_FF = 2816
N_EXP = 8
D_FF_E = 3584

T_CTX = BATCH * SEQ
T_LAT = DEC_BATCH * DEC_SEQ
T_ALL = T_CTX + T_LAT
N_SEG = 8
ROW_TILE = 256

VMEM_LIMIT = 56 * 1024 * 1024
LANES = 128

TM_PROJ = 1024
TN_EVEN_IN = 1792
TN_ODD_IN = 1536
TM_EPI = 512
TN_FFN_UP = D_FF // 2
MOE_UP_TN = D_FF_E // 2
HY_TC_LAT = 512
GDN_NS_CTX = 8
GDN_NS_LAT = 2


def _cparams(sem, vmem=None):
    return pltpu.CompilerParams(dimension_semantics=sem, vmem_limit_bytes=vmem)


def _sigmoid(x):
    return 1.0 / (1.0 + jnp.exp(-x))


def _seg_of_row(row):
    return jnp.where(row < T_CTX, 0, 1 + (row - T_CTX) // DEC_SEQ)


def _split_bf16(x):
    hi = x.astype(BF16)
    lo = (x - hi.astype(F32)).astype(BF16)
    return hi, lo


def _dot(a, b, dims=(((1,), (0,)), ((), ())), prec="bf16"):
    if prec == "bf16":
        return lax.dot_general(a.astype(BF16), b.astype(BF16), dims, preferred_element_type=F32)
    ah, al = _split_bf16(a.astype(F32))
    bh, bl = _split_bf16(b.astype(F32))
    r = lax.dot_general(ah, bh, dims, preferred_element_type=F32)
    r = r + lax.dot_general(ah, bl, dims, preferred_element_type=F32)
    r = r + lax.dot_general(al, bh, dims, preferred_element_type=F32)
    return r


NT_DIMS = (((1,), (1,)), ((), ()))


def _ada_kernel(cv_ref, w_ref, b_ref, o_ref):
    cv = cv_ref[...]
    s = cv * _sigmoid(cv)
    o_ref[0] = _dot(s, w_ref[0]) + b_ref[0]


def _ada_all(c, c_ctx, ada_w, ada_b):
    cv = jnp.zeros((N_SEG, D), F32).at[0].set(c_ctx).at[1:1 + DEC_BATCH].set(c)
    tn = 1536
    out = pl.pallas_call(
        _ada_kernel,
        grid=(DEPTH, 6 * D // tn),
        in_specs=[
            pl.BlockSpec((N_SEG, D), lambda l, j: (0, 0)),
            pl.BlockSpec((1, D, tn), lambda l, j: (l, 0, j)),
            pl.BlockSpec((1, 1, tn), lambda l, j: (l, 0, j)),
        ],
        out_specs=pl.BlockSpec((1, N_SEG, tn), lambda l, j: (l, 0, j)),
        out_shape=jax.ShapeDtypeStruct((DEPTH, N_SEG, 6 * D), F32),
        compiler_params=_cparams(("arbitrary", "arbitrary"), VMEM_LIMIT),
        name="ada",
    )(cv, ada_w, ada_b.reshape(DEPTH, 1, 6 * D))
    return out.reshape(DEPTH, N_SEG, 6, D).transpose(0, 2, 1, 3).reshape(DEPTH * 6 * N_SEG, 1, D)


def _mod_spec(layer, k, tm, row_of_step):
    base = (layer * 6 + k) * N_SEG

    def imap(*ids):
        return (base + _seg_of_row(row_of_step(*ids) * tm), 0, 0)

    return pl.BlockSpec((1, 1, D), imap)


def _modulate_math(x, g, scale, shift):
    ms = jnp.mean(x * x, axis=-1, keepdims=True)
    y = x * lax.rsqrt(ms + EPS) * g
    return y * (1.0 + scale) + shift


def _modulate_kernel(xc_ref, xl_ref, g_ref, sc_ref, sh_ref, x_ref, o_ref, *, n_ctx_tiles):
    x = jnp.where(pl.program_id(0) < n_ctx_tiles, xc_ref[...], xl_ref[...])
    x_ref[...] = x
    o_ref[...] = _modulate_math(x, g_ref[...], sc_ref[0], sh_ref[0]).astype(o_ref.dtype)


def _modulate_first(x_ctx, x_lat, g, mod, out_dtype):
    tm = TM_EPI
    n_ctx_tiles = T_CTX // tm
    row = pl.BlockSpec((tm, D), lambda i: (i, 0))
    return pl.pallas_call(
        functools.partial(_modulate_kernel, n_ctx_tiles=n_ctx_tiles),
        grid=(T_ALL // tm,),
        in_specs=[
            pl.BlockSpec((tm, D), lambda i: (jnp.minimum(i, n_ctx_tiles - 1), 0)),
            pl.BlockSpec((tm, D), lambda i: (jnp.maximum(i - n_ctx_tiles, 0), 0)),
            pl.BlockSpec((1, D), lambda i: (0, 0)),
            _mod_spec(0, 1, tm, lambda i: i),
            _mod_spec(0, 0, tm, lambda i: i),
        ],
        out_specs=[row, row],
        out_shape=[jax.ShapeDtypeStruct((T_ALL, D), F32), jax.ShapeDtypeStruct((T_ALL, D), out_dtype)],
        compiler_params=_cparams(("arbitrary",)),
        name="modulate",
    )(x_ctx, x_lat, g.reshape(1, D), mod, mod)


SWIGLU_CHUNK = 256


def _swiglu_chunks(h, wbf, o_ref):
    n = o_ref.shape[1]
    for c0 in range(0, n, SWIGLU_CHUNK):
        c1 = min(c0 + SWIGLU_CHUNK, n)
        a = jnp.dot(h, wbf[0, :, c0:c1], preferred_element_type=F32)
        b = jnp.dot(h, wbf[1, :, c0:c1], preferred_element_type=F32)
        o_ref[:, c0:c1] = (a * _sigmoid(a) * b).astype(o_ref.dtype)


EVEN_SHIFT = 4 * H_A
EVEN_SPLIT = A_QKV + A_GATE


def _mm_kernel(*refs, chunks, n_w, epilogue, shifted, n_ctx_tiles):
    pos = 0
    lhs = []
    for kw, paired in chunks:
        cnt = 2 if paired else 1
        lhs.append(refs[pos:pos + cnt])
        pos += cnt
    n_wrefs = n_w + (1 if shifted else 0)
    w_refs = refs[pos:pos + n_wrefs]
    rest = refs[pos + n_wrefs:]
    j, i = pl.program_id(0), pl.program_id(1)
    if epilogue == "residual":
        x_ref, gate_ref, o_ref, wbf = rest
    elif epilogue == "residual_mod":
        x_ref, gate_ref, g_ref, sc_ref, sh_ref, o_ref, h_ref, wbf = rest
    else:
        o_ref, wbf = rest

    @pl.when(i == 0)
    def _():
        if shifted:
            tn = wbf.shape[2]
            for jj in range(shifted):
                split = min(max(EVEN_SPLIT - jj * tn, 0), tn)

                @pl.when(j == jj)
                def _(split=split):
                    wa = w_refs[0][0]
                    if split == tn:
                        wbf[0] = wa.astype(BF16)
                    else:
                        parts = [wa[:, :split]] if split else []
                        parts += [wa[:, split + EVEN_SHIFT:], w_refs[1][0][:, :EVEN_SHIFT]]
                        wbf[0] = jnp.concatenate(parts, axis=1).astype(BF16)
        else:
            for k in range(n_w):
                wbf[k] = w_refs[k][0].astype(BF16)

    def lhs_chunk(c):
        r = lhs[c]
        if len(r) == 2:
            return jnp.where(i < n_ctx_tiles, r[0][...], r[1][...]).astype(BF16)
        return r[0][...].astype(BF16)

    def matmul(k):
        acc, off = None, 0
        for c, (kw, _) in enumerate(chunks):
            part = jnp.dot(lhs_chunk(c), wbf[k, off:off + kw, :], preferred_element_type=F32)
            acc = part if acc is None else acc + part
            off += kw
        return acc

    if epilogue == "swiglu":
        _swiglu_chunks(lhs_chunk(0), wbf, o_ref)
        return
    a = matmul(0)
    if epilogue == "residual":
        o_ref[...] = x_ref[...] + gate_ref[0] * a
    elif epilogue == "residual_mod":
        x_new = x_ref[...] + gate_ref[0] * a
        o_ref[...] = x_new
        h_ref[...] = _modulate_math(x_new, g_ref[...], sc_ref[0], sh_ref[0]).astype(h_ref.dtype)
    else:
        o_ref[...] = a.astype(o_ref.dtype)


def _mm(lhs, ws, *, tm, tn, out_dtype=F32, epilogue="none", wi=0, n_out=None, col0=0, shifted=False,
        resid=None, mod=None, layer=None, k_gate=None, next_norm=None):
    chunks, args, in_specs = [], [], []
    n_ctx_tiles = T_CTX // tm
    for part in lhs:
        if isinstance(part, tuple):
            kw = part[0].shape[1]
            chunks.append((kw, True))
            args += [part[0], part[1]]
            in_specs += [pl.BlockSpec((tm, kw), lambda j, i: (jnp.minimum(i, n_ctx_tiles - 1), 0)),
                         pl.BlockSpec((tm, kw), lambda j, i: (jnp.maximum(i - n_ctx_tiles, 0), 0))]
        else:
            kw = part.shape[1]
            chunks.append((kw, False))
            args.append(part)
            in_specs.append(pl.BlockSpec((tm, kw), lambda j, i: (i, 0)))
    kdim = sum(kw for kw, _ in chunks)
    n = n_out if n_out is not None else ws[0].shape[2]
    n_w = len(ws)
    if shifted:
        last = pl.cdiv(ws[0].shape[2], tn) - 1
        in_specs += [pl.BlockSpec((1, kdim, tn), lambda j, i: (wi, 0, j)),
                     pl.BlockSpec((1, kdim, tn), lambda j, i: (wi, 0, jnp.minimum(j + 1, last)))]
        args += [ws[0], ws[0]]
    else:
        in_specs += [pl.BlockSpec((1, kdim, tn), lambda j, i: (wi, 0, col0 + j)) for _ in ws]
        args += list(ws)
    out_specs = pl.BlockSpec((tm, tn), lambda j, i: (i, j))
    out_shape = jax.ShapeDtypeStruct((T_ALL, n), out_dtype)
    if epilogue in ("residual", "residual_mod"):
        base = (layer * 6 + k_gate) * N_SEG
        in_specs += [pl.BlockSpec((tm, tn), lambda j, i: (i, j)),
                     pl.BlockSpec((1, 1, tn), lambda j, i: (base + _seg_of_row(i * tm), 0, j))]
        args += [resid, mod]
    if epilogue == "residual_mod":
        g, n_layer, k_shift, h_dtype = next_norm
        assert tn == n == D
        in_specs += [pl.BlockSpec((1, D), lambda j, i: (0, 0)),
                     _mod_spec(n_layer, k_shift + 1, tm, lambda j, i: i),
                     _mod_spec(n_layer, k_shift, tm, lambda j, i: i)]
        args += [g.reshape(1, D), mod, mod]
        out_specs = [out_specs, pl.BlockSpec((tm, tn), lambda j, i: (i, j))]
        out_shape = [out_shape, jax.ShapeDtypeStruct((T_ALL, n), h_dtype)]
    return pl.pallas_call(
        functools.partial(_mm_kernel, chunks=tuple(chunks), n_w=n_w, epilogue=epilogue,
                          shifted=(n // tn if shifted else 0), n_ctx_tiles=n_ctx_tiles),
        grid=(n // tn, T_ALL // tm),
        in_specs=in_specs,
        out_specs=out_specs,
        out_shape=out_shape,
        scratch_shapes=[pltpu.VMEM((n_w, kdim, tn), BF16)],
        compiler_params=_cparams(("arbitrary", "arbitrary"), VMEM_LIMIT),
        name="mm_" + epilogue,
    )(*args)


def _conv3(x, prev_ref, next_ref, w, seq_len):
    i = pl.program_id(0)
    rows_n = x.shape[0]
    if seq_len is None:
        j = jnp.maximum(i - T_CTX // rows_n, 0)
        per = DEC_SEQ // rows_n
        is_ctx = i < T_CTX // rows_n
        first = is_ctx | (j % per == 0)
        last = is_ctx | (j % per == per - 1)
    else:
        first = (i * rows_n) % seq_len == 0
        last = ((i + 1) * rows_n) % seq_len == 0
    prev_row = jnp.where(first, 0.0, prev_ref[7:8, :])
    next_row = jnp.where(last, 0.0, next_ref[0:1, :])
    rows = lax.broadcasted_iota(jnp.int32, x.shape, 0)
    xm = jnp.where(rows == 0, prev_row, pltpu.roll(x, 1, 0))
    xp = jnp.where(rows == rows_n - 1, next_row, pltpu.roll(x, rows_n - 1, 0))
    return xm * w[0:1, :] + x * w[1:2, :] + xp * w[2:3, :]


def _halo_specs(width, col_block, row_off_tiles):
    per = ROW_TILE // 8
    last_blk = T_ALL // 8 - 1
    main = pl.BlockSpec((ROW_TILE, width), lambda i: (i + row_off_tiles, col_block))
    prev = pl.BlockSpec((8, width), lambda i: (jnp.maximum((i + row_off_tiles) * per - 1, 0), col_block))
    nxt = pl.BlockSpec((8, width), lambda i: (jnp.minimum((i + row_off_tiles + 1) * per, last_blk), col_block))
    return main, prev, nxt


def _cumsum_rows(x, reverse):
    n = x.shape[0]
    rows = lax.broadcasted_iota(jnp.int32, x.shape, 0)
    k = 1
    while k < n:
        if reverse:
            x = x + jnp.where(rows < n - k, pltpu.roll(x, n - k, 0), 0.0)
        else:
            x = x + jnp.where(rows >= k, pltpu.roll(x, k, 0), 0.0)
        k *= 2
    return x


def _gdn_prep_kernel(pm_ref, prev_ref, next_ref, ps_ref, cw_ref, par_ref, qkv_ref, bg_ref, *, seq_len):
    x = _conv3(pm_ref[...], prev_ref, next_ref, cw_ref[...], seq_len)
    x = x * _sigmoid(x)
    for h in range(H_A):
        sl = slice(h * DK_A, (h + 1) * DK_A)
        qh = x[:, sl]
        qkv_ref[:, sl] = qh * lax.rsqrt(jnp.sum(qh * qh, axis=-1, keepdims=True) + EPS) * (DK_A ** -0.5)
        sl = slice(H_A * DK_A + h * DK_A, H_A * DK_A + (h + 1) * DK_A)
        kh = x[:, sl]
        qkv_ref[:, sl] = kh * lax.rsqrt(jnp.sum(kh * kh, axis=-1, keepdims=True) + EPS)
    qkv_ref[:, 2 * H_A * DK_A:] = x[:, 2 * H_A * DK_A:]
    raw = ps_ref[...]
    lane = lax.broadcasted_iota(jnp.int32, raw.shape, 1)
    beta = _sigmoid(raw)
    z = raw + par_ref[1:2, :]
    softplus = jnp.maximum(z, 0.0) + jnp.log(1.0 + jnp.exp(-jnp.abs(z)))
    g = -jnp.exp(par_ref[0:1, :]) * softplus
    bg_ref[...] = jnp.where(lane < 2 * H_A, beta, jnp.where(lane < 4 * H_A, g, 0.0))


def _gdn_prep(p_main, p_small, conv_w, a_log, dt_bias, row_off_tiles, n_rows, seq_len):
    par = jnp.zeros((2, 128), F32)
    par = par.at[0, 2 * H_A:4 * H_A].set(a_log.reshape(-1)).at[1, 2 * H_A:4 * H_A].set(dt_bias.reshape(-1))
    main, prev, nxt = _halo_specs(A_QKV, 0, row_off_tiles)
    return pl.pallas_call(
        functools.partial(_gdn_prep_kernel, seq_len=seq_len),
        grid=(n_rows // ROW_TILE,),
        in_specs=[main, prev, nxt,
                  pl.BlockSpec((ROW_TILE, 128), lambda i: (i + row_off_tiles, 0)),
                  pl.BlockSpec((3, A_QKV), lambda i: (0, 0)),
                  pl.BlockSpec((2, 128), lambda i: (0, 0))],
        out_specs=[pl.BlockSpec((ROW_TILE, A_QKV), lambda i: (i, 0)),
                   pl.BlockSpec((ROW_TILE, 128), lambda i: (i, 0))],
        out_shape=[jax.ShapeDtypeStruct((n_rows, A_QKV), F32),
                   jax.ShapeDtypeStruct((n_rows, 128), F32)],
        compiler_params=_cparams(("arbitrary",)),
        name="gdn_prep",
    )(p_main, p_main, p_main, p_small, conv_w, par)


B_NN = (((2,), (1,)), ((0,), (0,)))
B_NT = (((2,), (2,)), ((0,), (0,)))
B_TN = (((1,), (1,)), ((0,), (0,)))


def _bmm(a, b, dims=B_NN):
    return _dot(a, b, dims)


def _gdn_scanb_kernel(*refs, ns, has_s0):
    if has_s0:
        qf_ref, qb_ref, bf_ref, bb_ref, s0_ref, of_ref, ob_ref, sfin_ref, st = refs
    else:
        qf_ref, qb_ref, bf_ref, bb_ref, of_ref, ob_ref, sfin_ref, st = refs
    step = pl.program_id(1)
    c = CHUNK
    nb = 2 * ns * H_A

    @pl.when(step == 0)
    def _():
        for d in range(2):
            st[d] = s0_ref[:, 0, d] if has_s0 else jnp.zeros(st.shape[1:], F32)

    q_l, k_l, v_l, beta_l, gc_l = [], [], [], [], []
    for d in range(2):
        for s in range(ns):
            qkv = (qf_ref if d == 0 else qb_ref)[s, 0]
            bg = (bf_ref if d == 0 else bb_ref)[s, 0]
            gcum = _cumsum_rows(bg, reverse=(d == 1))
            for h in range(H_A):
                q_l.append(qkv[:, h * DK_A:(h + 1) * DK_A])
                k_l.append(qkv[:, (H_A + h) * DK_A:(H_A + h + 1) * DK_A])
                v_l.append(qkv[:, 2 * H_A * DK_A + h * DV_A:2 * H_A * DK_A + (h + 1) * DV_A])
                col = d * H_A + h
                beta_l.append(bg[:, col:col + 1])
                gc_l.append(gcum[:, 2 * H_A + col:2 * H_A + col + 1])
    q, k, v = jnp.stack(q_l), jnp.stack(k_l), jnp.stack(v_l)
    beta, gc = jnp.stack(beta_l), jnp.stack(gc_l)

    ri = lax.broadcasted_iota(jnp.int32, (1, c, c), 1)
    ci = lax.broadcasted_iota(jnp.int32, (1, c, c), 2)
    eye = ri == ci
    half = nb // 2
    gc_row = jnp.sum(jnp.where(eye, gc, 0.0), axis=1, keepdims=True)
    dgc = gc - gc_row

    def masked_exp(x, keep):
        return jnp.where(keep, jnp.exp(jnp.where(keep, x, 0.0)), 0.0)

    decay = jnp.concatenate([masked_exp(dgc[:half], ri >= ci), masked_exp(dgc[half:], ri <= ci)], axis=0)
    kb = k * beta
    a_mat = jnp.where(eye, 0.0, _bmm(kb, k, B_NT) * decay)
    blk = 8
    diag = (ri // blk) == (ci // blk)
    pw = jnp.where(diag, a_mat, 0.0)
    r_mat = -pw
    for _ in range(2):
        pw = _bmm(pw, pw)
        r_mat = r_mat + pw + _bmm(r_mat, pw)
    while blk < c:
        off = ((ri // (2 * blk)) == (ci // (2 * blk))) & ((ri // blk) != (ci // blk))
        e_mat = jnp.where(off, a_mat, 0.0)
        x_mat = e_mat + _bmm(r_mat, e_mat)
        r_mat = r_mat - (x_mat + _bmm(x_mat, r_mat))
        blk *= 2
    eg = jnp.exp(gc)
    rhs = jnp.concatenate([v * beta, kb * eg], axis=2)
    sol = rhs + _bmm(r_mat, rhs)
    ub, wm = sol[:, :, :DV_A], sol[:, :, DV_A:]
    attn = _bmm(q, k, B_NT) * decay
    fwd1 = lax.broadcasted_iota(jnp.int32, (nb, 1, 1), 0) < nb // 2
    g_end = jnp.where(fwd1, gc[:, c - 1:c, :], gc[:, 0:1, :])
    qd = q * eg
    kd = k * jnp.exp(g_end - gc)
    state = st[...].reshape(nb, DK_A, DV_A)
    u = ub - _bmm(wm, state)
    o = _bmm(qd, state) + _bmm(attn, u)
    state = state * jnp.exp(g_end) + _bmm(kd, u, B_TN)
    st[...] = state.reshape(st.shape)
    for d in range(2):
        o_ref = of_ref if d == 0 else ob_ref
        for s in range(ns):
            for h in range(H_A):
                o_ref[s, 0, :, h * DV_A:(h + 1) * DV_A] = o[(d * ns + s) * H_A + h]

    @pl.when(step == pl.num_programs(1) - 1)
    def _():
        for d in range(2):
            sfin_ref[:, d] = st[d]


def _gdn_scanb(qkv, bg, s0, layer_i, n_seq, seq_len, ns):
    n = seq_len // CHUNK
    qkv4 = qkv.reshape(n_seq, n, CHUNK, A_QKV)
    bg4 = bg.reshape(n_seq, n, CHUNK, 128)
    fwd = lambda g, c: (g, c, 0, 0)
    bwd = lambda g, c: (g, n - 1 - c, 0, 0)
    in_specs = [pl.BlockSpec((ns, 1, CHUNK, A_QKV), fwd), pl.BlockSpec((ns, 1, CHUNK, A_QKV), bwd),
                pl.BlockSpec((ns, 1, CHUNK, 128), fwd), pl.BlockSpec((ns, 1, CHUNK, 128), bwd)]
    args = [qkv4, qkv4, bg4, bg4]
    if s0 is not None:
        in_specs.append(pl.BlockSpec((ns, 1, 2, H_A, DK_A, DV_A), lambda g, c: (g, layer_i, 0, 0, 0, 0)))
        args.append(s0)
    o_shape = jax.ShapeDtypeStruct((n_seq, n, CHUNK, H_A * DV_A), F32)
    of, ob, sfin = pl.pallas_call(
        functools.partial(_gdn_scanb_kernel, ns=ns, has_s0=s0 is not None),
        grid=(n_seq // ns, n),
        in_specs=in_specs,
        out_specs=[pl.BlockSpec((ns, 1, CHUNK, H_A * DV_A), fwd),
                   pl.BlockSpec((ns, 1, CHUNK, H_A * DV_A), bwd),
                   pl.BlockSpec((ns, 2, H_A, DK_A, DV_A), lambda g, c: (g, 0, 0, 0, 0))],
        out_shape=[o_shape, o_shape, jax.ShapeDtypeStruct((n_seq, 2, H_A, DK_A, DV_A), F32)],
        scratch_shapes=[pltpu.VMEM((2, ns, H_A, DK_A, DV_A), F32)],
        compiler_params=_cparams(("arbitrary", "arbitrary"), VMEM_LIMIT),
        name="gdn_scan",
    )(*args)
    rows = n_seq * seq_len
    return of.reshape(rows, H_A * DV_A), ob.reshape(rows, H_A * DV_A), sfin


def _gdn_post_kernel(of_ref, ob_ref, gate_ref, g_ref, o_ref):
    o = of_ref[...] + ob_ref[...]
    gate = gate_ref[...]
    for h in range(H_A):
        sl = slice(h * DV_A, (h + 1) * DV_A)
        oh = o[:, sl]
        y = oh * lax.rsqrt(jnp.mean(oh * oh, axis=-1, keepdims=True) + EPS) * g_ref[...]
        gh = gate[:, sl]
        o_ref[:, sl] = (y * (gh * _sigmoid(gh))).astype(o_ref.dtype)


def _gdn_post(of, ob, p_main, gdn_g, row_off_tiles, n_rows):
    gate_blk = A_QKV // A_GATE
    return pl.pallas_call(
        _gdn_post_kernel,
        grid=(n_rows // ROW_TILE,),
        in_specs=[pl.BlockSpec((ROW_TILE, A_GATE), lambda i: (i, 0)),
                  pl.BlockSpec((ROW_TILE, A_GATE), lambda i: (i, 0)),
                  pl.BlockSpec((ROW_TILE, A_GATE), lambda i: (i + row_off_tiles, gate_blk)),
                  pl.BlockSpec((1, DV_A), lambda i: (0, 0))],
        out_specs=pl.BlockSpec((ROW_TILE, A_GATE), lambda i: (i, 0)),
        out_shape=jax.ShapeDtypeStruct((n_rows, A_GATE), BF16),
        compiler_params=_cparams(("arbitrary",)),
        name="gdn_post",
    )(of, ob, p_main, gdn_g.reshape(1, DV_A))


NAT_W = H_B * DH_B
NAT_QCOL = (A_QKV + A_GATE) // NAT_W
NAT_SCALE = DH_B ** -0.5


def _nat_prep_kernel(q_ref, k_ref, bd_ref, gq_ref, gk_ref, qn_ref, kn_ref):
    bd = bd_ref[...]
    for x_ref, g_ref, o_ref in ((q_ref, gq_ref, qn_ref), (k_ref, gk_ref, kn_ref)):
        x = x_ref[...]
        ms = jnp.dot((x * x).astype(BF16), bd, preferred_element_type=F32)
        o_ref[...] = x * lax.rsqrt(ms + EPS) * g_ref[...]


def _nat_prep(p_main, qn_g, kn_g):
    grp = np.arange(NAT_W) // DH_B
    bd = jnp.asarray((grp[:, None] == grp[None, :]).astype(np.float32) / DH_B, BF16)
    spec = lambda cb: pl.BlockSpec((ROW_TILE, NAT_W), lambda i: (i, cb))
    return pl.pallas_call(
        _nat_prep_kernel,
        grid=(T_ALL // ROW_TILE,),
        in_specs=[spec(NAT_QCOL), spec(NAT_QCOL + 1),
                  pl.BlockSpec((NAT_W, NAT_W), lambda i: (0, 0)),
                  pl.BlockSpec((1, NAT_W), lambda i: (0, 0)),
                  pl.BlockSpec((1, NAT_W), lambda i: (0, 0))],
        out_specs=[spec(0), spec(0)],
        out_shape=[jax.ShapeDtypeStruct((T_ALL, NAT_W), F32)] * 2,
        compiler_params=_cparams(("arbitrary",)),
        name="nat_prep",
    )(p_main, p_main, bd, jnp.tile(qn_g, H_B).reshape(1, NAT_W), jnp.tile(kn_g, H_B).reshape(1, NAT_W))


def _pair_masks():
    lane = lax.broadcasted_iota(jnp.int32, (1, 2 * DH_B), 1)
    return lane < DH_B


def _nat_ctx_kernel(q_ref, k_ref, v_ref, o_ref):
    lo = _pair_masks()
    for p in range(H_B // 2):
        sl = slice(p * 2 * DH_B, (p + 1) * 2 * DH_B)
        q2, k2, v2 = q_ref[:, sl], k_ref[:, sl], v_ref[:, sl]
        halves = []
        for half in range(2):
            qm = jnp.where(lo if half == 0 else jnp.logical_not(lo), q2, 0.0)
            s = _dot(qm, k2, NT_DIMS) * NAT_SCALE
            e = jnp.exp(s - jnp.max(s, axis=-1, keepdims=True))
            pr = e / jnp.sum(e, axis=-1, keepdims=True)
            halves.append(_dot(pr, v2))
        o_ref[:, sl] = jnp.where(lo, halves[0], halves[1]).astype(o_ref.dtype)


def _nat_ctx(qn, kn, p_main):
    spec = lambda cb: pl.BlockSpec((SEQ, NAT_W), lambda b: (b, cb))
    return pl.pallas_call(
        _nat_ctx_kernel,
        grid=(BATCH,),
        in_specs=[spec(0), spec(0), spec(NAT_QCOL + 2)],
        out_specs=spec(0),
        out_shape=jax.ShapeDtypeStruct((T_CTX, NAT_W), BF16),
        compiler_params=_cparams(("arbitrary",)),
        name="nat_ctx",
    )(qn, kn, p_main)


def _nat_bias_kernel(r_ref, e_ref, ok_ref, o_ref):
    r = r_ref[...]
    hi, lo = _split_bf16(r)
    lo2 = (r - hi.astype(F32) - lo.astype(F32)).astype(BF16)
    e = e_ref[...]
    t = (jnp.dot(hi, e, preferred_element_type=F32) + jnp.dot(lo, e, preferred_element_type=F32)
         + jnp.dot(lo2, e, preferred_element_type=F32))
    o_ref[...] = jnp.where(ok_ref[...] > 0.5, t, -jnp.inf)


def _nat_bias(rpb):
    n_dr, n_dc = 2 * WIN_R - 1, 2 * WIN_C - 1
    qc = np.arange(GRID_W)[:, None]
    kc = np.arange(GRID_W)[None, :]
    dc = (kc - qc + WIN_C - 1).reshape(-1)
    c0 = np.clip(qc - WIN_C // 2, 0, GRID_W - WIN_C)
    ok = ((kc >= c0) & (kc < c0 + WIN_C)).reshape(1, -1).astype(np.float32)
    onehot = (np.arange(128)[:, None] == dc[None, :]).astype(np.float32)
    rows = H_B * n_dr
    rp = jnp.zeros((rows, 128), F32).at[:, :n_dc].set(rpb.reshape(rows, n_dc))
    tab = pl.pallas_call(
        _nat_bias_kernel,
        out_shape=jax.ShapeDtypeStruct((rows, GRID_W * GRID_W), F32),
        name="nat_bias",
    )(rp, jnp.asarray(onehot, BF16), jnp.asarray(ok))
    tab = tab.reshape(H_B, n_dr, GRID_W, GRID_W)
    pad = jnp.full((H_B, 1, GRID_W, GRID_W), -jnp.inf, F32)
    ext = jnp.concatenate([pad, tab, pad], axis=1)
    return jnp.concatenate([ext[:, :n_dr + 1], ext[:, 1:]], axis=-1)


NAT_QROWS = 2
NAT_KROWS = 10


def _nat_lat_kernel(q_ref, k_ref, v_ref, kc_ref, vc_ref, bias_ref, o_ref):
    rows = DEC_SEQ // GRID_W
    row_a = pl.program_id(1) * NAT_QROWS
    r0s = [jnp.clip(row_a + qi - WIN_R // 2, 0, rows - WIN_R) for qi in range(NAT_QROWS)]
    ws = jnp.minimum(r0s[0], rows - NAT_KROWS)
    start = pl.multiple_of(ws * GRID_W, GRID_W)
    n_loc = NAT_KROWS * GRID_W
    lo = _pair_masks()
    lane = lax.broadcasted_iota(jnp.int32, (GRID_W, 2 * GRID_W), 1)

    def bias_for(h):
        blocks = []
        for qi in range(NAT_QROWS):
            pieces = []
            for jp in range(NAT_KROWS // 2):
                rk = ws + 2 * jp
                ok0 = ((rk >= r0s[qi]) & (rk < r0s[qi] + WIN_R)).astype(jnp.int32)
                ok1 = ((rk + 1 >= r0s[qi]) & (rk + 1 < r0s[qi] + WIN_R)).astype(jnp.int32)
                d = jnp.clip(rk - (row_a + qi) + WIN_R - 1, -1, 2 * WIN_R - 2) + 1
                piece = bias_ref[h, d]
                pieces.append(jnp.where(jnp.where(lane < GRID_W, ok0, ok1) > 0, piece, -jnp.inf))
            blocks.append(jnp.concatenate(pieces, axis=1))
        return jnp.concatenate(blocks, axis=0)

    for p in range(H_B // 2):
        sl = slice(p * 2 * DH_B, (p + 1) * 2 * DH_B)
        q2 = q_ref[:, sl]
        kw = k_ref[0, pl.ds(start, n_loc), sl]
        vw = v_ref[pl.ds(start, n_loc), sl]
        kc, vc = kc_ref[0, :, sl], vc_ref[0, :, sl]
        halves = []
        for half in range(2):
            qm = jnp.where(lo if half == 0 else jnp.logical_not(lo), q2, 0.0)
            s_loc = _dot(qm, kw, NT_DIMS) * NAT_SCALE + bias_for(2 * p + half)
            s_ctx = _dot(qm, kc, NT_DIMS) * NAT_SCALE
            m = jnp.maximum(jnp.max(s_loc, axis=-1, keepdims=True), jnp.max(s_ctx, axis=-1, keepdims=True))
            e_loc, e_ctx = jnp.exp(s_loc - m), jnp.exp(s_ctx - m)
            inv = 1.0 / (jnp.sum(e_loc, axis=-1, keepdims=True) + jnp.sum(e_ctx, axis=-1, keepdims=True))
            halves.append(_dot(e_loc * inv, vw) + _dot(e_ctx * inv, vc))
        o_ref[:, sl] = jnp.where(lo, halves[0], halves[1]).astype(o_ref.dtype)


def _nat_lat(qn, kn, p_main, kc, vc, bias):
    steps = DEC_SEQ // GRID_W // NAT_QROWS
    tq = NAT_QROWS * GRID_W
    lat_tile0 = T_CTX // tq
    lat_seq0 = T_CTX // DEC_SEQ
    return pl.pallas_call(
        _nat_lat_kernel,
        grid=(DEC_BATCH, steps),
        in_specs=[pl.BlockSpec((tq, NAT_W), lambda b, r: (lat_tile0 + b * steps + r, 0)),
                  pl.BlockSpec((1, DEC_SEQ, NAT_W), lambda b, r: (lat_seq0 + b, 0, 0)),
                  pl.BlockSpec((DEC_SEQ, NAT_W), lambda b, r: (lat_seq0 + b, NAT_QCOL + 2)),
                  pl.BlockSpec((1, PAST_LEN, NAT_W), lambda b, r: (b, 0, 0)),
                  pl.BlockSpec((1, PAST_LEN, NAT_W), lambda b, r: (b, 0, 0)),
                  pl.BlockSpec(bias.shape, lambda b, r: (0, 0, 0, 0))],
        out_specs=pl.BlockSpec((tq, NAT_W), lambda b, r: (b * steps + r, 0)),
        out_shape=jax.ShapeDtypeStruct((T_LAT, NAT_W), BF16),
        compiler_params=_cparams(("arbitrary", "arbitrary"), VMEM_LIMIT),
        name="nat_lat",
    )(qn, kn.reshape(T_ALL // DEC_SEQ, DEC_SEQ, NAT_W), p_main, kc, vc, bias)


def _heads_to_lanes(cache):
    b, h, l, dh = cache.shape
    return cache.transpose(0, 2, 1, 3).reshape(b, l, h * dh)


def _finalize_kernel(k0_ref, v0_ref, s0_ref, k1_ref, v1_ref, s1_ref, ko_ref, vo_ref, so_ref):
    layer = pl.program_id(0)
    for idx, (k_ref, v_ref, s_ref) in enumerate(((k0_ref, v0_ref, s0_ref), (k1_ref, v1_ref, s1_ref))):
        @pl.when(layer == idx)
        def _(k_ref=k_ref, v_ref=v_ref, s_ref=s_ref):
            for h in range(H_B):
                ko_ref[0, 0, h] = k_ref[:, h * DH_B:(h + 1) * DH_B]
                vo_ref[0, 0, h] = v_ref[:, h * DH_B:(h + 1) * DH_B]
            so_ref[0, 0] = s_ref[0]


def _finalize_caches(kns, pms, states):
    n_even = len(kns)
    assert n_even == 2
    pick = lambda idx, l, b: b * (l if idx else 1 - l)
    tok = lambda idx, cb: pl.BlockSpec((SEQ, NAT_W), lambda l, b: (pick(idx, l, b), cb))
    st_in = lambda idx: pl.BlockSpec((1, 2, H_A, DK_A, DV_A), lambda l, b: (pick(idx, l, b), 0, 0, 0, 0))
    cache_out = pl.BlockSpec((1, 1, H_B, SEQ, DH_B), lambda l, b: (b, l, 0, 0, 0))
    cache_shape = jax.ShapeDtypeStruct((BATCH, n_even, H_B, SEQ, DH_B), F32)
    return pl.pallas_call(
        _finalize_kernel,
        grid=(n_even, BATCH),
        in_specs=[tok(0, 0), tok(0, NAT_QCOL + 2), st_in(0), tok(1, 0), tok(1, NAT_QCOL + 2), st_in(1)],
        out_specs=[cache_out, cache_out,
                   pl.BlockSpec((1, 1, 2, H_A, DK_A, DV_A), lambda l, b: (b, l, 0, 0, 0, 0))],
        out_shape=[cache_shape, cache_shape,
                   jax.ShapeDtypeStruct((BATCH, n_even, 2, H_A, DK_A, DV_A), F32)],
        compiler_params=_cparams(("arbitrary", "arbitrary")),
        name="finalize_caches",
    )(kns[0], pms[0], states[0], kns[1], pms[1], states[1])


def _even_mixer_layer(x, h, mod, l, i, w):
    p_main = _mm([h], [w["even_w_in"]], tm=TM_PROJ, tn=TN_EVEN_IN, wi=i, n_out=A_QKV + A_GATE + B_QKV,
                 shifted=True)
    p_small = _mm([h], [w["even_w_in"]], tm=TM_PROJ, tn=LANES, wi=i, n_out=LANES, col0=(A_QKV + A_GATE) // LANES)

    conv_w, a_log, dt_bias = w["gdn_conv_w"][i], w["gdn_a_log"][i], w["gdn_dt_bias"][i]
    ctx_tiles = T_CTX // ROW_TILE
    qkv_c, bg_c = _gdn_prep(p_main, p_small, conv_w, a_log, dt_bias, 0, T_CTX, SEQ)
    qkv_l, bg_l = _gdn_prep(p_main, p_small, conv_w, a_log, dt_bias, ctx_tiles, T_LAT, DEC_SEQ)
    of_c, ob_c, s_ctx = _gdn_scanb(qkv_c, bg_c, None, i, BATCH, SEQ, GDN_NS_CTX)
    of_l, ob_l, _ = _gdn_scanb(qkv_l, bg_l, w["state_gdn"], i, DEC_BATCH, DEC_SEQ, GDN_NS_LAT)
    oa_c = _gdn_post(of_c, ob_c, p_main, w["gdn_norm_g"][i], 0, T_CTX)
    oa_l = _gdn_post(of_l, ob_l, p_main, w["gdn_norm_g"][i], ctx_tiles, T_LAT)

    qn, kn = _nat_prep(p_main, w["nat_q_norm_g"][i], w["nat_k_norm_g"][i])
    ob_c = _nat_ctx(qn, kn, p_main)
    bias = _nat_bias(w["nat_rpb"][i])
    kc = _heads_to_lanes(w["cache_nat_k"][:, i])
    vc = _heads_to_lanes(w["cache_nat_v"][:, i])
    ob_l = _nat_lat(qn, kn, p_main, kc, vc, bias)

    x, h2 = _mm([(oa_c, oa_l), (ob_c, ob_l)], [w["even_w_out"]], tm=TM_EPI, tn=D, epilogue="residual_mod", wi=i,
                resid=x, mod=mod, layer=l, k_gate=2, next_norm=(w["norm_ffn_g"][l], l, 3, BF16))
    act = _mm([h2], [w["ffn_w_gate"], w["ffn_w_up"]], tm=TM_EPI, tn=TN_FFN_UP, out_dtype=BF16,
              epilogue="swiglu", wi=i)
    if l + 1 < DEPTH:
        x, h_next = _mm([act], [w["ffn_w_down"]], tm=TM_EPI, tn=D, epilogue="residual_mod", wi=i,
                        resid=x, mod=mod, layer=l, k_gate=5, next_norm=(w["norm_mix_g"][l + 1], l + 1, 0, BF16))
    else:
        x = _mm([act], [w["ffn_w_down"]], tm=TM_EPI, tn=D // 2, epilogue="residual", wi=i,
                resid=x, mod=mod, layer=l, k_gate=5)
        h_next = None
    return x, h_next, kn, p_main, s_ctx


def _dft_consts(seq):
    n = 2 * seq
    k = np.arange(seq)[:, None]
    s = np.arange(seq)[None, :]
    ang = 2.0 * np.pi * ((k * s) % n) / n
    fr = np.cos(ang)
    fi = -np.sin(ang)
    fi[0, :] = np.cos(np.pi * (np.arange(seq) % 2))
    fm = np.concatenate([fr, fi], axis=0)

    def to_bf16(a):
        return jnp.asarray(a.astype(np.float32)).astype(BF16)

    cw = np.full((n, 1), 2.0 / n)
    cw[0, 0] = cw[seq, 0] = 1.0 / n
    sg = np.ones((n, 1))
    sg[seq + 1:, 0] = -1.0
    cs = np.zeros((n, 128), np.float32)
    cs[:, 0:1] = cw
    cs[:, 1:2] = cw * sg
    return to_bf16(fm), to_bf16(fm.T.copy()), jnp.asarray(cs)


def _dft_apply(m, x):
    return jnp.dot(m, x.astype(BF16), preferred_element_type=F32)


def _hy_filter_kernel(z_ref, w1_ref, b1_ref, w2_ref, b2_ref, fq_ref, w3_ref, t_ref, dl_ref,
                      fm_ref, cs_ref, o_ref, hh_scr):
    c, d = pl.program_id(0), pl.program_id(1)

    @pl.when((c == 0) & (d == 0))
    def _():
        fq = fq_ref[...]
        hh = jnp.sin(fq * (_dot(z_ref[...], w1_ref[...], prec="bf16x3") + b1_ref[...]))
        hh_scr[...] = jnp.sin(fq * (_dot(hh, w2_ref[...], prec="bf16x3") + b2_ref[...]))

    filt = _dot(hh_scr[...], w3_ref[...], prec="bf16x3") * jnp.exp(-t_ref[...] * dl_ref[...])
    rows = lax.broadcasted_iota(jnp.int32, filt.shape, 0)
    filt = jnp.where((d == 1) & (rows == 0), 0.0, filt)
    spec = _dft_apply(fm_ref[...], filt)

    @pl.when(d == 0)
    def _():
        o_ref[...] = spec * cs_ref[:, 0:1]

    @pl.when(d == 1)
    def _():
        o_ref[...] = o_ref[...] + spec * cs_ref[:, 1:2]


def _hy_filter_spectrum(seq, consts, w1, b1, w2, b2, w3, freq):
    fm, _, cs = consts
    bands = (HY_EMB - 1) // 2
    t = np.linspace(0.0, 1.0, seq, dtype=np.float32)[:, None]
    wv = (np.float32(2.0 * math.pi / seq) * np.arange(seq, dtype=np.float32))[:, None]
    f = np.linspace(1e-4, bands - 1, bands, dtype=np.float32)[None, :]
    z = np.zeros((seq, 128), np.float32)
    z[:, 0:1] = t
    z[:, 1:1 + bands] = np.cos(f * wv)
    z[:, 1 + bands:HY_EMB] = -np.sin(f * wv)
    deltas = np.abs(np.linspace(math.log(HY_TARGET) / HY_FAST, math.log(HY_TARGET) / HY_SLOW, D,
                                dtype=np.float32))[None, :]
    w1p = jnp.zeros((128, HY_HID), F32).at[:HY_EMB].set(w1)
    tc = 256
    n = 2 * seq
    full = lambda shape: pl.BlockSpec(shape, lambda c, d: tuple(0 for _ in shape))
    return pl.pallas_call(
        _hy_filter_kernel,
        grid=(D // tc, 2),
        in_specs=[full((seq, 128)), full((128, HY_HID)), full((1, HY_HID)), full((HY_HID, HY_HID)),
                  full((1, HY_HID)), full((1, HY_HID)),
                  pl.BlockSpec((HY_HID, tc), lambda c, d: (0, d * (D // tc) + c)),
                  full((seq, 1)), pl.BlockSpec((1, tc), lambda c, d: (0, c)),
                  full((n, seq)), full((n, 128))],
        out_specs=pl.BlockSpec((n, tc), lambda c, d: (0, c)),
        out_shape=jax.ShapeDtypeStruct((n, D), F32),
        scratch_shapes=[pltpu.VMEM((seq, HY_HID), F32)],
        compiler_params=_cparams(("arbitrary", "arbitrary"), VMEM_LIMIT),
        name="hy_filter",
    )(jnp.asarray(z), w1p, b1.reshape(1, -1), w2, b2.reshape(1, -1), freq.reshape(1, -1), w3,
      jnp.asarray(t), jnp.asarray(deltas), fm, cs)


def _hy_conv_kernel(p0_ref, p1_ref, pv_ref, cw0_ref, cw1_ref, cwv_ref, kf_ref, bias_ref, fm_ref, ft_ref,
                    o_ref, *, seq):
    rows = lax.broadcasted_iota(jnp.int32, p0_ref.shape, 0)

    def conv3(x_ref, w_ref):
        x, w = x_ref[...], w_ref[...]
        xm = jnp.where(rows == 0, 0.0, pltpu.roll(x, 1, 0))
        xp = jnp.where(rows == seq - 1, 0.0, pltpu.roll(x, seq - 1, 0))
        return xm * w[0:1, :] + x * w[1:2, :] + xp * w[2:3, :]

    x0 = conv3(p0_ref, cw0_ref)
    u = conv3(pv_ref, cwv_ref) * conv3(p1_ref, cw1_ref)
    xs = _dft_apply(fm_ref[...], u)
    kf = kf_ref[...]
    xr, xi = xs[:seq], xs[seq:]
    kr, ki = kf[:seq], kf[seq:]
    row0 = lax.broadcasted_iota(jnp.int32, xr.shape, 0) == 0
    xiki = xi * ki
    yr = xr * kr - jnp.where(row0, 0.0, xiki)
    yi = jnp.where(row0, xiki, xr * ki + xi * kr)
    y = _dft_apply(ft_ref[...], jnp.concatenate([yr, yi], axis=0))
    o_ref[...] = ((y + u * bias_ref[...]) * x0).astype(o_ref.dtype)


def _hy_conv(p, conv_w, kf, bias, consts, seq, n_seq, seq_blk0, tc):
    fm, ft, _ = consts
    n = 2 * seq
    nc = D // tc
    full = lambda shape: pl.BlockSpec(shape, lambda b, c: (0, 0))
    grp = lambda g: pl.BlockSpec((seq, tc), lambda b, c: (seq_blk0 + b, g * nc + c))
    cwg = lambda g: pl.BlockSpec((3, tc), lambda b, c: (0, g * nc + c))
    return pl.pallas_call(
        functools.partial(_hy_conv_kernel, seq=seq),
        grid=(n_seq, nc),
        in_specs=[grp(0), grp(1), grp(2), cwg(0), cwg(1), cwg(2),
                  pl.BlockSpec((n, tc), lambda b, c: (0, c)),
                  pl.BlockSpec((1, tc), lambda b, c: (0, c)),
                  full((n, seq)), full((seq, n))],
        out_specs=pl.BlockSpec((seq, tc), lambda b, c: (b, c)),
        out_shape=jax.ShapeDtypeStruct((n_seq * seq, D), BF16),
        compiler_params=_cparams(("arbitrary", "arbitrary"), VMEM_LIMIT),
        name="hy_conv",
    )(p, p, p, conv_w, conv_w, conv_w, kf, bias.reshape(1, D), fm, ft)


MOE_TM = 256
MOE_TILES = 2 * T_ALL // MOE_TM + N_EXP
MOE_ROWS = MOE_TILES * MOE_TM
ROUTE_TM = 512
DISPATCH_TM = 512
COMBINE_TM = 256


def _router_kernel(h_ref, rw_ref, rb_ref, tri_ref, ei_ref, gf_ref, cnt_ref, carry):
    @pl.when(pl.program_id(0) == 0)
    def _():
        carry[...] = jnp.zeros(carry.shape, F32)

    logits = _dot(h_ref[...], rw_ref[...], prec="bf16x3") + rb_ref[...]
    lane = lax.broadcasted_iota(jnp.int32, logits.shape, 1)
    logits = jnp.where(lane < N_EXP, logits, -jnp.inf)
    m1 = jnp.max(logits, axis=-1, keepdims=True)
    i1 = jnp.min(jnp.where(logits == m1, lane, 128), axis=-1, keepdims=True)
    rest = jnp.where(lane == i1, -jnp.inf, logits)
    m2 = jnp.max(rest, axis=-1, keepdims=True)
    i2 = jnp.min(jnp.where(rest == m2, lane, 128), axis=-1, keepdims=True)
    e = jnp.exp(m2 - m1)
    g1 = 1.0 / (1.0 + e)
    g2 = e * g1
    pick = jnp.where((lane == i1) | (lane == i2), 1.0, 0.0)
    before = carry[...] + jnp.dot(tri_ref[...], pick.astype(BF16), preferred_element_type=F32)
    r1 = jnp.sum(jnp.where(lane == i1, before, 0.0), axis=-1, keepdims=True)
    r2 = jnp.sum(jnp.where(lane == i2, before, 0.0), axis=-1, keepdims=True)
    carry[...] = carry[...] + jnp.sum(pick, axis=0, keepdims=True)
    ints = jnp.where(lane == 0, i1, jnp.where(lane == 1, i2, 0))
    ranks = jnp.where(lane == 2, r1, jnp.where(lane == 3, r2, 0.0))
    ei_ref[...] = ints + ranks.astype(jnp.int32)
    gf_ref[...] = jnp.where(lane == 0, g1, jnp.where(lane == 1, g2, 0.0))
    cnt_ref[...] = carry[...].astype(jnp.int32)


def _router(h, router_w, router_b):
    rw = jnp.zeros((D, 128), F32).at[:, :N_EXP].set(router_w)
    rb = jnp.zeros((1, 128), F32).at[0, :N_EXP].set(router_b)
    tri = jnp.asarray(np.tril(np.ones((ROUTE_TM, ROUTE_TM), np.float32), -1), BF16)
    row = pl.BlockSpec((ROUTE_TM, 128), lambda i: (i, 0))
    return pl.pallas_call(
        _router_kernel,
        grid=(T_ALL // ROUTE_TM,),
        in_specs=[pl.BlockSpec((ROUTE_TM, D), lambda i: (i, 0)),
                  pl.BlockSpec((D, 128), lambda i: (0, 0)),
                  pl.BlockSpec((1, 128), lambda i: (0, 0)),
                  pl.BlockSpec((ROUTE_TM, ROUTE_TM), lambda i: (0, 0))],
        out_specs=[row, row, pl.BlockSpec((1, 128), lambda i: (0, 0))],
        out_shape=[jax.ShapeDtypeStruct((T_ALL, 128), jnp.int32),
                   jax.ShapeDtypeStruct((T_ALL, 128), F32),
                   jax.ShapeDtypeStruct((1, 128), jnp.int32)],
        scratch_shapes=[pltpu.VMEM((1, 128), F32)],
        compiler_params=_cparams(("arbitrary",)),
        name="moe_router",
    )(h, rw, rb, tri)


def _moe_dispatch_kernel(pos_ref, h_ref, init_ref, out_ref, sem):
    del init_ref
    base = pl.program_id(0) * DISPATCH_TM

    def copy(r, p):
        return pltpu.make_async_copy(h_ref.at[pl.ds(r, 1)], out_ref.at[pl.ds(p, 1)], sem)

    def start(r, c):
        copy(r, pos_ref[base + r]).start(priority=0)
        copy(r, pos_ref[T_ALL + base + r]).start(priority=1)
        return c

    def wait(r, c):
        copy(0, 0).wait()
        copy(0, 0).wait()
        return c

    lax.fori_loop(0, DISPATCH_TM, start, 0, unroll=8)
    lax.fori_loop(0, DISPATCH_TM, wait, 0, unroll=8)


def _moe_dispatch(h, flat_pos, init):
    if init is None:
        init = jnp.zeros((MOE_ROWS, D), F32)
    any_spec = pl.BlockSpec(memory_space=pl.ANY)
    return pl.pallas_call(
        _moe_dispatch_kernel,
        grid_spec=pltpu.PrefetchScalarGridSpec(
            num_scalar_prefetch=1, grid=(T_ALL // DISPATCH_TM,),
            in_specs=[pl.BlockSpec((DISPATCH_TM, D), lambda i, pos: (i, 0)), any_spec],
            out_specs=any_spec,
            scratch_shapes=[pltpu.SemaphoreType.DMA(())]),
        out_shape=jax.ShapeDtypeStruct((MOE_ROWS, D), F32),
        input_output_aliases={2: 0},
        compiler_params=_cparams(("arbitrary",)),
        name="moe_dispatch",
    )(flat_pos, h, init)


def _moe_up_kernel(te_ref, nu_ref, g_ref, wg_ref, wu_ref, o_ref, wbf):
    i = pl.program_id(1)
    new_w = (i == 0) | (te_ref[i] != te_ref[jnp.maximum(i - 1, 0)])

    @pl.when(new_w)
    def _():
        wbf[0] = wg_ref[0, 0].astype(BF16)
        wbf[1] = wu_ref[0, 0].astype(BF16)

    @pl.when(i < nu_ref[0])
    def _():
        _swiglu_chunks(g_ref[...].astype(BF16), wbf, o_ref)

    @pl.when(i >= nu_ref[0])
    def _():
        o_ref[...] = jnp.zeros(o_ref.shape, o_ref.dtype)


def _moe_up(tile_expert, n_used, rows, wg, wu, li, tn):
    wspec = pl.BlockSpec((1, 1, D, tn), lambda j, i, te, nu: (li, te[i], 0, j))
    return pl.pallas_call(
        _moe_up_kernel,
        grid_spec=pltpu.PrefetchScalarGridSpec(
            num_scalar_prefetch=2, grid=(D_FF_E // tn, MOE_TILES),
            in_specs=[pl.BlockSpec((MOE_TM, D), lambda j, i, te, nu: (i, 0)), wspec, wspec],
            out_specs=pl.BlockSpec((MOE_TM, tn), lambda j, i, te, nu: (i, j)),
            scratch_shapes=[pltpu.VMEM((2, D, tn), BF16)]),
        out_shape=jax.ShapeDtypeStruct((MOE_ROWS, D_FF_E), BF16),
        compiler_params=_cparams(("arbitrary", "arbitrary"), VMEM_LIMIT),
        name="moe_up",
    )(tile_expert, n_used, rows, wg, wu)


def _moe_down_kernel(te_ref, nu_ref, a_ref, wd_ref, o_ref, wbf):
    i = pl.program_id(1)
    new_w = (i == 0) | (te_ref[i] != te_ref[jnp.maximum(i - 1, 0)])

    @pl.when(new_w)
    def _():
        wbf[...] = wd_ref[0, 0].astype(BF16)

    @pl.when(i < nu_ref[0])
    def _():
        o_ref[...] = jnp.dot(a_ref[...], wbf[...], preferred_element_type=F32)

    @pl.when(i >= nu_ref[0])
    def _():
        o_ref[...] = jnp.zeros(o_ref.shape, o_ref.dtype)


def _moe_down(tile_expert, n_used, act, wd, li, tn):
    return pl.pallas_call(
        _moe_down_kernel,
        grid_spec=pltpu.PrefetchScalarGridSpec(
            num_scalar_prefetch=2, grid=(D // tn, MOE_TILES),
            in_specs=[pl.BlockSpec((MOE_TM, D_FF_E), lambda j, i, te, nu: (i, 0)),
                      pl.BlockSpec((1, 1, D_FF_E, tn), lambda j, i, te, nu: (li, te[i], 0, j))],
            out_specs=pl.BlockSpec((MOE_TM, tn), lambda j, i, te, nu: (i, j)),
            scratch_shapes=[pltpu.VMEM((D_FF_E, tn), BF16)]),
        out_shape=jax.ShapeDtypeStruct((MOE_ROWS, D), F32),
        compiler_params=_cparams(("arbitrary", "arbitrary"), VMEM_LIMIT),
        name="moe_down",
    )(tile_expert, n_used, act, wd)


def _moe_combine_kernel(pos_ref, x_ref, y_ref, gf_ref, gate_ref, *rest, with_next):
    if with_next:
        g_ref, sc_ref, sh_ref, o_ref, h_ref, ybuf, sems = rest
    else:
        o_ref, ybuf, sems = rest
    i = pl.program_id(0)
    buf = i % 2

    def copy(b, which, r, p):
        return pltpu.make_async_copy(y_ref.at[pl.ds(p, 1)], ybuf.at[b, which, pl.ds(r, 1)], sems.at[b])

    def gather(tile, b):
        base = tile * COMBINE_TM

        def start(r, c):
            copy(b, 0, r, pos_ref[base + r]).start(priority=0)
            copy(b, 1, r, pos_ref[T_ALL + base + r]).start(priority=1)
            return c
        lax.fori_loop(0, COMBINE_TM, start, 0, unroll=8)

    @pl.when(i == 0)
    def _():
        gather(0, 0)

    @pl.when(i + 1 < pl.num_programs(0))
    def _():
        gather(i + 1, 1 - buf)

    def wait(r, c):
        copy(buf, 0, 0, 0).wait()
        copy(buf, 1, 0, 0).wait()
        return c

    lax.fori_loop(0, COMBINE_TM, wait, 0, unroll=8)
    gf = gf_ref[...]
    f = gf[:, 0:1] * ybuf[buf, 0] + gf[:, 1:2] * ybuf[buf, 1]
    x_new = x_ref[...] + gate_ref[0] * f
    o_ref[...] = x_new
    if with_next:
        h_ref[...] = _modulate_math(x_new, g_ref[...], sc_ref[0], sh_ref[0]).astype(h_ref.dtype)


def _moe_combine(x, y, flat_pos, gf, mod, layer, next_norm):
    tm = COMBINE_TM
    base = (layer * 6 + 5) * N_SEG
    row = pl.BlockSpec((tm, D), lambda i, pos: (i, 0))
    in_specs = [row, pl.BlockSpec(memory_space=pl.ANY),
                pl.BlockSpec((tm, 128), lambda i, pos: (i, 0)),
                pl.BlockSpec((1, 1, D), lambda i, pos: (base + _seg_of_row(i * tm), 0, 0))]
    args = [flat_pos, x, y, gf, mod]
    out_specs, out_shape = row, jax.ShapeDtypeStruct((T_ALL, D), F32)
    if next_norm is not None:
        g, n_layer, k_shift = next_norm
        in_specs += [pl.BlockSpec((1, D), lambda i, pos: (0, 0)),
                     _mod_spec(n_layer, k_shift + 1, tm, lambda i, pos: i),
                     _mod_spec(n_layer, k_shift, tm, lambda i, pos: i)]
        args += [g.reshape(1, D), mod, mod]
        out_specs = [row, row]
        out_shape = [out_shape, jax.ShapeDtypeStruct((T_ALL, D), BF16)]
    return pl.pallas_call(
        functools.partial(_moe_combine_kernel, with_next=next_norm is not None),
        grid_spec=pltpu.PrefetchScalarGridSpec(
            num_scalar_prefetch=1, grid=(T_ALL // tm,), in_specs=in_specs, out_specs=out_specs,
            scratch_shapes=[pltpu.VMEM((2, 2, tm, D), F32), pltpu.SemaphoreType.DMA((2,))]),
        out_shape=out_shape,
        compiler_params=_cparams(("arbitrary",)),
        name="moe_combine",
    )(*args)


def _moe(x, h, mod, layer, li, w, next_norm, sorted_init):
    ei, gf, cnt = _router(h, w["moe_router_w"][li], w["moe_router_b"][li])
    experts, ranks, counts = ei[:, 0:2], ei[:, 2:4], cnt[0, :N_EXP]
    tiles = (counts + MOE_TM - 1) // MOE_TM
    tile_end = jnp.cumsum(tiles)
    row0 = (tile_end - tiles) * MOE_TM
    pos = (row0[experts] + ranks).astype(jnp.int32)
    n_used = tile_end[-1:].astype(jnp.int32)
    t_idx = jnp.minimum(jnp.arange(MOE_TILES, dtype=jnp.int32), n_used[0] - 1)
    tile_expert = jnp.sum(t_idx[:, None] >= tile_end[None, :], axis=1).astype(jnp.int32)
    flat_pos = pos.T.reshape(-1)
    sorted_rows = _moe_dispatch(h, flat_pos, sorted_init)
    act = _moe_up(tile_expert, n_used, sorted_rows, w["moe_w_gate"], w["moe_w_up"], li, MOE_UP_TN)
    y = _moe_down(tile_expert, n_used, act, w["moe_w_down"], li, D)
    return _moe_combine(x, y, flat_pos, gf, mod, layer, next_norm), sorted_rows


def _odd_mixer_layer(x, h, mod, l, i, w, spectra, consts, sorted_init):
    p = _mm([h], [w["odd_w_in"]], tm=TM_PROJ, tn=TN_ODD_IN, wi=i)
    cw = w["hy_conv_w"][i]
    z_c = _hy_conv(p, cw, spectra[SEQ][i], w["hy_bias"][i], consts[SEQ], SEQ, BATCH, 0, D)
    z_l = _hy_conv(p, cw, spectra[DEC_SEQ][i], w["hy_bias"][i], consts[DEC_SEQ], DEC_SEQ, DEC_BATCH,
                   T_CTX // DEC_SEQ, HY_TC_LAT)
    x, h2 = _mm([(z_c, z_l)], [w["odd_w_out"]], tm=TM_EPI, tn=D, epilogue="residual_mod", wi=i,
                resid=x, mod=mod, layer=l, k_gate=2, next_norm=(w["norm_ffn_g"][l], l, 3, F32))
    next_norm = (w["norm_mix_g"][l + 1], l + 1, 0) if l + 1 < DEPTH else None
    out, sorted_rows = _moe(x, h2, mod, l, i, w, next_norm, sorted_init)
    x, h_next = out if next_norm is not None else (out, None)
    return x, h_next, sorted_rows


def kernel(x_prompt, x_sample, state_gdn, cache_nat_k, cache_nat_v, c, c_ctx, ada_w, ada_b, norm_mix_g, norm_ffn_g, even_w_in, gdn_conv_w, gdn_a_log, gdn_dt_bias, gdn_norm_g, nat_q_norm_g, nat_k_norm_g, nat_rpb, even_w_out, ffn_w_gate, ffn_w_up, ffn_w_down, odd_w_in, hy_conv_w, hy_w1, hy_b1, hy_w2, hy_b2, hy_w3, hy_freq, hy_bias, odd_w_out, moe_router_w, moe_router_b, moe_w_gate, moe_w_up, moe_w_down):
    w = dict(state_gdn=state_gdn, cache_nat_k=cache_nat_k, cache_nat_v=cache_nat_v,
             norm_mix_g=norm_mix_g, norm_ffn_g=norm_ffn_g, even_w_in=even_w_in, gdn_conv_w=gdn_conv_w,
             gdn_a_log=gdn_a_log, gdn_dt_bias=gdn_dt_bias, gdn_norm_g=gdn_norm_g,
             nat_q_norm_g=nat_q_norm_g, nat_k_norm_g=nat_k_norm_g, nat_rpb=nat_rpb, even_w_out=even_w_out,
             ffn_w_gate=ffn_w_gate, ffn_w_up=ffn_w_up, ffn_w_down=ffn_w_down, odd_w_in=odd_w_in,
             hy_conv_w=hy_conv_w, hy_bias=hy_bias, odd_w_out=odd_w_out, moe_router_w=moe_router_w,
             moe_router_b=moe_router_b, moe_w_gate=moe_w_gate, moe_w_up=moe_w_up, moe_w_down=moe_w_down)
    assert SEQ == ROW_TILE and DEC_SEQ % ROW_TILE == 0
    mod = _ada_all(c, c_ctx, ada_w, ada_b)
    consts = {s: _dft_consts(s) for s in (SEQ, DEC_SEQ)}
    n_odd = DEPTH // 2
    spectra = {s: [_hy_filter_spectrum(s, consts[s], hy_w1[i], hy_b1[i], hy_w2[i], hy_b2[i], hy_w3[i], hy_freq[i])
                   for i in range(n_odd)] for s in (SEQ, DEC_SEQ)}
    x, h = _modulate_first(x_prompt.reshape(T_CTX, D), x_sample.reshape(T_LAT, D), norm_mix_g[0], mod, BF16)
    states, kns, pms = [], [], []
    sorted_rows = None
    for l in range(DEPTH):
        i = l // 2
        if l % 2 == 0:
            x, h, kn, p_main, s_ctx = _even_mixer_layer(x, h, mod, l, i, w)
            states.append(s_ctx)
            kns.append(kn)
            pms.append(p_main)
        else:
            x, h, sorted_rows = _odd_mixer_layer(x, h, mod, l, i, w, spectra, consts, sorted_rows)
    k_cache, v_cache, state_out = _finalize_caches(kns, pms, states)
    y_prompt = x[:T_CTX].reshape(BATCH, SEQ, D)
    y_sample = x[T_CTX:].reshape(DEC_BATCH, DEC_SEQ, D)
    return (y_prompt, y_sample, state_out, k_cache, v_cache)
```

```python
import functools
import math

import jax
import jax.numpy as jnp
import numpy as np
from jax import lax
from jax.experimental import pallas as pl
from jax.experimental.pallas import tpu as pltpu

F32 = jnp.float32
BF16 = jnp.bfloat16

D = 1024
BATCH = 16
SEQ = 256
DEPTH = 4
DEC_BATCH = 2
DEC_SEQ = 1024
PAST_LEN = 512
GRID_W = 64
EPS = 1e-6
H_A = 4
DK_A = 128
DV_A = 128
CHUNK = 64
H_B = 8
DH_B = 64
WIN_R = 8
WIN_C = 16
A_QKV = 2 * H_A * DK_A + H_A * DV_A
A_GATE = H_A * DV_A
B_QKV = 3 * H_B * DH_B
HY_EMB = 33
HY_HID = 64
HY_FAST = 0.3
HY_SLOW = 1.5
HY_TARGET = 1e-2
D_FF = 2816
N_EXP = 8
D_FF_E = 3584

T_CTX = BATCH * SEQ
T_LAT = DEC_BATCH * DEC_SEQ
T_ALL = T_CTX + T_LAT
N_SEG = 8
ROW_TILE = 256

VMEM_LIMIT = 56 * 1024 * 1024
LANES = 128

TM_PROJ = 1024
TN_EVEN_IN = 1792
TN_ODD_IN = 1536
TM_EPI = 512
TN_FFN_UP = D_FF // 2
MOE_UP_TN = D_FF_E // 2
HY_TC_LAT = 512
GDN_NS_CTX = 8
GDN_NS_LAT = 2


def _cparams(sem, vmem=None):
    return pltpu.CompilerParams(dimension_semantics=sem, vmem_limit_bytes=vmem)


def _sigmoid(x):
    return 1.0 / (1.0 + jnp.exp(-x))


def _seg_of_row(row):
    return jnp.where(row < T_CTX, 0, 1 + (row - T_CTX) // DEC_SEQ)


def _split_bf16(x):
    hi = x.astype(BF16)
    lo = (x - hi.astype(F32)).astype(BF16)
    return hi, lo


def _dot(a, b, dims=(((1,), (0,)), ((), ())), prec="bf16"):
    if prec == "bf16":
        return lax.dot_general(a.astype(BF16), b.astype(BF16), dims, preferred_element_type=F32)
    ah, al = _split_bf16(a.astype(F32))
    bh, bl = _split_bf16(b.astype(F32))
    r = lax.dot_general(ah, bh, dims, preferred_element_type=F32)
    r = r + lax.dot_general(ah, bl, dims, preferred_element_type=F32)
    r = r + lax.dot_general(al, bh, dims, preferred_element_type=F32)
    return r


NT_DIMS = (((1,), (1,)), ((), ()))


def _ada_kernel(cv_ref, w_ref, b_ref, o_ref):
    cv = cv_ref[...]
    s = cv * _sigmoid(cv)
    o_ref[0] = _dot(s, w_ref[0]) + b_ref[0]


def _ada_all(c, c_ctx, ada_w, ada_b):
    cv = jnp.zeros((N_SEG, D), F32).at[0].set(c_ctx).at[1:1 + DEC_BATCH].set(c)
    tn = 1536
    out = pl.pallas_call(
        _ada_kernel,
        grid=(DEPTH, 6 * D // tn),
        in_specs=[
            pl.BlockSpec((N_SEG, D), lambda l, j: (0, 0)),
            pl.BlockSpec((1, D, tn), lambda l, j: (l, 0, j)),
            pl.BlockSpec((1, 1, tn), lambda l, j: (l, 0, j)),
        ],
        out_specs=pl.BlockSpec((1, N_SEG, tn), lambda l, j: (l, 0, j)),
        out_shape=jax.ShapeDtypeStruct((DEPTH, N_SEG, 6 * D), F32),
        compiler_params=_cparams(("arbitrary", "arbitrary"), VMEM_LIMIT),
        name="ada",
    )(cv, ada_w, ada_b.reshape(DEPTH, 1, 6 * D))
    return out.reshape(DEPTH, N_SEG, 6, D).transpose(0, 2, 1, 3).reshape(DEPTH * 6 * N_SEG, 1, D)


def _mod_spec(layer, k, tm, row_of_step):
    base = (layer * 6 + k) * N_SEG

    def imap(*ids):
        return (base + _seg_of_row(row_of_step(*ids) * tm), 0, 0)

    return pl.BlockSpec((1, 1, D), imap)


def _modulate_math(x, g, scale, shift):
    ms = jnp.mean(x * x, axis=-1, keepdims=True)
    y = x * lax.rsqrt(ms + EPS) * g
    return y * (1.0 + scale) + shift


def _modulate_kernel(xc_ref, xl_ref, g_ref, sc_ref, sh_ref, x_ref, o_ref, *, n_ctx_tiles):
    x = jnp.where(pl.program_id(0) < n_ctx_tiles, xc_ref[...], xl_ref[...])
    x_ref[...] = x
    o_ref[...] = _modulate_math(x, g_ref[...], sc_ref[0], sh_ref[0]).astype(o_ref.dtype)


def _modulate_first(x_ctx, x_lat, g, mod, out_dtype):
    tm = TM_EPI
    n_ctx_tiles = T_CTX // tm
    row = pl.BlockSpec((tm, D), lambda i: (i, 0))
    return pl.pallas_call(
        functools.partial(_modulate_kernel, n_ctx_tiles=n_ctx_tiles),
        grid=(T_ALL // tm,),
        in_specs=[
            pl.BlockSpec((tm, D), lambda i: (jnp.minimum(i, n_ctx_tiles - 1), 0)),
            pl.BlockSpec((tm, D), lambda i: (jnp.maximum(i - n_ctx_tiles, 0), 0)),
            pl.BlockSpec((1, D), lambda i: (0, 0)),
            _mod_spec(0, 1, tm, lambda i: i),
            _mod_spec(0, 0, tm, lambda i: i),
        ],
        out_specs=[row, row],
        out_shape=[jax.ShapeDtypeStruct((T_ALL, D), F32), jax.ShapeDtypeStruct((T_ALL, D), out_dtype)],
        compiler_params=_cparams(("arbitrary",)),
        name="modulate",
    )(x_ctx, x_lat, g.reshape(1, D), mod, mod)


SWIGLU_CHUNK = 256


def _swiglu_chunks(h, wbf, o_ref):
    n = o_ref.shape[1]
    for c0 in range(0, n, SWIGLU_CHUNK):
        c1 = min(c0 + SWIGLU_CHUNK, n)
        a = jnp.dot(h, wbf[0, :, c0:c1], preferred_element_type=F32)
        b = jnp.dot(h, wbf[1, :, c0:c1], preferred_element_type=F32)
        o_ref[:, c0:c1] = (a * _sigmoid(a) * b).astype(o_ref.dtype)


EVEN_SHIFT = 4 * H_A
EVEN_SPLIT = A_QKV + A_GATE


def _mm_kernel(*refs, chunks, n_w, epilogue, shifted, n_ctx_tiles):
    pos = 0
    lhs = []
    for kw, paired in chunks:
        cnt = 2 if paired else 1
        lhs.append(refs[pos:pos + cnt])
        pos += cnt
    n_wrefs = n_w + (1 if shifted else 0)
    w_refs = refs[pos:pos + n_wrefs]
    rest = refs[pos + n_wrefs:]
    j, i = pl.program_id(0), pl.program_id(1)
    if epilogue == "residual":
        x_ref, gate_ref, o_ref, wbf = rest
    elif epilogue == "residual_mod":
        x_ref, gate_ref, g_ref, sc_ref, sh_ref, o_ref, h_ref, wbf = rest
    else:
        o_ref, wbf = rest

    @pl.when(i == 0)
    def _():
        if shifted:
            tn = wbf.shape[2]
            for jj in range(shifted):
                split = min(max(EVEN_SPLIT - jj * tn, 0), tn)

                @pl.when(j == jj)
                def _(split=split):
                    wa = w_refs[0][0]
                    if split == tn:
                        wbf[0] = wa.astype(BF16)
                    else:
                        parts = [wa[:, :split]] if split else []
                        parts += [wa[:, split + EVEN_SHIFT:], w_refs[1][0][:, :EVEN_SHIFT]]
                        wbf[0] = jnp.concatenate(parts, axis=1).astype(BF16)
        else:
            for k in range(n_w):
                wbf[k] = w_refs[k][0].astype(BF16)

    def lhs_chunk(c):
        r = lhs[c]
        if len(r) == 2:
            return jnp.where(i < n_ctx_tiles, r[0][...], r[1][...]).astype(BF16)
        return r[0][...].astype(BF16)

    def matmul(k):
        acc, off = None, 0
        for c, (kw, _) in enumerate(chunks):
            part = jnp.dot(lhs_chunk(c), wbf[k, off:off + kw, :], preferred_element_type=F32)
            acc = part if acc is None else acc + part
            off += kw
        return acc

    if epilogue == "swiglu":
        _swiglu_chunks(lhs_chunk(0), wbf, o_ref)
        return
    a = matmul(0)
    if epilogue == "residual":
        o_ref[...] = x_ref[...] + gate_ref[0] * a
    elif epilogue == "residual_mod":
        x_new = x_ref[...] + gate_ref[0] * a
        o_ref[...] = x_new
        h_ref[...] = _modulate_math(x_new, g_ref[...], sc_ref[0], sh_ref[0]).astype(h_ref.dtype)
    else:
        o_ref[...] = a.astype(o_ref.dtype)


def _mm(lhs, ws, *, tm, tn, out_dtype=F32, epilogue="none", wi=0, n_out=None, col0=0, shifted=False,
        resid=None, mod=None, layer=None, k_gate=None, next_norm=None):
    chunks, args, in_specs = [], [], []
    n_ctx_tiles = T_CTX // tm
    for part in lhs:
        if isinstance(part, tuple):
            kw = part[0].shape[1]
            chunks.append((kw, True))
            args += [part[0], part[1]]
            in_specs += [pl.BlockSpec((tm, kw), lambda j, i: (jnp.minimum(i, n_ctx_tiles - 1), 0)),
                         pl.BlockSpec((tm, kw), lambda j, i: (jnp.maximum(i - n_ctx_tiles, 0), 0))]
        else:
            kw = part.shape[1]
            chunks.append((kw, False))
            args.append(part)
            in_specs.append(pl.BlockSpec((tm, kw), lambda j, i: (i, 0)))
    kdim = sum(kw for kw, _ in chunks)
    n = n_out if n_out is not None else ws[0].shape[2]
    n_w = len(ws)
    if shifted:
        last = pl.cdiv(ws[0].shape[2], tn) - 1
        in_specs += [pl.BlockSpec((1, kdim, tn), lambda j, i: (wi, 0, j)),
                     pl.BlockSpec((1, kdim, tn), lambda j, i: (wi, 0, jnp.minimum(j + 1, last)))]
        args += [ws[0], ws[0]]
    else:
        in_specs += [pl.BlockSpec((1, kdim, tn), lambda j, i: (wi, 0, col0 + j)) for _ in ws]
        args += list(ws)
    out_specs = pl.BlockSpec((tm, tn), lambda j, i: (i, j))
    out_shape = jax.ShapeDtypeStruct((T_ALL, n), out_dtype)
    if epilogue in ("residual", "residual_mod"):
        base = (layer * 6 + k_gate) * N_SEG
        in_specs += [pl.BlockSpec((tm, tn), lambda j, i: (i, j)),
                     pl.BlockSpec((1, 1, tn), lambda j, i: (base + _seg_of_row(i * tm), 0, j))]
        args += [resid, mod]
    if epilogue == "residual_mod":
        g, n_layer, k_shift, h_dtype = next_norm
        assert tn == n == D
        in_specs += [pl.BlockSpec((1, D), lambda j, i: (0, 0)),
                     _mod_spec(n_layer, k_shift + 1, tm, lambda j, i: i),
                     _mod_spec(n_layer, k_shift, tm, lambda j, i: i)]
        args += [g.reshape(1, D), mod, mod]
        out_specs = [out_specs, pl.BlockSpec((tm, tn), lambda j, i: (i, j))]
        out_shape = [out_shape, jax.ShapeDtypeStruct((T_ALL, n), h_dtype)]
    return pl.pallas_call(
        functools.partial(_mm_kernel, chunks=tuple(chunks), n_w=n_w, epilogue=epilogue,
                          shifted=(n // tn if shifted else 0), n_ctx_tiles=n_ctx_tiles),
        grid=(n // tn, T_ALL // tm),
        in_specs=in_specs,
        out_specs=out_specs,
        out_shape=out_shape,
        scratch_shapes=[pltpu.VMEM((n_w, kdim, tn), BF16)],
        compiler_params=_cparams(("arbitrary", "arbitrary"), VMEM_LIMIT),
        name="mm_" + epilogue,
    )(*args)


def _conv3(x, prev_ref, next_ref, w, seq_len):
    i = pl.program_id(0)
    rows_n = x.shape[0]
    if seq_len is None:
        j = jnp.maximum(i - T_CTX // rows_n, 0)
        per = DEC_SEQ // rows_n
        is_ctx = i < T_CTX // rows_n
        first = is_ctx | (j % per == 0)
        last = is_ctx | (j % per == per - 1)
    else:
        first = (i * rows_n) % seq_len == 0
        last = ((i + 1) * rows_n) % seq_len == 0
    prev_row = jnp.where(first, 0.0, prev_ref[7:8, :])
    next_row = jnp.where(last, 0.0, next_ref[0:1, :])
    rows = lax.broadcasted_iota(jnp.int32, x.shape, 0)
    xm = jnp.where(rows == 0, prev_row, pltpu.roll(x, 1, 0))
    xp = jnp.where(rows == rows_n - 1, next_row, pltpu.roll(x, rows_n - 1, 0))
    return xm * w[0:1, :] + x * w[1:2, :] + xp * w[2:3, :]


def _halo_specs(width, col_block, row_off_tiles):
    per = ROW_TILE // 8
    last_blk = T_ALL // 8 - 1
    main = pl.BlockSpec((ROW_TILE, width), lambda i: (i + row_off_tiles, col_block))
    prev = pl.BlockSpec((8, width), lambda i: (jnp.maximum((i + row_off_tiles) * per - 1, 0), col_block))
    nxt = pl.BlockSpec((8, width), lambda i: (jnp.minimum((i + row_off_tiles + 1) * per, last_blk), col_block))
    return main, prev, nxt


def _cumsum_rows(x, reverse):
    n = x.shape[0]
    rows = lax.broadcasted_iota(jnp.int32, x.shape, 0)
    k = 1
    while k < n:
        if reverse:
            x = x + jnp.where(rows < n - k, pltpu.roll(x, n - k, 0), 0.0)
        else:
            x = x + jnp.where(rows >= k, pltpu.roll(x, k, 0), 0.0)
        k *= 2
    return x


def _gdn_prep_kernel(pm_ref, prev_ref, next_ref, ps_ref, cw_ref, par_ref, qkv_ref, bg_ref, *, seq_len):
    x = _conv3(pm_ref[...], prev_ref, next_ref, cw_ref[...], seq_len)
    x = x * _sigmoid(x)
    for h in range(H_A):
        sl = slice(h * DK_A, (h + 1) * DK_A)
        qh = x[:, sl]
        qkv_ref[:, sl] = qh * lax.rsqrt(jnp.sum(qh * qh, axis=-1, keepdims=True) + EPS) * (DK_A ** -0.5)
        sl = slice(H_A * DK_A + h * DK_A, H_A * DK_A + (h + 1) * DK_A)
        kh = x[:, sl]
        qkv_ref[:, sl] = kh * lax.rsqrt(jnp.sum(kh * kh, axis=-1, keepdims=True) + EPS)
    qkv_ref[:, 2 * H_A * DK_A:] = x[:, 2 * H_A * DK_A:]
    raw = ps_ref[...]
    lane = lax.broadcasted_iota(jnp.int32, raw.shape, 1)
    beta = _sigmoid(raw)
    z = raw + par_ref[1:2, :]
    softplus = jnp.maximum(z, 0.0) + jnp.log(1.0 + jnp.exp(-jnp.abs(z)))
    g = -jnp.exp(par_ref[0:1, :]) * softplus
    bg_ref[...] = jnp.where(lane < 2 * H_A, beta, jnp.where(lane < 4 * H_A, g, 0.0))


def _gdn_prep(p_main, p_small, conv_w, a_log, dt_bias, row_off_tiles, n_rows, seq_len):
    par = jnp.zeros((2, 128), F32)
    par = par.at[0, 2 * H_A:4 * H_A].set(a_log.reshape(-1)).at[1, 2 * H_A:4 * H_A].set(dt_bias.reshape(-1))
    main, prev, nxt = _halo_specs(A_QKV, 0, row_off_tiles)
    return pl.pallas_call(
        functools.partial(_gdn_prep_kernel, seq_len=seq_len),
        grid=(n_rows // ROW_TILE,),
        in_specs=[main, prev, nxt,
                  pl.BlockSpec((ROW_TILE, 128), lambda i: (i + row_off_tiles, 0)),
                  pl.BlockSpec((3, A_QKV), lambda i: (0, 0)),
                  pl.BlockSpec((2, 128), lambda i: (0, 0))],
        out_specs=[pl.BlockSpec((ROW_TILE, A_QKV), lambda i: (i, 0)),
                   pl.BlockSpec((ROW_TILE, 128), lambda i: (i, 0))],
        out_shape=[jax.ShapeDtypeStruct((n_rows, A_QKV), F32),
                   jax.ShapeDtypeStruct((n_rows, 128), F32)],
        compiler_params=_cparams(("arbitrary",)),
        name="gdn_prep",
    )(p_main, p_main, p_main, p_small, conv_w, par)


B_NN = (((2,), (1,)), ((0,), (0,)))
B_NT = (((2,), (2,)), ((0,), (0,)))
B_TN = (((1,), (1,)), ((0,), (0,)))


def _bmm(a, b, dims=B_NN):
    return _dot(a, b, dims)


def _gdn_scanb_kernel(*refs, ns, has_s0):
    if has_s0:
        qf_ref, qb_ref, bf_ref, bb_ref, s0_ref, of_ref, ob_ref, sfin_ref, st = refs
    else:
        qf_ref, qb_ref, bf_ref, bb_ref, of_ref, ob_ref, sfin_ref, st = refs
    step = pl.program_id(1)
    c = CHUNK
    nb = 2 * ns * H_A

    @pl.when(step == 0)
    def _():
        for d in range(2):
            st[d] = s0_ref[:, 0, d] if has_s0 else jnp.zeros(st.shape[1:], F32)

    q_l, k_l, v_l, beta_l, gc_l = [], [], [], [], []
    for d in range(2):
        for s in range(ns):
            qkv = (qf_ref if d == 0 else qb_ref)[s, 0]
            bg = (bf_ref if d == 0 else bb_ref)[s, 0]
            gcum = _cumsum_rows(bg, reverse=(d == 1))
            for h in range(H_A):
                q_l.append(qkv[:, h * DK_A:(h + 1) * DK_A])
                k_l.append(qkv[:, (H_A + h) * DK_A:(H_A + h + 1) * DK_A])
                v_l.append(qkv[:, 2 * H_A * DK_A + h * DV_A:2 * H_A * DK_A + (h + 1) * DV_A])
                col = d * H_A + h
                beta_l.append(bg[:, col:col + 1])
                gc_l.append(gcum[:, 2 * H_A + col:2 * H_A + col + 1])
    q, k, v = jnp.stack(q_l), jnp.stack(k_l), jnp.stack(v_l)
    beta, gc = jnp.stack(beta_l), jnp.stack(gc_l)

    ri = lax.broadcasted_iota(jnp.int32, (1, c, c), 1)
    ci = lax.broadcasted_iota(jnp.int32, (1, c, c), 2)
    eye = ri == ci
    half = nb // 2
    gc_row = jnp.sum(jnp.where(eye, gc, 0.0), axis=1, keepdims=True)
    dgc = gc - gc_row

    def masked_exp(x, keep):
        return jnp.where(keep, jnp.exp(jnp.where(keep, x, 0.0)), 0.0)

    decay = jnp.concatenate([masked_exp(dgc[:half], ri >= ci), masked_exp(dgc[half:], ri <= ci)], axis=0)
    kb = k * beta
    a_mat = jnp.where(eye, 0.0, _bmm(kb, k, B_NT) * decay)
    blk = 8
    diag = (ri // blk) == (ci // blk)
    pw = jnp.where(diag, a_mat, 0.0)
    r_mat = -pw
    for _ in range(2):
        pw = _bmm(pw, pw)
        r_mat = r_mat + pw + _bmm(r_mat, pw)
    while blk < c:
        off = ((ri // (2 * blk)) == (ci // (2 * blk))) & ((ri // blk) != (ci // blk))
        e_mat = jnp.where(off, a_mat, 0.0)
        x_mat = e_mat + _bmm(r_mat, e_mat)
        r_mat = r_mat - (x_mat + _bmm(x_mat, r_mat))
        blk *= 2
    eg = jnp.exp(gc)
    rhs = jnp.concatenate([v * beta, kb * eg], axis=2)
    sol = rhs + _bmm(r_mat, rhs)
    ub, wm = sol[:, :, :DV_A], sol[:, :, DV_A:]
    attn = _bmm(q, k, B_NT) * decay
    fwd1 = lax.broadcasted_iota(jnp.int32, (nb, 1, 1), 0) < nb // 2
    g_end = jnp.where(fwd1, gc[:, c - 1:c, :], gc[:, 0:1, :])
    qd = q * eg
    kd = k * jnp.exp(g_end - gc)
    state = st[...].reshape(nb, DK_A, DV_A)
    u = ub - _bmm(wm, state)
    o = _bmm(qd, state) + _bmm(attn, u)
    state = state * jnp.exp(g_end) + _bmm(kd, u, B_TN)
    st[...] = state.reshape(st.shape)
    for d in range(2):
        o_ref = of_ref if d == 0 else ob_ref
        for s in range(ns):
            for h in range(H_A):
                o_ref[s, 0, :, h * DV_A:(h + 1) * DV_A] = o[(d * ns + s) * H_A + h]

    @pl.when(step == pl.num_programs(1) - 1)
    def _():
        for d in range(2):
            sfin_ref[:, d] = st[d]


def _gdn_scanb(qkv, bg, s0, layer_i, n_seq, seq_len, ns):
    n = seq_len // CHUNK
    qkv4 = qkv.reshape(n_seq, n, CHUNK, A_QKV)
    bg4 = bg.reshape(n_seq, n, CHUNK, 128)
    fwd = lambda g, c: (g, c, 0, 0)
    bwd = lambda g, c: (g, n - 1 - c, 0, 0)
    in_specs = [pl.BlockSpec((ns, 1, CHUNK, A_QKV), fwd), pl.BlockSpec((ns, 1, CHUNK, A_QKV), bwd),
                pl.BlockSpec((ns, 1, CHUNK, 128), fwd), pl.BlockSpec((ns, 1, CHUNK, 128), bwd)]
    args = [qkv4, qkv4, bg4, bg4]
    if s0 is not None:
        in_specs.append(pl.BlockSpec((ns, 1, 2, H_A, DK_A, DV_A), lambda g, c: (g, layer_i, 0, 0, 0, 0)))
        args.append(s0)
    o_shape = jax.ShapeDtypeStruct((n_seq, n, CHUNK, H_A * DV_A), F32)
    of, ob, sfin = pl.pallas_call(
        functools.partial(_gdn_scanb_kernel, ns=ns, has_s0=s0 is not None),
        grid=(n_seq // ns, n),
        in_specs=in_specs,
        out_specs=[pl.BlockSpec((ns, 1, CHUNK, H_A * DV_A), fwd),
                   pl.BlockSpec((ns, 1, CHUNK, H_A * DV_A), bwd),
                   pl.BlockSpec((ns, 2, H_A, DK_A, DV_A), lambda g, c: (g, 0, 0, 0, 0))],
        out_shape=[o_shape, o_shape, jax.ShapeDtypeStruct((n_seq, 2, H_A, DK_A, DV_A), F32)],
        scratch_shapes=[pltpu.VMEM((2, ns, H_A, DK_A, DV_A), F32)],
        compiler_params=_cparams(("arbitrary", "arbitrary"), VMEM_LIMIT),
        name="gdn_scan",
    )(*args)
    rows = n_seq * seq_len
    return of.reshape(rows, H_A * DV_A), ob.reshape(rows, H_A * DV_A), sfin


def _gdn_post_kernel(of_ref, ob_ref, gate_ref, g_ref, o_ref):
    o = of_ref[...] + ob_ref[...]
    gate = gate_ref[...]
    for h in range(H_A):
        sl = slice(h * DV_A, (h + 1) * DV_A)
        oh = o[:, sl]
        y = oh * lax.rsqrt(jnp.mean(oh * oh, axis=-1, keepdims=True) + EPS) * g_ref[...]
        gh = gate[:, sl]
        o_ref[:, sl] = (y * (gh * _sigmoid(gh))).astype(o_ref.dtype)


def _gdn_post(of, ob, p_main, gdn_g, row_off_tiles, n_rows):
    gate_blk = A_QKV // A_GATE
    return pl.pallas_call(
        _gdn_post_kernel,
        grid=(n_rows // ROW_TILE,),
        in_specs=[pl.BlockSpec((ROW_TILE, A_GATE), lambda i: (i, 0)),
                  pl.BlockSpec((ROW_TILE, A_GATE), lambda i: (i, 0)),
                  pl.BlockSpec((ROW_TILE, A_GATE), lambda i: (i + row_off_tiles, gate_blk)),
                  pl.BlockSpec((1, DV_A), lambda i: (0, 0))],
        out_specs=pl.BlockSpec((ROW_TILE, A_GATE), lambda i: (i, 0)),
        out_shape=jax.ShapeDtypeStruct((n_rows, A_GATE), BF16),
        compiler_params=_cparams(("arbitrary",)),
        name="gdn_post",
    )(of, ob, p_main, gdn_g.reshape(1, DV_A))


NAT_W = H_B * DH_B
NAT_QCOL = (A_QKV + A_GATE) // NAT_W
NAT_SCALE = DH_B ** -0.5


def _nat_prep_kernel(q_ref, k_ref, bd_ref, gq_ref, gk_ref, qn_ref, kn_ref):
    bd = bd_ref[...]
    for x_ref, g_ref, o_ref in ((q_ref, gq_ref, qn_ref), (k_ref, gk_ref, kn_ref)):
        x = x_ref[...]
        ms = jnp.dot((x * x).astype(BF16), bd, preferred_element_type=F32)
        o_ref[...] = x * lax.rsqrt(ms + EPS) * g_ref[...]


def _nat_prep(p_main, qn_g, kn_g):
    grp = np.arange(NAT_W) // DH_B
    bd = jnp.asarray((grp[:, None] == grp[None, :]).astype(np.float32) / DH_B, BF16)
    spec = lambda cb: pl.BlockSpec((ROW_TILE, NAT_W), lambda i: (i, cb))
    return pl.pallas_call(
        _nat_prep_kernel,
        grid=(T_ALL // ROW_TILE,),
        in_specs=[spec(NAT_QCOL), spec(NAT_QCOL + 1),
                  pl.BlockSpec((NAT_W, NAT_W), lambda i: (0, 0)),
                  pl.BlockSpec((1, NAT_W), lambda i: (0, 0)),
                  pl.BlockSpec((1, NAT_W), lambda i: (0, 0))],
        out_specs=[spec(0), spec(0)],
        out_shape=[jax.ShapeDtypeStruct((T_ALL, NAT_W), F32)] * 2,
        compiler_params=_cparams(("arbitrary",)),
        name="nat_prep",
    )(p_main, p_main, bd, jnp.tile(qn_g, H_B).reshape(1, NAT_W), jnp.tile(kn_g, H_B).reshape(1, NAT_W))


def _pair_masks():
    lane = lax.broadcasted_iota(jnp.int32, (1, 2 * DH_B), 1)
    return lane < DH_B


def _nat_ctx_kernel(q_ref, k_ref, v_ref, o_ref):
    lo = _pair_masks()
    for p in range(H_B // 2):
        sl = slice(p * 2 * DH_B, (p + 1) * 2 * DH_B)
        q2, k2, v2 = q_ref[:, sl], k_ref[:, sl], v_ref[:, sl]
        halves = []
        for half in range(2):
            qm = jnp.where(lo if half == 0 else jnp.logical_not(lo), q2, 0.0)
            s = _dot(qm, k2, NT_DIMS) * NAT_SCALE
            e = jnp.exp(s - jnp.max(s, axis=-1, keepdims=True))
            pr = e / jnp.sum(e, axis=-1, keepdims=True)
            halves.append(_dot(pr, v2))
        o_ref[:, sl] = jnp.where(lo, halves[0], halves[1]).astype(o_ref.dtype)


def _nat_ctx(qn, kn, p_main):
    spec = lambda cb: pl.BlockSpec((SEQ, NAT_W), lambda b: (b, cb))
    return pl.pallas_call(
        _nat_ctx_kernel,
        grid=(BATCH,),
        in_specs=[spec(0), spec(0), spec(NAT_QCOL + 2)],
        out_specs=spec(0),
        out_shape=jax.ShapeDtypeStruct((T_CTX, NAT_W), BF16),
        compiler_params=_cparams(("arbitrary",)),
        name="nat_ctx",
    )(qn, kn, p_main)


def _nat_bias_kernel(r_ref, e_ref, ok_ref, o_ref):
    r = r_ref[...]
    hi, lo = _split_bf16(r)
    lo2 = (r - hi.astype(F32) - lo.astype(F32)).astype(BF16)
    e = e_ref[...]
    t = (jnp.dot(hi, e, preferred_element_type=F32) + jnp.dot(lo, e, preferred_element_type=F32)
         + jnp.dot(lo2, e, preferred_element_type=F32))
    o_ref[...] = jnp.where(ok_ref[...] > 0.5, t, -jnp.inf)


def _nat_bias(rpb):
    n_dr, n_dc = 2 * WIN_R - 1, 2 * WIN_C - 1
    qc = np.arange(GRID_W)[:, None]
    kc = np.arange(GRID_W)[None, :]
    dc = (kc - qc + WIN_C - 1).reshape(-1)
    c0 = np.clip(qc - WIN_C // 2, 0, GRID_W - WIN_C)
    ok = ((kc >= c0) & (kc < c0 + WIN_C)).reshape(1, -1).astype(np.float32)
    onehot = (np.arange(128)[:, None] == dc[None, :]).astype(np.float32)
    rows = H_B * n_dr
    rp = jnp.zeros((rows, 128), F32).at[:, :n_dc].set(rpb.reshape(rows, n_dc))
    tab = pl.pallas_call(
        _nat_bias_kernel,
        out_shape=jax.ShapeDtypeStruct((rows, GRID_W * GRID_W), F32),
        name="nat_bias",
    )(rp, jnp.asarray(onehot, BF16), jnp.asarray(ok))
    tab = tab.reshape(H_B, n_dr, GRID_W, GRID_W)
    pad = jnp.full((H_B, 1, GRID_W, GRID_W), -jnp.inf, F32)
    ext = jnp.concatenate([pad, tab, pad], axis=1)
    return jnp.concatenate([ext[:, :n_dr + 1], ext[:, 1:]], axis=-1)


NAT_QROWS = 2
NAT_KROWS = 10


def _nat_lat_kernel(q_ref, k_ref, v_ref, kc_ref, vc_ref, bias_ref, o_ref):
    rows = DEC_SEQ // GRID_W
    row_a = pl.program_id(1) * NAT_QROWS
    r0s = [jnp.clip(row_a + qi - WIN_R // 2, 0, rows - WIN_R) for qi in range(NAT_QROWS)]
    ws = jnp.minimum(r0s[0], rows - NAT_KROWS)
    start = pl.multiple_of(ws * GRID_W, GRID_W)
    n_loc = NAT_KROWS * GRID_W
    lo = _pair_masks()
    lane = lax.broadcasted_iota(jnp.int32, (GRID_W, 2 * GRID_W), 1)

    def bias_for(h):
        blocks = []
        for qi in range(NAT_QROWS):
            pieces = []
            for jp in range(NAT_KROWS // 2):
                rk = ws + 2 * jp
                ok0 = ((rk >= r0s[qi]) & (rk < r0s[qi] + WIN_R)).astype(jnp.int32)
                ok1 = ((rk + 1 >= r0s[qi]) & (rk + 1 < r0s[qi] + WIN_R)).astype(jnp.int32)
                d = jnp.clip(rk - (row_a + qi) + WIN_R - 1, -1, 2 * WIN_R - 2) + 1
                piece = bias_ref[h, d]
                pieces.append(jnp.where(jnp.where(lane < GRID_W, ok0, ok1) > 0, piece, -jnp.inf))
            blocks.append(jnp.concatenate(pieces, axis=1))
        return jnp.concatenate(blocks, axis=0)

    for p in range(H_B // 2):
        sl = slice(p * 2 * DH_B, (p + 1) * 2 * DH_B)
        q2 = q_ref[:, sl]
        kw = k_ref[0, pl.ds(start, n_loc), sl]
        vw = v_ref[pl.ds(start, n_loc), sl]
        kc, vc = kc_ref[0, :, sl], vc_ref[0, :, sl]
        halves = []
        for half in range(2):
            qm = jnp.where(lo if half == 0 else jnp.logical_not(lo), q2, 0.0)
            s_loc = _dot(qm, kw, NT_DIMS) * NAT_SCALE + bias_for(2 * p + half)
            s_ctx = _dot(qm, kc, NT_DIMS) * NAT_SCALE
            m = jnp.maximum(jnp.max(s_loc, axis=-1, keepdims=True), jnp.max(s_ctx, axis=-1, keepdims=True))
            e_loc, e_ctx = jnp.exp(s_loc - m), jnp.exp(s_ctx - m)
            inv = 1.0 / (jnp.sum(e_loc, axis=-1, keepdims=True) + jnp.sum(e_ctx, axis=-1, keepdims=True))
            halves.append(_dot(e_loc * inv, vw) + _dot(e_ctx * inv, vc))
        o_ref[:, sl] = jnp.where(lo, halves[0], halves[1]).astype(o_ref.dtype)


def _nat_lat(qn, kn, p_main, kc, vc, bias):
    steps = DEC_SEQ // GRID_W // NAT_QROWS
    tq = NAT_QROWS * GRID_W
    lat_tile0 = T_CTX // tq
    lat_seq0 = T_CTX // DEC_SEQ
    return pl.pallas_call(
        _nat_lat_kernel,
        grid=(DEC_BATCH, steps),
        in_specs=[pl.BlockSpec((tq, NAT_W), lambda b, r: (lat_tile0 + b * steps + r, 0)),
                  pl.BlockSpec((1, DEC_SEQ, NAT_W), lambda b, r: (lat_seq0 + b, 0, 0)),
                  pl.BlockSpec((DEC_SEQ, NAT_W), lambda b, r: (lat_seq0 + b, NAT_QCOL + 2)),
                  pl.BlockSpec((1, PAST_LEN, NAT_W), lambda b, r: (b, 0, 0)),
                  pl.BlockSpec((1, PAST_LEN, NAT_W), lambda b, r: (b, 0, 0)),
                  pl.BlockSpec(bias.shape, lambda b, r: (0, 0, 0, 0))],
        out_specs=pl.BlockSpec((tq, NAT_W), lambda b, r: (b * steps + r, 0)),
        out_shape=jax.ShapeDtypeStruct((T_LAT, NAT_W), BF16),
        compiler_params=_cparams(("arbitrary", "arbitrary"), VMEM_LIMIT),
        name="nat_lat",
    )(qn, kn.reshape(T_ALL // DEC_SEQ, DEC_SEQ, NAT_W), p_main, kc, vc, bias)


def _heads_to_lanes(cache):
    b, h, l, dh = cache.shape
    return cache.transpose(0, 2, 1, 3).reshape(b, l, h * dh)


def _finalize_kernel(k0_ref, v0_ref, s0_ref, k1_ref, v1_ref, s1_ref, ko_ref, vo_ref, so_ref):
    layer = pl.program_id(0)
    for idx, (k_ref, v_ref, s_ref) in enumerate(((k0_ref, v0_ref, s0_ref), (k1_ref, v1_ref, s1_ref))):
        @pl.when(layer == idx)
        def _(k_ref=k_ref, v_ref=v_ref, s_ref=s_ref):
            for h in range(H_B):
                ko_ref[0, 0, h] = k_ref[:, h * DH_B:(h + 1) * DH_B]
                vo_ref[0, 0, h] = v_ref[:, h * DH_B:(h + 1) * DH_B]
            so_ref[0, 0] = s_ref[0]


def _finalize_caches(kns, pms, states):
    n_even = len(kns)
    assert n_even == 2
    pick = lambda idx, l, b: b * (l if idx else 1 - l)
    tok = lambda idx, cb: pl.BlockSpec((SEQ, NAT_W), lambda l, b: (pick(idx, l, b), cb))
    st_in = lambda idx: pl.BlockSpec((1, 2, H_A, DK_A, DV_A), lambda l, b: (pick(idx, l, b), 0, 0, 0, 0))
    cache_out = pl.BlockSpec((1, 1, H_B, SEQ, DH_B), lambda l, b: (b, l, 0, 0, 0))
    cache_shape = jax.ShapeDtypeStruct((BATCH, n_even, H_B, SEQ, DH_B), F32)
    return pl.pallas_call(
        _finalize_kernel,
        grid=(n_even, BATCH),
        in_specs=[tok(0, 0), tok(0, NAT_QCOL + 2), st_in(0), tok(1, 0), tok(1, NAT_QCOL + 2), st_in(1)],
        out_specs=[cache_out, cache_out,
                   pl.BlockSpec((1, 1, 2, H_A, DK_A, DV_A), lambda l, b: (b, l, 0, 0, 0, 0))],
        out_shape=[cache_shape, cache_shape,
                   jax.ShapeDtypeStruct((BATCH, n_even, 2, H_A, DK_A, DV_A), F32)],
        compiler_params=_cparams(("arbitrary", "arbitrary")),
        name="finalize_caches",
    )(kns[0], pms[0], states[0], kns[1], pms[1], states[1])


def _even_mixer_layer(x, h, mod, l, i, w):
    p_main = _mm([h], [w["even_w_in"]], tm=TM_PROJ, tn=TN_EVEN_IN, wi=i, n_out=A_QKV + A_GATE + B_QKV,
                 shifted=True)
    p_small = _mm([h], [w["even_w_in"]], tm=TM_PROJ, tn=LANES, wi=i, n_out=LANES, col0=(A_QKV + A_GATE) // LANES)

    conv_w, a_log, dt_bias = w["gdn_conv_w"][i], w["gdn_a_log"][i], w["gdn_dt_bias"][i]
    ctx_tiles = T_CTX // ROW_TILE
    qkv_c, bg_c = _gdn_prep(p_main, p_small, conv_w, a_log, dt_bias, 0, T_CTX, SEQ)
    qkv_l, bg_l = _gdn_prep(p_main, p_small, conv_w, a_log, dt_bias, ctx_tiles, T_LAT, DEC_SEQ)
    of_c, ob_c, s_ctx = _gdn_scanb(qkv_c, bg_c, None, i, BATCH, SEQ, GDN_NS_CTX)
    of_l, ob_l, _ = _gdn_scanb(qkv_l, bg_l, w["state_gdn"], i, DEC_BATCH, DEC_SEQ, GDN_NS_LAT)
    oa_c = _gdn_post(of_c, ob_c, p_main, w["gdn_norm_g"][i], 0, T_CTX)
    oa_l = _gdn_post(of_l, ob_l, p_main, w["gdn_norm_g"][i], ctx_tiles, T_LAT)

    qn, kn = _nat_prep(p_main, w["nat_q_norm_g"][i], w["nat_k_norm_g"][i])
    ob_c = _nat_ctx(qn, kn, p_main)
    bias = _nat_bias(w["nat_rpb"][i])
    kc = _heads_to_lanes(w["cache_nat_k"][:, i])
    vc = _heads_to_lanes(w["cache_nat_v"][:, i])
    ob_l = _nat_lat(qn, kn, p_main, kc, vc, bias)

    x, h2 = _mm([(oa_c, oa_l), (ob_c, ob_l)], [w["even_w_out"]], tm=TM_EPI, tn=D, epilogue="residual_mod", wi=i,
                resid=x, mod=mod, layer=l, k_gate=2, next_norm=(w["norm_ffn_g"][l], l, 3, BF16))
    act = _mm([h2], [w["ffn_w_gate"], w["ffn_w_up"]], tm=TM_EPI, tn=TN_FFN_UP, out_dtype=BF16,
              epilogue="swiglu", wi=i)
    if l + 1 < DEPTH:
        x, h_next = _mm([act], [w["ffn_w_down"]], tm=TM_EPI, tn=D, epilogue="residual_mod", wi=i,
                        resid=x, mod=mod, layer=l, k_gate=5, next_norm=(w["norm_mix_g"][l + 1], l + 1, 0, BF16))
    else:
        x = _mm([act], [w["ffn_w_down"]], tm=TM_EPI, tn=D // 2, epilogue="residual", wi=i,
                resid=x, mod=mod, layer=l, k_gate=5)
        h_next = None
    return x, h_next, kn, p_main, s_ctx


def _dft_consts(seq):
    n = 2 * seq
    k = np.arange(seq)[:, None]
    s = np.arange(seq)[None, :]
    ang = 2.0 * np.pi * ((k * s) % n) / n
    fr = np.cos(ang)
    fi = -np.sin(ang)
    fi[0, :] = np.cos(np.pi * (np.arange(seq) % 2))
    fm = np.concatenate([fr, fi], axis=0)

    def to_bf16(a):
        return jnp.asarray(a.astype(np.float32)).astype(BF16)

    cw = np.full((n, 1), 2.0 / n)
    cw[0, 0] = cw[seq, 0] = 1.0 / n
    sg = np.ones((n, 1))
    sg[seq + 1:, 0] = -1.0
    cs = np.zeros((n, 128), np.float32)
    cs[:, 0:1] = cw
    cs[:, 1:2] = cw * sg
    return to_bf16(fm), to_bf16(fm.T.copy()), jnp.asarray(cs)


def _dft_apply(m, x):
    return jnp.dot(m, x.astype(BF16), preferred_element_type=F32)


def _hy_filter_kernel(z_ref, w1_ref, b1_ref, w2_ref, b2_ref, fq_ref, w3_ref, t_ref, dl_ref,
                      fm_ref, cs_ref, o_ref, hh_scr):
    c, d = pl.program_id(0), pl.program_id(1)

    @pl.when((c == 0) & (d == 0))
    def _():
        fq = fq_ref[...]
        hh = jnp.sin(fq * (_dot(z_ref[...], w1_ref[...], prec="bf16x3") + b1_ref[...]))
        hh_scr[...] = jnp.sin(fq * (_dot(hh, w2_ref[...], prec="bf16x3") + b2_ref[...]))

    filt = _dot(hh_scr[...], w3_ref[...], prec="bf16x3") * jnp.exp(-t_ref[...] * dl_ref[...])
    rows = lax.broadcasted_iota(jnp.int32, filt.shape, 0)
    filt = jnp.where((d == 1) & (rows == 0), 0.0, filt)
    spec = _dft_apply(fm_ref[...], filt)

    @pl.when(d == 0)
    def _():
        o_ref[...] = spec * cs_ref[:, 0:1]

    @pl.when(d == 1)
    def _():
        o_ref[...] = o_ref[...] + spec * cs_ref[:, 1:2]


def _hy_filter_spectrum(seq, consts, w1, b1, w2, b2, w3, freq):
    fm, _, cs = consts
    bands = (HY_EMB - 1) // 2
    t = np.linspace(0.0, 1.0, seq, dtype=np.float32)[:, None]
    wv = (np.float32(2.0 * math.pi / seq) * np.arange(seq, dtype=np.float32))[:, None]
    f = np.linspace(1e-4, bands - 1, bands, dtype=np.float32)[None, :]
    z = np.zeros((seq, 128), np.float32)
    z[:, 0:1] = t
    z[:, 1:1 + bands] = np.cos(f * wv)
    z[:, 1 + bands:HY_EMB] = -np.sin(f * wv)
    deltas = np.abs(np.linspace(math.log(HY_TARGET) / HY_FAST, math.log(HY_TARGET) / HY_SLOW, D,
                                dtype=np.float32))[None, :]
    w1p = jnp.zeros((128, HY_HID), F32).at[:HY_EMB].set(w1)
    tc = 256
    n = 2 * seq
    full = lambda shape: pl.BlockSpec(shape, lambda c, d: tuple(0 for _ in shape))
    return pl.pallas_call(
        _hy_filter_kernel,
        grid=(D // tc, 2),
        in_specs=[full((seq, 128)), full((128, HY_HID)), full((1, HY_HID)), full((HY_HID, HY_HID)),
                  full((1, HY_HID)), full((1, HY_HID)),
                  pl.BlockSpec((HY_HID, tc), lambda c, d: (0, d * (D // tc) + c)),
                  full((seq, 1)), pl.BlockSpec((1, tc), lambda c, d: (0, c)),
                  full((n, seq)), full((n, 128))],
        out_specs=pl.BlockSpec((n, tc), lambda c, d: (0, c)),
        out_shape=jax.ShapeDtypeStruct((n, D), F32),
        scratch_shapes=[pltpu.VMEM((seq, HY_HID), F32)],
        compiler_params=_cparams(("arbitrary", "arbitrary"), VMEM_LIMIT),
        name="hy_filter",
    )(jnp.asarray(z), w1p, b1.reshape(1, -1), w2, b2.reshape(1, -1), freq.reshape(1, -1), w3,
      jnp.asarray(t), jnp.asarray(deltas), fm, cs)


def _hy_conv_kernel(p0_ref, p1_ref, pv_ref, cw0_ref, cw1_ref, cwv_ref, kf_ref, bias_ref, fm_ref, ft_ref,
                    o_ref, *, seq):
    rows = lax.broadcasted_iota(jnp.int32, p0_ref.shape, 0)

    def conv3(x_ref, w_ref):
        x, w = x_ref[...], w_ref[...]
        xm = jnp.where(rows == 0, 0.0, pltpu.roll(x, 1, 0))
        xp = jnp.where(rows == seq - 1, 0.0, pltpu.roll(x, seq - 1, 0))
        return xm * w[0:1, :] + x * w[1:2, :] + xp * w[2:3, :]

    x0 = conv3(p0_ref, cw0_ref)
    u = conv3(pv_ref, cwv_ref) * conv3(p1_ref, cw1_ref)
    xs = _dft_apply(fm_ref[...], u)
    kf = kf_ref[...]
    xr, xi = xs[:seq], xs[seq:]
    kr, ki = kf[:seq], kf[seq:]
    row0 = lax.broadcasted_iota(jnp.int32, xr.shape, 0) == 0
    xiki = xi * ki
    yr = xr * kr - jnp.where(row0, 0.0, xiki)
    yi = jnp.where(row0, xiki, xr * ki + xi * kr)
    y = _dft_apply(ft_ref[...], jnp.concatenate([yr, yi], axis=0))
    o_ref[...] = ((y + u * bias_ref[...]) * x0).astype(o_ref.dtype)


def _hy_conv(p, conv_w, kf, bias, consts, seq, n_seq, seq_blk0, tc):
    fm, ft, _ = consts
    n = 2 * seq
    nc = D // tc
    full = lambda shape: pl.BlockSpec(shape, lambda b, c: (0, 0))
    grp = lambda g: pl.BlockSpec((seq, tc), lambda b, c: (seq_blk0 + b, g * nc + c))
    cwg = lambda g: pl.BlockSpec((3, tc), lambda b, c: (0, g * nc + c))
    return pl.pallas_call(
        functools.partial(_hy_conv_kernel, seq=seq),
        grid=(n_seq, nc),
        in_specs=[grp(0), grp(1), grp(2), cwg(0), cwg(1), cwg(2),
                  pl.BlockSpec((n, tc), lambda b, c: (0, c)),
                  pl.BlockSpec((1, tc), lambda b, c: (0, c)),
                  full((n, seq)), full((seq, n))],
        out_specs=pl.BlockSpec((seq, tc), lambda b, c: (b, c)),
        out_shape=jax.ShapeDtypeStruct((n_seq * seq, D), BF16),
        compiler_params=_cparams(("arbitrary", "arbitrary"), VMEM_LIMIT),
        name="hy_conv",
    )(p, p, p, conv_w, conv_w, conv_w, kf, bias.reshape(1, D), fm, ft)


MOE_TM = 256
MOE_TILES = 2 * T_ALL // MOE_TM + N_EXP
MOE_ROWS = MOE_TILES * MOE_TM
ROUTE_TM = 512
DISPATCH_TM = 1024
COMBINE_TM = 512
DMA_UNROLL = 16


def _router_kernel(h_ref, rw_ref, rb_ref, tri_ref, ei_ref, gf_ref, cnt_ref, carry):
    @pl.when(pl.program_id(0) == 0)
    def _():
        carry[...] = jnp.zeros(carry.shape, F32)

    logits = _dot(h_ref[...], rw_ref[...], prec="bf16x3") + rb_ref[...]
    lane = lax.broadcasted_iota(jnp.int32, logits.shape, 1)
    logits = jnp.where(lane < N_EXP, logits, -jnp.inf)
    m1 = jnp.max(logits, axis=-1, keepdims=True)
    i1 = jnp.min(jnp.where(logits == m1, lane, 128), axis=-1, keepdims=True)
    rest = jnp.where(lane == i1, -jnp.inf, logits)
    m2 = jnp.max(rest, axis=-1, keepdims=True)
    i2 = jnp.min(jnp.where(rest == m2, lane, 128), axis=-1, keepdims=True)
    e = jnp.exp(m2 - m1)
    g1 = 1.0 / (1.0 + e)
    g2 = e * g1
    pick = jnp.where((lane == i1) | (lane == i2), 1.0, 0.0)
    before = carry[...] + jnp.dot(tri_ref[...], pick.astype(BF16), preferred_element_type=F32)
    r1 = jnp.sum(jnp.where(lane == i1, before, 0.0), axis=-1, keepdims=True)
    r2 = jnp.sum(jnp.where(lane == i2, before, 0.0), axis=-1, keepdims=True)
    carry[...] = carry[...] + jnp.sum(pick, axis=0, keepdims=True)
    ints = jnp.where(lane == 0, i1, jnp.where(lane == 1, i2, 0))
    ranks = jnp.where(lane == 2, r1, jnp.where(lane == 3, r2, 0.0))
    ei_ref[...] = ints + ranks.astype(jnp.int32)
    gf_ref[...] = jnp.where(lane == 0, g1, jnp.where(lane == 1, g2, 0.0))
    cnt_ref[...] = carry[...].astype(jnp.int32)


def _router(h, router_w, router_b):
    rw = jnp.zeros((D, 128), F32).at[:, :N_EXP].set(router_w)
    rb = jnp.zeros((1, 128), F32).at[0, :N_EXP].set(router_b)
    tri = jnp.asarray(np.tril(np.ones((ROUTE_TM, ROUTE_TM), np.float32), -1), BF16)
    row = pl.BlockSpec((ROUTE_TM, 128), lambda i: (i, 0))
    return pl.pallas_call(
        _router_kernel,
        grid=(T_ALL // ROUTE_TM,),
        in_specs=[pl.BlockSpec((ROUTE_TM, D), lambda i: (i, 0)),
                  pl.BlockSpec((D, 128), lambda i: (0, 0)),
                  pl.BlockSpec((1, 128), lambda i: (0, 0)),
                  pl.BlockSpec((ROUTE_TM, ROUTE_TM), lambda i: (0, 0))],
        out_specs=[row, row, pl.BlockSpec((1, 128), lambda i: (0, 0))],
        out_shape=[jax.ShapeDtypeStruct((T_ALL, 128), jnp.int32),
                   jax.ShapeDtypeStruct((T_ALL, 128), F32),
                   jax.ShapeDtypeStruct((1, 128), jnp.int32)],
        scratch_shapes=[pltpu.VMEM((1, 128), F32)],
        compiler_params=_cparams(("arbitrary",)),
        name="moe_router",
    )(h, rw, rb, tri)


def _moe_dispatch_kernel(pos_ref, h_ref, init_ref, out_ref, sem):
    del init_ref
    base = pl.program_id(0) * DISPATCH_TM

    def copy(r, p):
        return pltpu.make_async_copy(h_ref.at[pl.ds(r, 1)], out_ref.at[pl.ds(p, 1)], sem)

    def start(r, c):
        copy(r, pos_ref[base + r]).start(priority=0)
        copy(r, pos_ref[T_ALL + base + r]).start(priority=1)
        return c

    def wait(r, c):
        copy(0, 0).wait()
        copy(0, 0).wait()
        return c

    lax.fori_loop(0, DISPATCH_TM, start, 0, unroll=DMA_UNROLL)
    lax.fori_loop(0, DISPATCH_TM, wait, 0, unroll=DMA_UNROLL)


def _moe_dispatch(h, flat_pos, init):
    if init is None:
        init = jnp.zeros((MOE_ROWS, D), F32)
    any_spec = pl.BlockSpec(memory_space=pl.ANY)
    return pl.pallas_call(
        _moe_dispatch_kernel,
        grid_spec=pltpu.PrefetchScalarGridSpec(
            num_scalar_prefetch=1, grid=(T_ALL // DISPATCH_TM,),
            in_specs=[pl.BlockSpec((DISPATCH_TM, D), lambda i, pos: (i, 0)), any_spec],
            out_specs=any_spec,
            scratch_shapes=[pltpu.SemaphoreType.DMA(())]),
        out_shape=jax.ShapeDtypeStruct((MOE_ROWS, D), F32),
        input_output_aliases={2: 0},
        compiler_params=_cparams(("arbitrary",)),
        name="moe_dispatch",
    )(flat_pos, h, init)


def _moe_up_kernel(te_ref, nu_ref, g_ref, wg_ref, wu_ref, o_ref, wbf):
    i = pl.program_id(1)
    new_w = (i == 0) | (te_ref[i] != te_ref[jnp.maximum(i - 1, 0)])

    @pl.when(new_w)
    def _():
        wbf[0] = wg_ref[0, 0].astype(BF16)
        wbf[1] = wu_ref[0, 0].astype(BF16)

    @pl.when(i < nu_ref[0])
    def _():
        _swiglu_chunks(g_ref[...].astype(BF16), wbf, o_ref)

    @pl.when(i >= nu_ref[0])
    def _():
        o_ref[...] = jnp.zeros(o_ref.shape, o_ref.dtype)


def _moe_up(tile_expert, n_used, rows, wg, wu, li, tn):
    wspec = pl.BlockSpec((1, 1, D, tn), lambda j, i, te, nu: (li, te[i], 0, j))
    return pl.pallas_call(
        _moe_up_kernel,
        grid_spec=pltpu.PrefetchScalarGridSpec(
            num_scalar_prefetch=2, grid=(D_FF_E // tn, MOE_TILES),
            in_specs=[pl.BlockSpec((MOE_TM, D), lambda j, i, te, nu: (i, 0)), wspec, wspec],
            out_specs=pl.BlockSpec((MOE_TM, tn), lambda j, i, te, nu: (i, j)),
            scratch_shapes=[pltpu.VMEM((2, D, tn), BF16)]),
        out_shape=jax.ShapeDtypeStruct((MOE_ROWS, D_FF_E), BF16),
        compiler_params=_cparams(("arbitrary", "arbitrary"), VMEM_LIMIT),
        name="moe_up",
    )(tile_expert, n_used, rows, wg, wu)


def _moe_down_kernel(te_ref, nu_ref, a_ref, wd_ref, o_ref, wbf):
    i = pl.program_id(1)
    new_w = (i == 0) | (te_ref[i] != te_ref[jnp.maximum(i - 1, 0)])

    @pl.when(new_w)
    def _():
        wbf[...] = wd_ref[0, 0].astype(BF16)

    @pl.when(i < nu_ref[0])
    def _():
        o_ref[...] = jnp.dot(a_ref[...], wbf[...], preferred_element_type=F32)

    @pl.when(i >= nu_ref[0])
    def _():
        o_ref[...] = jnp.zeros(o_ref.shape, o_ref.dtype)


def _moe_down(tile_expert, n_used, act, wd, li, tn):
    return pl.pallas_call(
        _moe_down_kernel,
        grid_spec=pltpu.PrefetchScalarGridSpec(
            num_scalar_prefetch=2, grid=(D // tn, MOE_TILES),
            in_specs=[pl.BlockSpec((MOE_TM, D_FF_E), lambda j, i, te, nu: (i, 0)),
                      pl.BlockSpec((1, 1, D_FF_E, tn), lambda j, i, te, nu: (li, te[i], 0, j))],
            out_specs=pl.BlockSpec((MOE_TM, tn), lambda j, i, te, nu: (i, j)),
            scratch_shapes=[pltpu.VMEM((D_FF_E, tn), BF16)]),
        out_shape=jax.ShapeDtypeStruct((MOE_ROWS, D), F32),
        compiler_params=_cparams(("arbitrary", "arbitrary"), VMEM_LIMIT),
        name="moe_down",
    )(tile_expert, n_used, act, wd)


def _moe_combine_kernel(pos_ref, x_ref, y_ref, gf_ref, gate_ref, *rest, with_next):
    if with_next:
        g_ref, sc_ref, sh_ref, o_ref, h_ref, ybuf, sems = rest
    else:
        o_ref, ybuf, sems = rest
    i = pl.program_id(0)
    buf = i % 2

    def copy(b, which, r, p):
        return pltpu.make_async_copy(y_ref.at[pl.ds(p, 1)], ybuf.at[b, which, pl.ds(r, 1)], sems.at[b])

    def gather(tile, b):
        base = tile * COMBINE_TM

        def start(r, c):
            copy(b, 0, r, pos_ref[base + r]).start(priority=0)
            copy(b, 1, r, pos_ref[T_ALL + base + r]).start(priority=1)
            return c
        lax.fori_loop(0, COMBINE_TM, start, 0, unroll=DMA_UNROLL)

    @pl.when(i == 0)
    def _():
        gather(0, 0)

    @pl.when(i + 1 < pl.num_programs(0))
    def _():
        gather(i + 1, 1 - buf)

    def wait(r, c):
        copy(buf, 0, 0, 0).wait()
        copy(buf, 1, 0, 0).wait()
        return c

    lax.fori_loop(0, COMBINE_TM, wait, 0, unroll=DMA_UNROLL)
    gf = gf_ref[...]
    f = gf[:, 0:1] * ybuf[buf, 0] + gf[:, 1:2] * ybuf[buf, 1]
    x_new = x_ref[...] + gate_ref[0] * f
    o_ref[...] = x_new
    if with_next:
        h_ref[...] = _modulate_math(x_new, g_ref[...], sc_ref[0], sh_ref[0]).astype(h_ref.dtype)


def _moe_combine(x, y, flat_pos, gf, mod, layer, next_norm):
    tm = COMBINE_TM
    base = (layer * 6 + 5) * N_SEG
    row = pl.BlockSpec((tm, D), lambda i, pos: (i, 0))
    in_specs = [row, pl.BlockSpec(memory_space=pl.ANY),
                pl.BlockSpec((tm, 128), lambda i, pos: (i, 0)),
                pl.BlockSpec((1, 1, D), lambda i, pos: (base + _seg_of_row(i * tm), 0, 0))]
    args = [flat_pos, x, y, gf, mod]
    out_specs, out_shape = row, jax.ShapeDtypeStruct((T_ALL, D), F32)
    if next_norm is not None:
        g, n_layer, k_shift = next_norm
        in_specs += [pl.BlockSpec((1, D), lambda i, pos: (0, 0)),
                     _mod_spec(n_layer, k_shift + 1, tm, lambda i, pos: i),
                     _mod_spec(n_layer, k_shift, tm, lambda i, pos: i)]
        args += [g.reshape(1, D), mod, mod]
        out_specs = [row, row]
        out_shape = [out_shape, jax.ShapeDtypeStruct((T_ALL, D), BF16)]
    return pl.pallas_call(
        functools.partial(_moe_combine_kernel, with_next=next_norm is not None),
        grid_spec=pltpu.PrefetchScalarGridSpec(
            num_scalar_prefetch=1, grid=(T_ALL // tm,), in_specs=in_specs, out_specs=out_specs,
            scratch_shapes=[pltpu.VMEM((2, 2, tm, D), F32), pltpu.SemaphoreType.DMA((2,))]),
        out_shape=out_shape,
        compiler_params=_cparams(("arbitrary",)),
        name="moe_combine",
    )(*args)


def _moe(x, h, mod, layer, li, w, next_norm, sorted_init):
    ei, gf, cnt = _router(h, w["moe_router_w"][li], w["moe_router_b"][li])
    experts, ranks, counts = ei[:, 0:2], ei[:, 2:4], cnt[0, :N_EXP]
    tiles = (counts + MOE_TM - 1) // MOE_TM
    tile_end = jnp.cumsum(tiles)
    row0 = (tile_end - tiles) * MOE_TM
    pos = (row0[experts] + ranks).astype(jnp.int32)
    n_used = tile_end[-1:].astype(jnp.int32)
    t_idx = jnp.minimum(jnp.arange(MOE_TILES, dtype=jnp.int32), n_used[0] - 1)
    tile_expert = jnp.sum(t_idx[:, None] >= tile_end[None, :], axis=1).astype(jnp.int32)
    flat_pos = pos.T.reshape(-1)
    sorted_rows = _moe_dispatch(h, flat_pos, sorted_init)
    act = _moe_up(tile_expert, n_used, sorted_rows, w["moe_w_gate"], w["moe_w_up"], li, MOE_UP_TN)
    y = _moe_down(tile_expert, n_used, act, w["moe_w_down"], li, D)
    return _moe_combine(x, y, flat_pos, gf, mod, layer, next_norm), sorted_rows


def _odd_mixer_layer(x, h, mod, l, i, w, spectra, consts, sorted_init):
    p = _mm([h], [w["odd_w_in"]], tm=TM_PROJ, tn=TN_ODD_IN, wi=i)
    cw = w["hy_conv_w"][i]
    z_c = _hy_conv(p, cw, spectra[SEQ][i], w["hy_bias"][i], consts[SEQ], SEQ, BATCH, 0, D)
    z_l = _hy_conv(p, cw, spectra[DEC_SEQ][i], w["hy_bias"][i], consts[DEC_SEQ], DEC_SEQ, DEC_BATCH,
                   T_CTX // DEC_SEQ, HY_TC_LAT)
    x, h2 = _mm([(z_c, z_l)], [w["odd_w_out"]], tm=TM_EPI, tn=D, epilogue="residual_mod", wi=i,
                resid=x, mod=mod, layer=l, k_gate=2, next_norm=(w["norm_ffn_g"][l], l, 3, F32))
    next_norm = (w["norm_mix_g"][l + 1], l + 1, 0) if l + 1 < DEPTH else None
    out, sorted_rows = _moe(x, h2, mod, l, i, w, next_norm, sorted_init)
    x, h_next = out if next_norm is not None else (out, None)
    return x, h_next, sorted_rows


def kernel(x_prompt, x_sample, state_gdn, cache_nat_k, cache_nat_v, c, c_ctx, ada_w, ada_b, norm_mix_g, norm_ffn_g, even_w_in, gdn_conv_w, gdn_a_log, gdn_dt_bias, gdn_norm_g, nat_q_norm_g, nat_k_norm_g, nat_rpb, even_w_out, ffn_w_gate, ffn_w_up, ffn_w_down, odd_w_in, hy_conv_w, hy_w1, hy_b1, hy_w2, hy_b2, hy_w3, hy_freq, hy_bias, odd_w_out, moe_router_w, moe_router_b, moe_w_gate, moe_w_up, moe_w_down):
    w = dict(state_gdn=state_gdn, cache_nat_k=cache_nat_k, cache_nat_v=cache_nat_v,
             norm_mix_g=norm_mix_g, norm_ffn_g=norm_ffn_g, even_w_in=even_w_in, gdn_conv_w=gdn_conv_w,
             gdn_a_log=gdn_a_log, gdn_dt_bias=gdn_dt_bias, gdn_norm_g=gdn_norm_g,
             nat_q_norm_g=nat_q_norm_g, nat_k_norm_g=nat_k_norm_g, nat_rpb=nat_rpb, even_w_out=even_w_out,
             ffn_w_gate=ffn_w_gate, ffn_w_up=ffn_w_up, ffn_w_down=ffn_w_down, odd_w_in=odd_w_in,
             hy_conv_w=hy_conv_w, hy_bias=hy_bias, odd_w_out=odd_w_out, moe_router_w=moe_router_w,
             moe_router_b=moe_router_b, moe_w_gate=moe_w_gate, moe_w_up=moe_w_up, moe_w_down=moe_w_down)
    assert SEQ == ROW_TILE and DEC_SEQ % ROW_TILE == 0
    mod = _ada_all(c, c_ctx, ada_w, ada_b)
    consts = {s: _dft_consts(s) for s in (SEQ, DEC_SEQ)}
    n_odd = DEPTH // 2
    spectra = {s: [_hy_filter_spectrum(s, consts[s], hy_w1[i], hy_b1[i], hy_w2[i], hy_b2[i], hy_w3[i], hy_freq[i])
                   for i in range(n_odd)] for s in (SEQ, DEC_SEQ)}
    x, h = _modulate_first(x_prompt.reshape(T_CTX, D), x_sample.reshape(T_LAT, D), norm_mix_g[0], mod, BF16)
    states, kns, pms = [], [], []
    sorted_rows = None
    for l in range(DEPTH):
        i = l // 2
        if l % 2 == 0:
            x, h, kn, p_main, s_ctx = _even_mixer_layer(x, h, mod, l, i, w)
            states.append(s_ctx)
            kns.append(kn)
            pms.append(p_main)
        else:
            x, h, sorted_rows = _odd_mixer_layer(x, h, mod, l, i, w, spectra, consts, sorted_rows)
    k_cache, v_cache, state_out = _finalize_caches(kns, pms, states)
    y_prompt = x[:T_CTX].reshape(BATCH, SEQ, D)
    y_sample = x[T_CTX:].reshape(DEC_BATCH, DEC_SEQ, D)
    return (y_prompt, y_sample, state_out, k_cache, v_cache)
```

```python
import functools
import math

import jax
import jax.numpy as jnp
import numpy as np
from jax import lax
from jax.experimental import pallas as pl
from jax.experimental.pallas import tpu as pltpu

F32 = jnp.float32
BF16 = jnp.bfloat16

D = 1024
BATCH = 16
SEQ = 256
DEPTH = 4
DEC_BATCH = 2
DEC_SEQ = 1024
PAST_LEN = 512
GRID_W = 64
EPS = 1e-6
H_A = 4
DK_A = 128
DV_A = 128
CHUNK = 64
H_B = 8
DH_B = 64
WIN_R = 8
WIN_C = 16
A_QKV = 2 * H_A * DK_A + H_A * DV_A
A_GATE = H_A * DV_A
B_QKV = 3 * H_B * DH_B
HY_EMB = 33
HY_HID = 64
HY_FAST = 0.3
HY_SLOW = 1.5
HY_TARGET = 1e-2
D_FF = 2816
N_EXP = 8
D_FF_E = 3584

T_CTX = BATCH * SEQ
T_LAT = DEC_BATCH * DEC_SEQ
T_ALL = T_CTX + T_LAT
N_SEG = 8
ROW_TILE = 256

VMEM_LIMIT = 56 * 1024 * 1024
LANES = 128

TM_PROJ = 1024
TN_EVEN_IN = 1792
TN_ODD_IN = 1536
TM_EPI = 512
TN_FFN_UP = D_FF // 2
MOE_UP_TN = D_FF_E // 2
HY_TC_LAT = 512
GDN_NS_CTX = 8
GDN_NS_LAT = 2


def _cparams(sem, vmem=None):
    return pltpu.CompilerParams(dimension_semantics=sem, vmem_limit_bytes=vmem)


def _sigmoid(x):
    return 1.0 / (1.0 + jnp.exp(-x))


def _seg_of_row(row):
    return jnp.where(row < T_CTX, 0, 1 + (row - T_CTX) // DEC_SEQ)


def _split_bf16(x):
    hi = x.astype(BF16)
    lo = (x - hi.astype(F32)).astype(BF16)
    return hi, lo


def _dot(a, b, dims=(((1,), (0,)), ((), ())), prec="bf16"):
    if prec == "bf16":
        return lax.dot_general(a.astype(BF16), b.astype(BF16), dims, preferred_element_type=F32)
    ah, al = _split_bf16(a.astype(F32))
    bh, bl = _split_bf16(b.astype(F32))
    r = lax.dot_general(ah, bh, dims, preferred_element_type=F32)
    r = r + lax.dot_general(ah, bl, dims, preferred_element_type=F32)
    r = r + lax.dot_general(al, bh, dims, preferred_element_type=F32)
    return r


NT_DIMS = (((1,), (1,)), ((), ()))


def _ada_kernel(cv_ref, w_ref, b_ref, o_ref):
    cv = cv_ref[...]
    s = cv * _sigmoid(cv)
    o_ref[0] = _dot(s, w_ref[0]) + b_ref[0]


def _ada_all(c, c_ctx, ada_w, ada_b):
    cv = jnp.zeros((N_SEG, D), F32).at[0].set(c_ctx).at[1:1 + DEC_BATCH].set(c)
    tn = 1536
    out = pl.pallas_call(
        _ada_kernel,
        grid=(DEPTH, 6 * D // tn),
        in_specs=[
            pl.BlockSpec((N_SEG, D), lambda l, j: (0, 0)),
            pl.BlockSpec((1, D, tn), lambda l, j: (l, 0, j)),
            pl.BlockSpec((1, 1, tn), lambda l, j: (l, 0, j)),
        ],
        out_specs=pl.BlockSpec((1, N_SEG, tn), lambda l, j: (l, 0, j)),
        out_shape=jax.ShapeDtypeStruct((DEPTH, N_SEG, 6 * D), F32),
        compiler_params=_cparams(("arbitrary", "arbitrary"), VMEM_LIMIT),
        name="ada",
    )(cv, ada_w, ada_b.reshape(DEPTH, 1, 6 * D))
    return out.reshape(DEPTH, N_SEG, 6, D).transpose(0, 2, 1, 3).reshape(DEPTH * 6 * N_SEG, 1, D)


def _mod_spec(layer, k, tm, row_of_step):
    base = (layer * 6 + k) * N_SEG

    def imap(*ids):
        return (base + _seg_of_row(row_of_step(*ids) * tm), 0, 0)

    return pl.BlockSpec((1, 1, D), imap)


def _modulate_math(x, g, scale, shift):
    ms = jnp.mean(x * x, axis=-1, keepdims=True)
    y = x * lax.rsqrt(ms + EPS) * g
    return y * (1.0 + scale) + shift


def _modulate_kernel(xc_ref, xl_ref, g_ref, sc_ref, sh_ref, x_ref, o_ref, *, n_ctx_tiles):
    x = jnp.where(pl.program_id(0) < n_ctx_tiles, xc_ref[...], xl_ref[...])
    x_ref[...] = x
    o_ref[...] = _modulate_math(x, g_ref[...], sc_ref[0], sh_ref[0]).astype(o_ref.dtype)


def _modulate_first(x_ctx, x_lat, g, mod, out_dtype):
    tm = TM_EPI
    n_ctx_tiles = T_CTX // tm
    row = pl.BlockSpec((tm, D), lambda i: (i, 0))
    return pl.pallas_call(
        functools.partial(_modulate_kernel, n_ctx_tiles=n_ctx_tiles),
        grid=(T_ALL // tm,),
        in_specs=[
            pl.BlockSpec((tm, D), lambda i: (jnp.minimum(i, n_ctx_tiles - 1), 0)),
            pl.BlockSpec((tm, D), lambda i: (jnp.maximum(i - n_ctx_tiles, 0), 0)),
            pl.BlockSpec((1, D), lambda i: (0, 0)),
            _mod_spec(0, 1, tm, lambda i: i),
            _mod_spec(0, 0, tm, lambda i: i),
        ],
        out_specs=[row, row],
        out_shape=[jax.ShapeDtypeStruct((T_ALL, D), F32), jax.ShapeDtypeStruct((T_ALL, D), out_dtype)],
        compiler_params=_cparams(("arbitrary",)),
        name="modulate",
    )(x_ctx, x_lat, g.reshape(1, D), mod, mod)


SWIGLU_CHUNK = 256


def _swiglu_chunks(h, wbf, o_ref):
    n = o_ref.shape[1]
    for c0 in range(0, n, SWIGLU_CHUNK):
        c1 = min(c0 + SWIGLU_CHUNK, n)
        a = jnp.dot(h, wbf[0, :, c0:c1], preferred_element_type=F32)
        b = jnp.dot(h, wbf[1, :, c0:c1], preferred_element_type=F32)
        o_ref[:, c0:c1] = (a * _sigmoid(a) * b).astype(o_ref.dtype)


EVEN_SHIFT = 4 * H_A
EVEN_SPLIT = A_QKV + A_GATE


def _mm_kernel(*refs, chunks, n_w, epilogue, shifted, n_ctx_tiles):
    pos = 0
    lhs = []
    for kw, paired in chunks:
        cnt = 2 if paired else 1
        lhs.append(refs[pos:pos + cnt])
        pos += cnt
    n_wrefs = n_w + (1 if shifted else 0)
    w_refs = refs[pos:pos + n_wrefs]
    rest = refs[pos + n_wrefs:]
    j, i = pl.program_id(0), pl.program_id(1)
    if epilogue == "residual":
        x_ref, gate_ref, o_ref, wbf = rest
    elif epilogue == "residual_mod":
        x_ref, gate_ref, g_ref, sc_ref, sh_ref, o_ref, h_ref, wbf = rest
    else:
        o_ref, wbf = rest

    @pl.when(i == 0)
    def _():
        if shifted:
            tn = wbf.shape[2]
            for jj in range(shifted):
                split = min(max(EVEN_SPLIT - jj * tn, 0), tn)

                @pl.when(j == jj)
                def _(split=split):
                    wa = w_refs[0][0]
                    if split == tn:
                        wbf[0] = wa.astype(BF16)
                    else:
                        parts = [wa[:, :split]] if split else []
                        parts += [wa[:, split + EVEN_SHIFT:], w_refs[1][0][:, :EVEN_SHIFT]]
                        wbf[0] = jnp.concatenate(parts, axis=1).astype(BF16)
        else:
            for k in range(n_w):
                wbf[k] = w_refs[k][0].astype(BF16)

    def lhs_chunk(c):
        r = lhs[c]
        if len(r) == 2:
            return jnp.where(i < n_ctx_tiles, r[0][...], r[1][...]).astype(BF16)
        return r[0][...].astype(BF16)

    def matmul(k):
        acc, off = None, 0
        for c, (kw, _) in enumerate(chunks):
            part = jnp.dot(lhs_chunk(c), wbf[k, off:off + kw, :], preferred_element_type=F32)
            acc = part if acc is None else acc + part
            off += kw
        return acc

    if epilogue == "swiglu":
        _swiglu_chunks(lhs_chunk(0), wbf, o_ref)
        return
    a = matmul(0)
    if epilogue == "residual":
        o_ref[...] = x_ref[...] + gate_ref[0] * a
    elif epilogue == "residual_mod":
        x_new = x_ref[...] + gate_ref[0] * a
        o_ref[...] = x_new
        h_ref[...] = _modulate_math(x_new, g_ref[...], sc_ref[0], sh_ref[0]).astype(h_ref.dtype)
    else:
        o_ref[...] = a.astype(o_ref.dtype)


def _mm(lhs, ws, *, tm, tn, out_dtype=F32, epilogue="none", wi=0, n_out=None, col0=0, shifted=False,
        resid=None, mod=None, layer=None, k_gate=None, next_norm=None):
    chunks, args, in_specs = [], [], []
    n_ctx_tiles = T_CTX // tm
    for part in lhs:
        if isinstance(part, tuple):
            kw = part[0].shape[1]
            chunks.append((kw, True))
            args += [part[0], part[1]]
            in_specs += [pl.BlockSpec((tm, kw), lambda j, i: (jnp.minimum(i, n_ctx_tiles - 1), 0)),
                         pl.BlockSpec((tm, kw), lambda j, i: (jnp.maximum(i - n_ctx_tiles, 0), 0))]
        else:
            kw = part.shape[1]
            chunks.append((kw, False))
            args.append(part)
            in_specs.append(pl.BlockSpec((tm, kw), lambda j, i: (i, 0)))
    kdim = sum(kw for kw, _ in chunks)
    n = n_out if n_out is not None else ws[0].shape[2]
    n_w = len(ws)
    if shifted:
        last = pl.cdiv(ws[0].shape[2], tn) - 1
        in_specs += [pl.BlockSpec((1, kdim, tn), lambda j, i: (wi, 0, j)),
                     pl.BlockSpec((1, kdim, tn), lambda j, i: (wi, 0, jnp.minimum(j + 1, last)))]
        args += [ws[0], ws[0]]
    else:
        in_specs += [pl.BlockSpec((1, kdim, tn), lambda j, i: (wi, 0, col0 + j)) for _ in ws]
        args += list(ws)
    out_specs = pl.BlockSpec((tm, tn), lambda j, i: (i, j))
    out_shape = jax.ShapeDtypeStruct((T_ALL, n), out_dtype)
    if epilogue in ("residual", "residual_mod"):
        base = (layer * 6 + k_gate) * N_SEG
        in_specs += [pl.BlockSpec((tm, tn), lambda j, i: (i, j)),
                     pl.BlockSpec((1, 1, tn), lambda j, i: (base + _seg_of_row(i * tm), 0, j))]
        args += [resid, mod]
    if epilogue == "residual_mod":
        g, n_layer, k_shift, h_dtype = next_norm
        assert tn == n == D
        in_specs += [pl.BlockSpec((1, D), lambda j, i: (0, 0)),
                     _mod_spec(n_layer, k_shift + 1, tm, lambda j, i: i),
                     _mod_spec(n_layer, k_shift, tm, lambda j, i: i)]
        args += [g.reshape(1, D), mod, mod]
        out_specs = [out_specs, pl.BlockSpec((tm, tn), lambda j, i: (i, j))]
        out_shape = [out_shape, jax.ShapeDtypeStruct((T_ALL, n), h_dtype)]
    return pl.pallas_call(
        functools.partial(_mm_kernel, chunks=tuple(chunks), n_w=n_w, epilogue=epilogue,
                          shifted=(n // tn if shifted else 0), n_ctx_tiles=n_ctx_tiles),
        grid=(n // tn, T_ALL // tm),
        in_specs=in_specs,
        out_specs=out_specs,
        out_shape=out_shape,
        scratch_shapes=[pltpu.VMEM((n_w, kdim, tn), BF16)],
        compiler_params=_cparams(("arbitrary", "arbitrary"), VMEM_LIMIT),
        name="mm_" + epilogue,
    )(*args)


def _conv3(x, prev_ref, next_ref, w, seq_len):
    i = pl.program_id(0)
    rows_n = x.shape[0]
    if seq_len is None:
        j = jnp.maximum(i - T_CTX // rows_n, 0)
        per = DEC_SEQ // rows_n
        is_ctx = i < T_CTX // rows_n
        first = is_ctx | (j % per == 0)
        last = is_ctx | (j % per == per - 1)
    else:
        first = (i * rows_n) % seq_len == 0
        last = ((i + 1) * rows_n) % seq_len == 0
    prev_row = jnp.where(first, 0.0, prev_ref[7:8, :])
    next_row = jnp.where(last, 0.0, next_ref[0:1, :])
    rows = lax.broadcasted_iota(jnp.int32, x.shape, 0)
    xm = jnp.where(rows == 0, prev_row, pltpu.roll(x, 1, 0))
    xp = jnp.where(rows == rows_n - 1, next_row, pltpu.roll(x, rows_n - 1, 0))
    return xm * w[0:1, :] + x * w[1:2, :] + xp * w[2:3, :]


def _halo_specs(width, col_block, row_off_tiles):
    per = ROW_TILE // 8
    last_blk = T_ALL // 8 - 1
    main = pl.BlockSpec((ROW_TILE, width), lambda i: (i + row_off_tiles, col_block))
    prev = pl.BlockSpec((8, width), lambda i: (jnp.maximum((i + row_off_tiles) * per - 1, 0), col_block))
    nxt = pl.BlockSpec((8, width), lambda i: (jnp.minimum((i + row_off_tiles + 1) * per, last_blk), col_block))
    return main, prev, nxt


def _cumsum_rows(x, reverse):
    n = x.shape[0]
    rows = lax.broadcasted_iota(jnp.int32, x.shape, 0)
    k = 1
    while k < n:
        if reverse:
            x = x + jnp.where(rows < n - k, pltpu.roll(x, n - k, 0), 0.0)
        else:
            x = x + jnp.where(rows >= k, pltpu.roll(x, k, 0), 0.0)
        k *= 2
    return x


def _gdn_prep_kernel(pm_ref, prev_ref, next_ref, ps_ref, cw_ref, par_ref, qkv_ref, bg_ref, *, seq_len):
    x = _conv3(pm_ref[...], prev_ref, next_ref, cw_ref[...], seq_len)
    x = x * _sigmoid(x)
    for h in range(H_A):
        sl = slice(h * DK_A, (h + 1) * DK_A)
        qh = x[:, sl]
        qkv_ref[:, sl] = qh * lax.rsqrt(jnp.sum(qh * qh, axis=-1, keepdims=True) + EPS) * (DK_A ** -0.5)
        sl = slice(H_A * DK_A + h * DK_A, H_A * DK_A + (h + 1) * DK_A)
        kh = x[:, sl]
        qkv_ref[:, sl] = kh * lax.rsqrt(jnp.sum(kh * kh, axis=-1, keepdims=True) + EPS)
    qkv_ref[:, 2 * H_A * DK_A:] = x[:, 2 * H_A * DK_A:]
    raw = ps_ref[...]
    lane = lax.broadcasted_iota(jnp.int32, raw.shape, 1)
    beta = _sigmoid(raw)
    z = raw + par_ref[1:2, :]
    softplus = jnp.maximum(z, 0.0) + jnp.log(1.0 + jnp.exp(-jnp.abs(z)))
    g = -jnp.exp(par_ref[0:1, :]) * softplus
    bg_ref[...] = jnp.where(lane < 2 * H_A, beta, jnp.where(lane < 4 * H_A, g, 0.0))


def _gdn_prep(p_main, p_small, conv_w, a_log, dt_bias, row_off_tiles, n_rows, seq_len):
    par = jnp.zeros((2, 128), F32)
    par = par.at[0, 2 * H_A:4 * H_A].set(a_log.reshape(-1)).at[1, 2 * H_A:4 * H_A].set(dt_bias.reshape(-1))
    main, prev, nxt = _halo_specs(A_QKV, 0, row_off_tiles)
    return pl.pallas_call(
        functools.partial(_gdn_prep_kernel, seq_len=seq_len),
        grid=(n_rows // ROW_TILE,),
        in_specs=[main, prev, nxt,
                  pl.BlockSpec((ROW_TILE, 128), lambda i: (i + row_off_tiles, 0)),
                  pl.BlockSpec((3, A_QKV), lambda i: (0, 0)),
                  pl.BlockSpec((2, 128), lambda i: (0, 0))],
        out_specs=[pl.BlockSpec((ROW_TILE, A_QKV), lambda i: (i, 0)),
                   pl.BlockSpec((ROW_TILE, 128), lambda i: (i, 0))],
        out_shape=[jax.ShapeDtypeStruct((n_rows, A_QKV), F32),
                   jax.ShapeDtypeStruct((n_rows, 128), F32)],
        compiler_params=_cparams(("arbitrary",)),
        name="gdn_prep",
    )(p_main, p_main, p_main, p_small, conv_w, par)


B_NN = (((2,), (1,)), ((0,), (0,)))
B_NT = (((2,), (2,)), ((0,), (0,)))
B_TN = (((1,), (1,)), ((0,), (0,)))


def _bmm(a, b, dims=B_NN):
    return _dot(a, b, dims)


def _gdn_scanb_kernel(*refs, ns, has_s0):
    if has_s0:
        qf_ref, qb_ref, bf_ref, bb_ref, s0_ref, of_ref, ob_ref, sfin_ref, st = refs
    else:
        qf_ref, qb_ref, bf_ref, bb_ref, of_ref, ob_ref, sfin_ref, st = refs
    step = pl.program_id(1)
    c = CHUNK
    nb = 2 * ns * H_A

    @pl.when(step == 0)
    def _():
        for d in range(2):
            st[d] = s0_ref[:, 0, d] if has_s0 else jnp.zeros(st.shape[1:], F32)

    q_l, k_l, v_l, beta_l, gc_l = [], [], [], [], []
    for d in range(2):
        for s in range(ns):
            qkv = (qf_ref if d == 0 else qb_ref)[s, 0]
            bg = (bf_ref if d == 0 else bb_ref)[s, 0]
            gcum = _cumsum_rows(bg, reverse=(d == 1))
            for h in range(H_A):
                q_l.append(qkv[:, h * DK_A:(h + 1) * DK_A])
                k_l.append(qkv[:, (H_A + h) * DK_A:(H_A + h + 1) * DK_A])
                v_l.append(qkv[:, 2 * H_A * DK_A + h * DV_A:2 * H_A * DK_A + (h + 1) * DV_A])
                col = d * H_A + h
                beta_l.append(bg[:, col:col + 1])
                gc_l.append(gcum[:, 2 * H_A + col:2 * H_A + col + 1])
    q, k, v = jnp.stack(q_l), jnp.stack(k_l), jnp.stack(v_l)
    beta, gc = jnp.stack(beta_l), jnp.stack(gc_l)

    ri = lax.broadcasted_iota(jnp.int32, (1, c, c), 1)
    ci = lax.broadcasted_iota(jnp.int32, (1, c, c), 2)
    eye = ri == ci
    half = nb // 2
    gc_row = jnp.sum(jnp.where(eye, gc, 0.0), axis=1, keepdims=True)
    dgc = gc - gc_row

    def masked_exp(x, keep):
        return jnp.where(keep, jnp.exp(jnp.where(keep, x, 0.0)), 0.0)

    decay = jnp.concatenate([masked_exp(dgc[:half], ri >= ci), masked_exp(dgc[half:], ri <= ci)], axis=0)
    kb = k * beta
    a_mat = jnp.where(eye, 0.0, _bmm(kb, k, B_NT) * decay)
    blk = 8
    diag = (ri // blk) == (ci // blk)
    pw = jnp.where(diag, a_mat, 0.0)
    r_mat = -pw
    for _ in range(2):
        pw = _bmm(pw, pw)
        r_mat = r_mat + pw + _bmm(r_mat, pw)
    while blk < c:
        off = ((ri // (2 * blk)) == (ci // (2 * blk))) & ((ri // blk) != (ci // blk))
        e_mat = jnp.where(off, a_mat, 0.0)
        x_mat = e_mat + _bmm(r_mat, e_mat)
        r_mat = r_mat - (x_mat + _bmm(x_mat, r_mat))
        blk *= 2
    eg = jnp.exp(gc)
    rhs = jnp.concatenate([v * beta, kb * eg], axis=2)
    sol = rhs + _bmm(r_mat, rhs)
    ub, wm = sol[:, :, :DV_A], sol[:, :, DV_A:]
    attn = _bmm(q, k, B_NT) * decay
    fwd1 = lax.broadcasted_iota(jnp.int32, (nb, 1, 1), 0) < nb // 2
    g_end = jnp.where(fwd1, gc[:, c - 1:c, :], gc[:, 0:1, :])
    qd = q * eg
    kd = k * jnp.exp(g_end - gc)
    state = st[...].reshape(nb, DK_A, DV_A)
    u = ub - _bmm(wm, state)
    o = _bmm(qd, state) + _bmm(attn, u)
    state = state * jnp.exp(g_end) + _bmm(kd, u, B_TN)
    st[...] = state.reshape(st.shape)
    for d in range(2):
        o_ref = of_ref if d == 0 else ob_ref
        for s in range(ns):
            for h in range(H_A):
                o_ref[s, 0, :, h * DV_A:(h + 1) * DV_A] = o[(d * ns + s) * H_A + h]

    @pl.when(step == pl.num_programs(1) - 1)
    def _():
        for d in range(2):
            sfin_ref[:, d] = st[d]


def _gdn_scanb(qkv, bg, s0, layer_i, n_seq, seq_len, ns):
    n = seq_len // CHUNK
    qkv4 = qkv.reshape(n_seq, n, CHUNK, A_QKV)
    bg4 = bg.reshape(n_seq, n, CHUNK, 128)
    fwd = lambda g, c: (g, c, 0, 0)
    bwd = lambda g, c: (g, n - 1 - c, 0, 0)
    in_specs = [pl.BlockSpec((ns, 1, CHUNK, A_QKV), fwd), pl.BlockSpec((ns, 1, CHUNK, A_QKV), bwd),
                pl.BlockSpec((ns, 1, CHUNK, 128), fwd), pl.BlockSpec((ns, 1, CHUNK, 128), bwd)]
    args = [qkv4, qkv4, bg4, bg4]
    if s0 is not None:
        in_specs.append(pl.BlockSpec((ns, 1, 2, H_A, DK_A, DV_A), lambda g, c: (g, layer_i, 0, 0, 0, 0)))
        args.append(s0)
    o_shape = jax.ShapeDtypeStruct((n_seq, n, CHUNK, H_A * DV_A), F32)
    of, ob, sfin = pl.pallas_call(
        functools.partial(_gdn_scanb_kernel, ns=ns, has_s0=s0 is not None),
        grid=(n_seq // ns, n),
        in_specs=in_specs,
        out_specs=[pl.BlockSpec((ns, 1, CHUNK, H_A * DV_A), fwd),
                   pl.BlockSpec((ns, 1, CHUNK, H_A * DV_A), bwd),
                   pl.BlockSpec((ns, 2, H_A, DK_A, DV_A), lambda g, c: (g, 0, 0, 0, 0))],
        out_shape=[o_shape, o_shape, jax.ShapeDtypeStruct((n_seq, 2, H_A, DK_A, DV_A), F32)],
        scratch_shapes=[pltpu.VMEM((2, ns, H_A, DK_A, DV_A), F32)],
        compiler_params=_cparams(("arbitrary", "arbitrary"), VMEM_LIMIT),
        name="gdn_scan",
    )(*args)
    rows = n_seq * seq_len
    return of.reshape(rows, H_A * DV_A), ob.reshape(rows, H_A * DV_A), sfin


def _gdn_post_kernel(of_ref, ob_ref, gate_ref, g_ref, o_ref):
    o = of_ref[...] + ob_ref[...]
    gate = gate_ref[...]
    for h in range(H_A):
        sl = slice(h * DV_A, (h + 1) * DV_A)
        oh = o[:, sl]
        y = oh * lax.rsqrt(jnp.mean(oh * oh, axis=-1, keepdims=True) + EPS) * g_ref[...]
        gh = gate[:, sl]
        o_ref[:, sl] = (y * (gh * _sigmoid(gh))).astype(o_ref.dtype)


def _gdn_post(of, ob, p_main, gdn_g, row_off_tiles, n_rows):
    gate_blk = A_QKV // A_GATE
    return pl.pallas_call(
        _gdn_post_kernel,
        grid=(n_rows // ROW_TILE,),
        in_specs=[pl.BlockSpec((ROW_TILE, A_GATE), lambda i: (i, 0)),
                  pl.BlockSpec((ROW_TILE, A_GATE), lambda i: (i, 0)),
                  pl.BlockSpec((ROW_TILE, A_GATE), lambda i: (i + row_off_tiles, gate_blk)),
                  pl.BlockSpec((1, DV_A), lambda i: (0, 0))],
        out_specs=pl.BlockSpec((ROW_TILE, A_GATE), lambda i: (i, 0)),
        out_shape=jax.ShapeDtypeStruct((n_rows, A_GATE), BF16),
        compiler_params=_cparams(("arbitrary",)),
        name="gdn_post",
    )(of, ob, p_main, gdn_g.reshape(1, DV_A))


NAT_W = H_B * DH_B
NAT_QCOL = (A_QKV + A_GATE) // NAT_W
NAT_SCALE = DH_B ** -0.5


def _nat_prep_kernel(q_ref, k_ref, bd_ref, gq_ref, gk_ref, qn_ref, kn_ref):
    bd = bd_ref[...]
    for x_ref, g_ref, o_ref in ((q_ref, gq_ref, qn_ref), (k_ref, gk_ref, kn_ref)):
        x = x_ref[...]
        ms = jnp.dot((x * x).astype(BF16), bd, preferred_element_type=F32)
        o_ref[...] = x * lax.rsqrt(ms + EPS) * g_ref[...]


def _nat_prep(p_main, qn_g, kn_g):
    grp = np.arange(NAT_W) // DH_B
    bd = jnp.asarray((grp[:, None] == grp[None, :]).astype(np.float32) / DH_B, BF16)
    spec = lambda cb: pl.BlockSpec((ROW_TILE, NAT_W), lambda i: (i, cb))
    return pl.pallas_call(
        _nat_prep_kernel,
        grid=(T_ALL // ROW_TILE,),
        in_specs=[spec(NAT_QCOL), spec(NAT_QCOL + 1),
                  pl.BlockSpec((NAT_W, NAT_W), lambda i: (0, 0)),
                  pl.BlockSpec((1, NAT_W), lambda i: (0, 0)),
                  pl.BlockSpec((1, NAT_W), lambda i: (0, 0))],
        out_specs=[spec(0), spec(0)],
        out_shape=[jax.ShapeDtypeStruct((T_ALL, NAT_W), F32)] * 2,
        compiler_params=_cparams(("arbitrary",)),
        name="nat_prep",
    )(p_main, p_main, bd, jnp.tile(qn_g, H_B).reshape(1, NAT_W), jnp.tile(kn_g, H_B).reshape(1, NAT_W))


def _pair_masks():
    lane = lax.broadcasted_iota(jnp.int32, (1, 2 * DH_B), 1)
    return lane < DH_B


def _nat_ctx_kernel(q_ref, k_ref, v_ref, o_ref):
    lo = _pair_masks()
    for p in range(H_B // 2):
        sl = slice(p * 2 * DH_B, (p + 1) * 2 * DH_B)
        q2, k2, v2 = q_ref[:, sl], k_ref[:, sl], v_ref[:, sl]
        halves = []
        for half in range(2):
            qm = jnp.where(lo if half == 0 else jnp.logical_not(lo), q2, 0.0)
            s = _dot(qm, k2, NT_DIMS) * NAT_SCALE
            e = jnp.exp(s - jnp.max(s, axis=-1, keepdims=True))
            pr = e / jnp.sum(e, axis=-1, keepdims=True)
            halves.append(_dot(pr, v2))
        o_ref[:, sl] = jnp.where(lo, halves[0], halves[1]).astype(o_ref.dtype)


def _nat_ctx(qn, kn, p_main):
    spec = lambda cb: pl.BlockSpec((SEQ, NAT_W), lambda b: (b, cb))
    return pl.pallas_call(
        _nat_ctx_kernel,
        grid=(BATCH,),
        in_specs=[spec(0), spec(0), spec(NAT_QCOL + 2)],
        out_specs=spec(0),
        out_shape=jax.ShapeDtypeStruct((T_CTX, NAT_W), BF16),
        compiler_params=_cparams(("arbitrary",)),
        name="nat_ctx",
    )(qn, kn, p_main)


def _nat_bias_kernel(r_ref, e_ref, ok_ref, o_ref):
    r = r_ref[...]
    hi, lo = _split_bf16(r)
    lo2 = (r - hi.astype(F32) - lo.astype(F32)).astype(BF16)
    e = e_ref[...]
    t = (jnp.dot(hi, e, preferred_element_type=F32) + jnp.dot(lo, e, preferred_element_type=F32)
         + jnp.dot(lo2, e, preferred_element_type=F32))
    o_ref[...] = jnp.where(ok_ref[...] > 0.5, t, -jnp.inf)


def _nat_bias(rpb):
    n_dr, n_dc = 2 * WIN_R - 1, 2 * WIN_C - 1
    qc = np.arange(GRID_W)[:, None]
    kc = np.arange(GRID_W)[None, :]
    dc = (kc - qc + WIN_C - 1).reshape(-1)
    c0 = np.clip(qc - WIN_C // 2, 0, GRID_W - WIN_C)
    ok = ((kc >= c0) & (kc < c0 + WIN_C)).reshape(1, -1).astype(np.float32)
    onehot = (np.arange(128)[:, None] == dc[None, :]).astype(np.float32)
    rows = H_B * n_dr
    rp = jnp.zeros((rows, 128), F32).at[:, :n_dc].set(rpb.reshape(rows, n_dc))
    tab = pl.pallas_call(
        _nat_bias_kernel,
        out_shape=jax.ShapeDtypeStruct((rows, GRID_W * GRID_W), F32),
        name="nat_bias",
    )(rp, jnp.asarray(onehot, BF16), jnp.asarray(ok))
    tab = tab.reshape(H_B, n_dr, GRID_W, GRID_W)
    pad = jnp.full((H_B, 1, GRID_W, GRID_W), -jnp.inf, F32)
    ext = jnp.concatenate([pad, tab, pad], axis=1)
    return jnp.concatenate([ext[:, :n_dr + 1], ext[:, 1:]], axis=-1)


NAT_QROWS = 2
NAT_KROWS = 10


def _nat_lat_kernel(q_ref, k_ref, v_ref, kc_ref, vc_ref, bias_ref, o_ref):
    rows = DEC_SEQ // GRID_W
    row_a = pl.program_id(1) * NAT_QROWS
    r0s = [jnp.clip(row_a + qi - WIN_R // 2, 0, rows - WIN_R) for qi in range(NAT_QROWS)]
    ws = jnp.minimum(r0s[0], rows - NAT_KROWS)
    start = pl.multiple_of(ws * GRID_W, GRID_W)
    n_loc = NAT_KROWS * GRID_W
    lo = _pair_masks()
    lane = lax.broadcasted_iota(jnp.int32, (GRID_W, 2 * GRID_W), 1)

    def bias_for(h):
        blocks = []
        for qi in range(NAT_QROWS):
            pieces = []
            for jp in range(NAT_KROWS // 2):
                rk = ws + 2 * jp
                ok0 = ((rk >= r0s[qi]) & (rk < r0s[qi] + WIN_R)).astype(jnp.int32)
                ok1 = ((rk + 1 >= r0s[qi]) & (rk + 1 < r0s[qi] + WIN_R)).astype(jnp.int32)
                d = jnp.clip(rk - (row_a + qi) + WIN_R - 1, -1, 2 * WIN_R - 2) + 1
                piece = bias_ref[h, d]
                pieces.append(jnp.where(jnp.where(lane < GRID_W, ok0, ok1) > 0, piece, -jnp.inf))
            blocks.append(jnp.concatenate(pieces, axis=1))
        return jnp.concatenate(blocks, axis=0)

    for p in range(H_B // 2):
        sl = slice(p * 2 * DH_B, (p + 1) * 2 * DH_B)
        q2 = q_ref[:, sl]
        kw = k_ref[0, pl.ds(start, n_loc), sl]
        vw = v_ref[pl.ds(start, n_loc), sl]
        kc, vc = kc_ref[0, :, sl], vc_ref[0, :, sl]
        halves = []
        for half in range(2):
            qm = jnp.where(lo if half == 0 else jnp.logical_not(lo), q2, 0.0)
            s_loc = _dot(qm, kw, NT_DIMS) * NAT_SCALE + bias_for(2 * p + half)
            s_ctx = _dot(qm, kc, NT_DIMS) * NAT_SCALE
            m = jnp.maximum(jnp.max(s_loc, axis=-1, keepdims=True), jnp.max(s_ctx, axis=-1, keepdims=True))
            e_loc, e_ctx = jnp.exp(s_loc - m), jnp.exp(s_ctx - m)
            inv = 1.0 / (jnp.sum(e_loc, axis=-1, keepdims=True) + jnp.sum(e_ctx, axis=-1, keepdims=True))
            halves.append(_dot(e_loc * inv, vw) + _dot(e_ctx * inv, vc))
        o_ref[:, sl] = jnp.where(lo, halves[0], halves[1]).astype(o_ref.dtype)


def _nat_lat(qn, kn, p_main, kc, vc, bias):
    steps = DEC_SEQ // GRID_W // NAT_QROWS
    tq = NAT_QROWS * GRID_W
    lat_tile0 = T_CTX // tq
    lat_seq0 = T_CTX // DEC_SEQ
    return pl.pallas_call(
        _nat_lat_kernel,
        grid=(DEC_BATCH, steps),
        in_specs=[pl.BlockSpec((tq, NAT_W), lambda b, r: (lat_tile0 + b * steps + r, 0)),
                  pl.BlockSpec((1, DEC_SEQ, NAT_W), lambda b, r: (lat_seq0 + b, 0, 0)),
                  pl.BlockSpec((DEC_SEQ, NAT_W), lambda b, r: (lat_seq0 + b, NAT_QCOL + 2)),
                  pl.BlockSpec((1, PAST_LEN, NAT_W), lambda b, r: (b, 0, 0)),
                  pl.BlockSpec((1, PAST_LEN, NAT_W), lambda b, r: (b, 0, 0)),
                  pl.BlockSpec(bias.shape, lambda b, r: (0, 0, 0, 0))],
        out_specs=pl.BlockSpec((tq, NAT_W), lambda b, r: (b * steps + r, 0)),
        out_shape=jax.ShapeDtypeStruct((T_LAT, NAT_W), BF16),
        compiler_params=_cparams(("arbitrary", "arbitrary"), VMEM_LIMIT),
        name="nat_lat",
    )(qn, kn.reshape(T_ALL // DEC_SEQ, DEC_SEQ, NAT_W), p_main, kc, vc, bias)


def _heads_to_lanes(cache):
    b, h, l, dh = cache.shape
    return cache.transpose(0, 2, 1, 3).reshape(b, l, h * dh)


def _finalize_kernel(k0_ref, v0_ref, s0_ref, k1_ref, v1_ref, s1_ref, ko_ref, vo_ref, so_ref):
    layer = pl.program_id(0)
    for idx, (k_ref, v_ref, s_ref) in enumerate(((k0_ref, v0_ref, s0_ref), (k1_ref, v1_ref, s1_ref))):
        @pl.when(layer == idx)
        def _(k_ref=k_ref, v_ref=v_ref, s_ref=s_ref):
            for h in range(H_B):
                ko_ref[0, 0, h] = k_ref[:, h * DH_B:(h + 1) * DH_B]
                vo_ref[0, 0, h] = v_ref[:, h * DH_B:(h + 1) * DH_B]
            so_ref[0, 0] = s_ref[0]


def _finalize_caches(kns, pms, states):
    n_even = len(kns)
    assert n_even == 2
    pick = lambda idx, l, b: b * (l if idx else 1 - l)
    tok = lambda idx, cb: pl.BlockSpec((SEQ, NAT_W), lambda l, b: (pick(idx, l, b), cb))
    st_in = lambda idx: pl.BlockSpec((1, 2, H_A, DK_A, DV_A), lambda l, b: (pick(idx, l, b), 0, 0, 0, 0))
    cache_out = pl.BlockSpec((1, 1, H_B, SEQ, DH_B), lambda l, b: (b, l, 0, 0, 0))
    cache_shape = jax.ShapeDtypeStruct((BATCH, n_even, H_B, SEQ, DH_B), F32)
    return pl.pallas_call(
        _finalize_kernel,
        grid=(n_even, BATCH),
        in_specs=[tok(0, 0), tok(0, NAT_QCOL + 2), st_in(0), tok(1, 0), tok(1, NAT_QCOL + 2), st_in(1)],
        out_specs=[cache_out, cache_out,
                   pl.BlockSpec((1, 1, 2, H_A, DK_A, DV_A), lambda l, b: (b, l, 0, 0, 0, 0))],
        out_shape=[cache_shape, cache_shape,
                   jax.ShapeDtypeStruct((BATCH, n_even, 2, H_A, DK_A, DV_A), F32)],
        compiler_params=_cparams(("arbitrary", "arbitrary")),
        name="finalize_caches",
    )(kns[0], pms[0], states[0], kns[1], pms[1], states[1])


def _even_mixer_layer(x, h, mod, l, i, w):
    p_main = _mm([h], [w["even_w_in"]], tm=TM_PROJ, tn=TN_EVEN_IN, wi=i, n_out=A_QKV + A_GATE + B_QKV,
                 shifted=True)
    p_small = _mm([h], [w["even_w_in"]], tm=TM_PROJ, tn=LANES, wi=i, n_out=LANES, col0=(A_QKV + A_GATE) // LANES)

    conv_w, a_log, dt_bias = w["gdn_conv_w"][i], w["gdn_a_log"][i], w["gdn_dt_bias"][i]
    ctx_tiles = T_CTX // ROW_TILE
    qkv_c, bg_c = _gdn_prep(p_main, p_small, conv_w, a_log, dt_bias, 0, T_CTX, SEQ)
    qkv_l, bg_l = _gdn_prep(p_main, p_small, conv_w, a_log, dt_bias, ctx_tiles, T_LAT, DEC_SEQ)
    of_c, ob_c, s_ctx = _gdn_scanb(qkv_c, bg_c, None, i, BATCH, SEQ, GDN_NS_CTX)
    of_l, ob_l, _ = _gdn_scanb(qkv_l, bg_l, w["state_gdn"], i, DEC_BATCH, DEC_SEQ, GDN_NS_LAT)
    oa_c = _gdn_post(of_c, ob_c, p_main, w["gdn_norm_g"][i], 0, T_CTX)
    oa_l = _gdn_post(of_l, ob_l, p_main, w["gdn_norm_g"][i], ctx_tiles, T_LAT)

    qn, kn = _nat_prep(p_main, w["nat_q_norm_g"][i], w["nat_k_norm_g"][i])
    ob_c = _nat_ctx(qn, kn, p_main)
    bias = _nat_bias(w["nat_rpb"][i])
    kc = _heads_to_lanes(w["cache_nat_k"][:, i])
    vc = _heads_to_lanes(w["cache_nat_v"][:, i])
    ob_l = _nat_lat(qn, kn, p_main, kc, vc, bias)

    x, h2 = _mm([(oa_c, oa_l), (ob_c, ob_l)], [w["even_w_out"]], tm=TM_EPI, tn=D, epilogue="residual_mod", wi=i,
                resid=x, mod=mod, layer=l, k_gate=2, next_norm=(w["norm_ffn_g"][l], l, 3, BF16))
    act = _mm([h2], [w["ffn_w_gate"], w["ffn_w_up"]], tm=TM_EPI, tn=TN_FFN_UP, out_dtype=BF16,
              epilogue="swiglu", wi=i)
    if l + 1 < DEPTH:
        x, h_next = _mm([act], [w["ffn_w_down"]], tm=TM_EPI, tn=D, epilogue="residual_mod", wi=i,
                        resid=x, mod=mod, layer=l, k_gate=5, next_norm=(w["norm_mix_g"][l + 1], l + 1, 0, BF16))
    else:
        x = _mm([act], [w["ffn_w_down"]], tm=TM_EPI, tn=D // 2, epilogue="residual", wi=i,
                resid=x, mod=mod, layer=l, k_gate=5)
        h_next = None
    return x, h_next, kn, p_main, s_ctx


def _dft_consts(seq):
    n = 2 * seq
    k = np.arange(seq)[:, None]
    s = np.arange(seq)[None, :]
    ang = 2.0 * np.pi * ((k * s) % n) / n
    fr = np.cos(ang)
    fi = -np.sin(ang)
    fi[0, :] = np.cos(np.pi * (np.arange(seq) % 2))
    fm = np.concatenate([fr, fi], axis=0)

    def to_bf16(a):
        return jnp.asarray(a.astype(np.float32)).astype(BF16)

    cw = np.full((n, 1), 2.0 / n)
    cw[0, 0] = cw[seq, 0] = 1.0 / n
    sg = np.ones((n, 1))
    sg[seq + 1:, 0] = -1.0
    cs = np.zeros((n, 128), np.float32)
    cs[:, 0:1] = cw
    cs[:, 1:2] = cw * sg
    return to_bf16(fm), to_bf16(fm.T.copy()), jnp.asarray(cs)


def _dft_apply(m, x):
    return jnp.dot(m, x.astype(BF16), preferred_element_type=F32)


def _hy_filter_kernel(z_ref, w1_ref, b1_ref, w2_ref, b2_ref, fq_ref, w3_ref, t_ref, dl_ref,
                      fm_ref, cs_ref, o_ref, hh_scr):
    c, d = pl.program_id(0), pl.program_id(1)

    @pl.when((c == 0) & (d == 0))
    def _():
        fq = fq_ref[...]
        hh = jnp.sin(fq * (_dot(z_ref[...], w1_ref[...], prec="bf16x3") + b1_ref[...]))
        hh_scr[...] = jnp.sin(fq * (_dot(hh, w2_ref[...], prec="bf16x3") + b2_ref[...]))

    filt = _dot(hh_scr[...], w3_ref[...], prec="bf16x3") * jnp.exp(-t_ref[...] * dl_ref[...])
    rows = lax.broadcasted_iota(jnp.int32, filt.shape, 0)
    filt = jnp.where((d == 1) & (rows == 0), 0.0, filt)
    spec = _dft_apply(fm_ref[...], filt)

    @pl.when(d == 0)
    def _():
        o_ref[...] = spec * cs_ref[:, 0:1]

    @pl.when(d == 1)
    def _():
        o_ref[...] = o_ref[...] + spec * cs_ref[:, 1:2]


def _hy_filter_spectrum(seq, consts, w1, b1, w2, b2, w3, freq):
    fm, _, cs = consts
    bands = (HY_EMB - 1) // 2
    t = np.linspace(0.0, 1.0, seq, dtype=np.float32)[:, None]
    wv = (np.float32(2.0 * math.pi / seq) * np.arange(seq, dtype=np.float32))[:, None]
    f = np.linspace(1e-4, bands - 1, bands, dtype=np.float32)[None, :]
    z = np.zeros((seq, 128), np.float32)
    z[:, 0:1] = t
    z[:, 1:1 + bands] = np.cos(f * wv)
    z[:, 1 + bands:HY_EMB] = -np.sin(f * wv)
    deltas = np.abs(np.linspace(math.log(HY_TARGET) / HY_FAST, math.log(HY_TARGET) / HY_SLOW, D,
                                dtype=np.float32))[None, :]
    w1p = jnp.zeros((128, HY_HID), F32).at[:HY_EMB].set(w1)
    tc = 256
    n = 2 * seq
    full = lambda shape: pl.BlockSpec(shape, lambda c, d: tuple(0 for _ in shape))
    return pl.pallas_call(
        _hy_filter_kernel,
        grid=(D // tc, 2),
        in_specs=[full((seq, 128)), full((128, HY_HID)), full((1, HY_HID)), full((HY_HID, HY_HID)),
                  full((1, HY_HID)), full((1, HY_HID)),
                  pl.BlockSpec((HY_HID, tc), lambda c, d: (0, d * (D // tc) + c)),
                  full((seq, 1)), pl.BlockSpec((1, tc), lambda c, d: (0, c)),
                  full((n, seq)), full((n, 128))],
        out_specs=pl.BlockSpec((n, tc), lambda c, d: (0, c)),
        out_shape=jax.ShapeDtypeStruct((n, D), F32),
        scratch_shapes=[pltpu.VMEM((seq, HY_HID), F32)],
        compiler_params=_cparams(("arbitrary", "arbitrary"), VMEM_LIMIT),
        name="hy_filter",
    )(jnp.asarray(z), w1p, b1.reshape(1, -1), w2, b2.reshape(1, -1), freq.reshape(1, -1), w3,
      jnp.asarray(t), jnp.asarray(deltas), fm, cs)


def _hy_conv_kernel(p0_ref, p1_ref, pv_ref, cw0_ref, cw1_ref, cwv_ref, kf_ref, bias_ref, fm_ref, ft_ref,
                    o_ref, *, seq):
    rows = lax.broadcasted_iota(jnp.int32, p0_ref.shape, 0)

    def conv3(x_ref, w_ref):
        x, w = x_ref[...], w_ref[...]
        xm = jnp.where(rows == 0, 0.0, pltpu.roll(x, 1, 0))
        xp = jnp.where(rows == seq - 1, 0.0, pltpu.roll(x, seq - 1, 0))
        return xm * w[0:1, :] + x * w[1:2, :] + xp * w[2:3, :]

    x0 = conv3(p0_ref, cw0_ref)
    u = conv3(pv_ref, cwv_ref) * conv3(p1_ref, cw1_ref)
    xs = _dft_apply(fm_ref[...], u)
    kf = kf_ref[...]
    xr, xi = xs[:seq], xs[seq:]
    kr, ki = kf[:seq], kf[seq:]
    row0 = lax.broadcasted_iota(jnp.int32, xr.shape, 0) == 0
    xiki = xi * ki
    yr = xr * kr - jnp.where(row0, 0.0, xiki)
    yi = jnp.where(row0, xiki, xr * ki + xi * kr)
    y = _dft_apply(ft_ref[...], jnp.concatenate([yr, yi], axis=0))
    o_ref[...] = ((y + u * bias_ref[...]) * x0).astype(o_ref.dtype)


def _hy_conv(p, conv_w, kf, bias, consts, seq, n_seq, seq_blk0, tc):
    fm, ft, _ = consts
    n = 2 * seq
    nc = D // tc
    full = lambda shape: pl.BlockSpec(shape, lambda b, c: (0, 0))
    grp = lambda g: pl.BlockSpec((seq, tc), lambda b, c: (seq_blk0 + b, g * nc + c))
    cwg = lambda g: pl.BlockSpec((3, tc), lambda b, c: (0, g * nc + c))
    return pl.pallas_call(
        functools.partial(_hy_conv_kernel, seq=seq),
        grid=(n_seq, nc),
        in_specs=[grp(0), grp(1), grp(2), cwg(0), cwg(1), cwg(2),
                  pl.BlockSpec((n, tc), lambda b, c: (0, c)),
                  pl.BlockSpec((1, tc), lambda b, c: (0, c)),
                  full((n, seq)), full((seq, n))],
        out_specs=pl.BlockSpec((seq, tc), lambda b, c: (b, c)),
        out_shape=jax.ShapeDtypeStruct((n_seq * seq, D), BF16),
        compiler_params=_cparams(("arbitrary", "arbitrary"), VMEM_LIMIT),
        name="hy_conv",
    )(p, p, p, conv_w, conv_w, conv_w, kf, bias.reshape(1, D), fm, ft)


MOE_TM = 256
MOE_TILES = 2 * T_ALL // MOE_TM + N_EXP
MOE_ROWS = MOE_TILES * MOE_TM
ROUTE_TM = 512
DISPATCH_TM = 2048
COMBINE_TM = 512
DMA_UNROLL = 32


def _router_kernel(h_ref, rw_ref, rb_ref, tri_ref, ei_ref, gf_ref, cnt_ref, carry):
    @pl.when(pl.program_id(0) == 0)
    def _():
        carry[...] = jnp.zeros(carry.shape, F32)

    logits = _dot(h_ref[...], rw_ref[...], prec="bf16x3") + rb_ref[...]
    lane = lax.broadcasted_iota(jnp.int32, logits.shape, 1)
    logits = jnp.where(lane < N_EXP, logits, -jnp.inf)
    m1 = jnp.max(logits, axis=-1, keepdims=True)
    i1 = jnp.min(jnp.where(logits == m1, lane, 128), axis=-1, keepdims=True)
    rest = jnp.where(lane == i1, -jnp.inf, logits)
    m2 = jnp.max(rest, axis=-1, keepdims=True)
    i2 = jnp.min(jnp.where(rest == m2, lane, 128), axis=-1, keepdims=True)
    e = jnp.exp(m2 - m1)
    g1 = 1.0 / (1.0 + e)
    g2 = e * g1
    pick = jnp.where((lane == i1) | (lane == i2), 1.0, 0.0)
    before = carry[...] + jnp.dot(tri_ref[...], pick.astype(BF16), preferred_element_type=F32)
    r1 = jnp.sum(jnp.where(lane == i1, before, 0.0), axis=-1, keepdims=True)
    r2 = jnp.sum(jnp.where(lane == i2, before, 0.0), axis=-1, keepdims=True)
    carry[...] = carry[...] + jnp.sum(pick, axis=0, keepdims=True)
    ints = jnp.where(lane == 0, i1, jnp.where(lane == 1, i2, 0))
    ranks = jnp.where(lane == 2, r1, jnp.where(lane == 3, r2, 0.0))
    ei_ref[...] = ints + ranks.astype(jnp.int32)
    gf_ref[...] = jnp.where(lane == 0, g1, jnp.where(lane == 1, g2, 0.0))
    cnt_ref[...] = carry[...].astype(jnp.int32)


def _router(h, router_w, router_b):
    rw = jnp.zeros((D, 128), F32).at[:, :N_EXP].set(router_w)
    rb = jnp.zeros((1, 128), F32).at[0, :N_EXP].set(router_b)
    tri = jnp.asarray(np.tril(np.ones((ROUTE_TM, ROUTE_TM), np.float32), -1), BF16)
    row = pl.BlockSpec((ROUTE_TM, 128), lambda i: (i, 0))
    return pl.pallas_call(
        _router_kernel,
        grid=(T_ALL // ROUTE_TM,),
        in_specs=[pl.BlockSpec((ROUTE_TM, D), lambda i: (i, 0)),
                  pl.BlockSpec((D, 128), lambda i: (0, 0)),
                  pl.BlockSpec((1, 128), lambda i: (0, 0)),
                  pl.BlockSpec((ROUTE_TM, ROUTE_TM), lambda i: (0, 0))],
        out_specs=[row, row, pl.BlockSpec((1, 128), lambda i: (0, 0))],
        out_shape=[jax.ShapeDtypeStruct((T_ALL, 128), jnp.int32),
                   jax.ShapeDtypeStruct((T_ALL, 128), F32),
                   jax.ShapeDtypeStruct((1, 128), jnp.int32)],
        scratch_shapes=[pltpu.VMEM((1, 128), F32)],
        compiler_params=_cparams(("arbitrary",)),
        name="moe_router",
    )(h, rw, rb, tri)


def _moe_dispatch_kernel(pos_ref, h_ref, init_ref, out_ref, sem):
    del init_ref
    base = pl.program_id(0) * DISPATCH_TM

    def copy(r, p):
        return pltpu.make_async_copy(h_ref.at[pl.ds(r, 1)], out_ref.at[pl.ds(p, 1)], sem)

    def start(r, c):
        copy(r, pos_ref[base + r]).start(priority=0)
        copy(r, pos_ref[T_ALL + base + r]).start(priority=1)
        return c

    def wait(r, c):
        copy(0, 0).wait()
        copy(0, 0).wait()
        return c

    lax.fori_loop(0, DISPATCH_TM, start, 0, unroll=DMA_UNROLL)
    lax.fori_loop(0, DISPATCH_TM, wait, 0, unroll=DMA_UNROLL)


def _moe_dispatch(h, flat_pos, init):
    if init is None:
        init = jnp.zeros((MOE_ROWS, D), F32)
    any_spec = pl.BlockSpec(memory_space=pl.ANY)
    return pl.pallas_call(
        _moe_dispatch_kernel,
        grid_spec=pltpu.PrefetchScalarGridSpec(
            num_scalar_prefetch=1, grid=(T_ALL // DISPATCH_TM,),
            in_specs=[pl.BlockSpec((DISPATCH_TM, D), lambda i, pos: (i, 0)), any_spec],
            out_specs=any_spec,
            scratch_shapes=[pltpu.SemaphoreType.DMA(())]),
        out_shape=jax.ShapeDtypeStruct((MOE_ROWS, D), F32),
        input_output_aliases={2: 0},
        compiler_params=_cparams(("arbitrary",)),
        name="moe_dispatch",
    )(flat_pos, h, init)


def _moe_up_kernel(te_ref, nu_ref, g_ref, wg_ref, wu_ref, o_ref, wbf):
    i = pl.program_id(1)
    new_w = (i == 0) | (te_ref[i] != te_ref[jnp.maximum(i - 1, 0)])

    @pl.when(new_w)
    def _():
        wbf[0] = wg_ref[0, 0].astype(BF16)
        wbf[1] = wu_ref[0, 0].astype(BF16)

    @pl.when(i < nu_ref[0])
    def _():
        _swiglu_chunks(g_ref[...].astype(BF16), wbf, o_ref)

    @pl.when(i >= nu_ref[0])
    def _():
        o_ref[...] = jnp.zeros(o_ref.shape, o_ref.dtype)


def _moe_up(tile_expert, n_used, rows, wg, wu, li, tn):
    wspec = pl.BlockSpec((1, 1, D, tn), lambda j, i, te, nu: (li, te[i], 0, j))
    return pl.pallas_call(
        _moe_up_kernel,
        grid_spec=pltpu.PrefetchScalarGridSpec(
            num_scalar_prefetch=2, grid=(D_FF_E // tn, MOE_TILES),
            in_specs=[pl.BlockSpec((MOE_TM, D), lambda j, i, te, nu: (i, 0)), wspec, wspec],
            out_specs=pl.BlockSpec((MOE_TM, tn), lambda j, i, te, nu: (i, j)),
            scratch_shapes=[pltpu.VMEM((2, D, tn), BF16)]),
        out_shape=jax.ShapeDtypeStruct((MOE_ROWS, D_FF_E), BF16),
        compiler_params=_cparams(("arbitrary", "arbitrary"), VMEM_LIMIT),
        name="moe_up",
    )(tile_expert, n_used, rows, wg, wu)


def _moe_down_kernel(te_ref, nu_ref, a_ref, wd_ref, o_ref, wbf):
    i = pl.program_id(1)
    new_w = (i == 0) | (te_ref[i] != te_ref[jnp.maximum(i - 1, 0)])

    @pl.when(new_w)
    def _():
        wbf[...] = wd_ref[0, 0].astype(BF16)

    @pl.when(i < nu_ref[0])
    def _():
        o_ref[...] = jnp.dot(a_ref[...], wbf[...], preferred_element_type=F32)

    @pl.when(i >= nu_ref[0])
    def _():
        o_ref[...] = jnp.zeros(o_ref.shape, o_ref.dtype)


def _moe_down(tile_expert, n_used, act, wd, li, tn):
    return pl.pallas_call(
        _moe_down_kernel,
        grid_spec=pltpu.PrefetchScalarGridSpec(
            num_scalar_prefetch=2, grid=(D // tn, MOE_TILES),
            in_specs=[pl.BlockSpec((MOE_TM, D_FF_E), lambda j, i, te, nu: (i, 0)),
                      pl.BlockSpec((1, 1, D_FF_E, tn), lambda j, i, te, nu: (li, te[i], 0, j))],
            out_specs=pl.BlockSpec((MOE_TM, tn), lambda j, i, te, nu: (i, j)),
            scratch_shapes=[pltpu.VMEM((D_FF_E, tn), BF16)]),
        out_shape=jax.ShapeDtypeStruct((MOE_ROWS, D), F32),
        compiler_params=_cparams(("arbitrary", "arbitrary"), VMEM_LIMIT),
        name="moe_down",
    )(tile_expert, n_used, act, wd)


def _moe_combine_kernel(pos_ref, x_ref, y_ref, gf_ref, gate_ref, *rest, with_next):
    if with_next:
        g_ref, sc_ref, sh_ref, o_ref, h_ref, ybuf, sems = rest
    else:
        o_ref, ybuf, sems = rest
    i = pl.program_id(0)
    buf = i % 2

    def copy(b, which, r, p):
        return pltpu.make_async_copy(y_ref.at[pl.ds(p, 1)], ybuf.at[b, which, pl.ds(r, 1)], sems.at[b])

    def gather(tile, b):
        base = tile * COMBINE_TM

        def start(r, c):
            copy(b, 0, r, pos_ref[base + r]).start(priority=0)
            copy(b, 1, r, pos_ref[T_ALL + base + r]).start(priority=1)
            return c
        lax.fori_loop(0, COMBINE_TM, start, 0, unroll=DMA_UNROLL)

    @pl.when(i == 0)
    def _():
        gather(0, 0)

    @pl.when(i + 1 < pl.num_programs(0))
    def _():
        gather(i + 1, 1 - buf)

    def wait(r, c):
        copy(buf, 0, 0, 0).wait()
        copy(buf, 1, 0, 0).wait()
        return c

    lax.fori_loop(0, COMBINE_TM, wait, 0, unroll=DMA_UNROLL)
    gf = gf_ref[...]
    f = gf[:, 0:1] * ybuf[buf, 0] + gf[:, 1:2] * ybuf[buf, 1]
    x_new = x_ref[...] + gate_ref[0] * f
    o_ref[...] = x_new
    if with_next:
        h_ref[...] = _modulate_math(x_new, g_ref[...], sc_ref[0], sh_ref[0]).astype(h_ref.dtype)


def _moe_combine(x, y, flat_pos, gf, mod, layer, next_norm):
    tm = COMBINE_TM
    base = (layer * 6 + 5) * N_SEG
    row = pl.BlockSpec((tm, D), lambda i, pos: (i, 0))
    in_specs = [row, pl.BlockSpec(memory_space=pl.ANY),
                pl.BlockSpec((tm, 128), lambda i, pos: (i, 0)),
                pl.BlockSpec((1, 1, D), lambda i, pos: (base + _seg_of_row(i * tm), 0, 0))]
    args = [flat_pos, x, y, gf, mod]
    out_specs, out_shape = row, jax.ShapeDtypeStruct((T_ALL, D), F32)
    if next_norm is not None:
        g, n_layer, k_shift = next_norm
        in_specs += [pl.BlockSpec((1, D), lambda i, pos: (0, 0)),
                     _mod_spec(n_layer, k_shift + 1, tm, lambda i, pos: i),
                     _mod_spec(n_layer, k_shift, tm, lambda i, pos: i)]
        args += [g.reshape(1, D), mod, mod]
        out_specs = [row, row]
        out_shape = [out_shape, jax.ShapeDtypeStruct((T_ALL, D), BF16)]
    return pl.pallas_call(
        functools.partial(_moe_combine_kernel, with_next=next_norm is not None),
        grid_spec=pltpu.PrefetchScalarGridSpec(
            num_scalar_prefetch=1, grid=(T_ALL // tm,), in_specs=in_specs, out_specs=out_specs,
            scratch_shapes=[pltpu.VMEM((2, 2, tm, D), F32), pltpu.SemaphoreType.DMA((2,))]),
        out_shape=out_shape,
        compiler_params=_cparams(("arbitrary",)),
        name="moe_combine",
    )(*args)


def _moe(x, h, mod, layer, li, w, next_norm, sorted_init):
    ei, gf, cnt = _router(h, w["moe_router_w"][li], w["moe_router_b"][li])
    experts, ranks, counts = ei[:, 0:2], ei[:, 2:4], cnt[0, :N_EXP]
    tiles = (counts + MOE_TM - 1) // MOE_TM
    tile_end = jnp.cumsum(tiles)
    row0 = (tile_end - tiles) * MOE_TM
    pos = (row0[experts] + ranks).astype(jnp.int32)
    n_used = tile_end[-1:].astype(jnp.int32)
    t_idx = jnp.minimum(jnp.arange(MOE_TILES, dtype=jnp.int32), n_used[0] - 1)
    tile_expert = jnp.sum(t_idx[:, None] >= tile_end[None, :], axis=1).astype(jnp.int32)
    flat_pos = pos.T.reshape(-1)
    sorted_rows = _moe_dispatch(h, flat_pos, sorted_init)
    act = _moe_up(tile_expert, n_used, sorted_rows, w["moe_w_gate"], w["moe_w_up"], li, MOE_UP_TN)
    y = _moe_down(tile_expert, n_used, act, w["moe_w_down"], li, D)
    return _moe_combine(x, y, flat_pos, gf, mod, layer, next_norm), sorted_rows


def _odd_mixer_layer(x, h, mod, l, i, w, spectra, consts, sorted_init):
    p = _mm([h], [w["odd_w_in"]], tm=TM_PROJ, tn=TN_ODD_IN, wi=i)
    cw = w["hy_conv_w"][i]
    z_c = _hy_conv(p, cw, spectra[SEQ][i], w["hy_bias"][i], consts[SEQ], SEQ, BATCH, 0, D)
    z_l = _hy_conv(p, cw, spectra[DEC_SEQ][i], w["hy_bias"][i], consts[DEC_SEQ], DEC_SEQ, DEC_BATCH,
                   T_CTX // DEC_SEQ, HY_TC_LAT)
    x, h2 = _mm([(z_c, z_l)], [w["odd_w_out"]], tm=TM_EPI, tn=D, epilogue="residual_mod", wi=i,
                resid=x, mod=mod, layer=l, k_gate=2, next_norm=(w["norm_ffn_g"][l], l, 3, F32))
    next_norm = (w["norm_mix_g"][l + 1], l + 1, 0) if l + 1 < DEPTH else None
    out, sorted_rows = _moe(x, h2, mod, l, i, w, next_norm, sorted_init)
    x, h_next = out if next_norm is not None else (out, None)
    return x, h_next, sorted_rows


def kernel(x_prompt, x_sample, state_gdn, cache_nat_k, cache_nat_v, c, c_ctx, ada_w, ada_b, norm_mix_g, norm_ffn_g, even_w_in, gdn_conv_w, gdn_a_log, gdn_dt_bias, gdn_norm_g, nat_q_norm_g, nat_k_norm_g, nat_rpb, even_w_out, ffn_w_gate, ffn_w_up, ffn_w_down, odd_w_in, hy_conv_w, hy_w1, hy_b1, hy_w2, hy_b2, hy_w3, hy_freq, hy_bias, odd_w_out, moe_router_w, moe_router_b, moe_w_gate, moe_w_up, moe_w_down):
    w = dict(state_gdn=state_gdn, cache_nat_k=cache_nat_k, cache_nat_v=cache_nat_v,
             norm_mix_g=norm_mix_g, norm_ffn_g=norm_ffn_g, even_w_in=even_w_in, gdn_conv_w=gdn_conv_w,
             gdn_a_log=gdn_a_log, gdn_dt_bias=gdn_dt_bias, gdn_norm_g=gdn_norm_g,
             nat_q_norm_g=nat_q_norm_g, nat_k_norm_g=nat_k_norm_g, nat_rpb=nat_rpb, even_w_out=even_w_out,
             ffn_w_gate=ffn_w_gate, ffn_w_up=ffn_w_up, ffn_w_down=ffn_w_down, odd_w_in=odd_w_in,
             hy_conv_w=hy_conv_w, hy_bias=hy_bias, odd_w_out=odd_w_out, moe_router_w=moe_router_w,
             moe_router_b=moe_router_b, moe_w_gate=moe_w_gate, moe_w_up=moe_w_up, moe_w_down=moe_w_down)
    assert SEQ == ROW_TILE and DEC_SEQ % ROW_TILE == 0
    mod = _ada_all(c, c_ctx, ada_w, ada_b)
    consts = {s: _dft_consts(s) for s in (SEQ, DEC_SEQ)}
    n_odd = DEPTH // 2
    spectra = {s: [_hy_filter_spectrum(s, consts[s], hy_w1[i], hy_b1[i], hy_w2[i], hy_b2[i], hy_w3[i], hy_freq[i])
                   for i in range(n_odd)] for s in (SEQ, DEC_SEQ)}
    x, h = _modulate_first(x_prompt.reshape(T_CTX, D), x_sample.reshape(T_LAT, D), norm_mix_g[0], mod, BF16)
    states, kns, pms = [], [], []
    sorted_rows = None
    for l in range(DEPTH):
        i = l // 2
        if l % 2 == 0:
            x, h, kn, p_main, s_ctx = _even_mixer_layer(x, h, mod, l, i, w)
            states.append(s_ctx)
            kns.append(kn)
            pms.append(p_main)
        else:
            x, h, sorted_rows = _odd_mixer_layer(x, h, mod, l, i, w, spectra, consts, sorted_rows)
    k_cache, v_cache, state_out = _finalize_caches(kns, pms, states)
    y_prompt = x[:T_CTX].reshape(BATCH, SEQ, D)
    y_sample = x[T_CTX:].reshape(DEC_BATCH, DEC_SEQ, D)
    return (y_prompt, y_sample, state_out, k_cache, v_cache)
```
